```python
import jax, jax.numpy as jnp
from jax import lax
import numpy as np

D_MODEL = 4096
BATCH = 1
SEQ = 8192
DEPTH = 2
DEC_BATCH = 4
DEC_SEQ = 4096
PAST_LEN = 128

N_MIXERS = 2
N_POOL_LAYERS = (DEPTH + 1) // 2
N_HGRN_LAYERS = DEPTH // 2
N_DENSE_LAYERS = (DEPTH + 1) // 2
N_MOE_LAYERS = DEPTH // 2
POOL_WINDOWS = (2, 4, 8, 16)
POOL_GROUPS = len(POOL_WINDOWS)
POOL_GROUP_DIM = D_MODEL // POOL_GROUPS
HGRN_EXPAND = 128
HGRN_HEADS = D_MODEL // HGRN_EXPAND
HGRN_DK = HGRN_EXPAND
HGRN_DV = D_MODEL // HGRN_HEADS
CHUNK = 16
D_FF = 2 * D_MODEL
N_EXPERTS = 8
TOP_K = 2
MOE_D_FF = D_MODEL // 4
EPS = 1e-6

kernel_name = "hybrid_pool_hgrn2_moe_encoder"


def rms_norm(x, g):
    xf = x.astype(jnp.float32)
    y = xf * lax.rsqrt(jnp.mean(xf * xf, axis=-1, keepdims=True) + EPS)
    return (y * g.astype(jnp.float32)).astype(x.dtype)


def swiglu(h, w_gate, w_up, w_down):
    return (jax.nn.silu(h @ w_gate) * (h @ w_up)) @ w_down


def moe_swiglu(h, router, w_gate, w_up, w_down):
    B, S, D = h.shape
    t = h.reshape(B * S, D)
    logits = (t @ router).astype(jnp.float32)
    vals, idx = lax.top_k(logits, TOP_K)
    gates = jax.nn.softmax(vals, axis=-1)
    combine = jnp.sum(jax.nn.one_hot(idx, N_EXPERTS, dtype=jnp.float32) * gates[..., None], axis=1)
    out = jnp.zeros((B * S, D), jnp.float32)
    for e in range(N_EXPERTS):
        out = out + combine[:, e:e + 1] * swiglu(t, w_gate[e], w_up[e], w_down[e]).astype(jnp.float32)
    return out.astype(h.dtype).reshape(B, S, D)


def pool_mixer(h, w_grp, scale):
    B, S, D = h.shape
    hf = h.astype(jnp.float32)
    csum = jnp.concatenate([jnp.zeros((B, 1, D), jnp.float32), jnp.cumsum(hf, axis=1)], axis=1)
    pos = jnp.arange(S)
    outs = []
    for gi, w in enumerate(POOL_WINDOWS):
        sl = slice(gi * POOL_GROUP_DIM, (gi + 1) * POOL_GROUP_DIM)
        lo = jnp.clip(pos - w // 2, 0, S)
        hi = jnp.clip(pos + w // 2, 0, S)
        cg = csum[:, :, sl]
        cnt = (hi - lo).astype(jnp.float32)
        mean = (cg[:, hi] - cg[:, lo]) / cnt[None, :, None]
        outs.append(mean - hf[:, :, sl])
    d = jnp.stack(outs, axis=2).astype(h.dtype)
    y = jnp.einsum('bsgc,gcd->bsgd', d, w_grp).reshape(B, S, D)
    return y * scale


def gla_chunk_scan(q, k, v, g):
    B, S, H, DK = q.shape
    DV = v.shape[-1]
    n = S // CHUNK

    def to_chunks(a):
        return a.reshape(B, n, CHUNK, H, a.shape[-1]).transpose(1, 0, 3, 2, 4)

    mask = jnp.tril(jnp.ones((CHUNK, CHUNK), bool))[:, :, None]

    def step(state, inp):
        qc, kc, vc, gc = inp
        b = jnp.cumsum(gc, axis=2)
        o_inter = jnp.einsum('bhtd,bhde->bhte', qc * jnp.exp(b), state)
        diff = b[:, :, :, None, :] - b[:, :, None, :, :]
        decay = jnp.where(mask, jnp.exp(jnp.where(mask, diff, 0.0)), 0.0)
        scores = jnp.einsum('bhtd,bhtsd,bhsd->bhts', qc, decay, kc)
        o = o_inter + jnp.einsum('bhts,bhse->bhte', scores, vc)
        b_last = b[:, :, -1:, :]
        state = jnp.exp(b_last[:, :, 0, :])[..., None] * state + jnp.einsum(
            'bhsd,bhse->bhde', kc * jnp.exp(b_last - b), vc)
        return state, o

    state0 = jnp.zeros((B, H, DK, DV), jnp.float32)
    _, o = lax.scan(step, state0, (to_chunks(q), to_chunks(k), to_chunks(v), to_chunks(g)))
    return o.transpose(1, 0, 3, 2, 4).reshape(B, S, H, DV)


def hgrn2_mixer(h, w_in, lb, gain, w_out):
    B, S, D = h.shape
    proj = h @ w_in
    q, f_fw, f_bw, v, g = jnp.split(proj, 5, axis=-1)

    def heads(a):
        return a.astype(jnp.float32).reshape(B, S, HGRN_HEADS, -1)

    q = jax.nn.silu(heads(q))
    v = heads(v)
    lbh = lb.astype(jnp.float32).reshape(HGRN_HEADS, HGRN_DK)

    def forget(fl):
        f = lbh + (1.0 - lbh) * jax.nn.sigmoid(heads(fl))
        return 1.0 - f, jnp.log(f)

    k_fw, g_fw = forget(f_fw)
    k_bw, g_bw = forget(f_bw)
    o_fw = gla_chunk_scan(q, k_fw, v, g_fw)

    def rev(a):
        return jnp.flip(a, axis=1)

    o_bw = rev(gla_chunk_scan(rev(q), rev(k_bw), rev(v), rev(g_bw)))
    o = (o_fw + o_bw).reshape(B, S, D).astype(h.dtype)
    o = rms_norm(o, gain) * jax.nn.silu(g)
    return o @ w_out


def trunk(x, norm_mix, norm_ffn, norm_final, pool_w, pool_scale, hgrn_w_in, lb_all,
          hgrn_norm, hgrn_w_out, ffn_w_gate, ffn_w_up, ffn_w_down,
          moe_router, moe_w_gate, moe_w_up, moe_w_down):
    for i in range(DEPTH):
        j = i // N_MIXERS
        h = rms_norm(x, norm_mix[i])
        if i % N_MIXERS == 0:
            x = x + pool_mixer(h, pool_w[j], pool_scale[j])
        else:
            x = x + hgrn2_mixer(h, hgrn_w_in[j], lb_all[i], hgrn_norm[j], hgrn_w_out[j])
        h = rms_norm(x, norm_ffn[i])
        if i % 2 == 0:
            x = x + swiglu(h, ffn_w_gate[j], ffn_w_up[j], ffn_w_down[j])
        else:
            x = x + moe_swiglu(h, moe_router[j], moe_w_gate[j], moe_w_up[j], moe_w_down[j])
    return rms_norm(x, norm_final)


def setup_inputs(seed: int = 0) -> dict:
    key = jax.random.key(seed)
    ks = jax.random.split(key, 20)

    def normal(k, shape, scale):
        return jax.random.normal(k, shape, jnp.float32) * scale

    D, G, F, E, FE = D_MODEL, POOL_GROUP_DIM, D_FF, N_EXPERTS, MOE_D_FF
    return {
        "x_prompt": normal(ks[0], (BATCH, SEQ, D), 1.0),
        "x_sample": normal(ks[1], (DEC_BATCH, DEC_SEQ, D), 1.0),
        "norm_mix": 1.0 + normal(ks[2], (DEPTH, D), 0.02),
        "norm_ffn": 1.0 + normal(ks[3], (DEPTH, D), 0.02),
        "norm_final": 1.0 + normal(ks[4], (D,), 0.02),
        "pool_w": normal(ks[5], (N_POOL_LAYERS, POOL_GROUPS, G, G), G ** -0.5),
        "pool_scale": 1.0 + normal(ks[6], (N_POOL_LAYERS, D), 0.02),
        "hgrn_w_in": normal(ks[7], (N_HGRN_LAYERS, D, 5 * D), D ** -0.5),
        "hgrn_lb": normal(ks[8], (DEPTH, HGRN_HEADS * HGRN_DK), 0.1),
        "hgrn_norm": 1.0 + normal(ks[9], (N_HGRN_LAYERS, D), 0.02),
        "hgrn_w_out": normal(ks[10], (N_HGRN_LAYERS, D, D), D ** -0.5),
        "ffn_w_gate": normal(ks[11], (N_DENSE_LAYERS, D, F), D ** -0.5),
        "ffn_w_up": normal(ks[12], (N_DENSE_LAYERS, D, F), D ** -0.5),
        "ffn_w_down": normal(ks[13], (N_DENSE_LAYERS, F, D), F ** -0.5),
        "moe_router": normal(ks[14], (N_MOE_LAYERS, D, E), D ** -0.5),
        "moe_w_gate": normal(ks[15], (N_MOE_LAYERS, E, D, FE), D ** -0.5),
        "moe_w_up": normal(ks[16], (N_MOE_LAYERS, E, D, FE), D ** -0.5),
        "moe_w_down": normal(ks[17], (N_MOE_LAYERS, E, FE, D), FE ** -0.5),
    }


def reference(x_prompt, x_sample, norm_mix, norm_ffn, norm_final, pool_w, pool_scale,
              hgrn_w_in, hgrn_lb, hgrn_norm, hgrn_w_out, ffn_w_gate, ffn_w_up, ffn_w_down,
              moe_router, moe_w_gate, moe_w_up, moe_w_down):
    lb_sm = jax.nn.softmax(hgrn_lb.astype(jnp.float32), axis=0)
    lb_all = jnp.cumsum(lb_sm, axis=0) - lb_sm[0:1]
    y_prompt = trunk(x_prompt, norm_mix, norm_ffn, norm_final, pool_w, pool_scale, hgrn_w_in, lb_all,
                     hgrn_norm, hgrn_w_out, ffn_w_gate, ffn_w_up, ffn_w_down,
                     moe_router, moe_w_gate, moe_w_up, moe_w_down)
    y_sample = trunk(x_sample, norm_mix, norm_ffn, norm_final, pool_w, pool_scale, hgrn_w_in, lb_all,
                     hgrn_norm, hgrn_w_out, ffn_w_gate, ffn_w_up, ffn_w_down,
                     moe_router, moe_w_gate, moe_w_up, moe_w_down)
    return (y_prompt, y_sample)
```

```python
import functools

import jax
import jax.numpy as jnp
from jax import lax
from jax.experimental import pallas as pl
from jax.experimental.pallas import tpu as pltpu

EPS = 1e-6
POOL_WINDOWS = (2, 4, 8, 16)
POOL_HALO = 8
HEAD_DIM = 128
TOP_K = 2
V7X_VMEM_LIMIT = 56 * 1024 * 1024
LANES = 128

bf16 = jnp.bfloat16
f32 = jnp.float32


def _cparams(sem):
    return pltpu.CompilerParams(dimension_semantics=sem, vmem_limit_bytes=V7X_VMEM_LIMIT)


def _tile(n, pref):
    t = min(n, pref)
    while n % t:
        t //= 2
    return t


def _rms(x, g):
    ms = jnp.mean(x * x, axis=-1, keepdims=True)
    return x * lax.rsqrt(ms + EPS) * g


def _silu(x):
    return x * jax.nn.sigmoid(x)


def _seq_flags(tok, n_tok, n_prompt, seq_p, seq_s):
    in_p = tok < n_prompt
    rel = jnp.where(in_p, tok, tok - n_prompt)
    slen = jnp.where(in_p, seq_p, seq_s)
    is_start = (rel % slen) == 0
    is_end = ((rel + n_tok) % slen) == 0
    return is_start, is_end


def _rmsnorm_kernel(x_ref, g_ref, o_ref):
    o_ref[...] = _rms(x_ref[...], g_ref[...]).astype(o_ref.dtype)


def _rmsnorm(x, g, out_dtype, row_start=0, n_rows=None):
    n, d = x.shape
    n_rows = n if n_rows is None else n_rows
    tr = _tile(n_rows, 512)
    off = row_start // tr
    return pl.pallas_call(
        _rmsnorm_kernel,
        grid=(n_rows // tr,),
        in_specs=[pl.BlockSpec((tr, d), lambda i: (i + off, 0)),
                  pl.BlockSpec((1, d), lambda i: (0, 0))],
        out_specs=pl.BlockSpec((tr, d), lambda i: (i, 0)),
        out_shape=jax.ShapeDtypeStruct((n_rows, d), out_dtype),
        compiler_params=_cparams(("parallel",)),
    )(x, g.reshape(1, d))


def _pool_kernel(xp_ref, x_ref, xn_ref, gmix_ref, w_ref, scale_ref, gffn_ref,
                 x1_ref, h1_ref, hcat_ref, *, seq):
    t, d = x_ref.shape
    dg = d // len(POOL_WINDOWS)
    is_start, is_end = _seq_flags(pl.program_id(0) * t, t, *seq)
    g = gmix_ref[...]
    x = x_ref[...]
    h = _rms(x, g)
    hcat_ref[0:POOL_HALO, :] = jnp.where(is_start, 0.0, _rms(xp_ref[...], g))
    hcat_ref[POOL_HALO:POOL_HALO + t, :] = h
    hcat_ref[POOL_HALO + t:2 * POOL_HALO + t, :] = jnp.where(is_end, 0.0, _rms(xn_ref[...], g))
    r = lax.broadcasted_iota(jnp.int32, (t, 1), 0)
    for gi, w in enumerate(POOL_WINDOWS):
        c0 = gi * dg
        half = w // 2
        s = hcat_ref[POOL_HALO - half:POOL_HALO - half + t, c0:c0 + dg]
        for j in range(-half + 1, half):
            s = s + hcat_ref[POOL_HALO + j:POOL_HALO + j + t, c0:c0 + dg]
        lo = jnp.where(is_start, jnp.maximum(r - half, 0), r - half)
        hi = jnp.where(is_end, jnp.minimum(r + half, t), r + half)
        cnt = (hi - lo).astype(f32)
        dlt = s / cnt - h[:, c0:c0 + dg]
        y = jnp.dot(dlt.astype(bf16), w_ref[gi], preferred_element_type=f32)
        x1_ref[:, c0:c0 + dg] = x[:, c0:c0 + dg] + y * scale_ref[:, c0:c0 + dg]
    h1_ref[...] = _rms(x1_ref[...], gffn_ref[...]).astype(h1_ref.dtype)


def _pool_layer(x, g_mix, w_grp, scale, g_ffn, seq):
    n, d = x.shape
    t = _tile(min(seq[1], seq[2]), 256)
    hb = t // POOL_HALO
    nhb = n // POOL_HALO
    row = lambda i: (0, 0)
    return pl.pallas_call(
        functools.partial(_pool_kernel, seq=seq),
        grid=(n // t,),
        in_specs=[
            pl.BlockSpec((POOL_HALO, d), lambda i: (jnp.maximum(i * hb - 1, 0), 0)),
            pl.BlockSpec((t, d), lambda i: (i, 0)),
            pl.BlockSpec((POOL_HALO, d), lambda i: (jnp.minimum((i + 1) * hb, nhb - 1), 0)),
            pl.BlockSpec((1, d), row),
            pl.BlockSpec(w_grp.shape, lambda i: (0, 0, 0)),
            pl.BlockSpec((1, d), row),
            pl.BlockSpec((1, d), row),
        ],
        out_specs=[pl.BlockSpec((t, d), lambda i: (i, 0)),
                   pl.BlockSpec((t, d), lambda i: (i, 0))],
        out_shape=[jax.ShapeDtypeStruct((n, d), f32), jax.ShapeDtypeStruct((n, d), bf16)],
        scratch_shapes=[pltpu.VMEM((t + 2 * POOL_HALO, d), f32)],
        compiler_params=_cparams(("parallel",)),
    )(x, x, x, g_mix.reshape(1, d), w_grp, scale.reshape(1, d), g_ffn.reshape(1, d))


def _glu_up_kernel(a_ref, wg_ref, wu_ref, o_ref):
    a = a_ref[...]
    gate = jnp.dot(a, wg_ref[...], preferred_element_type=f32)
    up = jnp.dot(a, wu_ref[...], preferred_element_type=f32)
    o_ref[...] = (_silu(gate) * up).astype(o_ref.dtype)


def _glu_up(a, wg, wu):
    m, k = a.shape
    e, _, f = wg.shape
    tm = _tile(m, 1024)
    tn = _tile(f, 512)
    nb = f // tn
    wspec = pl.BlockSpec((None, k, tn), lambda i, j: (j // nb, 0, j % nb))
    return pl.pallas_call(
        _glu_up_kernel,
        grid=(m // tm, e * nb),
        in_specs=[pl.BlockSpec((tm, k), lambda i, j: (i, 0)), wspec, wspec],
        out_specs=pl.BlockSpec((tm, tn), lambda i, j: (i, j)),
        out_shape=jax.ShapeDtypeStruct((m, e * f), bf16),
        compiler_params=_cparams(("parallel", "arbitrary")),
    )(a, wg, wu)


def _down_kernel(a_ref, w_ref, res_ref, o_ref, acc_ref):
    k = pl.program_id(2)

    @pl.when(k == 0)
    def _():
        acc_ref[...] = jnp.zeros_like(acc_ref)

    acc_ref[...] += jnp.dot(a_ref[...], w_ref[...], preferred_element_type=f32)

    @pl.when(k == pl.num_programs(2) - 1)
    def _():
        o_ref[...] = res_ref[...] + acc_ref[...]


def _down_res(a, w, res, tk):
    m, kk = a.shape
    n = w.shape[1]
    tm = _tile(m, 1024)
    tn = _tile(n, 512)
    tk = _tile(kk, tk)
    return pl.pallas_call(
        _down_kernel,
        grid=(m // tm, n // tn, kk // tk),
        in_specs=[pl.BlockSpec((tm, tk), lambda i, j, k: (i, k)),
                  pl.BlockSpec((tk, tn), lambda i, j, k: (k, j)),
                  pl.BlockSpec((tm, tn), lambda i, j, k: (i, j))],
        out_specs=pl.BlockSpec((tm, tn), lambda i, j, k: (i, j)),
        out_shape=jax.ShapeDtypeStruct((m, n), f32),
        scratch_shapes=[pltpu.VMEM((tm, tn), f32)],
        compiler_params=_cparams(("parallel", "parallel", "arbitrary")),
    )(a, w, res)


def _moe_down_kernel(a_ref, w_ref, c_ref, res_ref, o_ref, acc_ref):
    e = pl.program_id(2)

    @pl.when(e == 0)
    def _():
        acc_ref[...] = jnp.zeros_like(acc_ref)

    c = c_ref[...]
    lane = lax.broadcasted_iota(jnp.int32, c.shape, 1)
    ce = jnp.sum(jnp.where(lane == e, c, 0.0), axis=1, keepdims=True)
    acc_ref[...] += ce * jnp.dot(a_ref[...], w_ref[...], preferred_element_type=f32)

    @pl.when(e == pl.num_programs(2) - 1)
    def _():
        o_ref[...] = res_ref[...] + acc_ref[...]


def _moe_down(a, wd, combine, res):
    m = a.shape[0]
    e, fe, n = wd.shape
    tm = _tile(m, 1024)
    tn = _tile(n, 1024)
    return pl.pallas_call(
        _moe_down_kernel,
        grid=(m // tm, n // tn, e),
        in_specs=[pl.BlockSpec((tm, fe), lambda i, j, k: (i, k)),
                  pl.BlockSpec((None, fe, tn), lambda i, j, k: (k, 0, j)),
                  pl.BlockSpec((tm, e), lambda i, j, k: (i, 0)),
                  pl.BlockSpec((tm, tn), lambda i, j, k: (i, j))],
        out_specs=pl.BlockSpec((tm, tn), lambda i, j, k: (i, j)),
        out_shape=jax.ShapeDtypeStruct((m, n), f32),
        scratch_shapes=[pltpu.VMEM((tm, tn), f32)],
        compiler_params=_cparams(("parallel", "parallel", "arbitrary")),
    )(a, wd, combine, res)


def _store_heads(o_ref, val):
    for hh in range(o_ref.shape[0]):
        o_ref[hh] = val[:, hh * HEAD_DIM:(hh + 1) * HEAD_DIM].astype(o_ref.dtype)


def _proj_silu_heads_kernel(a_ref, w_ref, o_ref):
    _store_heads(o_ref, _silu(jnp.dot(a_ref[...], w_ref[...], preferred_element_type=f32)))


def _proj_heads_kernel(a_ref, w_ref, o_ref):
    _store_heads(o_ref, jnp.dot(a_ref[...], w_ref[...], preferred_element_type=f32))


def _proj_silu_kernel(a_ref, w_ref, o_ref):
    o_ref[...] = _silu(jnp.dot(a_ref[...], w_ref[...], preferred_element_type=f32)).astype(o_ref.dtype)


def _proj_forget_kernel(a_ref, w_ref, lb_ref, k_ref, g_ref, *, layer):
    logits = lb_ref[...]
    ex = jnp.exp(logits - jnp.max(logits, axis=0, keepdims=True))
    sm = ex / jnp.sum(ex, axis=0, keepdims=True)
    lb = jnp.sum(sm[0:layer + 1], axis=0, keepdims=True) - sm[0:1]
    f = lb + (1.0 - lb) * jax.nn.sigmoid(jnp.dot(a_ref[...], w_ref[...], preferred_element_type=f32))
    _store_heads(k_ref, 1.0 - f)
    _store_heads(g_ref, jnp.log(f))


def _proj_section(kernel, a, w, section, out_heads, n_out=1, extra=None):
    m, k = a.shape
    d = k
    tm = _tile(m, 1024)
    tn = _tile(d, 1024)
    nb = d // tn
    in_specs = [pl.BlockSpec((tm, k), lambda i, j: (i, 0)),
                pl.BlockSpec((k, tn), lambda i, j: (0, section * nb + j))]
    args = [a, w]
    if extra is not None:
        in_specs.append(pl.BlockSpec((extra.shape[0], tn), lambda i, j: (0, j)))
        args.append(extra)
    if out_heads:
        hpb = tn // HEAD_DIM
        ospec = pl.BlockSpec((hpb, tm, HEAD_DIM), lambda i, j: (j, i, 0))
        oshape = jax.ShapeDtypeStruct((d // HEAD_DIM, m, HEAD_DIM), bf16)
    else:
        ospec = pl.BlockSpec((tm, tn), lambda i, j: (i, j))
        oshape = jax.ShapeDtypeStruct((m, d), bf16)
    out = pl.pallas_call(
        kernel,
        grid=(m // tm, nb),
        in_specs=in_specs,
        out_specs=[ospec] * n_out if n_out > 1 else ospec,
        out_shape=[oshape] * n_out if n_out > 1 else oshape,
        compiler_params=_cparams(("parallel", "arbitrary")),
    )(*args)
    return out


def _dot_nt(a, b):
    return lax.dot_general(a, b, (((1,), (1,)), ((), ())), preferred_element_type=f32)


def _dot_tn(a, b):
    return lax.dot_general(a, b, (((0,), (0,)), ((), ())), preferred_element_type=f32)


def _mid_ref(p, level, offset):
    c, w = p.shape
    blk = 2 << level
    p3 = p.reshape(c // blk, blk, w)
    return jnp.broadcast_to(p3[:, offset:offset + 1, :], p3.shape).reshape(c, w)


def _neg_abs_exponents(gf, gb, b, x, level, rmod4):
    if level == 0:
        odd = (rmod4 & 1) == 1
        return jnp.where(odd, gf, 0.0), jnp.where(odd, 0.0, gb)
    if level == 1:
        c = gf.shape[0]
        gf_up = pltpu.roll(gf, c - 1, 0)
        gf_dn = pltpu.roll(gf, 1, 0)
        gb_up = pltpu.roll(gb, c - 1, 0)
        gb_dn = pltpu.roll(gb, 1, 0)
        ef = jnp.where(rmod4 == 0, gf_up, jnp.where(rmod4 == 1, 0.0, jnp.where(rmod4 == 2, gf, gf + gf_dn)))
        eb = jnp.where(rmod4 == 0, gb + gb_up, jnp.where(rmod4 == 1, gb, jnp.where(rmod4 == 2, 0.0, gb_dn)))
        return ef, eb
    half = 1 << level
    ef = b - _mid_ref(b, level, half - 1)
    eb = x - _mid_ref(x, level, half)
    return -jnp.abs(ef), -jnp.abs(eb)


def _chunk_prefix(tri_incl, tri_excl, gf16, gb16):
    b = jnp.dot(tri_incl, gf16, preferred_element_type=f32)
    x = jnp.dot(tri_excl, gb16, preferred_element_type=f32)
    return b, x


def _scan_bw_kernel(kb_ref, gb_ref, v_ref, s_ref, st_ref, *, seq, n_chunks):
    n_heads, c, _ = kb_ref.shape
    chunk = n_chunks - 1 - pl.program_id(0)
    _, is_end = _seq_flags(chunk * c, c, *seq)

    @pl.when(is_end)
    def _():
        st_ref[...] = jnp.zeros_like(st_ref)

    row = lax.broadcasted_iota(jnp.int32, (c, c), 0)
    col = lax.broadcasted_iota(jnp.int32, (c, c), 1)
    tri_excl = (col < row).astype(bf16)

    def head(h, carry):
        gb16 = gb_ref[h]
        x = jnp.dot(tri_excl, gb16, preferred_element_type=f32)
        tot = x[c - 1:c, :] + gb16[c - 1:c, :].astype(f32)
        st = st_ref[h]
        s_ref[h] = st.astype(s_ref.dtype)
        kx = (kb_ref[h].astype(f32) * jnp.exp(x)).astype(bf16)
        st_ref[h] = st * jnp.exp(tot) + _dot_tn(v_ref[h], kx)
        return carry

    lax.fori_loop(0, n_heads, head, 0)


def _scan_bw_states(kb, gb, v, seq, c):
    n_heads, n, dk = kb.shape
    n_chunks = n // c
    spec = pl.BlockSpec((n_heads, c, dk), lambda i: (0, n_chunks - 1 - i, 0))
    return pl.pallas_call(
        functools.partial(_scan_bw_kernel, seq=seq, n_chunks=n_chunks),
        grid=(n_chunks,),
        in_specs=[spec, spec, spec],
        out_specs=pl.BlockSpec((None, n_heads, dk, dk), lambda i: (n_chunks - 1 - i, 0, 0, 0)),
        out_shape=jax.ShapeDtypeStruct((n_chunks, n_heads, dk, dk), bf16),
        scratch_shapes=[pltpu.VMEM((n_heads, dk, dk), f32)],
        compiler_params=_cparams(("arbitrary",)),
    )(kb, gb, v)


def _scan_kernel(q_ref, kf_ref, kb_ref, gf_ref, gb_ref, v_ref, sb_ref, o_ref, st_ref, *, seq):
    n_heads, c, dk = q_ref.shape
    n_levels = c.bit_length() - 1
    is_start, _ = _seq_flags(pl.program_id(0) * c, c, *seq)

    @pl.when(is_start)
    def _():
        st_ref[...] = jnp.zeros_like(st_ref)

    row = lax.broadcasted_iota(jnp.int32, (c, c), 0)
    col = lax.broadcasted_iota(jnp.int32, (c, c), 1)
    tri_incl = (col <= row).astype(bf16)
    tri_excl = (col < row).astype(bf16)
    same_block = [(row >> (l + 1)) == (col >> (l + 1)) for l in range(n_levels - 1)]
    rtok = lax.broadcasted_iota(jnp.int32, (c, dk), 0)
    rmod4 = rtok & 3
    upper = [((rtok >> l) & 1) == 1 for l in range(n_levels)]

    def head(h, carry):
        q = q_ref[h].astype(f32)
        kf = kf_ref[h].astype(f32)
        kb = kb_ref[h].astype(f32)
        gf16 = gf_ref[h]
        gb16 = gb_ref[h]
        gf = gf16.astype(f32)
        gb = gb16.astype(f32)
        v = v_ref[h]
        b, x = _chunk_prefix(tri_incl, tri_excl, gf16, gb16)

        scores = jnp.zeros((c, c), f32)
        for l in range(n_levels):
            ef, eb = _neg_abs_exponents(gf, gb, b, x, l, rmod4)
            af = jnp.exp(ef)
            ab = jnp.exp(eb)
            up = upper[l]
            lhs = jnp.concatenate([jnp.where(up, q * af, 0.0), jnp.where(up, 0.0, q * ab)], axis=1)
            rhs = jnp.concatenate([jnp.where(up, 0.0, kf * af), jnp.where(up, kb * ab, 0.0)], axis=1)
            p = _dot_nt(lhs.astype(bf16), rhs.astype(bf16))
            scores = scores + (jnp.where(same_block[l], p, 0.0) if l < n_levels - 1 else p)

        b_last = b[c - 1:c, :]
        x_tot = x[c - 1:c, :] + gb[c - 1:c, :]
        q_in = jnp.concatenate([q * jnp.exp(b), q * jnp.exp(x_tot - x)], axis=1).astype(bf16)
        st_f = st_ref[h]
        st_cat = jnp.concatenate([st_f.astype(bf16), sb_ref[h]], axis=1)
        o = _dot_nt(q_in, st_cat)
        o = o + jnp.dot(scores.astype(bf16), v, preferred_element_type=f32)
        o = o + jnp.sum(q * (kf + kb), axis=1, keepdims=True) * v.astype(f32)
        o_ref[h] = o

        kx = (kf * jnp.exp(b_last - b)).astype(bf16)
        st_ref[h] = st_f * jnp.exp(b_last) + _dot_tn(v, kx)
        return carry

    lax.fori_loop(0, n_heads, head, 0)


def _scan(q, kf, kb, gf, gb, v, s_bw, seq, c):
    n_heads, n, dk = q.shape
    n_chunks = n // c
    spec = pl.BlockSpec((n_heads, c, dk), lambda i: (0, i, 0))
    return pl.pallas_call(
        functools.partial(_scan_kernel, seq=seq),
        grid=(n_chunks,),
        in_specs=[spec] * 6 + [pl.BlockSpec((None, n_heads, dk, dk), lambda i: (i, 0, 0, 0))],
        out_specs=spec,
        out_shape=jax.ShapeDtypeStruct((n_heads, n, dk), f32),
        scratch_shapes=[pltpu.VMEM((n_heads, dk, dk), f32)],
        compiler_params=_cparams(("arbitrary",)),
    )(q, kf, kb, gf, gb, v, s_bw)


def _norm_gate_kernel(o_ref, gate_ref, gain_ref, out_ref, row_ref):
    for hh in range(o_ref.shape[0]):
        row_ref[:, hh * HEAD_DIM:(hh + 1) * HEAD_DIM] = o_ref[hh]
    out_ref[...] = (_rms(row_ref[...], gain_ref[...]) * gate_ref[...].astype(f32)).astype(out_ref.dtype)


def _norm_gate(o, gate, gain):
    n_heads, n, dk = o.shape
    d = n_heads * dk
    tr = _tile(n, 256)
    return pl.pallas_call(
        _norm_gate_kernel,
        grid=(n // tr,),
        in_specs=[pl.BlockSpec((n_heads, tr, dk), lambda i: (0, i, 0)),
                  pl.BlockSpec((tr, d), lambda i: (i, 0)),
                  pl.BlockSpec((1, d), lambda i: (0, 0))],
        out_specs=pl.BlockSpec((tr, d), lambda i: (i, 0)),
        out_shape=jax.ShapeDtypeStruct((n, d), bf16),
        scratch_shapes=[pltpu.VMEM((tr, d), f32)],
        compiler_params=_cparams(("parallel",)),
    )(o, gate, gain.reshape(1, d))


def _router_kernel(x_ref, g_ref, rhi_ref, rlo_ref, h_ref, c_ref, *, n_experts):
    h = _rms(x_ref[...], g_ref[...])
    h_ref[...] = h.astype(h_ref.dtype)
    hi = h.astype(bf16)
    lo = (h - hi.astype(f32)).astype(bf16)
    rhi = rhi_ref[...]
    logits = (jnp.dot(hi, rhi, preferred_element_type=f32)
              + jnp.dot(hi, rlo_ref[...], preferred_element_type=f32)
              + jnp.dot(lo, rhi, preferred_element_type=f32))
    lane = lax.broadcasted_iota(jnp.int32, logits.shape, 1)
    neg = jnp.float32(-jnp.inf)
    logits = jnp.where(lane < n_experts, logits, neg)
    m1 = jnp.max(logits, axis=1, keepdims=True)
    i1 = jnp.min(jnp.where(logits == m1, lane, LANES), axis=1, keepdims=True)
    rest = jnp.where(lane == i1, neg, logits)
    m2 = jnp.max(rest, axis=1, keepdims=True)
    i2 = jnp.min(jnp.where(rest == m2, lane, LANES), axis=1, keepdims=True)
    e2 = jnp.exp(m2 - m1)
    w1 = 1.0 / (1.0 + e2)
    w2 = e2 / (1.0 + e2)
    comb = jnp.where(lane == i1, w1, 0.0) + jnp.where(lane == i2, w2, 0.0)
    c_ref[...] = comb[:, :n_experts]


def _router(x, g, router):
    n, d = x.shape
    e = router.shape[1]
    rpad = jnp.zeros((d, LANES), f32).at[:, :e].set(router)
    rhi = rpad.astype(bf16)
    rlo = (rpad - rhi.astype(f32)).astype(bf16)
    tr = _tile(n, 512)
    return pl.pallas_call(
        functools.partial(_router_kernel, n_experts=e),
        grid=(n // tr,),
        in_specs=[pl.BlockSpec((tr, d), lambda i: (i, 0)),
                  pl.BlockSpec((1, d), lambda i: (0, 0)),
                  pl.BlockSpec((d, LANES), lambda i: (0, 0)),
                  pl.BlockSpec((d, LANES), lambda i: (0, 0))],
        out_specs=[pl.BlockSpec((tr, d), lambda i: (i, 0)),
                   pl.BlockSpec((tr, e), lambda i: (i, 0))],
        out_shape=[jax.ShapeDtypeStruct((n, d), bf16), jax.ShapeDtypeStruct((n, e), f32)],
        compiler_params=_cparams(("parallel",)),
    )(x, g.reshape(1, d), rhi, rlo)


def kernel(x_prompt, x_sample, norm_mix, norm_ffn, norm_final, pool_w, pool_scale, hgrn_w_in, hgrn_lb,
           hgrn_norm, hgrn_w_out, ffn_w_gate, ffn_w_up, ffn_w_down, moe_router, moe_w_gate, moe_w_up,
           moe_w_down):
    bp, sp, d = x_prompt.shape
    bs, ss, _ = x_sample.shape
    n_prompt = bp * sp
    n = n_prompt + bs * ss
    seq = (n_prompt, sp, ss)
    depth = norm_mix.shape[0]
    chunk = _tile(min(sp, ss), 128)

    x = jnp.concatenate([x_prompt.reshape(n_prompt, d), x_sample.reshape(bs * ss, d)], axis=0)

    for i in range(depth):
        j = i // 2
        if i % 2 == 0:
            x, h = _pool_layer(x, norm_mix[i], pool_w[j].astype(bf16), pool_scale[j], norm_ffn[i], seq)
            act = _glu_up(h, ffn_w_gate[j].astype(bf16)[None], ffn_w_up[j].astype(bf16)[None])
            x = _down_res(act, ffn_w_down[j].astype(bf16), x, 4096)
        else:
            h = _rmsnorm(x, norm_mix[i], bf16)
            w_in = hgrn_w_in[j].astype(bf16)
            q = _proj_section(_proj_silu_heads_kernel, h, w_in, 0, True)
            forget = functools.partial(_proj_forget_kernel, layer=i)
            kf, gf = _proj_section(forget, h, w_in, 1, True, 2, hgrn_lb)
            kb, gb = _proj_section(forget, h, w_in, 2, True, 2, hgrn_lb)
            v = _proj_section(_proj_heads_kernel, h, w_in, 3, True)
            gate = _proj_section(_proj_silu_kernel, h, w_in, 4, False)
            s_bw = _scan_bw_states(kb, gb, v, seq, chunk)
            o = _scan(q, kf, kb, gf, gb, v, s_bw, seq, chunk)
            og = _norm_gate(o, gate, hgrn_norm[j])
            x = _down_res(og, hgrn_w_out[j].astype(bf16), x, 4096)
            h, combine = _router(x, norm_ffn[i], moe_router[j])
            act = _glu_up(h, moe_w_gate[j].astype(bf16), moe_w_up[j].astype(bf16))
            x = _moe_down(act, moe_w_down[j].astype(bf16), combine, x)

    y_prompt = _rmsnorm(x, norm_final, f32, 0, n_prompt).reshape(bp, sp, d)
    y_sample = _rmsnorm(x, norm_final, f32, n_prompt, bs * ss).reshape(bs, ss, d)
    return (y_prompt, y_sample)
```

```python
import functools

import jax
import jax.numpy as jnp
from jax import lax
from jax.experimental import pallas as pl
from jax.experimental.pallas import tpu as pltpu

EPS = 1e-6
POOL_WINDOWS = (2, 4, 8, 16)
POOL_HALO = 8
HEAD_DIM = 128
TOP_K = 2
V7X_VMEM_LIMIT = 56 * 1024 * 1024
LANES = 128

bf16 = jnp.bfloat16
f32 = jnp.float32


def _cparams(sem):
    return pltpu.CompilerParams(dimension_semantics=sem, vmem_limit_bytes=V7X_VMEM_LIMIT)


def _tile(n, pref):
    t = min(n, pref)
    while n % t:
        t //= 2
    return t


def _rms(x, g):
    ms = jnp.mean(x * x, axis=-1, keepdims=True)
    return x * lax.rsqrt(ms + EPS) * g


def _silu(x):
    return x * jax.nn.sigmoid(x)


def _seq_flags(tok, n_tok, n_prompt, seq_p, seq_s):
    in_p = tok < n_prompt
    rel = jnp.where(in_p, tok, tok - n_prompt)
    slen = jnp.where(in_p, seq_p, seq_s)
    is_start = (rel % slen) == 0
    is_end = ((rel + n_tok) % slen) == 0
    return is_start, is_end


def _rmsnorm_kernel(x_ref, g_ref, o_ref):
    o_ref[...] = _rms(x_ref[...], g_ref[...]).astype(o_ref.dtype)


def _rmsnorm(x, g, out_dtype, row_start=0, n_rows=None):
    n, d = x.shape
    n_rows = n if n_rows is None else n_rows
    tr = _tile(n_rows, 512)
    off = row_start // tr
    return pl.pallas_call(
        _rmsnorm_kernel,
        grid=(n_rows // tr,),
        in_specs=[pl.BlockSpec((tr, d), lambda i: (i + off, 0)),
                  pl.BlockSpec((1, d), lambda i: (0, 0))],
        out_specs=pl.BlockSpec((tr, d), lambda i: (i, 0)),
        out_shape=jax.ShapeDtypeStruct((n_rows, d), out_dtype),
        compiler_params=_cparams(("parallel",)),
    )(x, g.reshape(1, d))


def _pool_kernel(xp_ref, x_ref, xn_ref, gmix_ref, w_ref, scale_ref, gffn_ref,
                 x1_ref, h1_ref, hcat_ref, *, seq):
    t, d = x_ref.shape
    dg = d // len(POOL_WINDOWS)
    is_start, is_end = _seq_flags(pl.program_id(0) * t, t, *seq)
    g = gmix_ref[...]
    x = x_ref[...]
    h = _rms(x, g)
    hcat_ref[0:POOL_HALO, :] = jnp.where(is_start, 0.0, _rms(xp_ref[...], g))
    hcat_ref[POOL_HALO:POOL_HALO + t, :] = h
    hcat_ref[POOL_HALO + t:2 * POOL_HALO + t, :] = jnp.where(is_end, 0.0, _rms(xn_ref[...], g))
    r = lax.broadcasted_iota(jnp.int32, (t, 1), 0)
    for gi, w in enumerate(POOL_WINDOWS):
        c0 = gi * dg
        half = w // 2
        s = hcat_ref[POOL_HALO - half:POOL_HALO - half + t, c0:c0 + dg]
        for j in range(-half + 1, half):
            s = s + hcat_ref[POOL_HALO + j:POOL_HALO + j + t, c0:c0 + dg]
        lo = jnp.where(is_start, jnp.maximum(r - half, 0), r - half)
        hi = jnp.where(is_end, jnp.minimum(r + half, t), r + half)
        cnt = (hi - lo).astype(f32)
        dlt = s / cnt - h[:, c0:c0 + dg]
        y = jnp.dot(dlt.astype(bf16), w_ref[gi], preferred_element_type=f32)
        x1_ref[:, c0:c0 + dg] = x[:, c0:c0 + dg] + y * scale_ref[:, c0:c0 + dg]
    h1_ref[...] = _rms(x1_ref[...], gffn_ref[...]).astype(h1_ref.dtype)


def _pool_layer(x, g_mix, w_grp, scale, g_ffn, seq):
    n, d = x.shape
    t = _tile(min(seq[1], seq[2]), 256)
    hb = t // POOL_HALO
    nhb = n // POOL_HALO
    row = lambda i: (0, 0)
    return pl.pallas_call(
        functools.partial(_pool_kernel, seq=seq),
        grid=(n // t,),
        in_specs=[
            pl.BlockSpec((POOL_HALO, d), lambda i: (jnp.maximum(i * hb - 1, 0), 0)),
            pl.BlockSpec((t, d), lambda i: (i, 0)),
            pl.BlockSpec((POOL_HALO, d), lambda i: (jnp.minimum((i + 1) * hb, nhb - 1), 0)),
            pl.BlockSpec((1, d), row),
            pl.BlockSpec(w_grp.shape, lambda i: (0, 0, 0)),
            pl.BlockSpec((1, d), row),
            pl.BlockSpec((1, d), row),
        ],
        out_specs=[pl.BlockSpec((t, d), lambda i: (i, 0)),
                   pl.BlockSpec((t, d), lambda i: (i, 0))],
        out_shape=[jax.ShapeDtypeStruct((n, d), f32), jax.ShapeDtypeStruct((n, d), bf16)],
        scratch_shapes=[pltpu.VMEM((t + 2 * POOL_HALO, d), f32)],
        compiler_params=_cparams(("parallel",)),
    )(x, x, x, g_mix.reshape(1, d), w_grp, scale.reshape(1, d), g_ffn.reshape(1, d))


def _glu_up_kernel(a_ref, wg_ref, wu_ref, o_ref):
    a = a_ref[...]
    gate = jnp.dot(a, wg_ref[...], preferred_element_type=f32)
    up = jnp.dot(a, wu_ref[...], preferred_element_type=f32)
    o_ref[...] = (_silu(gate) * up).astype(o_ref.dtype)


def _glu_up(a, wg, wu):
    m, k = a.shape
    e, _, f = wg.shape
    tm = _tile(m, 1024)
    tn = _tile(f, 512)
    nb = f // tn
    wspec = pl.BlockSpec((None, k, tn), lambda i, j: (j // nb, 0, j % nb))
    return pl.pallas_call(
        _glu_up_kernel,
        grid=(m // tm, e * nb),
        in_specs=[pl.BlockSpec((tm, k), lambda i, j: (i, 0)), wspec, wspec],
        out_specs=pl.BlockSpec((tm, tn), lambda i, j: (i, j)),
        out_shape=jax.ShapeDtypeStruct((m, e * f), bf16),
        compiler_params=_cparams(("parallel", "arbitrary")),
    )(a, wg, wu)


def _down_kernel(a_ref, w_ref, res_ref, o_ref, acc_ref):
    k = pl.program_id(2)

    @pl.when(k == 0)
    def _():
        acc_ref[...] = jnp.zeros_like(acc_ref)

    acc_ref[...] += jnp.dot(a_ref[...], w_ref[...], preferred_element_type=f32)

    @pl.when(k == pl.num_programs(2) - 1)
    def _():
        o_ref[...] = res_ref[...] + acc_ref[...]


def _down_res(a, w, res, tk):
    m, kk = a.shape
    n = w.shape[1]
    tm = _tile(m, 1024)
    tn = _tile(n, 512)
    tk = _tile(kk, tk)
    return pl.pallas_call(
        _down_kernel,
        grid=(m // tm, n // tn, kk // tk),
        in_specs=[pl.BlockSpec((tm, tk), lambda i, j, k: (i, k)),
                  pl.BlockSpec((tk, tn), lambda i, j, k: (k, j)),
                  pl.BlockSpec((tm, tn), lambda i, j, k: (i, j))],
        out_specs=pl.BlockSpec((tm, tn), lambda i, j, k: (i, j)),
        out_shape=jax.ShapeDtypeStruct((m, n), f32),
        scratch_shapes=[pltpu.VMEM((tm, tn), f32)],
        compiler_params=_cparams(("parallel", "parallel", "arbitrary")),
    )(a, w, res)


def _moe_down_kernel(a_ref, w_ref, c_ref, res_ref, o_ref, acc_ref):
    e = pl.program_id(2)

    @pl.when(e == 0)
    def _():
        acc_ref[...] = jnp.zeros_like(acc_ref)

    c = c_ref[...]
    lane = lax.broadcasted_iota(jnp.int32, c.shape, 1)
    ce = jnp.sum(jnp.where(lane == e, c, 0.0), axis=1, keepdims=True)
    acc_ref[...] += ce * jnp.dot(a_ref[...], w_ref[...], preferred_element_type=f32)

    @pl.when(e == pl.num_programs(2) - 1)
    def _():
        o_ref[...] = res_ref[...] + acc_ref[...]


def _moe_down(a, wd, combine, res):
    m = a.shape[0]
    e, fe, n = wd.shape
    tm = _tile(m, 1024)
    tn = _tile(n, 1024)
    return pl.pallas_call(
        _moe_down_kernel,
        grid=(m // tm, n // tn, e),
        in_specs=[pl.BlockSpec((tm, fe), lambda i, j, k: (i, k)),
                  pl.BlockSpec((None, fe, tn), lambda i, j, k: (k, 0, j)),
                  pl.BlockSpec((tm, e), lambda i, j, k: (i, 0)),
                  pl.BlockSpec((tm, tn), lambda i, j, k: (i, j))],
        out_specs=pl.BlockSpec((tm, tn), lambda i, j, k: (i, j)),
        out_shape=jax.ShapeDtypeStruct((m, n), f32),
        scratch_shapes=[pltpu.VMEM((tm, tn), f32)],
        compiler_params=_cparams(("parallel", "parallel", "arbitrary")),
    )(a, wd, combine, res)


def _store_heads(o_ref, val):
    for hh in range(o_ref.shape[0]):
        o_ref[hh] = val[:, hh * HEAD_DIM:(hh + 1) * HEAD_DIM].astype(o_ref.dtype)


def _proj_silu_heads_kernel(a_ref, w_ref, o_ref):
    _store_heads(o_ref, _silu(jnp.dot(a_ref[...], w_ref[...], preferred_element_type=f32)))


def _proj_heads_kernel(a_ref, w_ref, o_ref):
    _store_heads(o_ref, jnp.dot(a_ref[...], w_ref[...], preferred_element_type=f32))


def _proj_silu_kernel(a_ref, w_ref, o_ref):
    o_ref[...] = _silu(jnp.dot(a_ref[...], w_ref[...], preferred_element_type=f32)).astype(o_ref.dtype)


def _proj_forget_kernel(a_ref, w_ref, lb_ref, k_ref, g_ref, *, layer):
    logits = lb_ref[...]
    ex = jnp.exp(logits - jnp.max(logits, axis=0, keepdims=True))
    sm = ex / jnp.sum(ex, axis=0, keepdims=True)
    lb = jnp.sum(sm[0:layer + 1], axis=0, keepdims=True) - sm[0:1]
    f = lb + (1.0 - lb) * jax.nn.sigmoid(jnp.dot(a_ref[...], w_ref[...], preferred_element_type=f32))
    _store_heads(k_ref, 1.0 - f)
    _store_heads(g_ref, jnp.log2(f))


def _proj_section(kernel, a, w, section, out_heads, n_out=1, extra=None):
    m, k = a.shape
    d = k
    tm = _tile(m, 1024)
    tn = _tile(d, 1024)
    nb = d // tn
    in_specs = [pl.BlockSpec((tm, k), lambda i, j: (i, 0)),
                pl.BlockSpec((k, tn), lambda i, j: (0, section * nb + j))]
    args = [a, w]
    if extra is not None:
        in_specs.append(pl.BlockSpec((extra.shape[0], tn), lambda i, j: (0, j)))
        args.append(extra)
    if out_heads:
        hpb = tn // HEAD_DIM
        ospec = pl.BlockSpec((hpb, tm, HEAD_DIM), lambda i, j: (j, i, 0))
        oshape = jax.ShapeDtypeStruct((d // HEAD_DIM, m, HEAD_DIM), bf16)
    else:
        ospec = pl.BlockSpec((tm, tn), lambda i, j: (i, j))
        oshape = jax.ShapeDtypeStruct((m, d), bf16)
    out = pl.pallas_call(
        kernel,
        grid=(m // tm, nb),
        in_specs=in_specs,
        out_specs=[ospec] * n_out if n_out > 1 else ospec,
        out_shape=[oshape] * n_out if n_out > 1 else oshape,
        compiler_params=_cparams(("parallel", "arbitrary")),
    )(*args)
    return out


MAX_DIRECT_LOG2_DECAY = 100.0
HEAD_UNROLL = 8


def _dot_nt(a, b):
    return lax.dot_general(a, b, (((1,), (1,)), ((), ())), preferred_element_type=f32)


def _dot_tn(a, b):
    return lax.dot_general(a, b, (((0,), (0,)), ((), ())), preferred_element_type=f32)


def _mid_ref(p, level, offset):
    c, w = p.shape
    blk = 2 << level
    p3 = p.reshape(c // blk, blk, w)
    return jnp.broadcast_to(p3[:, offset:offset + 1, :], p3.shape).reshape(c, w)


def _neg_abs_exponents(gf, gb, b, x, level, rmod4):
    if level == 0:
        odd = (rmod4 & 1) == 1
        return jnp.where(odd, gf, 0.0), jnp.where(odd, 0.0, gb)
    if level == 1:
        c = gf.shape[0]
        gf_up = pltpu.roll(gf, c - 1, 0)
        gf_dn = pltpu.roll(gf, 1, 0)
        gb_up = pltpu.roll(gb, c - 1, 0)
        gb_dn = pltpu.roll(gb, 1, 0)
        ef = jnp.where(rmod4 == 0, gf_up, jnp.where(rmod4 == 1, 0.0, jnp.where(rmod4 == 2, gf, gf + gf_dn)))
        eb = jnp.where(rmod4 == 0, gb + gb_up, jnp.where(rmod4 == 1, gb, jnp.where(rmod4 == 2, 0.0, gb_dn)))
        return ef, eb
    half = 1 << level
    ef = b - _mid_ref(b, level, half - 1)
    eb = x - _mid_ref(x, level, half)
    return -jnp.abs(ef), -jnp.abs(eb)


def _chunk_prefix(tri_incl, tri_excl, gf16, gb16):
    b = jnp.dot(tri_incl, gf16, preferred_element_type=f32)
    x = jnp.dot(tri_excl, gb16, preferred_element_type=f32)
    return b, x


def _scan_bw_kernel(kb_ref, gb_ref, v_ref, s_ref, st_ref, *, seq, n_chunks):
    n_heads, c, _ = kb_ref.shape
    chunk = n_chunks - 1 - pl.program_id(0)
    _, is_end = _seq_flags(chunk * c, c, *seq)

    @pl.when(is_end)
    def _():
        st_ref[...] = jnp.zeros_like(st_ref)

    row = lax.broadcasted_iota(jnp.int32, (c, c), 0)
    col = lax.broadcasted_iota(jnp.int32, (c, c), 1)
    tri_excl = (col < row).astype(bf16)

    def head(h, carry):
        gb16 = gb_ref[h]
        x = jnp.dot(tri_excl, gb16, preferred_element_type=f32)
        tot = x[c - 1:c, :] + gb16[c - 1:c, :].astype(f32)
        st = st_ref[h]
        s_ref[h] = st.astype(s_ref.dtype)
        kx = (kb_ref[h].astype(f32) * jnp.exp2(x)).astype(bf16)
        st_ref[h] = st * jnp.exp2(tot) + _dot_tn(v_ref[h], kx)
        return carry

    lax.fori_loop(0, n_heads, head, 0, unroll=HEAD_UNROLL)


def _scan_bw_states(kb, gb, v, seq, c):
    n_heads, n, dk = kb.shape
    n_chunks = n // c
    spec = pl.BlockSpec((n_heads, c, dk), lambda i: (0, n_chunks - 1 - i, 0))
    return pl.pallas_call(
        functools.partial(_scan_bw_kernel, seq=seq, n_chunks=n_chunks),
        grid=(n_chunks,),
        in_specs=[spec, spec, spec],
        out_specs=pl.BlockSpec((None, n_heads, dk, dk), lambda i: (n_chunks - 1 - i, 0, 0, 0)),
        out_shape=jax.ShapeDtypeStruct((n_chunks, n_heads, dk, dk), bf16),
        scratch_shapes=[pltpu.VMEM((n_heads, dk, dk), f32)],
        compiler_params=_cparams(("arbitrary",)),
    )(kb, gb, v)


def _scan_kernel(q_ref, kf_ref, kb_ref, gf_ref, gb_ref, v_ref, sb_ref, tril_ref, triu_ref,
                 o_ref, st_ref, qf_s, kf_s, qb_s, kb_s, qin_s, kx_s, sc_s, dec_s, tot_s, *, seq):
    n_heads, c, dk = q_ref.shape
    n_levels = c.bit_length() - 1
    mid = c // 2
    is_start, _ = _seq_flags(pl.program_id(0) * c, c, *seq)

    @pl.when(is_start)
    def _():
        st_ref[...] = jnp.zeros_like(st_ref)

    row = lax.broadcasted_iota(jnp.int32, (c, c), 0)
    col = lax.broadcasted_iota(jnp.int32, (c, c), 1)
    tri_incl = (col <= row).astype(bf16)
    tri_excl = (col < row).astype(bf16)

    def finish(h, kf_dec, b_last, q_in, scores, diag):
        v = v_ref[h]
        st_f = st_ref[h]
        st_cat = jnp.concatenate([st_f.astype(bf16), sb_ref[h]], axis=1)
        o = _dot_nt(q_in, st_cat) + jnp.dot(scores.astype(bf16), v, preferred_element_type=f32)
        o_ref[h] = o if diag is None else o + diag * v.astype(f32)
        st_ref[h] = st_f * jnp.exp2(b_last) + _dot_tn(v, kf_dec.astype(bf16))

    def direct_operands(h, carry):
        q = q_ref[h].astype(f32)
        gb16 = gb_ref[h]
        b, x = _chunk_prefix(tri_incl, tri_excl, gf_ref[h], gb16)
        b_mid = b[mid - 1:mid, :]
        x_mid = x[mid:mid + 1, :]
        db = b - b_mid
        dx = x - x_mid
        q_f = q * jnp.exp2(db)
        k_f = kf_ref[h].astype(f32) * jnp.exp2(-db)
        q_b = q * jnp.exp2(-dx)
        k_b = kb_ref[h].astype(f32) * jnp.exp2(dx)
        b_last = b[c - 1:c, :]
        x_tot = x[c - 1:c, :] + gb16[c - 1:c, :].astype(f32)
        qf_s[h] = q_f.astype(bf16)
        kf_s[h] = k_f.astype(bf16)
        qb_s[h] = q_b.astype(bf16)
        kb_s[h] = k_b.astype(bf16)
        qin_s[h] = jnp.concatenate([q_f * jnp.exp2(b_mid), q_b * jnp.exp2(x_tot - x_mid)], axis=1).astype(bf16)
        kx_s[h] = (k_f * jnp.exp2(b_last - b_mid)).astype(bf16)
        dec_s[h] = jnp.broadcast_to(jnp.exp2(b_last), dec_s.shape[1:])
        tot_s[h] = jnp.broadcast_to(jnp.minimum(b_last, x_tot), tot_s.shape[1:])
        return carry

    def direct_scores(h, carry):
        sc_s[h] = (_dot_nt(qf_s[h], kf_s[h]) * tril_ref[...]
                   + _dot_nt(qb_s[h], kb_s[h]) * triu_ref[...]).astype(bf16)
        return carry

    def direct_output(h, carry):
        v = v_ref[h]
        st_f = st_ref[h]
        st_cat = jnp.concatenate([st_f.astype(bf16), sb_ref[h]], axis=1)
        o_ref[h] = _dot_nt(qin_s[h], st_cat) + jnp.dot(sc_s[h], v, preferred_element_type=f32)
        st_ref[h] = st_f * dec_s[h][0:1, :] + _dot_tn(v, kx_s[h])
        return carry

    def head_levelled(h, carry):
        rtok = lax.broadcasted_iota(jnp.int32, (c, dk), 0)
        rmod4 = rtok & 3
        q = q_ref[h].astype(f32)
        kf = kf_ref[h].astype(f32)
        kb = kb_ref[h].astype(f32)
        gf16 = gf_ref[h]
        gb16 = gb_ref[h]
        gf = gf16.astype(f32)
        gb = gb16.astype(f32)
        b, x = _chunk_prefix(tri_incl, tri_excl, gf16, gb16)
        scores = jnp.zeros((c, c), f32)
        for l in range(n_levels):
            ef, eb = _neg_abs_exponents(gf, gb, b, x, l, rmod4)
            af = jnp.exp2(ef)
            ab = jnp.exp2(eb)
            up = ((rtok >> l) & 1) == 1
            lhs = jnp.concatenate([jnp.where(up, q * af, 0.0), jnp.where(up, 0.0, q * ab)], axis=1)
            rhs = jnp.concatenate([jnp.where(up, 0.0, kf * af), jnp.where(up, kb * ab, 0.0)], axis=1)
            p = _dot_nt(lhs.astype(bf16), rhs.astype(bf16))
            if l < n_levels - 1:
                p = jnp.where((row >> (l + 1)) == (col >> (l + 1)), p, 0.0)
            scores = scores + p
        b_last = b[c - 1:c, :]
        x_tot = x[c - 1:c, :] + gb[c - 1:c, :]
        q_in = jnp.concatenate([q * jnp.exp2(b), q * jnp.exp2(x_tot - x)], axis=1).astype(bf16)
        diag = jnp.sum(q * (kf + kb), axis=1, keepdims=True)
        finish(h, kf * jnp.exp2(b_last - b), b_last, q_in, scores, diag)
        return carry

    lax.fori_loop(0, n_heads, direct_operands, 0, unroll=HEAD_UNROLL)
    direct_ok = jnp.min(tot_s[...]) >= -MAX_DIRECT_LOG2_DECAY

    @pl.when(direct_ok)
    def _():
        lax.fori_loop(0, n_heads, direct_scores, 0, unroll=HEAD_UNROLL)
        lax.fori_loop(0, n_heads, direct_output, 0, unroll=HEAD_UNROLL)

    @pl.when(jnp.logical_not(direct_ok))
    def _():
        lax.fori_loop(0, n_heads, head_levelled, 0)


def _scan(q, kf, kb, gf, gb, v, s_bw, seq, c):
    n_heads, n, dk = q.shape
    n_chunks = n // c
    spec = pl.BlockSpec((n_heads, c, dk), lambda i: (0, i, 0))
    tri = pl.BlockSpec((c, c), lambda i: (0, 0))
    tril = jnp.tril(jnp.ones((c, c), f32))
    return pl.pallas_call(
        functools.partial(_scan_kernel, seq=seq),
        grid=(n_chunks,),
        in_specs=[spec] * 6 + [pl.BlockSpec((None, n_heads, dk, dk), lambda i: (i, 0, 0, 0)), tri, tri],
        out_specs=spec,
        out_shape=jax.ShapeDtypeStruct((n_heads, n, dk), f32),
        scratch_shapes=[pltpu.VMEM((n_heads, dk, dk), f32)]
        + [pltpu.VMEM((n_heads, c, dk), bf16)] * 4
        + [pltpu.VMEM((n_heads, c, 2 * dk), bf16), pltpu.VMEM((n_heads, c, dk), bf16),
           pltpu.VMEM((n_heads, c, c), bf16)] + [pltpu.VMEM((n_heads, 8, dk), f32)] * 2,
        compiler_params=_cparams(("arbitrary",)),
    )(q, kf, kb, gf, gb, v, s_bw, tril, tril.T)


def _norm_gate_kernel(o_ref, gate_ref, gain_ref, out_ref, row_ref):
    for hh in range(o_ref.shape[0]):
        row_ref[:, hh * HEAD_DIM:(hh + 1) * HEAD_DIM] = o_ref[hh]
    out_ref[...] = (_rms(row_ref[...], gain_ref[...]) * gate_ref[...].astype(f32)).astype(out_ref.dtype)


def _norm_gate(o, gate, gain):
    n_heads, n, dk = o.shape
    d = n_heads * dk
    tr = _tile(n, 256)
    return pl.pallas_call(
        _norm_gate_kernel,
        grid=(n // tr,),
        in_specs=[pl.BlockSpec((n_heads, tr, dk), lambda i: (0, i, 0)),
                  pl.BlockSpec((tr, d), lambda i: (i, 0)),
                  pl.BlockSpec((1, d), lambda i: (0, 0))],
        out_specs=pl.BlockSpec((tr, d), lambda i: (i, 0)),
        out_shape=jax.ShapeDtypeStruct((n, d), bf16),
        scratch_shapes=[pltpu.VMEM((tr, d), f32)],
        compiler_params=_cparams(("parallel",)),
    )(o, gate, gain.reshape(1, d))


def _router_kernel(x_ref, g_ref, rhi_ref, rlo_ref, h_ref, c_ref, *, n_experts):
    h = _rms(x_ref[...], g_ref[...])
    h_ref[...] = h.astype(h_ref.dtype)
    hi = h.astype(bf16)
    lo = (h - hi.astype(f32)).astype(bf16)
    rhi = rhi_ref[...]
    logits = (jnp.dot(hi, rhi, preferred_element_type=f32)
              + jnp.dot(hi, rlo_ref[...], preferred_element_type=f32)
              + jnp.dot(lo, rhi, preferred_element_type=f32))
    lane = lax.broadcasted_iota(jnp.int32, logits.shape, 1)
    neg = jnp.float32(-jnp.inf)
    logits = jnp.where(lane < n_experts, logits, neg)
    m1 = jnp.max(logits, axis=1, keepdims=True)
    i1 = jnp.min(jnp.where(logits == m1, lane, LANES), axis=1, keepdims=True)
    rest = jnp.where(lane == i1, neg, logits)
    m2 = jnp.max(rest, axis=1, keepdims=True)
    i2 = jnp.min(jnp.where(rest == m2, lane, LANES), axis=1, keepdims=True)
    e2 = jnp.exp(m2 - m1)
    w1 = 1.0 / (1.0 + e2)
    w2 = e2 / (1.0 + e2)
    comb = jnp.where(lane == i1, w1, 0.0) + jnp.where(lane == i2, w2, 0.0)
    c_ref[...] = comb[:, :n_experts]


def _router(x, g, router):
    n, d = x.shape
    e = router.shape[1]
    rpad = jnp.zeros((d, LANES), f32).at[:, :e].set(router)
    rhi = rpad.astype(bf16)
    rlo = (rpad - rhi.astype(f32)).astype(bf16)
    tr = _tile(n, 512)
    return pl.pallas_call(
        functools.partial(_router_kernel, n_experts=e),
        grid=(n // tr,),
        in_specs=[pl.BlockSpec((tr, d), lambda i: (i, 0)),
                  pl.BlockSpec((1, d), lambda i: (0, 0)),
                  pl.BlockSpec((d, LANES), lambda i: (0, 0)),
                  pl.BlockSpec((d, LANES), lambda i: (0, 0))],
        out_specs=[pl.BlockSpec((tr, d), lambda i: (i, 0)),
                   pl.BlockSpec((tr, e), lambda i: (i, 0))],
        out_shape=[jax.ShapeDtypeStruct((n, d), bf16), jax.ShapeDtypeStruct((n, e), f32)],
        compiler_params=_cparams(("parallel",)),
    )(x, g.reshape(1, d), rhi, rlo)


def kernel(x_prompt, x_sample, norm_mix, norm_ffn, norm_final, pool_w, pool_scale, hgrn_w_in, hgrn_lb,
           hgrn_norm, hgrn_w_out, ffn_w_gate, ffn_w_up, ffn_w_down, moe_router, moe_w_gate, moe_w_up,
           moe_w_down):
    bp, sp, d = x_prompt.shape
    bs, ss, _ = x_sample.shape
    n_prompt = bp * sp
    n = n_prompt + bs * ss
    seq = (n_prompt, sp, ss)
    depth = norm_mix.shape[0]
    chunk = _tile(min(sp, ss), 128)

    x = jnp.concatenate([x_prompt.reshape(n_prompt, d), x_sample.reshape(bs * ss, d)], axis=0)

    for i in range(depth):
        j = i // 2
        if i % 2 == 0:
            x, h = _pool_layer(x, norm_mix[i], pool_w[j].astype(bf16), pool_scale[j], norm_ffn[i], seq)
            act = _glu_up(h, ffn_w_gate[j].astype(bf16)[None], ffn_w_up[j].astype(bf16)[None])
            x = _down_res(act, ffn_w_down[j].astype(bf16), x, 4096)
        else:
            h = _rmsnorm(x, norm_mix[i], bf16)
            w_in = hgrn_w_in[j].astype(bf16)
            q = _proj_section(_proj_silu_heads_kernel, h, w_in, 0, True)
            forget = functools.partial(_proj_forget_kernel, layer=i)
            kf, gf = _proj_section(forget, h, w_in, 1, True, 2, hgrn_lb)
            kb, gb = _proj_section(forget, h, w_in, 2, True, 2, hgrn_lb)
            v = _proj_section(_proj_heads_kernel, h, w_in, 3, True)
            gate = _proj_section(_proj_silu_kernel, h, w_in, 4, False)
            s_bw = _scan_bw_states(kb, gb, v, seq, chunk)
            o = _scan(q, kf, kb, gf, gb, v, s_bw, seq, chunk)
            og = _norm_gate(o, gate, hgrn_norm[j])
            x = _down_res(og, hgrn_w_out[j].astype(bf16), x, 4096)
            h, combine = _router(x, norm_ffn[i], moe_router[j])
            act = _glu_up(h, moe_w_gate[j].astype(bf16), moe_w_up[j].astype(bf16))
            x = _moe_down(act, moe_w_down[j].astype(bf16), combine, x)

    y_prompt = _rmsnorm(x, norm_final, f32, 0, n_prompt).reshape(bp, sp, d)
    y_sample = _rmsnorm(x, norm_final, f32, n_prompt, bs * ss).reshape(bs, ss, d)
    return (y_prompt, y_sample)
```

```python
import functools

import jax
import jax.numpy as jnp
from jax import lax
from jax.experimental import pallas as pl
from jax.experimental.pallas import tpu as pltpu

EPS = 1e-6
POOL_WINDOWS = (2, 4, 8, 16)
POOL_HALO = 8
HEAD_DIM = 128
TOP_K = 2
V7X_VMEM_LIMIT = 56 * 1024 * 1024
LANES = 128
MOE_ROW_TILE = 512
MOE_FF_TILE = 256
MOE_TOKEN_TILE = 256
ROW_DMA_UNROLL = 8

bf16 = jnp.bfloat16
f32 = jnp.float32


def _cparams(sem):
    return pltpu.CompilerParams(dimension_semantics=sem, vmem_limit_bytes=V7X_VMEM_LIMIT)


def _tile(n, pref):
    t = min(n, pref)
    while n % t:
        t //= 2
    return t


def _rms(x, g):
    ms = jnp.mean(x * x, axis=-1, keepdims=True)
    return x * lax.rsqrt(ms + EPS) * g


def _silu(x):
    return x * jax.nn.sigmoid(x)


def _seq_flags(tok, n_tok, n_prompt, seq_p, seq_s):
    in_p = tok < n_prompt
    rel = jnp.where(in_p, tok, tok - n_prompt)
    slen = jnp.where(in_p, seq_p, seq_s)
    is_start = (rel % slen) == 0
    is_end = ((rel + n_tok) % slen) == 0
    return is_start, is_end


def _rmsnorm_kernel(x_ref, g_ref, o_ref):
    o_ref[...] = _rms(x_ref[...], g_ref[...]).astype(o_ref.dtype)


def _rmsnorm(x, g, out_dtype, row_start=0, n_rows=None):
    n, d = x.shape
    n_rows = n if n_rows is None else n_rows
    tr = _tile(n_rows, 512)
    off = row_start // tr
    return pl.pallas_call(
        _rmsnorm_kernel,
        grid=(n_rows // tr,),
        in_specs=[pl.BlockSpec((tr, d), lambda i: (i + off, 0)),
                  pl.BlockSpec((1, d), lambda i: (0, 0))],
        out_specs=pl.BlockSpec((tr, d), lambda i: (i, 0)),
        out_shape=jax.ShapeDtypeStruct((n_rows, d), out_dtype),
        compiler_params=_cparams(("parallel",)),
    )(x, g.reshape(1, d))


def _pool_kernel(ap_ref, a_ref, an_ref, bp_ref, b_ref, bn_ref, gmix_ref, w_ref, scale_ref, gffn_ref,
                 x1_ref, h1_ref, hcat_ref, *, seq):
    t, d = a_ref.shape
    dg = d // len(POOL_WINDOWS)
    tok = pl.program_id(0) * t
    is_start, is_end = _seq_flags(tok, t, *seq)
    in_a = tok < seq[0]
    g = gmix_ref[...]
    x = jnp.where(in_a, a_ref[...], b_ref[...])
    h = _rms(x, g)
    hcat_ref[0:POOL_HALO, :] = jnp.where(is_start, 0.0, _rms(jnp.where(in_a, ap_ref[...], bp_ref[...]), g))
    hcat_ref[POOL_HALO:POOL_HALO + t, :] = h
    hcat_ref[POOL_HALO + t:2 * POOL_HALO + t, :] = jnp.where(
        is_end, 0.0, _rms(jnp.where(in_a, an_ref[...], bn_ref[...]), g))
    r = lax.broadcasted_iota(jnp.int32, (t, 1), 0)
    for gi, w in enumerate(POOL_WINDOWS):
        c0 = gi * dg
        half = w // 2
        s = hcat_ref[POOL_HALO - half:POOL_HALO - half + t, c0:c0 + dg]
        for j in range(-half + 1, half):
            s = s + hcat_ref[POOL_HALO + j:POOL_HALO + j + t, c0:c0 + dg]
        lo = jnp.where(is_start, jnp.maximum(r - half, 0), r - half)
        hi = jnp.where(is_end, jnp.minimum(r + half, t), r + half)
        cnt = (hi - lo).astype(f32)
        dlt = s / cnt - h[:, c0:c0 + dg]
        y = jnp.dot(dlt.astype(bf16), w_ref[gi], preferred_element_type=f32)
        x1_ref[:, c0:c0 + dg] = x[:, c0:c0 + dg] + y * scale_ref[:, c0:c0 + dg]
    h1_ref[...] = _rms(x1_ref[...], gffn_ref[...]).astype(h1_ref.dtype)


def _stream_specs(rows, first_tile, t, d):
    hb = t // POOL_HALO
    last_halo = rows // POOL_HALO - 1
    last_tile = rows // t - 1
    return [
        pl.BlockSpec((POOL_HALO, d), lambda i: (jnp.clip((i - first_tile) * hb - 1, 0, last_halo), 0)),
        pl.BlockSpec((t, d), lambda i: (jnp.clip(i - first_tile, 0, last_tile), 0)),
        pl.BlockSpec((POOL_HALO, d), lambda i: (jnp.clip((i - first_tile + 1) * hb, 0, last_halo), 0)),
    ]


def _pool_layer(xa, xb, g_mix, w_grp, scale, g_ffn, seq):
    d = xa.shape[1]
    n = xa.shape[0] + xb.shape[0]
    t = _tile(min(seq[1], seq[2]), 256)
    row = lambda i: (0, 0)
    return pl.pallas_call(
        functools.partial(_pool_kernel, seq=seq),
        grid=(n // t,),
        in_specs=_stream_specs(xa.shape[0], 0, t, d) + _stream_specs(xb.shape[0], xa.shape[0] // t, t, d) + [
            pl.BlockSpec((1, d), row),
            pl.BlockSpec(w_grp.shape, lambda i: (0, 0, 0)),
            pl.BlockSpec((1, d), row),
            pl.BlockSpec((1, d), row),
        ],
        out_specs=[pl.BlockSpec((t, d), lambda i: (i, 0)),
                   pl.BlockSpec((t, d), lambda i: (i, 0))],
        out_shape=[jax.ShapeDtypeStruct((n, d), f32), jax.ShapeDtypeStruct((n, d), bf16)],
        scratch_shapes=[pltpu.VMEM((t + 2 * POOL_HALO, d), f32)],
        compiler_params=_cparams(("parallel",)),
    )(xa, xa, xa, xb, xb, xb, g_mix.reshape(1, d), w_grp, scale.reshape(1, d), g_ffn.reshape(1, d))


def _glu_up_kernel(a_ref, wg_ref, wu_ref, o_ref):
    a = a_ref[...]
    gate = jnp.dot(a, wg_ref[...], preferred_element_type=f32)
    up = jnp.dot(a, wu_ref[...], preferred_element_type=f32)
    o_ref[...] = (_silu(gate) * up).astype(o_ref.dtype)


def _glu_up(a, wg, wu):
    m, k = a.shape
    e, _, f = wg.shape
    tm = _tile(m, 1024)
    tn = _tile(f, 512)
    nb = f // tn
    wspec = pl.BlockSpec((None, k, tn), lambda i, j: (j // nb, 0, j % nb))
    return pl.pallas_call(
        _glu_up_kernel,
        grid=(m // tm, e * nb),
        in_specs=[pl.BlockSpec((tm, k), lambda i, j: (i, 0)), wspec, wspec],
        out_specs=pl.BlockSpec((tm, tn), lambda i, j: (i, j)),
        out_shape=jax.ShapeDtypeStruct((m, e * f), bf16),
        compiler_params=_cparams(("parallel", "arbitrary")),
    )(a, wg, wu)


def _down_kernel(a_ref, w_ref, res_ref, o_ref, acc_ref):
    k = pl.program_id(2)

    @pl.when(k == 0)
    def _():
        acc_ref[...] = jnp.zeros_like(acc_ref)

    acc_ref[...] += jnp.dot(a_ref[...], w_ref[...], preferred_element_type=f32)

    @pl.when(k == pl.num_programs(2) - 1)
    def _():
        o_ref[...] = res_ref[...] + acc_ref[...]


def _down1_kernel(a_ref, w_ref, res_ref, o_ref):
    o_ref[...] = res_ref[...] + jnp.dot(a_ref[...], w_ref[...], preferred_element_type=f32)


def _down_res(a, w, res, tk):
    m, kk = a.shape
    n = w.shape[1]
    tm = _tile(m, 1024)
    tn = _tile(n, 512)
    tk = _tile(kk, tk)
    if tk == kk:
        return pl.pallas_call(
            _down1_kernel,
            grid=(m // tm, n // tn),
            in_specs=[pl.BlockSpec((tm, kk), lambda i, j: (i, 0)),
                      pl.BlockSpec((kk, tn), lambda i, j: (0, j)),
                      pl.BlockSpec((tm, tn), lambda i, j: (i, j))],
            out_specs=pl.BlockSpec((tm, tn), lambda i, j: (i, j)),
            out_shape=jax.ShapeDtypeStruct((m, n), f32),
            compiler_params=_cparams(("parallel", "arbitrary")),
        )(a, w, res)
    return pl.pallas_call(
        _down_kernel,
        grid=(m // tm, n // tn, kk // tk),
        in_specs=[pl.BlockSpec((tm, tk), lambda i, j, k: (i, k)),
                  pl.BlockSpec((tk, tn), lambda i, j, k: (k, j)),
                  pl.BlockSpec((tm, tn), lambda i, j, k: (i, j))],
        out_specs=pl.BlockSpec((tm, tn), lambda i, j, k: (i, j)),
        out_shape=jax.ShapeDtypeStruct((m, n), f32),
        scratch_shapes=[pltpu.VMEM((tm, tn), f32)],
        compiler_params=_cparams(("parallel", "parallel", "arbitrary")),
    )(a, w, res)


def _store_heads(o_ref, val):
    for hh in range(o_ref.shape[0]):
        o_ref[hh] = val[:, hh * HEAD_DIM:(hh + 1) * HEAD_DIM].astype(o_ref.dtype)


def _proj_silu_heads_kernel(a_ref, w_ref, o_ref):
    _store_heads(o_ref, _silu(jnp.dot(a_ref[...], w_ref[...], preferred_element_type=f32)))


def _proj_heads_kernel(a_ref, w_ref, o_ref):
    _store_heads(o_ref, jnp.dot(a_ref[...], w_ref[...], preferred_element_type=f32))


def _proj_silu_kernel(a_ref, w_ref, o_ref):
    o_ref[...] = _silu(jnp.dot(a_ref[...], w_ref[...], preferred_element_type=f32)).astype(o_ref.dtype)


def _proj_forget_kernel(a_ref, w_ref, lb_ref, k_ref, g_ref, *, layer):
    logits = lb_ref[...]
    ex = jnp.exp(logits - jnp.max(logits, axis=0, keepdims=True))
    sm = ex / jnp.sum(ex, axis=0, keepdims=True)
    lb = jnp.sum(sm[0:layer + 1], axis=0, keepdims=True) - sm[0:1]
    f = lb + (1.0 - lb) * jax.nn.sigmoid(jnp.dot(a_ref[...], w_ref[...], preferred_element_type=f32))
    _store_heads(k_ref, 1.0 - f)
    _store_heads(g_ref, jnp.log2(f))


def _proj_section(kernel, a, w, section, out_heads, n_out=1, extra=None):
    m, k = a.shape
    d = k
    tm = _tile(m, 1024)
    tn = _tile(d, 1024)
    nb = d // tn
    in_specs = [pl.BlockSpec((tm, k), lambda i, j: (i, 0)),
                pl.BlockSpec((k, tn), lambda i, j: (0, section * nb + j))]
    args = [a, w]
    if extra is not None:
        in_specs.append(pl.BlockSpec((extra.shape[0], tn), lambda i, j: (0, j)))
        args.append(extra)
    if out_heads:
        hpb = tn // HEAD_DIM
        ospec = pl.BlockSpec((hpb, tm, HEAD_DIM), lambda i, j: (j, i, 0))
        oshape = jax.ShapeDtypeStruct((d // HEAD_DIM, m, HEAD_DIM), bf16)
    else:
        ospec = pl.BlockSpec((tm, tn), lambda i, j: (i, j))
        oshape = jax.ShapeDtypeStruct((m, d), bf16)
    out = pl.pallas_call(
        kernel,
        grid=(m // tm, nb),
        in_specs=in_specs,
        out_specs=[ospec] * n_out if n_out > 1 else ospec,
        out_shape=[oshape] * n_out if n_out > 1 else oshape,
        compiler_params=_cparams(("parallel", "arbitrary")),
    )(*args)
    return out


MAX_DIRECT_LOG2_DECAY = 100.0
HEAD_UNROLL = 8


def _dot_nt(a, b):
    return lax.dot_general(a, b, (((1,), (1,)), ((), ())), preferred_element_type=f32)


def _dot_tn(a, b):
    return lax.dot_general(a, b, (((0,), (0,)), ((), ())), preferred_element_type=f32)


def _mid_ref(p, level, offset):
    c, w = p.shape
    blk = 2 << level
    p3 = p.reshape(c // blk, blk, w)
    return jnp.broadcast_to(p3[:, offset:offset + 1, :], p3.shape).reshape(c, w)


def _neg_abs_exponents(gf, gb, b, x, level, rmod4):
    if level == 0:
        odd = (rmod4 & 1) == 1
        return jnp.where(odd, gf, 0.0), jnp.where(odd, 0.0, gb)
    if level == 1:
        c = gf.shape[0]
        gf_up = pltpu.roll(gf, c - 1, 0)
        gf_dn = pltpu.roll(gf, 1, 0)
        gb_up = pltpu.roll(gb, c - 1, 0)
        gb_dn = pltpu.roll(gb, 1, 0)
        ef = jnp.where(rmod4 == 0, gf_up, jnp.where(rmod4 == 1, 0.0, jnp.where(rmod4 == 2, gf, gf + gf_dn)))
        eb = jnp.where(rmod4 == 0, gb + gb_up, jnp.where(rmod4 == 1, gb, jnp.where(rmod4 == 2, 0.0, gb_dn)))
        return ef, eb
    half = 1 << level
    ef = b - _mid_ref(b, level, half - 1)
    eb = x - _mid_ref(x, level, half)
    return -jnp.abs(ef), -jnp.abs(eb)


def _chunk_prefix(tri_incl, tri_excl, gf16, gb16):
    b = jnp.dot(tri_incl, gf16, preferred_element_type=f32)
    x = jnp.dot(tri_excl, gb16, preferred_element_type=f32)
    return b, x


def _scan_bw_kernel(kb_ref, gb_ref, v_ref, s_ref, st_ref, *, seq, n_chunks):
    n_heads, c, _ = kb_ref.shape
    chunk = n_chunks - 1 - pl.program_id(0)
    _, is_end = _seq_flags(chunk * c, c, *seq)

    @pl.when(is_end)
    def _():
        st_ref[...] = jnp.zeros_like(st_ref)

    row = lax.broadcasted_iota(jnp.int32, (c, c), 0)
    col = lax.broadcasted_iota(jnp.int32, (c, c), 1)
    tri_excl = (col < row).astype(bf16)

    def head(h, carry):
        gb16 = gb_ref[h]
        x = jnp.dot(tri_excl, gb16, preferred_element_type=f32)
        tot = x[c - 1:c, :] + gb16[c - 1:c, :].astype(f32)
        st = st_ref[h]
        s_ref[h] = st.astype(s_ref.dtype)
        kx = (kb_ref[h].astype(f32) * jnp.exp2(x)).astype(bf16)
        st_ref[h] = st * jnp.exp2(tot) + _dot_tn(v_ref[h], kx)
        return carry

    lax.fori_loop(0, n_heads, head, 0, unroll=HEAD_UNROLL)


def _scan_bw_states(kb, gb, v, seq, c):
    n_heads, n, dk = kb.shape
    n_chunks = n // c
    spec = pl.BlockSpec((n_heads, c, dk), lambda i: (0, n_chunks - 1 - i, 0))
    return pl.pallas_call(
        functools.partial(_scan_bw_kernel, seq=seq, n_chunks=n_chunks),
        grid=(n_chunks,),
        in_specs=[spec, spec, spec],
        out_specs=pl.BlockSpec((None, n_heads, dk, dk), lambda i: (n_chunks - 1 - i, 0, 0, 0)),
        out_shape=jax.ShapeDtypeStruct((n_chunks, n_heads, dk, dk), bf16),
        scratch_shapes=[pltpu.VMEM((n_heads, dk, dk), f32)],
        compiler_params=_cparams(("arbitrary",)),
    )(kb, gb, v)


def _scan_kernel(q_ref, kf_ref, kb_ref, gf_ref, gb_ref, v_ref, sb_ref, tril_ref, triu_ref,
                 o_ref, st_ref, qf_s, kf_s, qb_s, kb_s, qin_s, kx_s, sc_s, dec_s, tot_s, *, seq):
    n_heads, c, dk = q_ref.shape
    n_levels = c.bit_length() - 1
    mid = c // 2
    is_start, _ = _seq_flags(pl.program_id(0) * c, c, *seq)

    @pl.when(is_start)
    def _():
        st_ref[...] = jnp.zeros_like(st_ref)

    row = lax.broadcasted_iota(jnp.int32, (c, c), 0)
    col = lax.broadcasted_iota(jnp.int32, (c, c), 1)
    tri_incl = (col <= row).astype(bf16)
    tri_excl = (col < row).astype(bf16)

    def finish(h, kf_dec, b_last, q_in, scores, diag):
        v = v_ref[h]
        st_f = st_ref[h]
        st_cat = jnp.concatenate([st_f.astype(bf16), sb_ref[h]], axis=1)
        o = _dot_nt(q_in, st_cat) + jnp.dot(scores.astype(bf16), v, preferred_element_type=f32)
        o_ref[h] = o if diag is None else o + diag * v.astype(f32)
        st_ref[h] = st_f * jnp.exp2(b_last) + _dot_tn(v, kf_dec.astype(bf16))

    def direct_operands(h, carry):
        q = q_ref[h].astype(f32)
        gb16 = gb_ref[h]
        b, x = _chunk_prefix(tri_incl, tri_excl, gf_ref[h], gb16)
        b_mid = b[mid - 1:mid, :]
        x_mid = x[mid:mid + 1, :]
        db = b - b_mid
        dx = x - x_mid
        q_f = q * jnp.exp2(db)
        k_f = kf_ref[h].astype(f32) * jnp.exp2(-db)
        q_b = q * jnp.exp2(-dx)
        k_b = kb_ref[h].astype(f32) * jnp.exp2(dx)
        b_last = b[c - 1:c, :]
        x_tot = x[c - 1:c, :] + gb16[c - 1:c, :].astype(f32)
        qf_s[h] = q_f.astype(bf16)
        kf_s[h] = k_f.astype(bf16)
        qb_s[h] = q_b.astype(bf16)
        kb_s[h] = k_b.astype(bf16)
        qin_s[h] = jnp.concatenate([q_f * jnp.exp2(b_mid), q_b * jnp.exp2(x_tot - x_mid)], axis=1).astype(bf16)
        kx_s[h] = (k_f * jnp.exp2(b_last - b_mid)).astype(bf16)
        dec_s[h] = jnp.broadcast_to(jnp.exp2(b_last), dec_s.shape[1:])
        tot_s[h] = jnp.broadcast_to(jnp.minimum(b_last, x_tot), tot_s.shape[1:])
        return carry

    def direct_scores(h, carry):
        sc_s[h] = (_dot_nt(qf_s[h], kf_s[h]) * tril_ref[...]
                   + _dot_nt(qb_s[h], kb_s[h]) * triu_ref[...]).astype(bf16)
        return carry

    def direct_output(h, carry):
        v = v_ref[h]
        st_f = st_ref[h]
        st_cat = jnp.concatenate([st_f.astype(bf16), sb_ref[h]], axis=1)
        o_ref[h] = _dot_nt(qin_s[h], st_cat) + jnp.dot(sc_s[h], v, preferred_element_type=f32)
        st_ref[h] = st_f * dec_s[h][0:1, :] + _dot_tn(v, kx_s[h])
        return carry

    def head_levelled(h, carry):
        rtok = lax.broadcasted_iota(jnp.int32, (c, dk), 0)
        rmod4 = rtok & 3
        q = q_ref[h].astype(f32)
        kf = kf_ref[h].astype(f32)
        kb = kb_ref[h].astype(f32)
        gf16 = gf_ref[h]
        gb16 = gb_ref[h]
        gf = gf16.astype(f32)
        gb = gb16.astype(f32)
        b, x = _chunk_prefix(tri_incl, tri_excl, gf16, gb16)
        scores = jnp.zeros((c, c), f32)
        for l in range(n_levels):
            ef, eb = _neg_abs_exponents(gf, gb, b, x, l, rmod4)
            af = jnp.exp2(ef)
            ab = jnp.exp2(eb)
            up = ((rtok >> l) & 1) == 1
            lhs = jnp.concatenate([jnp.where(up, q * af, 0.0), jnp.where(up, 0.0, q * ab)], axis=1)
            rhs = jnp.concatenate([jnp.where(up, 0.0, kf * af), jnp.where(up, kb * ab, 0.0)], axis=1)
            p = _dot_nt(lhs.astype(bf16), rhs.astype(bf16))
            if l < n_levels - 1:
                p = jnp.where((row >> (l + 1)) == (col >> (l + 1)), p, 0.0)
            scores = scores + p
        b_last = b[c - 1:c, :]
        x_tot = x[c - 1:c, :] + gb[c - 1:c, :]
        q_in = jnp.concatenate([q * jnp.exp2(b), q * jnp.exp2(x_tot - x)], axis=1).astype(bf16)
        diag = jnp.sum(q * (kf + kb), axis=1, keepdims=True)
        finish(h, kf * jnp.exp2(b_last - b), b_last, q_in, scores, diag)
        return carry

    lax.fori_loop(0, n_heads, direct_operands, 0, unroll=HEAD_UNROLL)
    direct_ok = jnp.min(tot_s[...]) >= -MAX_DIRECT_LOG2_DECAY

    @pl.when(direct_ok)
    def _():
        lax.fori_loop(0, n_heads, direct_scores, 0, unroll=HEAD_UNROLL)
        lax.fori_loop(0, n_heads, direct_output, 0, unroll=HEAD_UNROLL)

    @pl.when(jnp.logical_not(direct_ok))
    def _():
        lax.fori_loop(0, n_heads, head_levelled, 0)


def _scan(q, kf, kb, gf, gb, v, s_bw, seq, c):
    n_heads, n, dk = q.shape
    n_chunks = n // c
    spec = pl.BlockSpec((n_heads, c, dk), lambda i: (0, i, 0))
    tri = pl.BlockSpec((c, c), lambda i: (0, 0))
    tril = jnp.tril(jnp.ones((c, c), f32))
    return pl.pallas_call(
        functools.partial(_scan_kernel, seq=seq),
        grid=(n_chunks,),
        in_specs=[spec] * 6 + [pl.BlockSpec((None, n_heads, dk, dk), lambda i: (i, 0, 0, 0)), tri, tri],
        out_specs=spec,
        out_shape=jax.ShapeDtypeStruct((n_heads, n, dk), f32),
        scratch_shapes=[pltpu.VMEM((n_heads, dk, dk), f32)]
        + [pltpu.VMEM((n_heads, c, dk), bf16)] * 4
        + [pltpu.VMEM((n_heads, c, 2 * dk), bf16), pltpu.VMEM((n_heads, c, dk), bf16),
           pltpu.VMEM((n_heads, c, c), bf16)] + [pltpu.VMEM((n_heads, 8, dk), f32)] * 2,
        compiler_params=_cparams(("arbitrary",)),
    )(q, kf, kb, gf, gb, v, s_bw, tril, tril.T)


def _norm_gate_kernel(o_ref, gate_ref, gain_ref, out_ref, row_ref):
    for hh in range(o_ref.shape[0]):
        row_ref[:, hh * HEAD_DIM:(hh + 1) * HEAD_DIM] = o_ref[hh]
    out_ref[...] = (_rms(row_ref[...], gain_ref[...]) * gate_ref[...].astype(f32)).astype(out_ref.dtype)


def _norm_gate(o, gate, gain):
    n_heads, n, dk = o.shape
    d = n_heads * dk
    tr = _tile(n, 256)
    return pl.pallas_call(
        _norm_gate_kernel,
        grid=(n // tr,),
        in_specs=[pl.BlockSpec((n_heads, tr, dk), lambda i: (0, i, 0)),
                  pl.BlockSpec((tr, d), lambda i: (i, 0)),
                  pl.BlockSpec((1, d), lambda i: (0, 0))],
        out_specs=pl.BlockSpec((tr, d), lambda i: (i, 0)),
        out_shape=jax.ShapeDtypeStruct((n, d), bf16),
        scratch_shapes=[pltpu.VMEM((tr, d), f32)],
        compiler_params=_cparams(("parallel",)),
    )(o, gate, gain.reshape(1, d))


def _pack_halves(x):
    half = x.shape[1] // 2
    bits = lax.bitcast_convert_type(x.astype(bf16).astype(f32), jnp.int32)
    return lax.shift_right_logical(bits[:, :half], 16) | (bits[:, half:] & jnp.int32(-65536))


def _unpack_halves(words):
    lo = lax.bitcast_convert_type(lax.shift_left(words, 16), f32)
    hi = lax.bitcast_convert_type(words & jnp.int32(-65536), f32)
    return lo, hi


def _router_kernel(x_ref, g_ref, rhi_ref, rlo_ref, h_ref, e_ref, rank_ref, w_ref, cnt_ref, run_ref,
                   *, n_experts):
    @pl.when(pl.program_id(0) == 0)
    def _():
        run_ref[...] = jnp.zeros_like(run_ref)

    h = _rms(x_ref[...], g_ref[...])
    h_ref[...] = _pack_halves(h)
    hi = h.astype(bf16)
    lo = (h - hi.astype(f32)).astype(bf16)
    rhi = rhi_ref[...]
    logits = (jnp.dot(hi, rhi, preferred_element_type=f32)
              + jnp.dot(hi, rlo_ref[...], preferred_element_type=f32)
              + jnp.dot(lo, rhi, preferred_element_type=f32))
    tr = logits.shape[0]
    lane = lax.broadcasted_iota(jnp.int32, logits.shape, 1)
    neg = jnp.float32(-jnp.inf)
    logits = jnp.where(lane < n_experts, logits, neg)
    m1 = jnp.max(logits, axis=1, keepdims=True)
    i1 = jnp.min(jnp.where(logits == m1, lane, LANES), axis=1, keepdims=True)
    rest = jnp.where(lane == i1, neg, logits)
    m2 = jnp.max(rest, axis=1, keepdims=True)
    i2 = jnp.min(jnp.where(rest == m2, lane, LANES), axis=1, keepdims=True)
    e2 = jnp.exp(m2 - m1)
    w_ref[...] = jnp.concatenate([1.0 / (1.0 + e2), e2 / (1.0 + e2)], axis=1)
    e_ref[...] = jnp.concatenate([i1, i2], axis=1)
    member = jnp.where((lane == i1) | (lane == i2), 1.0, 0.0)
    row = lax.broadcasted_iota(jnp.int32, (tr, tr), 0)
    col = lax.broadcasted_iota(jnp.int32, (tr, tr), 1)
    before = jnp.dot((col < row).astype(bf16), member.astype(bf16), preferred_element_type=f32)
    before = before + run_ref[...]
    r1 = jnp.sum(jnp.where(lane == i1, before, 0.0), axis=1, keepdims=True)
    r2 = jnp.sum(jnp.where(lane == i2, before, 0.0), axis=1, keepdims=True)
    rank_ref[...] = jnp.concatenate([r1, r2], axis=1).astype(jnp.int32)
    run = run_ref[...] + jnp.sum(member, axis=0, keepdims=True)
    run_ref[...] = run
    cnt_ref[...] = run.astype(jnp.int32)


def _router(x, g, router):
    n, d = x.shape
    e = router.shape[1]
    rpad = jnp.zeros((d, LANES), f32).at[:, :e].set(router)
    rhi = rpad.astype(bf16)
    rlo = (rpad - rhi.astype(f32)).astype(bf16)
    tr = _tile(n, 512)
    pair = pl.BlockSpec((tr, TOP_K), lambda i: (i, 0))
    return pl.pallas_call(
        functools.partial(_router_kernel, n_experts=e),
        grid=(n // tr,),
        in_specs=[pl.BlockSpec((tr, d), lambda i: (i, 0)),
                  pl.BlockSpec((1, d), lambda i: (0, 0)),
                  pl.BlockSpec((d, LANES), lambda i: (0, 0)),
                  pl.BlockSpec((d, LANES), lambda i: (0, 0))],
        out_specs=[pl.BlockSpec((tr, d // 2), lambda i: (i, 0)), pair, pair, pair,
                   pl.BlockSpec((1, LANES), lambda i: (0, 0))],
        out_shape=[jax.ShapeDtypeStruct((n, d // 2), jnp.int32),
                   jax.ShapeDtypeStruct((n, TOP_K), jnp.int32),
                   jax.ShapeDtypeStruct((n, TOP_K), jnp.int32),
                   jax.ShapeDtypeStruct((n, TOP_K), f32),
                   jax.ShapeDtypeStruct((1, LANES), jnp.int32)],
        scratch_shapes=[pltpu.VMEM((1, LANES), f32)],
        compiler_params=_cparams(("arbitrary",)),
    )(x, g.reshape(1, d), rhi, rlo)


def _row_copies(pos_ref, i, src, dst, sem, gather):
    out = []
    for k in range(TOP_K):
        p = pos_ref[k, i]
        if gather:
            out.append(pltpu.make_async_copy(src.at[pl.ds(p, 1)], dst.at[k, pl.ds(i, 1)], sem))
        else:
            out.append(pltpu.make_async_copy(src.at[pl.ds(i, 1)], dst.at[pl.ds(p, 1)], sem))
    return out


def _all_rows(n_rows, pos_ref, src, dst, sem, gather):
    def start(i, carry):
        for cp in _row_copies(pos_ref, i, src, dst, sem, gather):
            cp.start()
        return carry

    def wait(i, carry):
        for cp in _row_copies(pos_ref, i, src, dst, sem, gather):
            cp.wait()
        return carry

    lax.fori_loop(0, n_rows, start, 0, unroll=ROW_DMA_UNROLL)
    lax.fori_loop(0, n_rows, wait, 0, unroll=ROW_DMA_UNROLL)


def _moe_dispatch_kernel(nv_ref, pos_ref, h_ref, xs_ref, zero_s, sem):
    tm = zero_s.shape[0]

    @pl.when(pl.program_id(0) == 0)
    def _():
        zero_s[...] = jnp.zeros_like(zero_s)

        def fill(r, carry):
            @pl.when(nv_ref[r] < tm)
            def _():
                cp = pltpu.make_async_copy(zero_s, xs_ref.at[pl.ds(pl.multiple_of(r * tm, tm), tm)], sem)
                cp.start()
                cp.wait()
            return carry

        lax.fori_loop(0, xs_ref.shape[0] // tm, fill, 0)

    _all_rows(h_ref.shape[0], pos_ref, h_ref, xs_ref, sem, gather=False)


def _moe_dispatch(hp, pos3, tile_rows, tm):
    n, dw = hp.shape
    n_tiles, _, tr = pos3.shape
    grid_spec = pltpu.PrefetchScalarGridSpec(
        num_scalar_prefetch=1,
        grid=(n_tiles,),
        in_specs=[pl.BlockSpec((None, TOP_K, tr), lambda i, nv: (i, 0, 0), memory_space=pltpu.SMEM),
                  pl.BlockSpec((tr, dw), lambda i, nv: (i, 0))],
        out_specs=pl.BlockSpec(memory_space=pl.ANY),
        scratch_shapes=[pltpu.VMEM((tm, dw), jnp.int32), pltpu.SemaphoreType.DMA(())],
    )
    return pl.pallas_call(
        _moe_dispatch_kernel,
        grid_spec=grid_spec,
        out_shape=jax.ShapeDtypeStruct((tile_rows.shape[0] * tm, dw), jnp.int32),
        compiler_params=_cparams(("arbitrary",)),
    )(tile_rows, pos3, hp)


def _moe_group_kernel(te_ref, nv_ref, xs_ref, wg_ref, wu_ref, wd_ref, y_ref, x_s, acc_s):
    r = pl.program_id(0)
    f = pl.program_id(1)
    last = pl.num_programs(1) - 1
    nv = nv_ref[r]
    half = xs_ref.shape[1]

    @pl.when(nv > 0)
    def _():
        @pl.when(f == 0)
        def _():
            lo, hi = _unpack_halves(xs_ref[...])
            x_s[:, :half] = lo.astype(bf16)
            x_s[:, half:] = hi.astype(bf16)
            acc_s[...] = jnp.zeros_like(acc_s)

        x = x_s[...]
        gate = jnp.dot(x, wg_ref[...], preferred_element_type=f32)
        up = jnp.dot(x, wu_ref[...], preferred_element_type=f32)
        acc_s[...] += jnp.dot((_silu(gate) * up).astype(bf16), wd_ref[...], preferred_element_type=f32)

        @pl.when(f == last)
        def _():
            y_ref[...] = _pack_halves(acc_s[...])

    @pl.when((nv == 0) & (f == last))
    def _():
        y_ref[...] = jnp.zeros_like(y_ref)


def _moe_group(xs, tile_expert, tile_rows, wg, wu, wd, tm):
    p, dw = xs.shape
    _, d, fe = wg.shape
    tf = _tile(fe, MOE_FF_TILE)
    grid_spec = pltpu.PrefetchScalarGridSpec(
        num_scalar_prefetch=2,
        grid=(p // tm, fe // tf),
        in_specs=[pl.BlockSpec((tm, dw), lambda r, f, te, nv: (r, 0)),
                  pl.BlockSpec((None, d, tf), lambda r, f, te, nv: (te[r], 0, f)),
                  pl.BlockSpec((None, d, tf), lambda r, f, te, nv: (te[r], 0, f)),
                  pl.BlockSpec((None, tf, d), lambda r, f, te, nv: (te[r], f, 0))],
        out_specs=pl.BlockSpec((tm, dw), lambda r, f, te, nv: (r, 0)),
        scratch_shapes=[pltpu.VMEM((tm, d), bf16), pltpu.VMEM((tm, d), f32)],
    )
    return pl.pallas_call(
        _moe_group_kernel,
        grid_spec=grid_spec,
        out_shape=jax.ShapeDtypeStruct((p, dw), jnp.int32),
        compiler_params=_cparams(("arbitrary", "arbitrary")),
    )(tile_expert, tile_rows, xs, wg, wu, wd)


def _moe_combine_kernel(pos_ref, x_ref, w_ref, g_ref, y_ref, o_ref, ybuf, sem, *, final_norm):
    tr, d = x_ref.shape
    half = d // 2
    _all_rows(tr, pos_ref, y_ref, ybuf, sem, gather=True)
    w = w_ref[...]
    lo0, hi0 = _unpack_halves(ybuf[0])
    lo1, hi1 = _unpack_halves(ybuf[1])
    o_lo = x_ref[:, :half] + (w[:, 0:1] * lo0 + w[:, 1:2] * lo1)
    o_hi = x_ref[:, half:] + (w[:, 0:1] * hi0 + w[:, 1:2] * hi1)
    if final_norm:
        ms = (jnp.sum(o_lo * o_lo, axis=1, keepdims=True) + jnp.sum(o_hi * o_hi, axis=1, keepdims=True)) / d
        inv = lax.rsqrt(ms + EPS)
        o_lo = o_lo * inv * g_ref[:, :half]
        o_hi = o_hi * inv * g_ref[:, half:]
    o_ref[:, :half] = o_lo
    o_ref[:, half:] = o_hi


def _moe_combine(x, w, y, pos3, gain, final_norm, row_start, n_rows):
    d = x.shape[1]
    tr = pos3.shape[2]
    off = row_start // tr
    return pl.pallas_call(
        functools.partial(_moe_combine_kernel, final_norm=final_norm),
        grid=(n_rows // tr,),
        in_specs=[pl.BlockSpec((None, TOP_K, tr), lambda i: (i + off, 0, 0), memory_space=pltpu.SMEM),
                  pl.BlockSpec((tr, d), lambda i: (i + off, 0)),
                  pl.BlockSpec((tr, TOP_K), lambda i: (i + off, 0)),
                  pl.BlockSpec((1, d), lambda i: (0, 0)),
                  pl.BlockSpec(memory_space=pl.ANY)],
        out_specs=pl.BlockSpec((tr, d), lambda i: (i, 0)),
        out_shape=jax.ShapeDtypeStruct((n_rows, d), f32),
        scratch_shapes=[pltpu.VMEM((TOP_K, tr, d // 2), jnp.int32), pltpu.SemaphoreType.DMA(())],
        compiler_params=_cparams(("arbitrary",)),
    )(pos3, x, w, gain.reshape(1, d), y)


def _moe_layer(x, g, router, wg, wu, wd, final_gain, parts):
    n, d = x.shape
    n_exp = router.shape[1]
    tm = MOE_ROW_TILE
    hp, eid, rank, w, cnt = _router(x, g, router)
    cnt = cnt[0, :n_exp]
    padded = (cnt + tm - 1) // tm * tm
    seg_end = jnp.cumsum(padded)
    seg_start = seg_end - padded
    pos = seg_start[eid] + rank
    n_rows_padded = TOP_K * n + n_exp * tm
    tile0 = jnp.arange(n_rows_padded // tm, dtype=jnp.int32) * tm
    tile_expert = jnp.minimum(jnp.searchsorted(seg_end, tile0, side="right"), n_exp - 1).astype(jnp.int32)
    tile_rows = jnp.clip(cnt[tile_expert] - (tile0 - seg_start[tile_expert]), 0, tm).astype(jnp.int32)
    tr = _tile(n, MOE_TOKEN_TILE)
    pos3 = pos.reshape(n // tr, tr, TOP_K).transpose(0, 2, 1)
    xs = _moe_dispatch(hp, pos3, tile_rows, tm)
    y = _moe_group(xs, tile_expert, tile_rows, wg, wu, wd, tm)
    if final_gain is None:
        return _moe_combine(x, w, y, pos3, g, False, 0, n)
    return [_moe_combine(x, w, y, pos3, final_gain, True, start, rows) for start, rows in parts]


def kernel(x_prompt, x_sample, norm_mix, norm_ffn, norm_final, pool_w, pool_scale, hgrn_w_in, hgrn_lb,
           hgrn_norm, hgrn_w_out, ffn_w_gate, ffn_w_up, ffn_w_down, moe_router, moe_w_gate, moe_w_up,
           moe_w_down):
    bp, sp, d = x_prompt.shape
    bs, ss, _ = x_sample.shape
    n_prompt = bp * sp
    n = n_prompt + bs * ss
    seq = (n_prompt, sp, ss)
    depth = norm_mix.shape[0]
    chunk = _tile(min(sp, ss), 128)

    parts = ((0, n_prompt), (n_prompt, bs * ss))
    streams = (x_prompt.reshape(n_prompt, d), x_sample.reshape(bs * ss, d))
    x = None if depth else jnp.concatenate(streams, axis=0)
    outs = None

    for i in range(depth):
        j = i // 2
        if i % 2 == 0:
            xa, xb = streams if x is None else (x[:n_prompt], x[n_prompt:])
            x, h = _pool_layer(xa, xb, norm_mix[i], pool_w[j].astype(bf16), pool_scale[j], norm_ffn[i], seq)
            act = _glu_up(h, ffn_w_gate[j].astype(bf16)[None], ffn_w_up[j].astype(bf16)[None])
            x = _down_res(act, ffn_w_down[j].astype(bf16), x, 4096)
        else:
            h = _rmsnorm(x, norm_mix[i], bf16)
            w_in = hgrn_w_in[j].astype(bf16)
            q = _proj_section(_proj_silu_heads_kernel, h, w_in, 0, True)
            forget = functools.partial(_proj_forget_kernel, layer=i)
            kf, gf = _proj_section(forget, h, w_in, 1, True, 2, hgrn_lb)
            kb, gb = _proj_section(forget, h, w_in, 2, True, 2, hgrn_lb)
            v = _proj_section(_proj_heads_kernel, h, w_in, 3, True)
            gate = _proj_section(_proj_silu_kernel, h, w_in, 4, False)
            s_bw = _scan_bw_states(kb, gb, v, seq, chunk)
            o = _scan(q, kf, kb, gf, gb, v, s_bw, seq, chunk)
            og = _norm_gate(o, gate, hgrn_norm[j])
            x = _down_res(og, hgrn_w_out[j].astype(bf16), x, 4096)
            last = i == depth - 1
            res = _moe_layer(x, norm_ffn[i], moe_router[j], moe_w_gate[j].astype(bf16),
                             moe_w_up[j].astype(bf16), moe_w_down[j].astype(bf16),
                             norm_final if last else None, parts)
            x, outs = (None, res) if last else (res, None)

    if outs is None:
        outs = [_rmsnorm(x, norm_final, f32, start, rows) for start, rows in parts]
    return (outs[0].reshape(bp, sp, d), outs[1].reshape(bs, ss, d))
```

```python
import functools

import jax
import jax.numpy as jnp
from jax import lax
from jax.experimental import pallas as pl
from jax.experimental.pallas import tpu as pltpu

EPS = 1e-6
POOL_WINDOWS = (2, 4, 8, 16)
POOL_HALO = 8
POOL_PAD = 16
POOL_BLOCK = 128
HEAD_DIM = 128
TOP_K = 2
V7X_VMEM_LIMIT = 56 * 1024 * 1024
LANES = 128
MOE_ROW_TILE = 512
MOE_FF_TILE = 512
MOE_TOKEN_TILE = 256
ROW_DMA_UNROLL = 8

bf16 = jnp.bfloat16
f32 = jnp.float32


def _cparams(sem):
    return pltpu.CompilerParams(dimension_semantics=sem, vmem_limit_bytes=V7X_VMEM_LIMIT)


def _tile(n, pref):
    t = min(n, pref)
    while n % t:
        t //= 2
    return t


def _rms(x, g):
    ms = jnp.mean(x * x, axis=-1, keepdims=True)
    return x * lax.rsqrt(ms + EPS) * g


def _silu(x):
    return x * jax.nn.sigmoid(x)


def _seq_flags(tok, n_tok, n_prompt, seq_p, seq_s):
    in_p = tok < n_prompt
    rel = jnp.where(in_p, tok, tok - n_prompt)
    slen = jnp.where(in_p, seq_p, seq_s)
    is_start = (rel % slen) == 0
    is_end = ((rel + n_tok) % slen) == 0
    return is_start, is_end


def _rmsnorm_kernel(x_ref, g_ref, o_ref):
    o_ref[...] = _rms(x_ref[...], g_ref[...]).astype(o_ref.dtype)


def _rmsnorm(x, g, out_dtype, row_start=0, n_rows=None):
    n, d = x.shape
    n_rows = n if n_rows is None else n_rows
    tr = _tile(n_rows, 512)
    off = row_start // tr
    return pl.pallas_call(
        _rmsnorm_kernel,
        grid=(n_rows // tr,),
        in_specs=[pl.BlockSpec((tr, d), lambda i: (i + off, 0)),
                  pl.BlockSpec((1, d), lambda i: (0, 0))],
        out_specs=pl.BlockSpec((tr, d), lambda i: (i, 0)),
        out_shape=jax.ShapeDtypeStruct((n_rows, d), out_dtype),
        compiler_params=_cparams(("parallel",)),
    )(x, g.reshape(1, d))


def _pool_kernel(ap_ref, a_ref, an_ref, bp_ref, b_ref, bn_ref, gmix_ref, w_ref, scale_ref, gffn_ref,
                 x1_ref, h1_ref, hi_s, lo_s, *, seq):
    t, d = a_ref.shape
    dg = d // len(POOL_WINDOWS)
    tok = pl.program_id(0) * t
    is_start, is_end = _seq_flags(tok, t, *seq)
    in_a = tok < seq[0]
    g = gmix_ref[...]
    x = jnp.where(in_a, a_ref[...], b_ref[...])
    h = _rms(x, g)
    h_prev = jnp.where(is_start, 0.0, _rms(jnp.where(in_a, ap_ref[...], bp_ref[...]), g))
    h_next = jnp.where(is_end, 0.0, _rms(jnp.where(in_a, an_ref[...], bn_ref[...]), g))
    zeros = jnp.zeros_like(h_prev)

    def put(r0, val):
        hi = val.astype(bf16)
        hi_s[r0:r0 + val.shape[0], :] = hi
        lo_s[r0:r0 + val.shape[0], :] = (val - hi.astype(f32)).astype(bf16)

    put(0, jnp.concatenate([zeros, h_prev], axis=0))
    put(POOL_PAD, h)
    put(POOL_PAD + t, jnp.concatenate([h_next, zeros], axis=0))
    tail = hi_s.shape[0] - (2 * POOL_PAD + t)
    hi_s[2 * POOL_PAD + t:, :] = jnp.zeros((tail, d), bf16)
    lo_s[2 * POOL_PAD + t:, :] = jnp.zeros((tail, d), bf16)

    rr = lax.broadcasted_iota(jnp.int32, (POOL_BLOCK, 2 * POOL_BLOCK), 0)
    cc = lax.broadcasted_iota(jnp.int32, (POOL_BLOCK, 2 * POOL_BLOCK), 1)
    r = lax.broadcasted_iota(jnp.int32, (t, 1), 0)
    for gi, w in enumerate(POOL_WINDOWS):
        c0 = gi * dg
        half = w // 2
        band = ((cc >= POOL_PAD + rr - half) & (cc < POOL_PAD + rr + half)).astype(bf16)
        sums = []
        for r0 in range(0, t, POOL_BLOCK):
            sums.append(jnp.dot(band, hi_s[r0:r0 + 2 * POOL_BLOCK, c0:c0 + dg], preferred_element_type=f32)
                        + jnp.dot(band, lo_s[r0:r0 + 2 * POOL_BLOCK, c0:c0 + dg], preferred_element_type=f32))
        s = jnp.concatenate(sums, axis=0)
        lo = jnp.where(is_start, jnp.maximum(r - half, 0), r - half)
        hi = jnp.where(is_end, jnp.minimum(r + half, t), r + half)
        cnt = (hi - lo).astype(f32)
        dlt = s / cnt - h[:, c0:c0 + dg]
        y = jnp.dot(dlt.astype(bf16), w_ref[gi], preferred_element_type=f32)
        x1_ref[:, c0:c0 + dg] = x[:, c0:c0 + dg] + y * scale_ref[:, c0:c0 + dg]
    h1_ref[...] = _rms(x1_ref[...], gffn_ref[...]).astype(h1_ref.dtype)


def _stream_specs(rows, first_tile, t, d):
    hb = t // POOL_HALO
    last_halo = rows // POOL_HALO - 1
    last_tile = rows // t - 1
    return [
        pl.BlockSpec((POOL_HALO, d), lambda i: (jnp.clip((i - first_tile) * hb - 1, 0, last_halo), 0)),
        pl.BlockSpec((t, d), lambda i: (jnp.clip(i - first_tile, 0, last_tile), 0)),
        pl.BlockSpec((POOL_HALO, d), lambda i: (jnp.clip((i - first_tile + 1) * hb, 0, last_halo), 0)),
    ]


def _pool_layer(xa, xb, g_mix, w_grp, scale, g_ffn, seq):
    d = xa.shape[1]
    n = xa.shape[0] + xb.shape[0]
    t = _tile(min(seq[1], seq[2]), 256)
    row = lambda i: (0, 0)
    return pl.pallas_call(
        functools.partial(_pool_kernel, seq=seq),
        grid=(n // t,),
        in_specs=_stream_specs(xa.shape[0], 0, t, d) + _stream_specs(xb.shape[0], xa.shape[0] // t, t, d) + [
            pl.BlockSpec((1, d), row),
            pl.BlockSpec(w_grp.shape, lambda i: (0, 0, 0)),
            pl.BlockSpec((1, d), row),
            pl.BlockSpec((1, d), row),
        ],
        out_specs=[pl.BlockSpec((t, d), lambda i: (i, 0)),
                   pl.BlockSpec((t, d), lambda i: (i, 0))],
        out_shape=[jax.ShapeDtypeStruct((n, d), f32), jax.ShapeDtypeStruct((n, d), bf16)],
        scratch_shapes=[pltpu.VMEM((t + POOL_BLOCK, d), bf16)] * 2,
        compiler_params=_cparams(("parallel",)),
    )(xa, xa, xa, xb, xb, xb, g_mix.reshape(1, d), w_grp, scale.reshape(1, d), g_ffn.reshape(1, d))


def _glu_up_kernel(a_ref, wg_ref, wu_ref, o_ref):
    a = a_ref[...]
    gate = jnp.dot(a, wg_ref[...], preferred_element_type=f32)
    up = jnp.dot(a, wu_ref[...], preferred_element_type=f32)
    o_ref[...] = (_silu(gate) * up).astype(o_ref.dtype)


def _glu_up(a, wg, wu):
    m, k = a.shape
    e, _, f = wg.shape
    tm = _tile(m, 1024)
    tn = _tile(f, 512)
    nb = f // tn
    wspec = pl.BlockSpec((None, k, tn), lambda i, j: (j // nb, 0, j % nb))
    return pl.pallas_call(
        _glu_up_kernel,
        grid=(m // tm, e * nb),
        in_specs=[pl.BlockSpec((tm, k), lambda i, j: (i, 0)), wspec, wspec],
        out_specs=pl.BlockSpec((tm, tn), lambda i, j: (i, j)),
        out_shape=jax.ShapeDtypeStruct((m, e * f), bf16),
        compiler_params=_cparams(("parallel", "arbitrary")),
    )(a, wg, wu)


def _down_kernel(a_ref, w_ref, res_ref, o_ref, acc_ref):
    k = pl.program_id(2)

    @pl.when(k == 0)
    def _():
        acc_ref[...] = jnp.zeros_like(acc_ref)

    acc_ref[...] += jnp.dot(a_ref[...], w_ref[...], preferred_element_type=f32)

    @pl.when(k == pl.num_programs(2) - 1)
    def _():
        o_ref[...] = res_ref[...] + acc_ref[...]


def _down1_kernel(a_ref, w_ref, res_ref, o_ref):
    o_ref[...] = res_ref[...] + jnp.dot(a_ref[...], w_ref[...], preferred_element_type=f32)


def _down_res(a, w, res, tk):
    m, kk = a.shape
    n = w.shape[1]
    tm = _tile(m, 1024)
    tn = _tile(n, 512)
    tk = _tile(kk, tk)
    if tk == kk:
        return pl.pallas_call(
            _down1_kernel,
            grid=(m // tm, n // tn),
            in_specs=[pl.BlockSpec((tm, kk), lambda i, j: (i, 0)),
                      pl.BlockSpec((kk, tn), lambda i, j: (0, j)),
                      pl.BlockSpec((tm, tn), lambda i, j: (i, j))],
            out_specs=pl.BlockSpec((tm, tn), lambda i, j: (i, j)),
            out_shape=jax.ShapeDtypeStruct((m, n), f32),
            compiler_params=_cparams(("parallel", "arbitrary")),
        )(a, w, res)
    return pl.pallas_call(
        _down_kernel,
        grid=(m // tm, n // tn, kk // tk),
        in_specs=[pl.BlockSpec((tm, tk), lambda i, j, k: (i, k)),
                  pl.BlockSpec((tk, tn), lambda i, j, k: (k, j)),
                  pl.BlockSpec((tm, tn), lambda i, j, k: (i, j))],
        out_specs=pl.BlockSpec((tm, tn), lambda i, j, k: (i, j)),
        out_shape=jax.ShapeDtypeStruct((m, n), f32),
        scratch_shapes=[pltpu.VMEM((tm, tn), f32)],
        compiler_params=_cparams(("parallel", "parallel", "arbitrary")),
    )(a, w, res)


def _store_heads(o_ref, val):
    for hh in range(o_ref.shape[0]):
        o_ref[hh] = val[:, hh * HEAD_DIM:(hh + 1) * HEAD_DIM].astype(o_ref.dtype)


def _proj_silu_heads_kernel(a_ref, w_ref, o_ref):
    _store_heads(o_ref, _silu(jnp.dot(a_ref[...], w_ref[...], preferred_element_type=f32)))


def _proj_heads_kernel(a_ref, w_ref, o_ref):
    _store_heads(o_ref, jnp.dot(a_ref[...], w_ref[...], preferred_element_type=f32))


def _proj_silu_kernel(a_ref, w_ref, o_ref):
    o_ref[...] = _silu(jnp.dot(a_ref[...], w_ref[...], preferred_element_type=f32)).astype(o_ref.dtype)


def _proj_forget_kernel(a_ref, w_ref, lb_ref, k_ref, g_ref, *, layer):
    logits = lb_ref[...]
    ex = jnp.exp(logits - jnp.max(logits, axis=0, keepdims=True))
    sm = ex / jnp.sum(ex, axis=0, keepdims=True)
    lb = jnp.sum(sm[0:layer + 1], axis=0, keepdims=True) - sm[0:1]
    f = lb + (1.0 - lb) * jax.nn.sigmoid(jnp.dot(a_ref[...], w_ref[...], preferred_element_type=f32))
    _store_heads(k_ref, 1.0 - f)
    _store_heads(g_ref, jnp.log2(f))


def _proj_section(kernel, a, w, section, out_heads, n_out=1, extra=None):
    m, k = a.shape
    d = k
    tm = _tile(m, 1024)
    tn = _tile(d, 1024)
    nb = d // tn
    in_specs = [pl.BlockSpec((tm, k), lambda i, j: (i, 0)),
                pl.BlockSpec((k, tn), lambda i, j: (0, section * nb + j))]
    args = [a, w]
    if extra is not None:
        in_specs.append(pl.BlockSpec((extra.shape[0], tn), lambda i, j: (0, j)))
        args.append(extra)
    if out_heads:
        hpb = tn // HEAD_DIM
        ospec = pl.BlockSpec((hpb, tm, HEAD_DIM), lambda i, j: (j, i, 0))
        oshape = jax.ShapeDtypeStruct((d // HEAD_DIM, m, HEAD_DIM), bf16)
    else:
        ospec = pl.BlockSpec((tm, tn), lambda i, j: (i, j))
        oshape = jax.ShapeDtypeStruct((m, d), bf16)
    out = pl.pallas_call(
        kernel,
        grid=(m // tm, nb),
        in_specs=in_specs,
        out_specs=[ospec] * n_out if n_out > 1 else ospec,
        out_shape=[oshape] * n_out if n_out > 1 else oshape,
        compiler_params=_cparams(("parallel", "arbitrary")),
    )(*args)
    return out


MAX_DIRECT_LOG2_DECAY = 100.0
HEAD_UNROLL = 8


def _dot_nt(a, b):
    return lax.dot_general(a, b, (((1,), (1,)), ((), ())), preferred_element_type=f32)


def _dot_tn(a, b):
    return lax.dot_general(a, b, (((0,), (0,)), ((), ())), preferred_element_type=f32)


def _mid_ref(p, level, offset):
    c, w = p.shape
    blk = 2 << level
    p3 = p.reshape(c // blk, blk, w)
    return jnp.broadcast_to(p3[:, offset:offset + 1, :], p3.shape).reshape(c, w)


def _neg_abs_exponents(gf, gb, b, x, level, rmod4):
    if level == 0:
        odd = (rmod4 & 1) == 1
        return jnp.where(odd, gf, 0.0), jnp.where(odd, 0.0, gb)
    if level == 1:
        c = gf.shape[0]
        gf_up = pltpu.roll(gf, c - 1, 0)
        gf_dn = pltpu.roll(gf, 1, 0)
        gb_up = pltpu.roll(gb, c - 1, 0)
        gb_dn = pltpu.roll(gb, 1, 0)
        ef = jnp.where(rmod4 == 0, gf_up, jnp.where(rmod4 == 1, 0.0, jnp.where(rmod4 == 2, gf, gf + gf_dn)))
        eb = jnp.where(rmod4 == 0, gb + gb_up, jnp.where(rmod4 == 1, gb, jnp.where(rmod4 == 2, 0.0, gb_dn)))
        return ef, eb
    half = 1 << level
    ef = b - _mid_ref(b, level, half - 1)
    eb = x - _mid_ref(x, level, half)
    return -jnp.abs(ef), -jnp.abs(eb)


def _chunk_prefix(tri_incl, tri_excl, gf16, gb16):
    b = jnp.dot(tri_incl, gf16, preferred_element_type=f32)
    x = jnp.dot(tri_excl, gb16, preferred_element_type=f32)
    return b, x


def _scan_bw_kernel(kb_ref, gb_ref, v_ref, s_ref, st_ref, *, seq, n_chunks):
    n_heads, c, _ = kb_ref.shape
    chunk = n_chunks - 1 - pl.program_id(0)
    _, is_end = _seq_flags(chunk * c, c, *seq)

    @pl.when(is_end)
    def _():
        st_ref[...] = jnp.zeros_like(st_ref)

    row = lax.broadcasted_iota(jnp.int32, (c, c), 0)
    col = lax.broadcasted_iota(jnp.int32, (c, c), 1)
    tri_excl = (col < row).astype(bf16)

    def head(h, carry):
        gb16 = gb_ref[h]
        x = jnp.dot(tri_excl, gb16, preferred_element_type=f32)
        tot = x[c - 1:c, :] + gb16[c - 1:c, :].astype(f32)
        st = st_ref[h]
        s_ref[h] = st.astype(s_ref.dtype)
        kx = (kb_ref[h].astype(f32) * jnp.exp2(x)).astype(bf16)
        st_ref[h] = st * jnp.exp2(tot) + _dot_tn(v_ref[h], kx)
        return carry

    lax.fori_loop(0, n_heads, head, 0, unroll=HEAD_UNROLL)


def _scan_bw_states(kb, gb, v, seq, c):
    n_heads, n, dk = kb.shape
    n_chunks = n // c
    spec = pl.BlockSpec((n_heads, c, dk), lambda i: (0, n_chunks - 1 - i, 0))
    return pl.pallas_call(
        functools.partial(_scan_bw_kernel, seq=seq, n_chunks=n_chunks),
        grid=(n_chunks,),
        in_specs=[spec, spec, spec],
        out_specs=pl.BlockSpec((None, n_heads, dk, dk), lambda i: (n_chunks - 1 - i, 0, 0, 0)),
        out_shape=jax.ShapeDtypeStruct((n_chunks, n_heads, dk, dk), bf16),
        scratch_shapes=[pltpu.VMEM((n_heads, dk, dk), f32)],
        compiler_params=_cparams(("arbitrary",)),
    )(kb, gb, v)


def _scan_kernel(q_ref, kf_ref, kb_ref, gf_ref, gb_ref, v_ref, sb_ref, tril_ref, triu_ref, gate_ref,
                 gain_ref, out_ref, st_ref, o_s, qf_s, kf_s, qb_s, kb_s, qin_s, kx_s, sc_s, dec_s, tot_s,
                 *, seq):
    n_heads, c, dk = q_ref.shape
    n_levels = c.bit_length() - 1
    mid = c // 2
    is_start, _ = _seq_flags(pl.program_id(0) * c, c, *seq)

    @pl.when(is_start)
    def _():
        st_ref[...] = jnp.zeros_like(st_ref)

    row = lax.broadcasted_iota(jnp.int32, (c, c), 0)
    col = lax.broadcasted_iota(jnp.int32, (c, c), 1)
    tri_incl = (col <= row).astype(bf16)
    tri_excl = (col < row).astype(bf16)

    def finish(h, kf_dec, b_last, q_in, scores, diag):
        v = v_ref[h]
        st_f = st_ref[h]
        st_cat = jnp.concatenate([st_f.astype(bf16), sb_ref[h]], axis=1)
        o = _dot_nt(q_in, st_cat) + jnp.dot(scores.astype(bf16), v, preferred_element_type=f32)
        o_s[h] = o if diag is None else o + diag * v.astype(f32)
        st_ref[h] = st_f * jnp.exp2(b_last) + _dot_tn(v, kf_dec.astype(bf16))

    def direct_operands(h, carry):
        q = q_ref[h].astype(f32)
        gb16 = gb_ref[h]
        b, x = _chunk_prefix(tri_incl, tri_excl, gf_ref[h], gb16)
        b_mid = b[mid - 1:mid, :]
        x_mid = x[mid:mid + 1, :]
        db = b - b_mid
        dx = x - x_mid
        q_f = q * jnp.exp2(db)
        k_f = kf_ref[h].astype(f32) * jnp.exp2(-db)
        q_b = q * jnp.exp2(-dx)
        k_b = kb_ref[h].astype(f32) * jnp.exp2(dx)
        b_last = b[c - 1:c, :]
        x_tot = x[c - 1:c, :] + gb16[c - 1:c, :].astype(f32)
        qf_s[h] = q_f.astype(bf16)
        kf_s[h] = k_f.astype(bf16)
        qb_s[h] = q_b.astype(bf16)
        kb_s[h] = k_b.astype(bf16)
        qin_s[h] = jnp.concatenate([q_f * jnp.exp2(b_mid), q_b * jnp.exp2(x_tot - x_mid)], axis=1).astype(bf16)
        kx_s[h] = (k_f * jnp.exp2(b_last - b_mid)).astype(bf16)
        dec_s[h] = jnp.broadcast_to(jnp.exp2(b_last), dec_s.shape[1:])
        tot_s[h] = jnp.broadcast_to(jnp.minimum(b_last, x_tot), tot_s.shape[1:])
        return carry

    def direct_scores(h, carry):
        sc_s[h] = (_dot_nt(qf_s[h], kf_s[h]) * tril_ref[...]
                   + _dot_nt(qb_s[h], kb_s[h]) * triu_ref[...]).astype(bf16)
        return carry

    def direct_output(h, carry):
        v = v_ref[h]
        st_f = st_ref[h]
        st_cat = jnp.concatenate([st_f.astype(bf16), sb_ref[h]], axis=1)
        o_s[h] = _dot_nt(qin_s[h], st_cat) + jnp.dot(sc_s[h], v, preferred_element_type=f32)
        st_ref[h] = st_f * dec_s[h][0:1, :] + _dot_tn(v, kx_s[h])
        return carry

    def head_levelled(h, carry):
        rtok = lax.broadcasted_iota(jnp.int32, (c, dk), 0)
        rmod4 = rtok & 3
        q = q_ref[h].astype(f32)
        kf = kf_ref[h].astype(f32)
        kb = kb_ref[h].astype(f32)
        gf16 = gf_ref[h]
        gb16 = gb_ref[h]
        gf = gf16.astype(f32)
        gb = gb16.astype(f32)
        b, x = _chunk_prefix(tri_incl, tri_excl, gf16, gb16)
        scores = jnp.zeros((c, c), f32)
        for l in range(n_levels):
            ef, eb = _neg_abs_exponents(gf, gb, b, x, l, rmod4)
            af = jnp.exp2(ef)
            ab = jnp.exp2(eb)
            up = ((rtok >> l) & 1) == 1
            lhs = jnp.concatenate([jnp.where(up, q * af, 0.0), jnp.where(up, 0.0, q * ab)], axis=1)
            rhs = jnp.concatenate([jnp.where(up, 0.0, kf * af), jnp.where(up, kb * ab, 0.0)], axis=1)
            p = _dot_nt(lhs.astype(bf16), rhs.astype(bf16))
            if l < n_levels - 1:
                p = jnp.where((row >> (l + 1)) == (col >> (l + 1)), p, 0.0)
            scores = scores + p
        b_last = b[c - 1:c, :]
        x_tot = x[c - 1:c, :] + gb[c - 1:c, :]
        q_in = jnp.concatenate([q * jnp.exp2(b), q * jnp.exp2(x_tot - x)], axis=1).astype(bf16)
        diag = jnp.sum(q * (kf + kb), axis=1, keepdims=True)
        finish(h, kf * jnp.exp2(b_last - b), b_last, q_in, scores, diag)
        return carry

    lax.fori_loop(0, n_heads, direct_operands, 0, unroll=HEAD_UNROLL)
    direct_ok = jnp.min(tot_s[...]) >= -MAX_DIRECT_LOG2_DECAY

    @pl.when(direct_ok)
    def _():
        lax.fori_loop(0, n_heads, direct_scores, 0, unroll=HEAD_UNROLL)
        lax.fori_loop(0, n_heads, direct_output, 0, unroll=HEAD_UNROLL)

    @pl.when(jnp.logical_not(direct_ok))
    def _():
        lax.fori_loop(0, n_heads, head_levelled, 0)

    sq = jnp.zeros((c, dk), f32)
    for hh in range(n_heads):
        o = o_s[hh]
        sq = sq + o * o
    inv = lax.rsqrt(jnp.sum(sq, axis=1, keepdims=True) / (n_heads * dk) + EPS)
    for hh in range(n_heads):
        cols = slice(hh * dk, (hh + 1) * dk)
        out_ref[:, cols] = (o_s[hh] * inv * gain_ref[:, cols] * gate_ref[:, cols].astype(f32)
                            ).astype(out_ref.dtype)


def _scan(q, kf, kb, gf, gb, v, s_bw, gate, gain, seq, c):
    n_heads, n, dk = q.shape
    d = n_heads * dk
    n_chunks = n // c
    spec = pl.BlockSpec((n_heads, c, dk), lambda i: (0, i, 0))
    tri = pl.BlockSpec((c, c), lambda i: (0, 0))
    tril = jnp.tril(jnp.ones((c, c), f32))
    return pl.pallas_call(
        functools.partial(_scan_kernel, seq=seq),
        grid=(n_chunks,),
        in_specs=[spec] * 6 + [pl.BlockSpec((None, n_heads, dk, dk), lambda i: (i, 0, 0, 0)), tri, tri,
                               pl.BlockSpec((c, d), lambda i: (i, 0)),
                               pl.BlockSpec((1, d), lambda i: (0, 0))],
        out_specs=pl.BlockSpec((c, d), lambda i: (i, 0)),
        out_shape=jax.ShapeDtypeStruct((n, d), bf16),
        scratch_shapes=[pltpu.VMEM((n_heads, dk, dk), f32), pltpu.VMEM((n_heads, c, dk), f32)]
        + [pltpu.VMEM((n_heads, c, dk), bf16)] * 4
        + [pltpu.VMEM((n_heads, c, 2 * dk), bf16), pltpu.VMEM((n_heads, c, dk), bf16),
           pltpu.VMEM((n_heads, c, c), bf16)] + [pltpu.VMEM((n_heads, 8, dk), f32)] * 2,
        compiler_params=_cparams(("arbitrary",)),
    )(q, kf, kb, gf, gb, v, s_bw, tril, tril.T, gate, gain.reshape(1, d))


def _pack_halves(x):
    half = x.shape[1] // 2
    bits = lax.bitcast_convert_type(x.astype(bf16).astype(f32), jnp.int32)
    return lax.shift_right_logical(bits[:, :half], 16) | (bits[:, half:] & jnp.int32(-65536))


def _unpack_halves(words):
    lo = lax.bitcast_convert_type(lax.shift_left(words, 16), f32)
    hi = lax.bitcast_convert_type(words & jnp.int32(-65536), f32)
    return lo, hi


def _router_kernel(x_ref, g_ref, rhi_ref, rlo_ref, h_ref, e_ref, rank_ref, w_ref, cnt_ref, run_ref,
                   *, n_experts):
    @pl.when(pl.program_id(0) == 0)
    def _():
        run_ref[...] = jnp.zeros_like(run_ref)

    h = _rms(x_ref[...], g_ref[...])
    h_ref[...] = _pack_halves(h)
    hi = h.astype(bf16)
    lo = (h - hi.astype(f32)).astype(bf16)
    rhi = rhi_ref[...]
    logits = (jnp.dot(hi, rhi, preferred_element_type=f32)
              + jnp.dot(hi, rlo_ref[...], preferred_element_type=f32)
              + jnp.dot(lo, rhi, preferred_element_type=f32))
    tr = logits.shape[0]
    lane = lax.broadcasted_iota(jnp.int32, logits.shape, 1)
    neg = jnp.float32(-jnp.inf)
    logits = jnp.where(lane < n_experts, logits, neg)
    m1 = jnp.max(logits, axis=1, keepdims=True)
    i1 = jnp.min(jnp.where(logits == m1, lane, LANES), axis=1, keepdims=True)
    rest = jnp.where(lane == i1, neg, logits)
    m2 = jnp.max(rest, axis=1, keepdims=True)
    i2 = jnp.min(jnp.where(rest == m2, lane, LANES), axis=1, keepdims=True)
    e2 = jnp.exp(m2 - m1)
    w_ref[...] = jnp.concatenate([1.0 / (1.0 + e2), e2 / (1.0 + e2)], axis=1)
    e_ref[...] = jnp.concatenate([i1, i2], axis=1)
    member = jnp.where((lane == i1) | (lane == i2), 1.0, 0.0)
    row = lax.broadcasted_iota(jnp.int32, (tr, tr), 0)
    col = lax.broadcasted_iota(jnp.int32, (tr, tr), 1)
    before = jnp.dot((col < row).astype(bf16), member.astype(bf16), preferred_element_type=f32)
    before = before + run_ref[...]
    r1 = jnp.sum(jnp.where(lane == i1, before, 0.0), axis=1, keepdims=True)
    r2 = jnp.sum(jnp.where(lane == i2, before, 0.0), axis=1, keepdims=True)
    rank_ref[...] = jnp.concatenate([r1, r2], axis=1).astype(jnp.int32)
    run = run_ref[...] + jnp.sum(member, axis=0, keepdims=True)
    run_ref[...] = run
    cnt_ref[...] = run.astype(jnp.int32)


def _router(x, g, router):
    n, d = x.shape
    e = router.shape[1]
    rpad = jnp.zeros((d, LANES), f32).at[:, :e].set(router)
    rhi = rpad.astype(bf16)
    rlo = (rpad - rhi.astype(f32)).astype(bf16)
    tr = _tile(n, 512)
    pair = pl.BlockSpec((tr, TOP_K), lambda i: (i, 0))
    return pl.pallas_call(
        functools.partial(_router_kernel, n_experts=e),
        grid=(n // tr,),
        in_specs=[pl.BlockSpec((tr, d), lambda i: (i, 0)),
                  pl.BlockSpec((1, d), lambda i: (0, 0)),
                  pl.BlockSpec((d, LANES), lambda i: (0, 0)),
                  pl.BlockSpec((d, LANES), lambda i: (0, 0))],
        out_specs=[pl.BlockSpec((tr, d // 2), lambda i: (i, 0)), pair, pair, pair,
                   pl.BlockSpec((1, LANES), lambda i: (0, 0))],
        out_shape=[jax.ShapeDtypeStruct((n, d // 2), jnp.int32),
                   jax.ShapeDtypeStruct((n, TOP_K), jnp.int32),
                   jax.ShapeDtypeStruct((n, TOP_K), jnp.int32),
                   jax.ShapeDtypeStruct((n, TOP_K), f32),
                   jax.ShapeDtypeStruct((1, LANES), jnp.int32)],
        scratch_shapes=[pltpu.VMEM((1, LANES), f32)],
        compiler_params=_cparams(("arbitrary",)),
    )(x, g.reshape(1, d), rhi, rlo)


def _row_copies(pos_ref, i, src, dst, sem, gather):
    out = []
    for k in range(TOP_K):
        p = pos_ref[k, i]
        if gather:
            out.append(pltpu.make_async_copy(src.at[pl.ds(p, 1)], dst.at[k, pl.ds(i, 1)], sem))
        else:
            out.append(pltpu.make_async_copy(src.at[pl.ds(i, 1)], dst.at[pl.ds(p, 1)], sem))
    return out


def _all_rows(n_rows, pos_ref, src, dst, sem, gather):
    def start(i, carry):
        for cp in _row_copies(pos_ref, i, src, dst, sem, gather):
            cp.start()
        return carry

    def wait(i, carry):
        for cp in _row_copies(pos_ref, i, src, dst, sem, gather):
            cp.wait()
        return carry

    lax.fori_loop(0, n_rows, start, 0, unroll=ROW_DMA_UNROLL)
    lax.fori_loop(0, n_rows, wait, 0, unroll=ROW_DMA_UNROLL)


def _moe_dispatch_kernel(nv_ref, pos_ref, h_ref, xs_ref, zero_s, sem):
    tm = zero_s.shape[0]

    @pl.when(pl.program_id(0) == 0)
    def _():
        zero_s[...] = jnp.zeros_like(zero_s)

        def fill(r, carry):
            @pl.when(nv_ref[r] < tm)
            def _():
                cp = pltpu.make_async_copy(zero_s, xs_ref.at[pl.ds(pl.multiple_of(r * tm, tm), tm)], sem)
                cp.start()
                cp.wait()
            return carry

        lax.fori_loop(0, xs_ref.shape[0] // tm, fill, 0)

    _all_rows(h_ref.shape[0], pos_ref, h_ref, xs_ref, sem, gather=False)


def _moe_dispatch(hp, pos3, tile_rows, tm):
    n, dw = hp.shape
    n_tiles, _, tr = pos3.shape
    grid_spec = pltpu.PrefetchScalarGridSpec(
        num_scalar_prefetch=1,
        grid=(n_tiles,),
        in_specs=[pl.BlockSpec((None, TOP_K, tr), lambda i, nv: (i, 0, 0), memory_space=pltpu.SMEM),
                  pl.BlockSpec((tr, dw), lambda i, nv: (i, 0))],
        out_specs=pl.BlockSpec(memory_space=pl.ANY),
        scratch_shapes=[pltpu.VMEM((tm, dw), jnp.int32), pltpu.SemaphoreType.DMA(())],
    )
    return pl.pallas_call(
        _moe_dispatch_kernel,
        grid_spec=grid_spec,
        out_shape=jax.ShapeDtypeStruct((tile_rows.shape[0] * tm, dw), jnp.int32),
        compiler_params=_cparams(("arbitrary",)),
    )(tile_rows, pos3, hp)


def _moe_up_kernel(te_ref, nv_ref, xs_ref, wg_ref, wu_ref, a_ref, x_s):
    r = pl.program_id(0)
    half = xs_ref.shape[1]

    @pl.when(nv_ref[r] > 0)
    def _():
        @pl.when(pl.program_id(1) == 0)
        def _():
            lo, hi = _unpack_halves(xs_ref[...])
            x_s[:, :half] = lo.astype(bf16)
            x_s[:, half:] = hi.astype(bf16)

        x = x_s[...]
        gate = jnp.dot(x, wg_ref[...], preferred_element_type=f32)
        up = jnp.dot(x, wu_ref[...], preferred_element_type=f32)
        a_ref[...] = (_silu(gate) * up).astype(a_ref.dtype)

    @pl.when(nv_ref[r] == 0)
    def _():
        a_ref[...] = jnp.zeros_like(a_ref)


def _moe_down_kernel(te_ref, nv_ref, a_ref, wd_ref, y_ref):
    @pl.when(nv_ref[pl.program_id(0)] > 0)
    def _():
        y_ref[...] = _pack_halves(jnp.dot(a_ref[...], wd_ref[...], preferred_element_type=f32))

    @pl.when(nv_ref[pl.program_id(0)] == 0)
    def _():
        y_ref[...] = jnp.zeros_like(y_ref)


def _moe_group(xs, tile_expert, tile_rows, wg, wu, wd, tm):
    p, dw = xs.shape
    _, d, fe = wg.shape
    tf = _tile(fe, MOE_FF_TILE)
    wspec = pl.BlockSpec((None, d, tf), lambda r, f, te, nv: (te[r], 0, f))
    act = pl.pallas_call(
        _moe_up_kernel,
        grid_spec=pltpu.PrefetchScalarGridSpec(
            num_scalar_prefetch=2,
            grid=(p // tm, fe // tf),
            in_specs=[pl.BlockSpec((tm, dw), lambda r, f, te, nv: (r, 0)), wspec, wspec],
            out_specs=pl.BlockSpec((tm, tf), lambda r, f, te, nv: (r, f)),
            scratch_shapes=[pltpu.VMEM((tm, d), bf16)],
        ),
        out_shape=jax.ShapeDtypeStruct((p, fe), bf16),
        compiler_params=_cparams(("arbitrary", "arbitrary")),
    )(tile_expert, tile_rows, xs, wg, wu)
    return pl.pallas_call(
        _moe_down_kernel,
        grid_spec=pltpu.PrefetchScalarGridSpec(
            num_scalar_prefetch=2,
            grid=(p // tm,),
            in_specs=[pl.BlockSpec((tm, fe), lambda r, te, nv: (r, 0)),
                      pl.BlockSpec((None, fe, d), lambda r, te, nv: (te[r], 0, 0))],
            out_specs=pl.BlockSpec((tm, dw), lambda r, te, nv: (r, 0)),
        ),
        out_shape=jax.ShapeDtypeStruct((p, dw), jnp.int32),
        compiler_params=_cparams(("arbitrary",)),
    )(tile_expert, tile_rows, act, wd)


def _moe_combine_kernel(pos_ref, x_ref, w_ref, g_ref, y_ref, o_ref, ybuf, sem, *, final_norm):
    tr, d = x_ref.shape
    half = d // 2
    _all_rows(tr, pos_ref, y_ref, ybuf, sem, gather=True)
    w = w_ref[...]
    lo0, hi0 = _unpack_halves(ybuf[0])
    lo1, hi1 = _unpack_halves(ybuf[1])
    o_lo = x_ref[:, :half] + (w[:, 0:1] * lo0 + w[:, 1:2] * lo1)
    o_hi = x_ref[:, half:] + (w[:, 0:1] * hi0 + w[:, 1:2] * hi1)
    if final_norm:
        ms = (jnp.sum(o_lo * o_lo, axis=1, keepdims=True) + jnp.sum(o_hi * o_hi, axis=1, keepdims=True)) / d
        inv = lax.rsqrt(ms + EPS)
        o_lo = o_lo * inv * g_ref[:, :half]
        o_hi = o_hi * inv * g_ref[:, half:]
    o_ref[:, :half] = o_lo
    o_ref[:, half:] = o_hi


def _moe_combine(x, w, y, pos3, gain, final_norm, row_start, n_rows):
    d = x.shape[1]
    tr = pos3.shape[2]
    off = row_start // tr
    return pl.pallas_call(
        functools.partial(_moe_combine_kernel, final_norm=final_norm),
        grid=(n_rows // tr,),
        in_specs=[pl.BlockSpec((None, TOP_K, tr), lambda i: (i + off, 0, 0), memory_space=pltpu.SMEM),
                  pl.BlockSpec((tr, d), lambda i: (i + off, 0)),
                  pl.BlockSpec((tr, TOP_K), lambda i: (i + off, 0)),
                  pl.BlockSpec((1, d), lambda i: (0, 0)),
                  pl.BlockSpec(memory_space=pl.ANY)],
        out_specs=pl.BlockSpec((tr, d), lambda i: (i, 0)),
        out_shape=jax.ShapeDtypeStruct((n_rows, d), f32),
        scratch_shapes=[pltpu.VMEM((TOP_K, tr, d // 2), jnp.int32), pltpu.SemaphoreType.DMA(())],
        compiler_params=_cparams(("arbitrary",)),
    )(pos3, x, w, gain.reshape(1, d), y)


def _moe_layer(x, g, router, wg, wu, wd, final_gain, parts):
    n, d = x.shape
    n_exp = router.shape[1]
    tm = MOE_ROW_TILE
    hp, eid, rank, w, cnt = _router(x, g, router)
    cnt = cnt[0, :n_exp]
    padded = (cnt + tm - 1) // tm * tm
    seg_end = jnp.cumsum(padded)
    seg_start = seg_end - padded
    pos = seg_start[eid] + rank
    n_rows_padded = TOP_K * n + n_exp * tm
    tile0 = jnp.arange(n_rows_padded // tm, dtype=jnp.int32) * tm
    tile_expert = jnp.minimum(jnp.searchsorted(seg_end, tile0, side="right"), n_exp - 1).astype(jnp.int32)
    tile_rows = jnp.clip(cnt[tile_expert] - (tile0 - seg_start[tile_expert]), 0, tm).astype(jnp.int32)
    tr = _tile(n, MOE_TOKEN_TILE)
    pos3 = pos.reshape(n // tr, tr, TOP_K).transpose(0, 2, 1)
    xs = _moe_dispatch(hp, pos3, tile_rows, tm)
    y = _moe_group(xs, tile_expert, tile_rows, wg, wu, wd, tm)
    if final_gain is None:
        return _moe_combine(x, w, y, pos3, g, False, 0, n)
    return [_moe_combine(x, w, y, pos3, final_gain, True, start, rows) for start, rows in parts]


def kernel(x_prompt, x_sample, norm_mix, norm_ffn, norm_final, pool_w, pool_scale, hgrn_w_in, hgrn_lb,
           hgrn_norm, hgrn_w_out, ffn_w_gate, ffn_w_up, ffn_w_down, moe_router, moe_w_gate, moe_w_up,
           moe_w_down):
    bp, sp, d = x_prompt.shape
    bs, ss, _ = x_sample.shape
    n_prompt = bp * sp
    n = n_prompt + bs * ss
    seq = (n_prompt, sp, ss)
    depth = norm_mix.shape[0]
    chunk = _tile(min(sp, ss), 128)

    parts = ((0, n_prompt), (n_prompt, bs * ss))
    streams = (x_prompt.reshape(n_prompt, d), x_sample.reshape(bs * ss, d))
    x = None if depth else jnp.concatenate(streams, axis=0)
    outs = None

    for i in range(depth):
        j = i // 2
        if i % 2 == 0:
            xa, xb = streams if x is None else (x[:n_prompt], x[n_prompt:])
            x, h = _pool_layer(xa, xb, norm_mix[i], pool_w[j].astype(bf16), pool_scale[j], norm_ffn[i], seq)
            act = _glu_up(h, ffn_w_gate[j].astype(bf16)[None], ffn_w_up[j].astype(bf16)[None])
            x = _down_res(act, ffn_w_down[j].astype(bf16), x, 4096)
        else:
            h = _rmsnorm(x, norm_mix[i], bf16)
            w_in = hgrn_w_in[j].astype(bf16)
            q = _proj_section(_proj_silu_heads_kernel, h, w_in, 0, True)
            forget = functools.partial(_proj_forget_kernel, layer=i)
            kf, gf = _proj_section(forget, h, w_in, 1, True, 2, hgrn_lb)
            kb, gb = _proj_section(forget, h, w_in, 2, True, 2, hgrn_lb)
            v = _proj_section(_proj_heads_kernel, h, w_in, 3, True)
            gate = _proj_section(_proj_silu_kernel, h, w_in, 4, False)
            s_bw = _scan_bw_states(kb, gb, v, seq, chunk)
            og = _scan(q, kf, kb, gf, gb, v, s_bw, gate, hgrn_norm[j], seq, chunk)
            x = _down_res(og, hgrn_w_out[j].astype(bf16), x, 4096)
            last = i == depth - 1
            res = _moe_layer(x, norm_ffn[i], moe_router[j], moe_w_gate[j].astype(bf16),
                             moe_w_up[j].astype(bf16), moe_w_down[j].astype(bf16),
                             norm_final if last else None, parts)
            x, outs = (None, res) if last else (res, None)

    if outs is None:
        outs = [_rmsnorm(x, norm_final, f32, start, rows) for start, rows in parts]
    return (outs[0].reshape(bp, sp, d), outs[1].reshape(bs, ss, d))
```

```python
import functools

import jax
import jax.numpy as jnp
from jax import lax
from jax.experimental import pallas as pl
from jax.experimental.pallas import tpu as pltpu

EPS = 1e-6
POOL_WINDOWS = (2, 4, 8, 16)
POOL_HALO = 8
POOL_PAD = 16
POOL_BLOCK = 128
HEAD_DIM = 128
TOP_K = 2
V7X_VMEM_LIMIT = 56 * 1024 * 1024
LANES = 128
MOE_ROW_TILE = 512
MOE_FF_TILE = 512
MOE_DISPATCH_TILE = 512
MOE_COMBINE_TILE = 256
ROW_DMA_UNROLL = 8
EPILOGUE_COLS = 256

bf16 = jnp.bfloat16
f32 = jnp.float32


def _cparams(sem):
    return pltpu.CompilerParams(dimension_semantics=sem, vmem_limit_bytes=V7X_VMEM_LIMIT)


def _tile(n, pref):
    t = min(n, pref)
    while n % t:
        t //= 2
    return t


def _rms(x, g):
    ms = jnp.mean(x * x, axis=-1, keepdims=True)
    return x * lax.rsqrt(ms + EPS) * g


def _silu(x):
    return x * jax.nn.sigmoid(x)


def _seq_flags(tok, n_tok, n_prompt, seq_p, seq_s):
    in_p = tok < n_prompt
    rel = jnp.where(in_p, tok, tok - n_prompt)
    slen = jnp.where(in_p, seq_p, seq_s)
    is_start = (rel % slen) == 0
    is_end = ((rel + n_tok) % slen) == 0
    return is_start, is_end


def _rmsnorm_kernel(x_ref, g_ref, o_ref):
    o_ref[...] = _rms(x_ref[...], g_ref[...]).astype(o_ref.dtype)


def _rmsnorm(x, g, out_dtype, row_start=0, n_rows=None):
    n, d = x.shape
    n_rows = n if n_rows is None else n_rows
    tr = _tile(n_rows, 512)
    off = row_start // tr
    return pl.pallas_call(
        _rmsnorm_kernel,
        grid=(n_rows // tr,),
        in_specs=[pl.BlockSpec((tr, d), lambda i: (i + off, 0)),
                  pl.BlockSpec((1, d), lambda i: (0, 0))],
        out_specs=pl.BlockSpec((tr, d), lambda i: (i, 0)),
        out_shape=jax.ShapeDtypeStruct((n_rows, d), out_dtype),
        compiler_params=_cparams(("parallel",)),
    )(x, g.reshape(1, d))


def _pool_kernel(ap_ref, a_ref, an_ref, bp_ref, b_ref, bn_ref, gmix_ref, w_ref, scale_ref, gffn_ref,
                 x1_ref, h1_ref, hi_s, lo_s, *, seq):
    t, d = a_ref.shape
    dg = d // len(POOL_WINDOWS)
    tok = pl.program_id(0) * t
    is_start, is_end = _seq_flags(tok, t, *seq)
    in_a = tok < seq[0]
    g = gmix_ref[...]
    x = jnp.where(in_a, a_ref[...], b_ref[...])
    h = _rms(x, g)
    h_prev = jnp.where(is_start, 0.0, _rms(jnp.where(in_a, ap_ref[...], bp_ref[...]), g))
    h_next = jnp.where(is_end, 0.0, _rms(jnp.where(in_a, an_ref[...], bn_ref[...]), g))
    zeros = jnp.zeros_like(h_prev)

    def put(r0, val):
        hi = val.astype(bf16)
        hi_s[r0:r0 + val.shape[0], :] = hi
        lo_s[r0:r0 + val.shape[0], :] = (val - hi.astype(f32)).astype(bf16)

    put(0, jnp.concatenate([zeros, h_prev], axis=0))
    put(POOL_PAD, h)
    put(POOL_PAD + t, jnp.concatenate([h_next, zeros], axis=0))
    tail = hi_s.shape[0] - (2 * POOL_PAD + t)
    hi_s[2 * POOL_PAD + t:, :] = jnp.zeros((tail, d), bf16)
    lo_s[2 * POOL_PAD + t:, :] = jnp.zeros((tail, d), bf16)

    rr = lax.broadcasted_iota(jnp.int32, (POOL_BLOCK, 2 * POOL_BLOCK), 0)
    cc = lax.broadcasted_iota(jnp.int32, (POOL_BLOCK, 2 * POOL_BLOCK), 1)
    r = lax.broadcasted_iota(jnp.int32, (t, 1), 0)
    for gi, w in enumerate(POOL_WINDOWS):
        c0 = gi * dg
        half = w // 2
        band = ((cc >= POOL_PAD + rr - half) & (cc < POOL_PAD + rr + half)).astype(bf16)
        sums = []
        for r0 in range(0, t, POOL_BLOCK):
            sums.append(jnp.dot(band, hi_s[r0:r0 + 2 * POOL_BLOCK, c0:c0 + dg], preferred_element_type=f32)
                        + jnp.dot(band, lo_s[r0:r0 + 2 * POOL_BLOCK, c0:c0 + dg], preferred_element_type=f32))
        s = jnp.concatenate(sums, axis=0)
        lo = jnp.where(is_start, jnp.maximum(r - half, 0), r - half)
        hi = jnp.where(is_end, jnp.minimum(r + half, t), r + half)
        cnt = (hi - lo).astype(f32)
        dlt = s / cnt - h[:, c0:c0 + dg]
        y = jnp.dot(dlt.astype(bf16), w_ref[gi], preferred_element_type=f32)
        x1_ref[:, c0:c0 + dg] = x[:, c0:c0 + dg] + y * scale_ref[:, c0:c0 + dg]
    h1_ref[...] = _rms(x1_ref[...], gffn_ref[...]).astype(h1_ref.dtype)


def _stream_specs(rows, first_tile, t, d):
    hb = t // POOL_HALO
    last_halo = rows // POOL_HALO - 1
    last_tile = rows // t - 1
    return [
        pl.BlockSpec((POOL_HALO, d), lambda i: (jnp.clip((i - first_tile) * hb - 1, 0, last_halo), 0)),
        pl.BlockSpec((t, d), lambda i: (jnp.clip(i - first_tile, 0, last_tile), 0)),
        pl.BlockSpec((POOL_HALO, d), lambda i: (jnp.clip((i - first_tile + 1) * hb, 0, last_halo), 0)),
    ]


def _pool_layer(xa, xb, g_mix, w_grp, scale, g_ffn, seq):
    d = xa.shape[1]
    n = xa.shape[0] + xb.shape[0]
    t = _tile(min(seq[1], seq[2]), 256)
    row = lambda i: (0, 0)
    return pl.pallas_call(
        functools.partial(_pool_kernel, seq=seq),
        grid=(n // t,),
        in_specs=_stream_specs(xa.shape[0], 0, t, d) + _stream_specs(xb.shape[0], xa.shape[0] // t, t, d) + [
            pl.BlockSpec((1, d), row),
            pl.BlockSpec(w_grp.shape, lambda i: (0, 0, 0)),
            pl.BlockSpec((1, d), row),
            pl.BlockSpec((1, d), row),
        ],
        out_specs=[pl.BlockSpec((t, d), lambda i: (i, 0)),
                   pl.BlockSpec((t, d), lambda i: (i, 0))],
        out_shape=[jax.ShapeDtypeStruct((n, d), f32), jax.ShapeDtypeStruct((n, d), bf16)],
        scratch_shapes=[pltpu.VMEM((t + POOL_BLOCK, d), bf16)] * 2,
        compiler_params=_cparams(("parallel",)),
    )(xa, xa, xa, xb, xb, xb, g_mix.reshape(1, d), w_grp, scale.reshape(1, d), g_ffn.reshape(1, d))


def _glu_up_kernel(a_ref, wg_ref, wu_ref, o_ref):
    a = a_ref[...]
    gate = jnp.dot(a, wg_ref[...], preferred_element_type=f32)
    up = jnp.dot(a, wu_ref[...], preferred_element_type=f32)
    o_ref[...] = (_silu(gate) * up).astype(o_ref.dtype)


def _glu_up(a, wg, wu):
    m, k = a.shape
    e, _, f = wg.shape
    tm = _tile(m, 1024)
    tn = _tile(f, 512)
    nb = f // tn
    wspec = pl.BlockSpec((None, k, tn), lambda i, j: (j // nb, 0, j % nb))
    return pl.pallas_call(
        _glu_up_kernel,
        grid=(m // tm, e * nb),
        in_specs=[pl.BlockSpec((tm, k), lambda i, j: (i, 0)), wspec, wspec],
        out_specs=pl.BlockSpec((tm, tn), lambda i, j: (i, j)),
        out_shape=jax.ShapeDtypeStruct((m, e * f), bf16),
        compiler_params=_cparams(("parallel", "arbitrary")),
    )(a, wg, wu)


def _down_kernel(a_ref, w_ref, res_ref, o_ref, acc_ref):
    k = pl.program_id(2)

    @pl.when(k == 0)
    def _():
        acc_ref[...] = jnp.zeros_like(acc_ref)

    acc_ref[...] += jnp.dot(a_ref[...], w_ref[...], preferred_element_type=f32)

    @pl.when(k == pl.num_programs(2) - 1)
    def _():
        o_ref[...] = res_ref[...] + acc_ref[...]


def _down1_kernel(a_ref, w_ref, res_ref, o_ref):
    o_ref[...] = res_ref[...] + jnp.dot(a_ref[...], w_ref[...], preferred_element_type=f32)


def _down_res(a, w, res, tk, tm=1024):
    m, kk = a.shape
    n = w.shape[1]
    tm = _tile(m, tm)
    tn = _tile(n, 512)
    tk = _tile(kk, tk)
    if tk == kk:
        return pl.pallas_call(
            _down1_kernel,
            grid=(m // tm, n // tn),
            in_specs=[pl.BlockSpec((tm, kk), lambda i, j: (i, 0)),
                      pl.BlockSpec((kk, tn), lambda i, j: (0, j)),
                      pl.BlockSpec((tm, tn), lambda i, j: (i, j))],
            out_specs=pl.BlockSpec((tm, tn), lambda i, j: (i, j)),
            out_shape=jax.ShapeDtypeStruct((m, n), f32),
            compiler_params=_cparams(("parallel", "arbitrary")),
        )(a, w, res)
    return pl.pallas_call(
        _down_kernel,
        grid=(m // tm, n // tn, kk // tk),
        in_specs=[pl.BlockSpec((tm, tk), lambda i, j, k: (i, k)),
                  pl.BlockSpec((tk, tn), lambda i, j, k: (k, j)),
                  pl.BlockSpec((tm, tn), lambda i, j, k: (i, j))],
        out_specs=pl.BlockSpec((tm, tn), lambda i, j, k: (i, j)),
        out_shape=jax.ShapeDtypeStruct((m, n), f32),
        scratch_shapes=[pltpu.VMEM((tm, tn), f32)],
        compiler_params=_cparams(("parallel", "parallel", "arbitrary")),
    )(a, w, res)


def _column_blocks(a_ref, w_ref):
    a = a_ref[...]
    tn = w_ref.shape[1]
    cb = min(tn, EPILOGUE_COLS)
    for c0 in range(0, tn, cb):
        yield c0, jnp.dot(a, w_ref[:, c0:c0 + cb], preferred_element_type=f32)


def _store_head_block(o_ref, c0, val):
    for hh in range(val.shape[1] // HEAD_DIM):
        o_ref[c0 // HEAD_DIM + hh] = val[:, hh * HEAD_DIM:(hh + 1) * HEAD_DIM].astype(o_ref.dtype)


def _proj_silu_heads_kernel(a_ref, w_ref, o_ref):
    for c0, r in _column_blocks(a_ref, w_ref):
        _store_head_block(o_ref, c0, _silu(r))


def _proj_heads_kernel(a_ref, w_ref, o_ref):
    for c0, r in _column_blocks(a_ref, w_ref):
        _store_head_block(o_ref, c0, r)


def _proj_silu_kernel(a_ref, w_ref, o_ref):
    for c0, r in _column_blocks(a_ref, w_ref):
        o_ref[:, c0:c0 + r.shape[1]] = _silu(r).astype(o_ref.dtype)


def _proj_forget_kernel(a_ref, w_ref, lb_ref, k_ref, g_ref, *, layer):
    logits = lb_ref[...]
    ex = jnp.exp(logits - jnp.max(logits, axis=0, keepdims=True))
    sm = ex / jnp.sum(ex, axis=0, keepdims=True)
    lb_all = jnp.sum(sm[0:layer + 1], axis=0, keepdims=True) - sm[0:1]
    for c0, r in _column_blocks(a_ref, w_ref):
        lb = lb_all[:, c0:c0 + r.shape[1]]
        f = lb + (1.0 - lb) * jax.nn.sigmoid(r)
        _store_head_block(k_ref, c0, 1.0 - f)
        _store_head_block(g_ref, c0, jnp.log2(f))


def _proj_section(kernel, a, w, section, out_heads, n_out=1, extra=None):
    m, k = a.shape
    d = k
    tm = _tile(m, 1024)
    tn = _tile(d, 1024)
    nb = d // tn
    in_specs = [pl.BlockSpec((tm, k), lambda i, j: (i, 0)),
                pl.BlockSpec((k, tn), lambda i, j: (0, section * nb + j))]
    args = [a, w]
    if extra is not None:
        in_specs.append(pl.BlockSpec((extra.shape[0], tn), lambda i, j: (0, j)))
        args.append(extra)
    if out_heads:
        hpb = tn // HEAD_DIM
        ospec = pl.BlockSpec((hpb, tm, HEAD_DIM), lambda i, j: (j, i, 0))
        oshape = jax.ShapeDtypeStruct((d // HEAD_DIM, m, HEAD_DIM), bf16)
    else:
        ospec = pl.BlockSpec((tm, tn), lambda i, j: (i, j))
        oshape = jax.ShapeDtypeStruct((m, d), bf16)
    out = pl.pallas_call(
        kernel,
        grid=(m // tm, nb),
        in_specs=in_specs,
        out_specs=[ospec] * n_out if n_out > 1 else ospec,
        out_shape=[oshape] * n_out if n_out > 1 else oshape,
        compiler_params=_cparams(("parallel", "arbitrary")),
    )(*args)
    return out


MAX_DIRECT_LOG2_DECAY = 100.0
HEAD_UNROLL = 8


def _dot_nt(a, b):
    return lax.dot_general(a, b, (((1,), (1,)), ((), ())), preferred_element_type=f32)


def _dot_tn(a, b):
    return lax.dot_general(a, b, (((0,), (0,)), ((), ())), preferred_element_type=f32)


def _mid_ref(p, level, offset):
    c, w = p.shape
    blk = 2 << level
    p3 = p.reshape(c // blk, blk, w)
    return jnp.broadcast_to(p3[:, offset:offset + 1, :], p3.shape).reshape(c, w)


def _neg_abs_exponents(gf, gb, b, x, level, rmod4):
    if level == 0:
        odd = (rmod4 & 1) == 1
        return jnp.where(odd, gf, 0.0), jnp.where(odd, 0.0, gb)
    if level == 1:
        c = gf.shape[0]
        gf_up = pltpu.roll(gf, c - 1, 0)
        gf_dn = pltpu.roll(gf, 1, 0)
        gb_up = pltpu.roll(gb, c - 1, 0)
        gb_dn = pltpu.roll(gb, 1, 0)
        ef = jnp.where(rmod4 == 0, gf_up, jnp.where(rmod4 == 1, 0.0, jnp.where(rmod4 == 2, gf, gf + gf_dn)))
        eb = jnp.where(rmod4 == 0, gb + gb_up, jnp.where(rmod4 == 1, gb, jnp.where(rmod4 == 2, 0.0, gb_dn)))
        return ef, eb
    half = 1 << level
    ef = b - _mid_ref(b, level, half - 1)
    eb = x - _mid_ref(x, level, half)
    return -jnp.abs(ef), -jnp.abs(eb)


def _chunk_prefix(tri_incl, tri_excl, gf16, gb16):
    b = jnp.dot(tri_incl, gf16, preferred_element_type=f32)
    x = jnp.dot(tri_excl, gb16, preferred_element_type=f32)
    return b, x


def _scan_bw_kernel(kb_ref, gb_ref, v_ref, s_ref, st_ref, *, seq, n_chunks):
    n_heads, c, _ = kb_ref.shape
    chunk = n_chunks - 1 - pl.program_id(0)
    _, is_end = _seq_flags(chunk * c, c, *seq)

    @pl.when(is_end)
    def _():
        st_ref[...] = jnp.zeros_like(st_ref)

    row = lax.broadcasted_iota(jnp.int32, (c, c), 0)
    col = lax.broadcasted_iota(jnp.int32, (c, c), 1)
    tri_excl = (col < row).astype(bf16)

    def head(h, carry):
        gb16 = gb_ref[h]
        x = jnp.dot(tri_excl, gb16, preferred_element_type=f32)
        tot = x[c - 1:c, :] + gb16[c - 1:c, :].astype(f32)
        st = st_ref[h]
        s_ref[h] = st.astype(s_ref.dtype)
        kx = (kb_ref[h].astype(f32) * jnp.exp2(x)).astype(bf16)
        st_ref[h] = st * jnp.exp2(tot) + _dot_tn(v_ref[h], kx)
        return carry

    lax.fori_loop(0, n_heads, head, 0, unroll=HEAD_UNROLL)


def _scan_bw_states(kb, gb, v, seq, c):
    n_heads, n, dk = kb.shape
    n_chunks = n // c
    spec = pl.BlockSpec((n_heads, c, dk), lambda i: (0, n_chunks - 1 - i, 0))
    return pl.pallas_call(
        functools.partial(_scan_bw_kernel, seq=seq, n_chunks=n_chunks),
        grid=(n_chunks,),
        in_specs=[spec, spec, spec],
        out_specs=pl.BlockSpec((None, n_heads, dk, dk), lambda i: (n_chunks - 1 - i, 0, 0, 0)),
        out_shape=jax.ShapeDtypeStruct((n_chunks, n_heads, dk, dk), bf16),
        scratch_shapes=[pltpu.VMEM((n_heads, dk, dk), f32)],
        compiler_params=_cparams(("arbitrary",)),
    )(kb, gb, v)


def _scan_kernel(q_ref, kf_ref, kb_ref, gf_ref, gb_ref, v_ref, sb_ref, tril_ref, triu_ref, gate_ref,
                 gain_ref, out_ref, st_ref, o_s, qf_s, kf_s, qb_s, kb_s, qin_s, kx_s, sc_s, dec_s, tot_s,
                 *, seq):
    n_heads, c, dk = q_ref.shape
    n_levels = c.bit_length() - 1
    mid = c // 2
    is_start, _ = _seq_flags(pl.program_id(0) * c, c, *seq)

    @pl.when(is_start)
    def _():
        st_ref[...] = jnp.zeros_like(st_ref)

    row = lax.broadcasted_iota(jnp.int32, (c, c), 0)
    col = lax.broadcasted_iota(jnp.int32, (c, c), 1)
    tri_incl = (col <= row).astype(bf16)
    tri_excl = (col < row).astype(bf16)

    def finish(h, kf_dec, b_last, q_in, scores, diag):
        v = v_ref[h]
        st_f = st_ref[h]
        st_cat = jnp.concatenate([st_f.astype(bf16), sb_ref[h]], axis=1)
        o = _dot_nt(q_in, st_cat) + jnp.dot(scores.astype(bf16), v, preferred_element_type=f32)
        o_s[h] = o if diag is None else o + diag * v.astype(f32)
        st_ref[h] = st_f * jnp.exp2(b_last) + _dot_tn(v, kf_dec.astype(bf16))

    def direct_operands(h, carry):
        q = q_ref[h].astype(f32)
        gb16 = gb_ref[h]
        b, x = _chunk_prefix(tri_incl, tri_excl, gf_ref[h], gb16)
        b_mid = b[mid - 1:mid, :]
        x_mid = x[mid:mid + 1, :]
        db = b - b_mid
        dx = x - x_mid
        q_f = q * jnp.exp2(db)
        k_f = kf_ref[h].astype(f32) * jnp.exp2(-db)
        q_b = q * jnp.exp2(-dx)
        k_b = kb_ref[h].astype(f32) * jnp.exp2(dx)
        b_last = b[c - 1:c, :]
        x_tot = x[c - 1:c, :] + gb16[c - 1:c, :].astype(f32)
        qf_s[h] = q_f.astype(bf16)
        kf_s[h] = k_f.astype(bf16)
        qb_s[h] = q_b.astype(bf16)
        kb_s[h] = k_b.astype(bf16)
        qin_s[h] = jnp.concatenate([q_f * jnp.exp2(b_mid), q_b * jnp.exp2(x_tot - x_mid)], axis=1).astype(bf16)
        kx_s[h] = (k_f * jnp.exp2(b_last - b_mid)).astype(bf16)
        dec_s[h] = jnp.broadcast_to(jnp.exp2(b_last), dec_s.shape[1:])
        tot_s[h] = jnp.broadcast_to(jnp.minimum(b_last, x_tot), tot_s.shape[1:])
        return carry

    def direct_scores(h, carry):
        sc_s[h] = (_dot_nt(qf_s[h], kf_s[h]) * tril_ref[...]
                   + _dot_nt(qb_s[h], kb_s[h]) * triu_ref[...]).astype(bf16)
        return carry

    def direct_output(h, carry):
        v = v_ref[h]
        st_f = st_ref[h]
        st_cat = jnp.concatenate([st_f.astype(bf16), sb_ref[h]], axis=1)
        o_s[h] = _dot_nt(qin_s[h], st_cat) + jnp.dot(sc_s[h], v, preferred_element_type=f32)
        st_ref[h] = st_f * dec_s[h][0:1, :] + _dot_tn(v, kx_s[h])
        return carry

    def head_levelled(h, carry):
        rtok = lax.broadcasted_iota(jnp.int32, (c, dk), 0)
        rmod4 = rtok & 3
        q = q_ref[h].astype(f32)
        kf = kf_ref[h].astype(f32)
        kb = kb_ref[h].astype(f32)
        gf16 = gf_ref[h]
        gb16 = gb_ref[h]
        gf = gf16.astype(f32)
        gb = gb16.astype(f32)
        b, x = _chunk_prefix(tri_incl, tri_excl, gf16, gb16)
        scores = jnp.zeros((c, c), f32)
        for l in range(n_levels):
            ef, eb = _neg_abs_exponents(gf, gb, b, x, l, rmod4)
            af = jnp.exp2(ef)
            ab = jnp.exp2(eb)
            up = ((rtok >> l) & 1) == 1
            lhs = jnp.concatenate([jnp.where(up, q * af, 0.0), jnp.where(up, 0.0, q * ab)], axis=1)
            rhs = jnp.concatenate([jnp.where(up, 0.0, kf * af), jnp.where(up, kb * ab, 0.0)], axis=1)
            p = _dot_nt(lhs.astype(bf16), rhs.astype(bf16))
            if l < n_levels - 1:
                p = jnp.where((row >> (l + 1)) == (col >> (l + 1)), p, 0.0)
            scores = scores + p
        b_last = b[c - 1:c, :]
        x_tot = x[c - 1:c, :] + gb[c - 1:c, :]
        q_in = jnp.concatenate([q * jnp.exp2(b), q * jnp.exp2(x_tot - x)], axis=1).astype(bf16)
        diag = jnp.sum(q * (kf + kb), axis=1, keepdims=True)
        finish(h, kf * jnp.exp2(b_last - b), b_last, q_in, scores, diag)
        return carry

    lax.fori_loop(0, n_heads, direct_operands, 0, unroll=HEAD_UNROLL)
    direct_ok = jnp.min(tot_s[...]) >= -MAX_DIRECT_LOG2_DECAY

    @pl.when(direct_ok)
    def _():
        lax.fori_loop(0, n_heads, direct_scores, 0, unroll=HEAD_UNROLL)
        lax.fori_loop(0, n_heads, direct_output, 0, unroll=HEAD_UNROLL)

    @pl.when(jnp.logical_not(direct_ok))
    def _():
        lax.fori_loop(0, n_heads, head_levelled, 0)

    sq = jnp.zeros((c, dk), f32)
    for hh in range(n_heads):
        o = o_s[hh]
        sq = sq + o * o
    inv = lax.rsqrt(jnp.sum(sq, axis=1, keepdims=True) / (n_heads * dk) + EPS)
    for hh in range(n_heads):
        cols = slice(hh * dk, (hh + 1) * dk)
        out_ref[:, cols] = (o_s[hh] * inv * gain_ref[:, cols] * gate_ref[:, cols].astype(f32)
                            ).astype(out_ref.dtype)


def _scan(q, kf, kb, gf, gb, v, s_bw, gate, gain, seq, c):
    n_heads, n, dk = q.shape
    d = n_heads * dk
    n_chunks = n // c
    spec = pl.BlockSpec((n_heads, c, dk), lambda i: (0, i, 0))
    tri = pl.BlockSpec((c, c), lambda i: (0, 0))
    tril = jnp.tril(jnp.ones((c, c), f32))
    return pl.pallas_call(
        functools.partial(_scan_kernel, seq=seq),
        grid=(n_chunks,),
        in_specs=[spec] * 6 + [pl.BlockSpec((None, n_heads, dk, dk), lambda i: (i, 0, 0, 0)), tri, tri,
                               pl.BlockSpec((c, d), lambda i: (i, 0)),
                               pl.BlockSpec((1, d), lambda i: (0, 0))],
        out_specs=pl.BlockSpec((c, d), lambda i: (i, 0)),
        out_shape=jax.ShapeDtypeStruct((n, d), bf16),
        scratch_shapes=[pltpu.VMEM((n_heads, dk, dk), f32), pltpu.VMEM((n_heads, c, dk), f32)]
        + [pltpu.VMEM((n_heads, c, dk), bf16)] * 4
        + [pltpu.VMEM((n_heads, c, 2 * dk), bf16), pltpu.VMEM((n_heads, c, dk), bf16),
           pltpu.VMEM((n_heads, c, c), bf16)] + [pltpu.VMEM((n_heads, 8, dk), f32)] * 2,
        compiler_params=_cparams(("arbitrary",)),
    )(q, kf, kb, gf, gb, v, s_bw, tril, tril.T, gate, gain.reshape(1, d))


def _pack_halves(x):
    half = x.shape[1] // 2
    bits = lax.bitcast_convert_type(x.astype(bf16).astype(f32), jnp.int32)
    return lax.shift_right_logical(bits[:, :half], 16) | (bits[:, half:] & jnp.int32(-65536))


def _unpack_halves(words):
    lo = lax.bitcast_convert_type(lax.shift_left(words, 16), f32)
    hi = lax.bitcast_convert_type(words & jnp.int32(-65536), f32)
    return lo, hi


def _router_kernel(x_ref, g_ref, rhi_ref, rlo_ref, h_ref, e_ref, rank_ref, w_ref, cnt_ref, run_ref,
                   *, n_experts):
    @pl.when(pl.program_id(0) == 0)
    def _():
        run_ref[...] = jnp.zeros_like(run_ref)

    h = _rms(x_ref[...], g_ref[...])
    h_ref[...] = _pack_halves(h)
    hi = h.astype(bf16)
    lo = (h - hi.astype(f32)).astype(bf16)
    rhi = rhi_ref[...]
    logits = (jnp.dot(hi, rhi, preferred_element_type=f32)
              + jnp.dot(hi, rlo_ref[...], preferred_element_type=f32)
              + jnp.dot(lo, rhi, preferred_element_type=f32))
    tr = logits.shape[0]
    lane = lax.broadcasted_iota(jnp.int32, logits.shape, 1)
    neg = jnp.float32(-jnp.inf)
    logits = jnp.where(lane < n_experts, logits, neg)
    m1 = jnp.max(logits, axis=1, keepdims=True)
    i1 = jnp.min(jnp.where(logits == m1, lane, LANES), axis=1, keepdims=True)
    rest = jnp.where(lane == i1, neg, logits)
    m2 = jnp.max(rest, axis=1, keepdims=True)
    i2 = jnp.min(jnp.where(rest == m2, lane, LANES), axis=1, keepdims=True)
    e2 = jnp.exp(m2 - m1)
    w_ref[...] = jnp.concatenate([1.0 / (1.0 + e2), e2 / (1.0 + e2)], axis=1)
    e_ref[...] = jnp.concatenate([i1, i2], axis=1)
    member = jnp.where((lane == i1) | (lane == i2), 1.0, 0.0)
    row = lax.broadcasted_iota(jnp.int32, (tr, tr), 0)
    col = lax.broadcasted_iota(jnp.int32, (tr, tr), 1)
    before = jnp.dot((col < row).astype(bf16), member.astype(bf16), preferred_element_type=f32)
    before = before + run_ref[...]
    r1 = jnp.sum(jnp.where(lane == i1, before, 0.0), axis=1, keepdims=True)
    r2 = jnp.sum(jnp.where(lane == i2, before, 0.0), axis=1, keepdims=True)
    rank_ref[...] = jnp.concatenate([r1, r2], axis=1).astype(jnp.int32)
    run = run_ref[...] + jnp.sum(member, axis=0, keepdims=True)
    run_ref[...] = run
    cnt_ref[...] = run.astype(jnp.int32)


def _router(x, g, router):
    n, d = x.shape
    e = router.shape[1]
    rpad = jnp.zeros((d, LANES), f32).at[:, :e].set(router)
    rhi = rpad.astype(bf16)
    rlo = (rpad - rhi.astype(f32)).astype(bf16)
    tr = _tile(n, 512)
    pair = pl.BlockSpec((tr, TOP_K), lambda i: (i, 0))
    return pl.pallas_call(
        functools.partial(_router_kernel, n_experts=e),
        grid=(n // tr,),
        in_specs=[pl.BlockSpec((tr, d), lambda i: (i, 0)),
                  pl.BlockSpec((1, d), lambda i: (0, 0)),
                  pl.BlockSpec((d, LANES), lambda i: (0, 0)),
                  pl.BlockSpec((d, LANES), lambda i: (0, 0))],
        out_specs=[pl.BlockSpec((tr, d // 2), lambda i: (i, 0)), pair, pair, pair,
                   pl.BlockSpec((1, LANES), lambda i: (0, 0))],
        out_shape=[jax.ShapeDtypeStruct((n, d // 2), jnp.int32),
                   jax.ShapeDtypeStruct((n, TOP_K), jnp.int32),
                   jax.ShapeDtypeStruct((n, TOP_K), jnp.int32),
                   jax.ShapeDtypeStruct((n, TOP_K), f32),
                   jax.ShapeDtypeStruct((1, LANES), jnp.int32)],
        scratch_shapes=[pltpu.VMEM((1, LANES), f32)],
        compiler_params=_cparams(("arbitrary",)),
    )(x, g.reshape(1, d), rhi, rlo)


def _row_copies(pos_ref, i, src, dst, sem, gather):
    out = []
    for k in range(TOP_K):
        p = pos_ref[k, i]
        if gather:
            out.append(pltpu.make_async_copy(src.at[pl.ds(p, 1)], dst.at[k, pl.ds(i, 1)], sem))
        else:
            out.append(pltpu.make_async_copy(src.at[pl.ds(i, 1)], dst.at[pl.ds(p, 1)], sem))
    return out


def _start_rows(n_rows, pos_ref, src, dst, sem, gather):
    def start(i, carry):
        for cp in _row_copies(pos_ref, i, src, dst, sem, gather):
            cp.start()
        return carry

    lax.fori_loop(0, n_rows, start, 0, unroll=ROW_DMA_UNROLL)


def _wait_rows(n_rows, pos_ref, src, dst, sem, gather):
    def wait(i, carry):
        for cp in _row_copies(pos_ref, i, src, dst, sem, gather):
            cp.wait()
        return carry

    lax.fori_loop(0, n_rows, wait, 0, unroll=ROW_DMA_UNROLL)


def _all_rows(n_rows, pos_ref, src, dst, sem, gather):
    _start_rows(n_rows, pos_ref, src, dst, sem, gather)
    _wait_rows(n_rows, pos_ref, src, dst, sem, gather)


def _moe_dispatch_kernel(nv_ref, pos_ref, h_ref, xs_ref, zero_s, sem):
    tm = zero_s.shape[0]

    @pl.when(pl.program_id(0) == 0)
    def _():
        zero_s[...] = jnp.zeros_like(zero_s)

        def fill(r, carry):
            @pl.when(nv_ref[r] < tm)
            def _():
                cp = pltpu.make_async_copy(zero_s, xs_ref.at[pl.ds(pl.multiple_of(r * tm, tm), tm)], sem)
                cp.start()
                cp.wait()
            return carry

        lax.fori_loop(0, xs_ref.shape[0] // tm, fill, 0)

    _all_rows(h_ref.shape[0], pos_ref, h_ref, xs_ref, sem, gather=False)


def _moe_dispatch(hp, pos3, tile_rows, tm):
    n, dw = hp.shape
    n_tiles, _, tr = pos3.shape
    grid_spec = pltpu.PrefetchScalarGridSpec(
        num_scalar_prefetch=1,
        grid=(n_tiles,),
        in_specs=[pl.BlockSpec((None, TOP_K, tr), lambda i, nv: (i, 0, 0), memory_space=pltpu.SMEM),
                  pl.BlockSpec((tr, dw), lambda i, nv: (i, 0))],
        out_specs=pl.BlockSpec(memory_space=pl.ANY),
        scratch_shapes=[pltpu.VMEM((tm, dw), jnp.int32), pltpu.SemaphoreType.DMA(())],
    )
    return pl.pallas_call(
        _moe_dispatch_kernel,
        grid_spec=grid_spec,
        out_shape=jax.ShapeDtypeStruct((tile_rows.shape[0] * tm, dw), jnp.int32),
        compiler_params=_cparams(("arbitrary",)),
    )(tile_rows, pos3, hp)


def _moe_up_kernel(te_ref, nv_ref, xs_ref, wg_ref, wu_ref, a_ref, x_s):
    r = pl.program_id(0)
    half = xs_ref.shape[1]

    @pl.when(nv_ref[r] > 0)
    def _():
        @pl.when(pl.program_id(1) == 0)
        def _():
            lo, hi = _unpack_halves(xs_ref[...])
            x_s[:, :half] = lo.astype(bf16)
            x_s[:, half:] = hi.astype(bf16)

        x = x_s[...]
        gate = jnp.dot(x, wg_ref[...], preferred_element_type=f32)
        up = jnp.dot(x, wu_ref[...], preferred_element_type=f32)
        a_ref[...] = (_silu(gate) * up).astype(a_ref.dtype)

    @pl.when(nv_ref[r] == 0)
    def _():
        a_ref[...] = jnp.zeros_like(a_ref)


def _moe_down_kernel(te_ref, nv_ref, a_ref, wd_ref, y_ref):
    @pl.when(nv_ref[pl.program_id(0)] > 0)
    def _():
        y_ref[...] = _pack_halves(jnp.dot(a_ref[...], wd_ref[...], preferred_element_type=f32))

    @pl.when(nv_ref[pl.program_id(0)] == 0)
    def _():
        y_ref[...] = jnp.zeros_like(y_ref)


def _moe_group(xs, tile_expert, tile_rows, wg, wu, wd, tm):
    p, dw = xs.shape
    _, d, fe = wg.shape
    tf = _tile(fe, MOE_FF_TILE)
    wspec = pl.BlockSpec((None, d, tf), lambda r, f, te, nv: (te[r], 0, f))
    act = pl.pallas_call(
        _moe_up_kernel,
        grid_spec=pltpu.PrefetchScalarGridSpec(
            num_scalar_prefetch=2,
            grid=(p // tm, fe // tf),
            in_specs=[pl.BlockSpec((tm, dw), lambda r, f, te, nv: (r, 0)), wspec, wspec],
            out_specs=pl.BlockSpec((tm, tf), lambda r, f, te, nv: (r, f)),
            scratch_shapes=[pltpu.VMEM((tm, d), bf16)],
        ),
        out_shape=jax.ShapeDtypeStruct((p, fe), bf16),
        compiler_params=_cparams(("arbitrary", "arbitrary")),
    )(tile_expert, tile_rows, xs, wg, wu)
    return pl.pallas_call(
        _moe_down_kernel,
        grid_spec=pltpu.PrefetchScalarGridSpec(
            num_scalar_prefetch=2,
            grid=(p // tm,),
            in_specs=[pl.BlockSpec((tm, fe), lambda r, te, nv: (r, 0)),
                      pl.BlockSpec((None, fe, d), lambda r, te, nv: (te[r], 0, 0))],
            out_specs=pl.BlockSpec((tm, dw), lambda r, te, nv: (r, 0)),
        ),
        out_shape=jax.ShapeDtypeStruct((p, dw), jnp.int32),
        compiler_params=_cparams(("arbitrary",)),
    )(tile_expert, tile_rows, act, wd)


def _moe_combine_kernel(pos_ref, pos_next_ref, x_ref, w_ref, g_ref, y_ref, o_ref, ybuf, sem, *, final_norm):
    tr, d = x_ref.shape
    half = d // 2
    i = pl.program_id(0)
    slot = i % 2

    @pl.when(i == 0)
    def _():
        _start_rows(tr, pos_ref, y_ref, ybuf.at[0], sem.at[0], gather=True)

    @pl.when(i + 1 < pl.num_programs(0))
    def _():
        _start_rows(tr, pos_next_ref, y_ref, ybuf.at[1 - slot], sem.at[1 - slot], gather=True)

    _wait_rows(tr, pos_ref, y_ref, ybuf.at[slot], sem.at[slot], gather=True)
    w = w_ref[...]
    lo0, hi0 = _unpack_halves(ybuf[slot, 0])
    lo1, hi1 = _unpack_halves(ybuf[slot, 1])
    o_lo = x_ref[:, :half] + (w[:, 0:1] * lo0 + w[:, 1:2] * lo1)
    o_hi = x_ref[:, half:] + (w[:, 0:1] * hi0 + w[:, 1:2] * hi1)
    if final_norm:
        ms = (jnp.sum(o_lo * o_lo, axis=1, keepdims=True) + jnp.sum(o_hi * o_hi, axis=1, keepdims=True)) / d
        inv = lax.rsqrt(ms + EPS)
        o_lo = o_lo * inv * g_ref[:, :half]
        o_hi = o_hi * inv * g_ref[:, half:]
    o_ref[:, :half] = o_lo
    o_ref[:, half:] = o_hi


def _moe_combine(x, w, y, pos3, gain, final_norm, row_start, n_rows):
    d = x.shape[1]
    tr = pos3.shape[2]
    off = row_start // tr
    last = off + n_rows // tr - 1
    return pl.pallas_call(
        functools.partial(_moe_combine_kernel, final_norm=final_norm),
        grid=(n_rows // tr,),
        in_specs=[pl.BlockSpec((None, TOP_K, tr), lambda i: (i + off, 0, 0), memory_space=pltpu.SMEM),
                  pl.BlockSpec((None, TOP_K, tr), lambda i: (jnp.minimum(i + off + 1, last), 0, 0),
                               memory_space=pltpu.SMEM),
                  pl.BlockSpec((tr, d), lambda i: (i + off, 0)),
                  pl.BlockSpec((tr, TOP_K), lambda i: (i + off, 0)),
                  pl.BlockSpec((1, d), lambda i: (0, 0)),
                  pl.BlockSpec(memory_space=pl.ANY)],
        out_specs=pl.BlockSpec((tr, d), lambda i: (i, 0)),
        out_shape=jax.ShapeDtypeStruct((n_rows, d), f32),
        scratch_shapes=[pltpu.VMEM((2, TOP_K, tr, d // 2), jnp.int32), pltpu.SemaphoreType.DMA((2,))],
        compiler_params=_cparams(("arbitrary",)),
    )(pos3, pos3, x, w, gain.reshape(1, d), y)


def _moe_layer(x, g, router, wg, wu, wd, final_gain, parts):
    n, d = x.shape
    n_exp = router.shape[1]
    tm = MOE_ROW_TILE
    hp, eid, rank, w, cnt = _router(x, g, router)
    cnt = cnt[0, :n_exp]
    padded = (cnt + tm - 1) // tm * tm
    seg_end = jnp.cumsum(padded)
    seg_start = seg_end - padded
    pos = seg_start[eid] + rank
    n_rows_padded = TOP_K * n + n_exp * tm
    tile0 = jnp.arange(n_rows_padded // tm, dtype=jnp.int32) * tm
    tile_expert = jnp.minimum(jnp.searchsorted(seg_end, tile0, side="right"), n_exp - 1).astype(jnp.int32)
    tile_rows = jnp.clip(cnt[tile_expert] - (tile0 - seg_start[tile_expert]), 0, tm).astype(jnp.int32)
    def by_tile(tr):
        return pos.reshape(n // tr, tr, TOP_K).transpose(0, 2, 1)

    pos3 = by_tile(_tile(n, MOE_COMBINE_TILE))
    xs = _moe_dispatch(hp, by_tile(_tile(n, MOE_DISPATCH_TILE)), tile_rows, tm)
    y = _moe_group(xs, tile_expert, tile_rows, wg, wu, wd, tm)
    if final_gain is None:
        return _moe_combine(x, w, y, pos3, g, False, 0, n)
    return [_moe_combine(x, w, y, pos3, final_gain, True, start, rows) for start, rows in parts]


def kernel(x_prompt, x_sample, norm_mix, norm_ffn, norm_final, pool_w, pool_scale, hgrn_w_in, hgrn_lb,
           hgrn_norm, hgrn_w_out, ffn_w_gate, ffn_w_up, ffn_w_down, moe_router, moe_w_gate, moe_w_up,
           moe_w_down):
    bp, sp, d = x_prompt.shape
    bs, ss, _ = x_sample.shape
    n_prompt = bp * sp
    n = n_prompt + bs * ss
    seq = (n_prompt, sp, ss)
    depth = norm_mix.shape[0]
    chunk = _tile(min(sp, ss), 128)

    parts = ((0, n_prompt), (n_prompt, bs * ss))
    streams = (x_prompt.reshape(n_prompt, d), x_sample.reshape(bs * ss, d))
    x = None if depth else jnp.concatenate(streams, axis=0)
    outs = None

    for i in range(depth):
        j = i // 2
        if i % 2 == 0:
            xa, xb = streams if x is None else (x[:n_prompt], x[n_prompt:])
            x, h = _pool_layer(xa, xb, norm_mix[i], pool_w[j].astype(bf16), pool_scale[j], norm_ffn[i], seq)
            act = _glu_up(h, ffn_w_gate[j].astype(bf16)[None], ffn_w_up[j].astype(bf16)[None])
            x = _down_res(act, ffn_w_down[j].astype(bf16), x, 8192, 512)
        else:
            h = _rmsnorm(x, norm_mix[i], bf16)
            w_in = hgrn_w_in[j].astype(bf16)
            q = _proj_section(_proj_silu_heads_kernel, h, w_in, 0, True)
            forget = functools.partial(_proj_forget_kernel, layer=i)
            kf, gf = _proj_section(forget, h, w_in, 1, True, 2, hgrn_lb)
            kb, gb = _proj_section(forget, h, w_in, 2, True, 2, hgrn_lb)
            v = _proj_section(_proj_heads_kernel, h, w_in, 3, True)
            gate = _proj_section(_proj_silu_kernel, h, w_in, 4, False)
            s_bw = _scan_bw_states(kb, gb, v, seq, chunk)
            og = _scan(q, kf, kb, gf, gb, v, s_bw, gate, hgrn_norm[j], seq, chunk)
            x = _down_res(og, hgrn_w_out[j].astype(bf16), x, 4096)
            last = i == depth - 1
            res = _moe_layer(x, norm_ffn[i], moe_router[j], moe_w_gate[j].astype(bf16),
                             moe_w_up[j].astype(bf16), moe_w_down[j].astype(bf16),
                             norm_final if last else None, parts)
            x, outs = (None, res) if last else (res, None)

    if outs is None:
        outs = [_rmsnorm(x, norm_final, f32, start, rows) for start, rows in parts]
    return (outs[0].reshape(bp, sp, d), outs[1].reshape(bs, ss, d))
```

```python
import functools

import jax
import jax.numpy as jnp
from jax import lax
from jax.experimental import pallas as pl
from jax.experimental.pallas import tpu as pltpu

EPS = 1e-6
POOL_WINDOWS = (2, 4, 8, 16)
POOL_HALO = 8
POOL_PAD = 16
POOL_BLOCK = 128
HEAD_DIM = 128
TOP_K = 2
V7X_VMEM_LIMIT = 56 * 1024 * 1024
LANES = 128
MOE_ROW_TILE = 512
MOE_FF_TILE = 512
MOE_DISPATCH_TILE = 512
MOE_COMBINE_TILE = 256
ROW_DMA_UNROLL = 8
EPILOGUE_COLS = 256

bf16 = jnp.bfloat16
f32 = jnp.float32


def _cparams(sem):
    return pltpu.CompilerParams(dimension_semantics=sem, vmem_limit_bytes=V7X_VMEM_LIMIT)


def _tile(n, pref):
    t = min(n, pref)
    while n % t:
        t //= 2
    return t


def _rms(x, g):
    ms = jnp.mean(x * x, axis=-1, keepdims=True)
    return x * lax.rsqrt(ms + EPS) * g


def _silu(x):
    return x * jax.nn.sigmoid(x)


def _seq_flags(tok, n_tok, n_prompt, seq_p, seq_s):
    in_p = tok < n_prompt
    rel = jnp.where(in_p, tok, tok - n_prompt)
    slen = jnp.where(in_p, seq_p, seq_s)
    is_start = (rel % slen) == 0
    is_end = ((rel + n_tok) % slen) == 0
    return is_start, is_end


def _rmsnorm_kernel(x_ref, g_ref, o_ref):
    o_ref[...] = _rms(x_ref[...], g_ref[...]).astype(o_ref.dtype)


def _rmsnorm(x, g, out_dtype, row_start=0, n_rows=None):
    n, d = x.shape
    n_rows = n if n_rows is None else n_rows
    tr = _tile(n_rows, 512)
    off = row_start // tr
    return pl.pallas_call(
        _rmsnorm_kernel,
        grid=(n_rows // tr,),
        in_specs=[pl.BlockSpec((tr, d), lambda i: (i + off, 0)),
                  pl.BlockSpec((1, d), lambda i: (0, 0))],
        out_specs=pl.BlockSpec((tr, d), lambda i: (i, 0)),
        out_shape=jax.ShapeDtypeStruct((n_rows, d), out_dtype),
        compiler_params=_cparams(("parallel",)),
    )(x, g.reshape(1, d))


def _pool_kernel(ap_ref, a_ref, an_ref, bp_ref, b_ref, bn_ref, gmix_ref, w_ref, scale_ref, gffn_ref,
                 x1_ref, h1_ref, hi_s, lo_s, *, seq):
    t, d = a_ref.shape
    dg = d // len(POOL_WINDOWS)
    tok = pl.program_id(0) * t
    is_start, is_end = _seq_flags(tok, t, *seq)
    in_a = tok < seq[0]
    g = gmix_ref[...]
    x = jnp.where(in_a, a_ref[...], b_ref[...])
    h = _rms(x, g)
    h_prev = jnp.where(is_start, 0.0, _rms(jnp.where(in_a, ap_ref[...], bp_ref[...]), g))
    h_next = jnp.where(is_end, 0.0, _rms(jnp.where(in_a, an_ref[...], bn_ref[...]), g))
    zeros = jnp.zeros_like(h_prev)

    def put(r0, val):
        hi = val.astype(bf16)
        hi_s[r0:r0 + val.shape[0], :] = hi
        lo_s[r0:r0 + val.shape[0], :] = (val - hi.astype(f32)).astype(bf16)

    put(0, jnp.concatenate([zeros, h_prev], axis=0))
    put(POOL_PAD, h)
    put(POOL_PAD + t, jnp.concatenate([h_next, zeros], axis=0))
    tail = hi_s.shape[0] - (2 * POOL_PAD + t)
    hi_s[2 * POOL_PAD + t:, :] = jnp.zeros((tail, d), bf16)
    lo_s[2 * POOL_PAD + t:, :] = jnp.zeros((tail, d), bf16)

    rr = lax.broadcasted_iota(jnp.int32, (POOL_BLOCK, 2 * POOL_BLOCK), 0)
    cc = lax.broadcasted_iota(jnp.int32, (POOL_BLOCK, 2 * POOL_BLOCK), 1)
    r = lax.broadcasted_iota(jnp.int32, (t, 1), 0)
    for gi, w in enumerate(POOL_WINDOWS):
        c0 = gi * dg
        half = w // 2
        band = ((cc >= POOL_PAD + rr - half) & (cc < POOL_PAD + rr + half)).astype(bf16)
        sums = []
        for r0 in range(0, t, POOL_BLOCK):
            sums.append(jnp.dot(band, hi_s[r0:r0 + 2 * POOL_BLOCK, c0:c0 + dg], preferred_element_type=f32)
                        + jnp.dot(band, lo_s[r0:r0 + 2 * POOL_BLOCK, c0:c0 + dg], preferred_element_type=f32))
        s = jnp.concatenate(sums, axis=0)
        lo = jnp.where(is_start, jnp.maximum(r - half, 0), r - half)
        hi = jnp.where(is_end, jnp.minimum(r + half, t), r + half)
        cnt = (hi - lo).astype(f32)
        dlt = s / cnt - h[:, c0:c0 + dg]
        y = jnp.dot(dlt.astype(bf16), w_ref[gi], preferred_element_type=f32)
        x1_ref[:, c0:c0 + dg] = x[:, c0:c0 + dg] + y * scale_ref[:, c0:c0 + dg]
    h1_ref[...] = _rms(x1_ref[...], gffn_ref[...]).astype(h1_ref.dtype)


def _stream_specs(rows, first_tile, t, d):
    hb = t // POOL_HALO
    last_halo = rows // POOL_HALO - 1
    last_tile = rows // t - 1
    return [
        pl.BlockSpec((POOL_HALO, d), lambda i: (jnp.clip((i - first_tile) * hb - 1, 0, last_halo), 0)),
        pl.BlockSpec((t, d), lambda i: (jnp.clip(i - first_tile, 0, last_tile), 0)),
        pl.BlockSpec((POOL_HALO, d), lambda i: (jnp.clip((i - first_tile + 1) * hb, 0, last_halo), 0)),
    ]


def _pool_layer(xa, xb, g_mix, w_grp, scale, g_ffn, seq):
    d = xa.shape[1]
    n = xa.shape[0] + xb.shape[0]
    t = _tile(min(seq[1], seq[2]), 256)
    row = lambda i: (0, 0)
    return pl.pallas_call(
        functools.partial(_pool_kernel, seq=seq),
        grid=(n // t,),
        in_specs=_stream_specs(xa.shape[0], 0, t, d) + _stream_specs(xb.shape[0], xa.shape[0] // t, t, d) + [
            pl.BlockSpec((1, d), row),
            pl.BlockSpec(w_grp.shape, lambda i: (0, 0, 0)),
            pl.BlockSpec((1, d), row),
            pl.BlockSpec((1, d), row),
        ],
        out_specs=[pl.BlockSpec((t, d), lambda i: (i, 0)),
                   pl.BlockSpec((t, d), lambda i: (i, 0))],
        out_shape=[jax.ShapeDtypeStruct((n, d), f32), jax.ShapeDtypeStruct((n, d), bf16)],
        scratch_shapes=[pltpu.VMEM((t + POOL_BLOCK, d), bf16)] * 2,
        compiler_params=_cparams(("parallel",)),
    )(xa, xa, xa, xb, xb, xb, g_mix.reshape(1, d), w_grp, scale.reshape(1, d), g_ffn.reshape(1, d))


def _glu_up_kernel(a_ref, wg_ref, wu_ref, o_ref):
    a = a_ref[...]
    gate = jnp.dot(a, wg_ref[...], preferred_element_type=f32)
    up = jnp.dot(a, wu_ref[...], preferred_element_type=f32)
    o_ref[...] = (_silu(gate) * up).astype(o_ref.dtype)


def _glu_up(a, wg, wu):
    m, k = a.shape
    e, _, f = wg.shape
    tm = _tile(m, 1024)
    tn = _tile(f, 512)
    nb = f // tn
    wspec = pl.BlockSpec((None, k, tn), lambda i, j: (j // nb, 0, j % nb))
    return pl.pallas_call(
        _glu_up_kernel,
        grid=(m // tm, e * nb),
        in_specs=[pl.BlockSpec((tm, k), lambda i, j: (i, 0)), wspec, wspec],
        out_specs=pl.BlockSpec((tm, tn), lambda i, j: (i, j)),
        out_shape=jax.ShapeDtypeStruct((m, e * f), bf16),
        compiler_params=_cparams(("parallel", "arbitrary")),
    )(a, wg, wu)


def _down_kernel(a_ref, w_ref, res_ref, o_ref):
    o_ref[...] = res_ref[...] + jnp.dot(a_ref[...], w_ref[...], preferred_element_type=f32)


def _down_res(a, w, res, tm):
    m, kk = a.shape
    n = w.shape[1]
    tm = _tile(m, tm)
    tn = _tile(n, 512)
    return pl.pallas_call(
        _down_kernel,
        grid=(m // tm, n // tn),
        in_specs=[pl.BlockSpec((tm, kk), lambda i, j: (i, 0)),
                  pl.BlockSpec((kk, tn), lambda i, j: (0, j)),
                  pl.BlockSpec((tm, tn), lambda i, j: (i, j))],
        out_specs=pl.BlockSpec((tm, tn), lambda i, j: (i, j)),
        out_shape=jax.ShapeDtypeStruct((m, n), f32),
        compiler_params=_cparams(("parallel", "arbitrary")),
    )(a, w, res)


def _column_blocks(a_ref, w_ref):
    a = a_ref[...]
    tn = w_ref.shape[1]
    cb = min(tn, EPILOGUE_COLS)
    for c0 in range(0, tn, cb):
        yield c0, jnp.dot(a, w_ref[:, c0:c0 + cb], preferred_element_type=f32)


def _store_head_block(o_ref, c0, val):
    for hh in range(val.shape[1] // HEAD_DIM):
        o_ref[c0 // HEAD_DIM + hh] = val[:, hh * HEAD_DIM:(hh + 1) * HEAD_DIM].astype(o_ref.dtype)


def _proj_silu_heads_kernel(a_ref, w_ref, o_ref):
    for c0, r in _column_blocks(a_ref, w_ref):
        _store_head_block(o_ref, c0, _silu(r))


def _proj_heads_kernel(a_ref, w_ref, o_ref):
    for c0, r in _column_blocks(a_ref, w_ref):
        _store_head_block(o_ref, c0, r)


def _proj_silu_kernel(a_ref, w_ref, o_ref):
    for c0, r in _column_blocks(a_ref, w_ref):
        o_ref[:, c0:c0 + r.shape[1]] = _silu(r).astype(o_ref.dtype)


def _proj_forget_kernel(a_ref, w_ref, lb_ref, k_ref, g_ref, *, layer):
    logits = lb_ref[...]
    ex = jnp.exp(logits - jnp.max(logits, axis=0, keepdims=True))
    sm = ex / jnp.sum(ex, axis=0, keepdims=True)
    lb_all = jnp.sum(sm[0:layer + 1], axis=0, keepdims=True) - sm[0:1]
    for c0, r in _column_blocks(a_ref, w_ref):
        lb = lb_all[:, c0:c0 + r.shape[1]]
        f = lb + (1.0 - lb) * jax.nn.sigmoid(r)
        _store_head_block(k_ref, c0, 1.0 - f)
        _store_head_block(g_ref, c0, jnp.log2(f))


def _proj_section(kernel, a, w, section, out_heads, n_out=1, extra=None):
    m, k = a.shape
    d = k
    tm = _tile(m, 1024)
    tn = _tile(d, 1024)
    nb = d // tn
    in_specs = [pl.BlockSpec((tm, k), lambda i, j: (i, 0)),
                pl.BlockSpec((k, tn), lambda i, j: (0, section * nb + j))]
    args = [a, w]
    if extra is not None:
        in_specs.append(pl.BlockSpec((extra.shape[0], tn), lambda i, j: (0, j)))
        args.append(extra)
    if out_heads:
        hpb = tn // HEAD_DIM
        ospec = pl.BlockSpec((hpb, tm, HEAD_DIM), lambda i, j: (j, i, 0))
        oshape = jax.ShapeDtypeStruct((d // HEAD_DIM, m, HEAD_DIM), bf16)
    else:
        ospec = pl.BlockSpec((tm, tn), lambda i, j: (i, j))
        oshape = jax.ShapeDtypeStruct((m, d), bf16)
    out = pl.pallas_call(
        kernel,
        grid=(m // tm, nb),
        in_specs=in_specs,
        out_specs=[ospec] * n_out if n_out > 1 else ospec,
        out_shape=[oshape] * n_out if n_out > 1 else oshape,
        compiler_params=_cparams(("parallel", "arbitrary")),
    )(*args)
    return out


MAX_DIRECT_LOG2_DECAY = 100.0
HEAD_UNROLL = 8


def _dot_nt(a, b):
    return lax.dot_general(a, b, (((1,), (1,)), ((), ())), preferred_element_type=f32)


def _dot_tn(a, b):
    return lax.dot_general(a, b, (((0,), (0,)), ((), ())), preferred_element_type=f32)


def _mid_ref(p, level, offset):
    c, w = p.shape
    blk = 2 << level
    p3 = p.reshape(c // blk, blk, w)
    return jnp.broadcast_to(p3[:, offset:offset + 1, :], p3.shape).reshape(c, w)


def _neg_abs_exponents(gf, gb, b, x, level, rmod4):
    if level == 0:
        odd = (rmod4 & 1) == 1
        return jnp.where(odd, gf, 0.0), jnp.where(odd, 0.0, gb)
    if level == 1:
        c = gf.shape[0]
        gf_up = pltpu.roll(gf, c - 1, 0)
        gf_dn = pltpu.roll(gf, 1, 0)
        gb_up = pltpu.roll(gb, c - 1, 0)
        gb_dn = pltpu.roll(gb, 1, 0)
        ef = jnp.where(rmod4 == 0, gf_up, jnp.where(rmod4 == 1, 0.0, jnp.where(rmod4 == 2, gf, gf + gf_dn)))
        eb = jnp.where(rmod4 == 0, gb + gb_up, jnp.where(rmod4 == 1, gb, jnp.where(rmod4 == 2, 0.0, gb_dn)))
        return ef, eb
    half = 1 << level
    ef = b - _mid_ref(b, level, half - 1)
    eb = x - _mid_ref(x, level, half)
    return -jnp.abs(ef), -jnp.abs(eb)


def _chunk_prefix(tri_incl, tri_excl, gf16, gb16):
    b = jnp.dot(tri_incl, gf16, preferred_element_type=f32)
    x = jnp.dot(tri_excl, gb16, preferred_element_type=f32)
    return b, x


def _scan_bw_kernel(kb_ref, gb_ref, v_ref, s_ref, st_ref, *, seq, n_chunks):
    n_heads, c, _ = kb_ref.shape
    chunk = n_chunks - 1 - pl.program_id(0)
    _, is_end = _seq_flags(chunk * c, c, *seq)

    @pl.when(is_end)
    def _():
        st_ref[...] = jnp.zeros_like(st_ref)

    row = lax.broadcasted_iota(jnp.int32, (c, c), 0)
    col = lax.broadcasted_iota(jnp.int32, (c, c), 1)
    tri_excl = (col < row).astype(bf16)

    def head(h, carry):
        gb16 = gb_ref[h]
        x = jnp.dot(tri_excl, gb16, preferred_element_type=f32)
        tot = x[c - 1:c, :] + gb16[c - 1:c, :].astype(f32)
        st = st_ref[h]
        s_ref[h] = st.astype(s_ref.dtype)
        kx = (kb_ref[h].astype(f32) * jnp.exp2(x)).astype(bf16)
        st_ref[h] = st * jnp.exp2(tot) + _dot_tn(v_ref[h], kx)
        return carry

    lax.fori_loop(0, n_heads, head, 0, unroll=2 * HEAD_UNROLL)


def _scan_bw_states(kb, gb, v, seq, c):
    n_heads, n, dk = kb.shape
    n_chunks = n // c
    spec = pl.BlockSpec((n_heads, c, dk), lambda i: (0, n_chunks - 1 - i, 0))
    return pl.pallas_call(
        functools.partial(_scan_bw_kernel, seq=seq, n_chunks=n_chunks),
        grid=(n_chunks,),
        in_specs=[spec, spec, spec],
        out_specs=pl.BlockSpec((None, n_heads, dk, dk), lambda i: (n_chunks - 1 - i, 0, 0, 0)),
        out_shape=jax.ShapeDtypeStruct((n_chunks, n_heads, dk, dk), bf16),
        scratch_shapes=[pltpu.VMEM((n_heads, dk, dk), f32)],
        compiler_params=_cparams(("arbitrary",)),
    )(kb, gb, v)


def _scan_kernel(q_ref, kf_ref, kb_ref, gf_ref, gb_ref, v_ref, sb_ref, tril_ref, triu_ref, gate_ref,
                 gain_ref, out_ref, st_ref, o_s, qf_s, kf_s, qb_s, kb_s, qin_s, kx_s, sc_s, dec_s, tot_s,
                 *, seq):
    n_heads, c, dk = q_ref.shape
    n_levels = c.bit_length() - 1
    mid = c // 2
    is_start, _ = _seq_flags(pl.program_id(0) * c, c, *seq)

    @pl.when(is_start)
    def _():
        st_ref[...] = jnp.zeros_like(st_ref)

    row = lax.broadcasted_iota(jnp.int32, (c, c), 0)
    col = lax.broadcasted_iota(jnp.int32, (c, c), 1)
    tri_incl = (col <= row).astype(bf16)
    tri_excl = (col < row).astype(bf16)

    def finish(h, kf_dec, b_last, q_in, scores, diag):
        v = v_ref[h]
        st_f = st_ref[h]
        st_cat = jnp.concatenate([st_f.astype(bf16), sb_ref[h]], axis=1)
        o = _dot_nt(q_in, st_cat) + jnp.dot(scores.astype(bf16), v, preferred_element_type=f32)
        o_s[h] = o if diag is None else o + diag * v.astype(f32)
        st_ref[h] = st_f * jnp.exp2(b_last) + _dot_tn(v, kf_dec.astype(bf16))

    def direct_operands(h, carry):
        q = q_ref[h].astype(f32)
        gb16 = gb_ref[h]
        b, x = _chunk_prefix(tri_incl, tri_excl, gf_ref[h], gb16)
        b_mid = b[mid - 1:mid, :]
        x_mid = x[mid:mid + 1, :]
        db = b - b_mid
        dx = x - x_mid
        q_f = q * jnp.exp2(db)
        k_f = kf_ref[h].astype(f32) * jnp.exp2(-db)
        q_b = q * jnp.exp2(-dx)
        k_b = kb_ref[h].astype(f32) * jnp.exp2(dx)
        b_last = b[c - 1:c, :]
        x_tot = x[c - 1:c, :] + gb16[c - 1:c, :].astype(f32)
        qf_s[h] = q_f.astype(bf16)
        kf_s[h] = k_f.astype(bf16)
        qb_s[h] = q_b.astype(bf16)
        kb_s[h] = k_b.astype(bf16)
        qin_s[h] = jnp.concatenate([q_f * jnp.exp2(b_mid), q_b * jnp.exp2(x_tot - x_mid)], axis=1).astype(bf16)
        kx_s[h] = (k_f * jnp.exp2(b_last - b_mid)).astype(bf16)
        dec_s[h] = jnp.broadcast_to(jnp.exp2(b_last), dec_s.shape[1:])
        tot_s[h] = jnp.broadcast_to(jnp.minimum(b_last, x_tot), tot_s.shape[1:])
        return carry

    def direct_scores(h, carry):
        sc_s[h] = (jnp.where(tril_ref[...] != 0.0, _dot_nt(qf_s[h], kf_s[h]), 0.0)
                   + jnp.where(triu_ref[...] != 0.0, _dot_nt(qb_s[h], kb_s[h]), 0.0)).astype(bf16)
        return carry

    def direct_output(h, carry):
        v = v_ref[h]
        st_f = st_ref[h]
        st_cat = jnp.concatenate([st_f.astype(bf16), sb_ref[h]], axis=1)
        o_s[h] = _dot_nt(qin_s[h], st_cat) + jnp.dot(sc_s[h], v, preferred_element_type=f32)
        st_ref[h] = st_f * dec_s[h][0:1, :] + _dot_tn(v, kx_s[h])
        return carry

    def head_levelled(h, carry):
        rtok = lax.broadcasted_iota(jnp.int32, (c, dk), 0)
        rmod4 = rtok & 3
        q = q_ref[h].astype(f32)
        kf = kf_ref[h].astype(f32)
        kb = kb_ref[h].astype(f32)
        gf16 = gf_ref[h]
        gb16 = gb_ref[h]
        gf = gf16.astype(f32)
        gb = gb16.astype(f32)
        b, x = _chunk_prefix(tri_incl, tri_excl, gf16, gb16)
        scores = jnp.zeros((c, c), f32)
        for l in range(n_levels):
            ef, eb = _neg_abs_exponents(gf, gb, b, x, l, rmod4)
            af = jnp.exp2(ef)
            ab = jnp.exp2(eb)
            up = ((rtok >> l) & 1) == 1
            lhs = jnp.concatenate([jnp.where(up, q * af, 0.0), jnp.where(up, 0.0, q * ab)], axis=1)
            rhs = jnp.concatenate([jnp.where(up, 0.0, kf * af), jnp.where(up, kb * ab, 0.0)], axis=1)
            p = _dot_nt(lhs.astype(bf16), rhs.astype(bf16))
            if l < n_levels - 1:
                p = jnp.where((row >> (l + 1)) == (col >> (l + 1)), p, 0.0)
            scores = scores + p
        b_last = b[c - 1:c, :]
        x_tot = x[c - 1:c, :] + gb[c - 1:c, :]
        q_in = jnp.concatenate([q * jnp.exp2(b), q * jnp.exp2(x_tot - x)], axis=1).astype(bf16)
        diag = jnp.sum(q * (kf + kb), axis=1, keepdims=True)
        finish(h, kf * jnp.exp2(b_last - b), b_last, q_in, scores, diag)
        return carry

    lax.fori_loop(0, n_heads, direct_operands, 0, unroll=HEAD_UNROLL)
    direct_ok = jnp.min(tot_s[...]) >= -MAX_DIRECT_LOG2_DECAY

    @pl.when(direct_ok)
    def _():
        lax.fori_loop(0, n_heads, direct_scores, 0, unroll=2 * HEAD_UNROLL)
        lax.fori_loop(0, n_heads, direct_output, 0, unroll=2 * HEAD_UNROLL)

    @pl.when(jnp.logical_not(direct_ok))
    def _():
        lax.fori_loop(0, n_heads, head_levelled, 0)

    sq = jnp.zeros((c, dk), f32)
    for hh in range(n_heads):
        o = o_s[hh]
        sq = sq + o * o
    inv = lax.rsqrt(jnp.sum(sq, axis=1, keepdims=True) / (n_heads * dk) + EPS)
    for hh in range(n_heads):
        cols = slice(hh * dk, (hh + 1) * dk)
        out_ref[:, cols] = (o_s[hh] * inv * gain_ref[:, cols] * gate_ref[:, cols].astype(f32)
                            ).astype(out_ref.dtype)


def _scan(q, kf, kb, gf, gb, v, s_bw, gate, gain, seq, c):
    n_heads, n, dk = q.shape
    d = n_heads * dk
    n_chunks = n // c
    spec = pl.BlockSpec((n_heads, c, dk), lambda i: (0, i, 0))
    tri = pl.BlockSpec((c, c), lambda i: (0, 0))
    tril = jnp.tril(jnp.ones((c, c), f32))
    return pl.pallas_call(
        functools.partial(_scan_kernel, seq=seq),
        grid=(n_chunks,),
        in_specs=[spec] * 6 + [pl.BlockSpec((None, n_heads, dk, dk), lambda i: (i, 0, 0, 0)), tri, tri,
                               pl.BlockSpec((c, d), lambda i: (i, 0)),
                               pl.BlockSpec((1, d), lambda i: (0, 0))],
        out_specs=pl.BlockSpec((c, d), lambda i: (i, 0)),
        out_shape=jax.ShapeDtypeStruct((n, d), bf16),
        scratch_shapes=[pltpu.VMEM((n_heads, dk, dk), f32), pltpu.VMEM((n_heads, c, dk), f32)]
        + [pltpu.VMEM((n_heads, c, dk), bf16)] * 4
        + [pltpu.VMEM((n_heads, c, 2 * dk), bf16), pltpu.VMEM((n_heads, c, dk), bf16),
           pltpu.VMEM((n_heads, c, c), bf16)] + [pltpu.VMEM((n_heads, 8, dk), f32)] * 2,
        compiler_params=_cparams(("arbitrary",)),
    )(q, kf, kb, gf, gb, v, s_bw, tril, tril.T, gate, gain.reshape(1, d))


def _pack_halves(x):
    half = x.shape[1] // 2
    bits = lax.bitcast_convert_type(x.astype(bf16).astype(f32), jnp.int32)
    return lax.shift_right_logical(bits[:, :half], 16) | (bits[:, half:] & jnp.int32(-65536))


def _unpack_halves(words):
    lo = lax.bitcast_convert_type(lax.shift_left(words, 16), f32)
    hi = lax.bitcast_convert_type(words & jnp.int32(-65536), f32)
    return lo, hi


def _router_kernel(x_ref, g_ref, r2_ref, h_ref, e_ref, rank_ref, w_ref, cnt_ref, run_ref,
                   *, n_experts):
    @pl.when(pl.program_id(0) == 0)
    def _():
        run_ref[...] = jnp.zeros_like(run_ref)

    h = _rms(x_ref[...], g_ref[...])
    h_ref[...] = _pack_halves(h)
    hi = h.astype(bf16)
    lo = (h - hi.astype(f32)).astype(bf16)
    tr = h.shape[0]
    prod = jnp.dot(jnp.concatenate([hi, lo], axis=0), r2_ref[...], preferred_element_type=f32)
    logits = prod[:tr, :LANES] + prod[:tr, LANES:] + prod[tr:, :LANES]
    lane = lax.broadcasted_iota(jnp.int32, logits.shape, 1)
    neg = jnp.float32(-jnp.inf)
    logits = jnp.where(lane < n_experts, logits, neg)
    m1 = jnp.max(logits, axis=1, keepdims=True)
    i1 = jnp.min(jnp.where(logits == m1, lane, LANES), axis=1, keepdims=True)
    rest = jnp.where(lane == i1, neg, logits)
    m2 = jnp.max(rest, axis=1, keepdims=True)
    i2 = jnp.min(jnp.where(rest == m2, lane, LANES), axis=1, keepdims=True)
    e2 = jnp.exp(m2 - m1)
    w_ref[...] = jnp.concatenate([1.0 / (1.0 + e2), e2 / (1.0 + e2)], axis=1)
    e_ref[...] = jnp.concatenate([i1, i2], axis=1)
    member = jnp.where((lane == i1) | (lane == i2), 1.0, 0.0)
    row = lax.broadcasted_iota(jnp.int32, (tr, tr), 0)
    col = lax.broadcasted_iota(jnp.int32, (tr, tr), 1)
    before = jnp.dot((col < row).astype(bf16), member.astype(bf16), preferred_element_type=f32)
    before = before + run_ref[...]
    r1 = jnp.sum(jnp.where(lane == i1, before, 0.0), axis=1, keepdims=True)
    r2 = jnp.sum(jnp.where(lane == i2, before, 0.0), axis=1, keepdims=True)
    rank_ref[...] = jnp.concatenate([r1, r2], axis=1).astype(jnp.int32)
    run = run_ref[...] + jnp.sum(member, axis=0, keepdims=True)
    run_ref[...] = run
    cnt_ref[...] = run.astype(jnp.int32)


def _router(x, g, router):
    n, d = x.shape
    e = router.shape[1]
    rpad = jnp.zeros((d, LANES), f32).at[:, :e].set(router)
    rhi = rpad.astype(bf16)
    r2 = jnp.concatenate([rhi, (rpad - rhi.astype(f32)).astype(bf16)], axis=1)
    tr = _tile(n, 512)
    pair = pl.BlockSpec((tr, TOP_K), lambda i: (i, 0))
    return pl.pallas_call(
        functools.partial(_router_kernel, n_experts=e),
        grid=(n // tr,),
        in_specs=[pl.BlockSpec((tr, d), lambda i: (i, 0)),
                  pl.BlockSpec((1, d), lambda i: (0, 0)),
                  pl.BlockSpec((d, 2 * LANES), lambda i: (0, 0))],
        out_specs=[pl.BlockSpec((tr, d // 2), lambda i: (i, 0)), pair, pair, pair,
                   pl.BlockSpec((1, LANES), lambda i: (0, 0))],
        out_shape=[jax.ShapeDtypeStruct((n, d // 2), jnp.int32),
                   jax.ShapeDtypeStruct((n, TOP_K), jnp.int32),
                   jax.ShapeDtypeStruct((n, TOP_K), jnp.int32),
                   jax.ShapeDtypeStruct((n, TOP_K), f32),
                   jax.ShapeDtypeStruct((1, LANES), jnp.int32)],
        scratch_shapes=[pltpu.VMEM((1, LANES), f32)],
        compiler_params=_cparams(("arbitrary",)),
    )(x, g.reshape(1, d), r2)


def _row_copies(pos_ref, i, src, dst, sem, gather):
    out = []
    for k in range(TOP_K):
        p = pos_ref[k, i]
        if gather:
            out.append(pltpu.make_async_copy(src.at[pl.ds(p, 1)], dst.at[k, pl.ds(i, 1)], sem))
        else:
            out.append(pltpu.make_async_copy(src.at[pl.ds(i, 1)], dst.at[pl.ds(p, 1)], sem))
    return out


def _start_rows(n_rows, pos_ref, src, dst, sem, gather):
    def start(i, carry):
        for cp in _row_copies(pos_ref, i, src, dst, sem, gather):
            cp.start()
        return carry

    lax.fori_loop(0, n_rows, start, 0, unroll=ROW_DMA_UNROLL)


def _wait_rows(n_rows, pos_ref, src, dst, sem, gather):
    def wait(i, carry):
        for cp in _row_copies(pos_ref, i, src, dst, sem, gather):
            cp.wait()
        return carry

    lax.fori_loop(0, n_rows, wait, 0, unroll=ROW_DMA_UNROLL)


def _all_rows(n_rows, pos_ref, src, dst, sem, gather):
    _start_rows(n_rows, pos_ref, src, dst, sem, gather)
    _wait_rows(n_rows, pos_ref, src, dst, sem, gather)


def _moe_dispatch_kernel(nv_ref, pos_ref, h_ref, xs_ref, zero_s, sem):
    tm = zero_s.shape[0]

    @pl.when(pl.program_id(0) == 0)
    def _():
        zero_s[...] = jnp.zeros_like(zero_s)

        def fill(r, carry):
            @pl.when(nv_ref[r] < tm)
            def _():
                cp = pltpu.make_async_copy(zero_s, xs_ref.at[pl.ds(pl.multiple_of(r * tm, tm), tm)], sem)
                cp.start()
                cp.wait()
            return carry

        lax.fori_loop(0, xs_ref.shape[0] // tm, fill, 0)

    _all_rows(h_ref.shape[0], pos_ref, h_ref, xs_ref, sem, gather=False)


def _moe_dispatch(hp, pos3, tile_rows, tm):
    n, dw = hp.shape
    n_tiles, _, tr = pos3.shape
    grid_spec = pltpu.PrefetchScalarGridSpec(
        num_scalar_prefetch=1,
        grid=(n_tiles,),
        in_specs=[pl.BlockSpec((None, TOP_K, tr), lambda i, nv: (i, 0, 0), memory_space=pltpu.SMEM),
                  pl.BlockSpec((tr, dw), lambda i, nv: (i, 0))],
        out_specs=pl.BlockSpec(memory_space=pl.ANY),
        scratch_shapes=[pltpu.VMEM((tm, dw), jnp.int32), pltpu.SemaphoreType.DMA(())],
    )
    return pl.pallas_call(
        _moe_dispatch_kernel,
        grid_spec=grid_spec,
        out_shape=jax.ShapeDtypeStruct((tile_rows.shape[0] * tm, dw), jnp.int32),
        compiler_params=_cparams(("arbitrary",)),
    )(tile_rows, pos3, hp)


def _moe_up_kernel(te_ref, nv_ref, xs_ref, wg_ref, wu_ref, a_ref, x_s):
    r = pl.program_id(0)
    half = xs_ref.shape[1]

    @pl.when(nv_ref[r] > 0)
    def _():
        @pl.when(pl.program_id(1) == 0)
        def _():
            lo, hi = _unpack_halves(xs_ref[...])
            x_s[:, :half] = lo.astype(bf16)
            x_s[:, half:] = hi.astype(bf16)

        x = x_s[...]
        gate = jnp.dot(x, wg_ref[...], preferred_element_type=f32)
        up = jnp.dot(x, wu_ref[...], preferred_element_type=f32)
        a_ref[...] = (_silu(gate) * up).astype(a_ref.dtype)

    @pl.when(nv_ref[r] == 0)
    def _():
        a_ref[...] = jnp.zeros_like(a_ref)


def _moe_down_kernel(te_ref, nv_ref, a_ref, wd_ref, y_ref):
    @pl.when(nv_ref[pl.program_id(0)] > 0)
    def _():
        y_ref[...] = _pack_halves(jnp.dot(a_ref[...], wd_ref[...], preferred_element_type=f32))

    @pl.when(nv_ref[pl.program_id(0)] == 0)
    def _():
        y_ref[...] = jnp.zeros_like(y_ref)


def _moe_group(xs, tile_expert, tile_rows, wg, wu, wd, tm):
    p, dw = xs.shape
    _, d, fe = wg.shape
    tf = _tile(fe, MOE_FF_TILE)
    wspec = pl.BlockSpec((None, d, tf), lambda r, f, te, nv: (te[r], 0, f))
    act = pl.pallas_call(
        _moe_up_kernel,
        grid_spec=pltpu.PrefetchScalarGridSpec(
            num_scalar_prefetch=2,
            grid=(p // tm, fe // tf),
            in_specs=[pl.BlockSpec((tm, dw), lambda r, f, te, nv: (r, 0)), wspec, wspec],
            out_specs=pl.BlockSpec((tm, tf), lambda r, f, te, nv: (r, f)),
            scratch_shapes=[pltpu.VMEM((tm, d), bf16)],
        ),
        out_shape=jax.ShapeDtypeStruct((p, fe), bf16),
        compiler_params=_cparams(("arbitrary", "arbitrary")),
    )(tile_expert, tile_rows, xs, wg, wu)
    return pl.pallas_call(
        _moe_down_kernel,
        grid_spec=pltpu.PrefetchScalarGridSpec(
            num_scalar_prefetch=2,
            grid=(p // tm,),
            in_specs=[pl.BlockSpec((tm, fe), lambda r, te, nv: (r, 0)),
                      pl.BlockSpec((None, fe, d), lambda r, te, nv: (te[r], 0, 0))],
            out_specs=pl.BlockSpec((tm, dw), lambda r, te, nv: (r, 0)),
        ),
        out_shape=jax.ShapeDtypeStruct((p, dw), jnp.int32),
        compiler_params=_cparams(("arbitrary",)),
    )(tile_expert, tile_rows, act, wd)


def _moe_combine_kernel(pos_ref, pos_next_ref, x_ref, w_ref, g_ref, y_ref, o_ref, ybuf, sem, *, final_norm):
    tr, d = x_ref.shape
    half = d // 2
    i = pl.program_id(0)
    slot = i % 2

    @pl.when(i == 0)
    def _():
        _start_rows(tr, pos_ref, y_ref, ybuf.at[0], sem.at[0], gather=True)

    @pl.when(i + 1 < pl.num_programs(0))
    def _():
        _start_rows(tr, pos_next_ref, y_ref, ybuf.at[1 - slot], sem.at[1 - slot], gather=True)

    _wait_rows(tr, pos_ref, y_ref, ybuf.at[slot], sem.at[slot], gather=True)
    w = w_ref[...]
    lo0, hi0 = _unpack_halves(ybuf[slot, 0])
    lo1, hi1 = _unpack_halves(ybuf[slot, 1])
    o_lo = x_ref[:, :half] + (w[:, 0:1] * lo0 + w[:, 1:2] * lo1)
    o_hi = x_ref[:, half:] + (w[:, 0:1] * hi0 + w[:, 1:2] * hi1)
    if final_norm:
        ms = (jnp.sum(o_lo * o_lo, axis=1, keepdims=True) + jnp.sum(o_hi * o_hi, axis=1, keepdims=True)) / d
        inv = lax.rsqrt(ms + EPS)
        o_lo = o_lo * inv * g_ref[:, :half]
        o_hi = o_hi * inv * g_ref[:, half:]
    o_ref[:, :half] = o_lo
    o_ref[:, half:] = o_hi


def _moe_combine(x, w, y, pos3, gain, final_norm, row_start, n_rows):
    d = x.shape[1]
    tr = pos3.shape[2]
    off = row_start // tr
    last = off + n_rows // tr - 1
    return pl.pallas_call(
        functools.partial(_moe_combine_kernel, final_norm=final_norm),
        grid=(n_rows // tr,),
        in_specs=[pl.BlockSpec((None, TOP_K, tr), lambda i: (i + off, 0, 0), memory_space=pltpu.SMEM),
                  pl.BlockSpec((None, TOP_K, tr), lambda i: (jnp.minimum(i + off + 1, last), 0, 0),
                               memory_space=pltpu.SMEM),
                  pl.BlockSpec((tr, d), lambda i: (i + off, 0)),
                  pl.BlockSpec((tr, TOP_K), lambda i: (i + off, 0)),
                  pl.BlockSpec((1, d), lambda i: (0, 0)),
                  pl.BlockSpec(memory_space=pl.ANY)],
        out_specs=pl.BlockSpec((tr, d), lambda i: (i, 0)),
        out_shape=jax.ShapeDtypeStruct((n_rows, d), f32),
        scratch_shapes=[pltpu.VMEM((2, TOP_K, tr, d // 2), jnp.int32), pltpu.SemaphoreType.DMA((2,))],
        compiler_params=_cparams(("arbitrary",)),
    )(pos3, pos3, x, w, gain.reshape(1, d), y)


def _moe_layer(x, g, router, wg, wu, wd, final_gain, parts):
    n, d = x.shape
    n_exp = router.shape[1]
    tm = MOE_ROW_TILE
    hp, eid, rank, w, cnt = _router(x, g, router)
    cnt = cnt[0, :n_exp]
    padded = (cnt + tm - 1) // tm * tm
    seg_end = jnp.cumsum(padded)
    seg_start = seg_end - padded
    pos = seg_start[eid] + rank
    n_rows_padded = TOP_K * n + n_exp * tm
    tile0 = jnp.arange(n_rows_padded // tm, dtype=jnp.int32) * tm
    tile_expert = jnp.minimum(jnp.searchsorted(seg_end, tile0, side="right"), n_exp - 1).astype(jnp.int32)
    tile_rows = jnp.clip(cnt[tile_expert] - (tile0 - seg_start[tile_expert]), 0, tm).astype(jnp.int32)
    def by_tile(tr):
        return pos.reshape(n // tr, tr, TOP_K).transpose(0, 2, 1)

    pos3 = by_tile(_tile(n, MOE_COMBINE_TILE))
    xs = _moe_dispatch(hp, by_tile(_tile(n, MOE_DISPATCH_TILE)), tile_rows, tm)
    y = _moe_group(xs, tile_expert, tile_rows, wg, wu, wd, tm)
    if final_gain is None:
        return _moe_combine(x, w, y, pos3, g, False, 0, n)
    return [_moe_combine(x, w, y, pos3, final_gain, True, start, rows) for start, rows in parts]


def kernel(x_prompt, x_sample, norm_mix, norm_ffn, norm_final, pool_w, pool_scale, hgrn_w_in, hgrn_lb,
           hgrn_norm, hgrn_w_out, ffn_w_gate, ffn_w_up, ffn_w_down, moe_router, moe_w_gate, moe_w_up,
           moe_w_down):
    bp, sp, d = x_prompt.shape
    bs, ss, _ = x_sample.shape
    n_prompt = bp * sp
    n = n_prompt + bs * ss
    seq = (n_prompt, sp, ss)
    depth = norm_mix.shape[0]
    chunk = _tile(min(sp, ss), 128)

    parts = ((0, n_prompt), (n_prompt, bs * ss))
    streams = (x_prompt.reshape(n_prompt, d), x_sample.reshape(bs * ss, d))
    x = None if depth else jnp.concatenate(streams, axis=0)
    outs = None

    for i in range(depth):
        j = i // 2
        if i % 2 == 0:
            xa, xb = streams if x is None else (x[:n_prompt], x[n_prompt:])
            x, h = _pool_layer(xa, xb, norm_mix[i], pool_w[j].astype(bf16), pool_scale[j], norm_ffn[i], seq)
            act = _glu_up(h, ffn_w_gate[j].astype(bf16)[None], ffn_w_up[j].astype(bf16)[None])
            x = _down_res(act, ffn_w_down[j].astype(bf16), x, 512)
        else:
            h = _rmsnorm(x, norm_mix[i], bf16)
            w_in = hgrn_w_in[j].astype(bf16)
            q = _proj_section(_proj_silu_heads_kernel, h, w_in, 0, True)
            forget = functools.partial(_proj_forget_kernel, layer=i)
            kf, gf = _proj_section(forget, h, w_in, 1, True, 2, hgrn_lb)
            kb, gb = _proj_section(forget, h, w_in, 2, True, 2, hgrn_lb)
            v = _proj_section(_proj_heads_kernel, h, w_in, 3, True)
            gate = _proj_section(_proj_silu_kernel, h, w_in, 4, False)
            s_bw = _scan_bw_states(kb, gb, v, seq, chunk)
            og = _scan(q, kf, kb, gf, gb, v, s_bw, gate, hgrn_norm[j], seq, chunk)
            x = _down_res(og, hgrn_w_out[j].astype(bf16), x, 1024)
            last = i == depth - 1
            res = _moe_layer(x, norm_ffn[i], moe_router[j], moe_w_gate[j].astype(bf16),
                             moe_w_up[j].astype(bf16), moe_w_down[j].astype(bf16),
                             norm_final if last else None, parts)
            x, outs = (None, res) if last else (res, None)

    if outs is None:
        outs = [_rmsnorm(x, norm_final, f32, start, rows) for start, rows in parts]
    return (outs[0].reshape(bp, sp, d), outs[1].reshape(bs, ss, d))
```

```python
import functools

import jax
import jax.numpy as jnp
from jax import lax
from jax.experimental import pallas as pl
from jax.experimental.pallas import tpu as pltpu

EPS = 1e-6
POOL_WINDOWS = (2, 4, 8, 16)
POOL_HALO = 8
POOL_PAD = 16
POOL_BLOCK = 128
HEAD_DIM = 128
TOP_K = 2
V7X_VMEM_LIMIT = 56 * 1024 * 1024
LANES = 128
SUBLANES = 8
MOE_ROW_TILE = 512
MOE_FF_TILE = 512
MOE_DISPATCH_TILE = 512
MOE_COMBINE_TILE = 256
ROW_DMA_UNROLL = 8
EPILOGUE_COLS = 256

bf16 = jnp.bfloat16
f32 = jnp.float32


def _cparams(sem):
    return pltpu.CompilerParams(dimension_semantics=sem, vmem_limit_bytes=V7X_VMEM_LIMIT)


def _tile(n, pref):
    t = min(n, pref)
    while n % t:
        t //= 2
    return t


def _rms(x, g):
    ms = jnp.mean(x * x, axis=-1, keepdims=True)
    return x * lax.rsqrt(ms + EPS) * g


def _silu(x):
    return x * jax.nn.sigmoid(x)


def _seq_flags(tok, n_tok, n_prompt, seq_p, seq_s):
    in_p = tok < n_prompt
    rel = jnp.where(in_p, tok, tok - n_prompt)
    slen = jnp.where(in_p, seq_p, seq_s)
    is_start = (rel % slen) == 0
    is_end = ((rel + n_tok) % slen) == 0
    return is_start, is_end


def _rmsnorm_kernel(x_ref, g_ref, o_ref):
    o_ref[...] = _rms(x_ref[...], g_ref[...]).astype(o_ref.dtype)


def _rmsnorm(x, g, out_dtype, row_start=0, n_rows=None):
    n, d = x.shape
    n_rows = n if n_rows is None else n_rows
    tr = _tile(n_rows, 512)
    off = row_start // tr
    return pl.pallas_call(
        _rmsnorm_kernel,
        grid=(n_rows // tr,),
        in_specs=[pl.BlockSpec((tr, d), lambda i: (i + off, 0)),
                  pl.BlockSpec((1, d), lambda i: (0, 0))],
        out_specs=pl.BlockSpec((tr, d), lambda i: (i, 0)),
        out_shape=jax.ShapeDtypeStruct((n_rows, d), out_dtype),
        compiler_params=_cparams(("parallel",)),
    )(x, g.reshape(1, d))


def _pool_kernel(ap_ref, a_ref, an_ref, bp_ref, b_ref, bn_ref, gmix_ref, w_ref, scale_ref, gffn_ref,
                 x1_ref, h1_ref, hi_s, lo_s, *, seq):
    t, d = a_ref.shape
    dg = d // len(POOL_WINDOWS)
    tok = pl.program_id(0) * t
    is_start, is_end = _seq_flags(tok, t, *seq)
    in_a = tok < seq[0]
    g = gmix_ref[...]
    x = jnp.where(in_a, a_ref[...], b_ref[...])
    h = _rms(x, g)
    h_prev = jnp.where(is_start, 0.0, _rms(jnp.where(in_a, ap_ref[...], bp_ref[...]), g))
    h_next = jnp.where(is_end, 0.0, _rms(jnp.where(in_a, an_ref[...], bn_ref[...]), g))
    zeros = jnp.zeros_like(h_prev)

    def put(r0, val):
        hi = val.astype(bf16)
        hi_s[r0:r0 + val.shape[0], :] = hi
        lo_s[r0:r0 + val.shape[0], :] = (val - hi.astype(f32)).astype(bf16)

    put(0, jnp.concatenate([zeros, h_prev], axis=0))
    put(POOL_PAD, h)
    put(POOL_PAD + t, jnp.concatenate([h_next, zeros], axis=0))
    tail = hi_s.shape[0] - (2 * POOL_PAD + t)
    hi_s[2 * POOL_PAD + t:, :] = jnp.zeros((tail, d), bf16)
    lo_s[2 * POOL_PAD + t:, :] = jnp.zeros((tail, d), bf16)

    rr = lax.broadcasted_iota(jnp.int32, (POOL_BLOCK, 2 * POOL_BLOCK), 0)
    cc = lax.broadcasted_iota(jnp.int32, (POOL_BLOCK, 2 * POOL_BLOCK), 1)
    r = lax.broadcasted_iota(jnp.int32, (t, 1), 0)
    for gi, w in enumerate(POOL_WINDOWS):
        c0 = gi * dg
        half = w // 2
        band = ((cc >= POOL_PAD + rr - half) & (cc < POOL_PAD + rr + half)).astype(bf16)
        sums = []
        for r0 in range(0, t, POOL_BLOCK):
            sums.append(jnp.dot(band, hi_s[r0:r0 + 2 * POOL_BLOCK, c0:c0 + dg], preferred_element_type=f32)
                        + jnp.dot(band, lo_s[r0:r0 + 2 * POOL_BLOCK, c0:c0 + dg], preferred_element_type=f32))
        s = jnp.concatenate(sums, axis=0)
        lo = jnp.where(is_start, jnp.maximum(r - half, 0), r - half)
        hi = jnp.where(is_end, jnp.minimum(r + half, t), r + half)
        cnt = (hi - lo).astype(f32)
        dlt = s / cnt - h[:, c0:c0 + dg]
        y = jnp.dot(dlt.astype(bf16), w_ref[gi], preferred_element_type=f32)
        x1_ref[:, c0:c0 + dg] = x[:, c0:c0 + dg] + y * scale_ref[:, c0:c0 + dg]
    h1_ref[...] = _rms(x1_ref[...], gffn_ref[...]).astype(h1_ref.dtype)


def _stream_specs(rows, first_tile, t, d):
    hb = t // POOL_HALO
    last_halo = rows // POOL_HALO - 1
    last_tile = rows // t - 1
    return [
        pl.BlockSpec((POOL_HALO, d), lambda i: (jnp.clip((i - first_tile) * hb - 1, 0, last_halo), 0)),
        pl.BlockSpec((t, d), lambda i: (jnp.clip(i - first_tile, 0, last_tile), 0)),
        pl.BlockSpec((POOL_HALO, d), lambda i: (jnp.clip((i - first_tile + 1) * hb, 0, last_halo), 0)),
    ]


def _pool_layer(xa, xb, g_mix, w_grp, scale, g_ffn, seq):
    d = xa.shape[1]
    n = xa.shape[0] + xb.shape[0]
    t = _tile(min(seq[1], seq[2]), 256)
    row = lambda i: (0, 0)
    return pl.pallas_call(
        functools.partial(_pool_kernel, seq=seq),
        grid=(n // t,),
        in_specs=_stream_specs(xa.shape[0], 0, t, d) + _stream_specs(xb.shape[0], xa.shape[0] // t, t, d) + [
            pl.BlockSpec((1, d), row),
            pl.BlockSpec(w_grp.shape, lambda i: (0, 0, 0)),
            pl.BlockSpec((1, d), row),
            pl.BlockSpec((1, d), row),
        ],
        out_specs=[pl.BlockSpec((t, d), lambda i: (i, 0)),
                   pl.BlockSpec((t, d), lambda i: (i, 0))],
        out_shape=[jax.ShapeDtypeStruct((n, d), f32), jax.ShapeDtypeStruct((n, d), bf16)],
        scratch_shapes=[pltpu.VMEM((t + POOL_BLOCK, d), bf16)] * 2,
        compiler_params=_cparams(("parallel",)),
    )(xa, xa, xa, xb, xb, xb, g_mix.reshape(1, d), w_grp, scale.reshape(1, d), g_ffn.reshape(1, d))


def _glu_up_kernel(a_ref, wg_ref, wu_ref, o_ref):
    a = a_ref[...]
    gate = jnp.dot(a, wg_ref[...], preferred_element_type=f32)
    up = jnp.dot(a, wu_ref[...], preferred_element_type=f32)
    o_ref[...] = (_silu(gate) * up).astype(o_ref.dtype)


def _glu_up(a, wg, wu):
    m, k = a.shape
    e, _, f = wg.shape
    tm = _tile(m, 1024)
    tn = _tile(f, 512)
    nb = f // tn
    wspec = pl.BlockSpec((None, k, tn), lambda i, j: (j // nb, 0, j % nb))
    return pl.pallas_call(
        _glu_up_kernel,
        grid=(m // tm, e * nb),
        in_specs=[pl.BlockSpec((tm, k), lambda i, j: (i, 0)), wspec, wspec],
        out_specs=pl.BlockSpec((tm, tn), lambda i, j: (i, j)),
        out_shape=jax.ShapeDtypeStruct((m, e * f), bf16),
        compiler_params=_cparams(("parallel", "arbitrary")),
    )(a, wg, wu)


def _down_kernel(a_ref, w_ref, res_ref, o_ref):
    o_ref[...] = res_ref[...] + jnp.dot(a_ref[...], w_ref[...], preferred_element_type=f32)


def _down_res(a, w, res, tm):
    m, kk = a.shape
    n = w.shape[1]
    tm = _tile(m, tm)
    tn = _tile(n, 512)
    return pl.pallas_call(
        _down_kernel,
        grid=(m // tm, n // tn),
        in_specs=[pl.BlockSpec((tm, kk), lambda i, j: (i, 0)),
                  pl.BlockSpec((kk, tn), lambda i, j: (0, j)),
                  pl.BlockSpec((tm, tn), lambda i, j: (i, j))],
        out_specs=pl.BlockSpec((tm, tn), lambda i, j: (i, j)),
        out_shape=jax.ShapeDtypeStruct((m, n), f32),
        compiler_params=_cparams(("parallel", "arbitrary")),
    )(a, w, res)


def _column_blocks(a_ref, w_ref):
    a = a_ref[...]
    tn = w_ref.shape[1]
    cb = min(tn, EPILOGUE_COLS)
    for c0 in range(0, tn, cb):
        yield c0, jnp.dot(a, w_ref[:, c0:c0 + cb], preferred_element_type=f32)


def _store_head_block(o_ref, c0, val):
    for hh in range(val.shape[1] // HEAD_DIM):
        o_ref[c0 // HEAD_DIM + hh] = val[:, hh * HEAD_DIM:(hh + 1) * HEAD_DIM].astype(o_ref.dtype)


def _proj_silu_heads_kernel(a_ref, w_ref, o_ref):
    for c0, r in _column_blocks(a_ref, w_ref):
        _store_head_block(o_ref, c0, _silu(r))


def _proj_heads_kernel(a_ref, w_ref, o_ref):
    for c0, r in _column_blocks(a_ref, w_ref):
        _store_head_block(o_ref, c0, r)


def _proj_silu_kernel(a_ref, w_ref, o_ref):
    for c0, r in _column_blocks(a_ref, w_ref):
        o_ref[:, c0:c0 + r.shape[1]] = _silu(r).astype(o_ref.dtype)


def _proj_forget_kernel(a_ref, w_ref, lb_ref, k_ref, g_ref, *, layer):
    logits = lb_ref[...]
    ex = jnp.exp(logits - jnp.max(logits, axis=0, keepdims=True))
    sm = ex / jnp.sum(ex, axis=0, keepdims=True)
    lb_all = jnp.sum(sm[0:layer + 1], axis=0, keepdims=True) - sm[0:1]
    for c0, r in _column_blocks(a_ref, w_ref):
        lb = lb_all[:, c0:c0 + r.shape[1]]
        f = lb + (1.0 - lb) * jax.nn.sigmoid(r)
        _store_head_block(k_ref, c0, 1.0 - f)
        _store_head_block(g_ref, c0, jnp.log2(f))


def _proj_section(kernel, a, w, section, out_heads, n_out=1, extra=None):
    m, k = a.shape
    d = k
    tm = _tile(m, 1024)
    tn = _tile(d, 1024)
    nb = d // tn
    in_specs = [pl.BlockSpec((tm, k), lambda i, j: (i, 0)),
                pl.BlockSpec((k, tn), lambda i, j: (0, section * nb + j))]
    args = [a, w]
    if extra is not None:
        in_specs.append(pl.BlockSpec((extra.shape[0], tn), lambda i, j: (0, j)))
        args.append(extra)
    if out_heads:
        hpb = tn // HEAD_DIM
        ospec = pl.BlockSpec((hpb, tm, HEAD_DIM), lambda i, j: (j, i, 0))
        oshape = jax.ShapeDtypeStruct((d // HEAD_DIM, m, HEAD_DIM), bf16)
    else:
        ospec = pl.BlockSpec((tm, tn), lambda i, j: (i, j))
        oshape = jax.ShapeDtypeStruct((m, d), bf16)
    out = pl.pallas_call(
        kernel,
        grid=(m // tm, nb),
        in_specs=in_specs,
        out_specs=[ospec] * n_out if n_out > 1 else ospec,
        out_shape=[oshape] * n_out if n_out > 1 else oshape,
        compiler_params=_cparams(("parallel", "arbitrary")),
    )(*args)
    return out


MAX_DIRECT_LOG2_DECAY = 100.0
HEAD_UNROLL = 8


def _dot_nt(a, b):
    return lax.dot_general(a, b, (((1,), (1,)), ((), ())), preferred_element_type=f32)


def _dot_tn(a, b):
    return lax.dot_general(a, b, (((0,), (0,)), ((), ())), preferred_element_type=f32)


def _mid_ref(p, level, offset):
    c, w = p.shape
    blk = 2 << level
    p3 = p.reshape(c // blk, blk, w)
    return jnp.broadcast_to(p3[:, offset:offset + 1, :], p3.shape).reshape(c, w)


def _neg_abs_exponents(gf, gb, b, x, level, rmod4):
    if level == 0:
        odd = (rmod4 & 1) == 1
        return jnp.where(odd, gf, 0.0), jnp.where(odd, 0.0, gb)
    if level == 1:
        c = gf.shape[0]
        gf_up = pltpu.roll(gf, c - 1, 0)
        gf_dn = pltpu.roll(gf, 1, 0)
        gb_up = pltpu.roll(gb, c - 1, 0)
        gb_dn = pltpu.roll(gb, 1, 0)
        ef = jnp.where(rmod4 == 0, gf_up, jnp.where(rmod4 == 1, 0.0, jnp.where(rmod4 == 2, gf, gf + gf_dn)))
        eb = jnp.where(rmod4 == 0, gb + gb_up, jnp.where(rmod4 == 1, gb, jnp.where(rmod4 == 2, 0.0, gb_dn)))
        return ef, eb
    half = 1 << level
    ef = b - _mid_ref(b, level, half - 1)
    eb = x - _mid_ref(x, level, half)
    return -jnp.abs(ef), -jnp.abs(eb)


def _chunk_prefix(tri_incl, tri_excl, gf16, gb16):
    b = jnp.dot(tri_incl, gf16, preferred_element_type=f32)
    x = jnp.dot(tri_excl, gb16, preferred_element_type=f32)
    return b, x


def _scan_bw_kernel(kb_ref, gb_ref, v_ref, s_ref, st_ref, *, seq, n_chunks):
    n_heads, c, _ = kb_ref.shape
    chunk = n_chunks - 1 - pl.program_id(0)
    _, is_end = _seq_flags(chunk * c, c, *seq)

    @pl.when(is_end)
    def _():
        st_ref[...] = jnp.zeros_like(st_ref)

    row = lax.broadcasted_iota(jnp.int32, (c, c), 0)
    col = lax.broadcasted_iota(jnp.int32, (c, c), 1)
    tri_excl = (col < row).astype(bf16)

    def head(h, carry):
        gb16 = gb_ref[h]
        x = jnp.dot(tri_excl, gb16, preferred_element_type=f32)
        tot = x[c - 1:c, :] + gb16[c - 1:c, :].astype(f32)
        st = st_ref[h]
        s_ref[h] = st.astype(s_ref.dtype)
        kx = (kb_ref[h].astype(f32) * jnp.exp2(x)).astype(bf16)
        st_ref[h] = st * jnp.exp2(tot) + _dot_tn(v_ref[h], kx)
        return carry

    lax.fori_loop(0, n_heads, head, 0, unroll=2 * HEAD_UNROLL)


def _scan_bw_states(kb, gb, v, seq, c):
    n_heads, n, dk = kb.shape
    n_chunks = n // c
    spec = pl.BlockSpec((n_heads, c, dk), lambda i: (0, n_chunks - 1 - i, 0))
    return pl.pallas_call(
        functools.partial(_scan_bw_kernel, seq=seq, n_chunks=n_chunks),
        grid=(n_chunks,),
        in_specs=[spec, spec, spec],
        out_specs=pl.BlockSpec((None, n_heads, dk, dk), lambda i: (n_chunks - 1 - i, 0, 0, 0)),
        out_shape=jax.ShapeDtypeStruct((n_chunks, n_heads, dk, dk), bf16),
        scratch_shapes=[pltpu.VMEM((n_heads, dk, dk), f32)],
        compiler_params=_cparams(("arbitrary",)),
    )(kb, gb, v)


def _scan_kernel(q_ref, kf_ref, kb_ref, gf_ref, gb_ref, v_ref, sb_ref, tril_ref, triu_ref, gate_ref,
                 gain_ref, out_ref, st_ref, o_s, qf_s, kf_s, qb_s, kb_s, qin_s, kx_s, sc_s, dec_s, tot_s,
                 *, seq):
    n_heads, c, dk = q_ref.shape
    n_levels = c.bit_length() - 1
    mid = c // 2
    is_start, _ = _seq_flags(pl.program_id(0) * c, c, *seq)

    @pl.when(is_start)
    def _():
        st_ref[...] = jnp.zeros_like(st_ref)

    row = lax.broadcasted_iota(jnp.int32, (c, c), 0)
    col = lax.broadcasted_iota(jnp.int32, (c, c), 1)
    tri_incl = (col <= row).astype(bf16)
    tri_excl = (col < row).astype(bf16)

    def finish(h, kf_dec, b_last, q_in, scores, diag):
        v = v_ref[h]
        st_f = st_ref[h]
        st_cat = jnp.concatenate([st_f.astype(bf16), sb_ref[h]], axis=1)
        o = _dot_nt(q_in, st_cat) + jnp.dot(scores.astype(bf16), v, preferred_element_type=f32)
        o_s[h] = o if diag is None else o + diag * v.astype(f32)
        st_ref[h] = st_f * jnp.exp2(b_last) + _dot_tn(v, kf_dec.astype(bf16))

    def direct_operands(h, carry):
        q = q_ref[h].astype(f32)
        gb16 = gb_ref[h]
        b, x = _chunk_prefix(tri_incl, tri_excl, gf_ref[h], gb16)
        b_mid = b[mid - 1:mid, :]
        x_mid = x[mid:mid + 1, :]
        db = b - b_mid
        dx = x - x_mid
        q_f = q * jnp.exp2(db)
        k_f = kf_ref[h].astype(f32) * jnp.exp2(-db)
        q_b = q * jnp.exp2(-dx)
        k_b = kb_ref[h].astype(f32) * jnp.exp2(dx)
        b_last = b[c - 1:c, :]
        x_tot = x[c - 1:c, :] + gb16[c - 1:c, :].astype(f32)
        qf_s[h] = q_f.astype(bf16)
        kf_s[h] = k_f.astype(bf16)
        qb_s[h] = q_b.astype(bf16)
        kb_s[h] = k_b.astype(bf16)
        qin_s[h] = jnp.concatenate([q_f * jnp.exp2(b_mid), q_b * jnp.exp2(x_tot - x_mid)], axis=1).astype(bf16)
        kx_s[h] = (k_f * jnp.exp2(b_last - b_mid)).astype(bf16)
        dec_s[h] = jnp.broadcast_to(jnp.exp2(b_last), dec_s.shape[1:])
        tot_s[h] = jnp.broadcast_to(jnp.minimum(b_last, x_tot), tot_s.shape[1:])
        return carry

    def direct_scores(h, carry):
        sc_s[h] = (jnp.where(tril_ref[...] != 0.0, _dot_nt(qf_s[h], kf_s[h]), 0.0)
                   + jnp.where(triu_ref[...] != 0.0, _dot_nt(qb_s[h], kb_s[h]), 0.0)).astype(bf16)
        return carry

    def direct_output(h, carry):
        v = v_ref[h]
        st_f = st_ref[h]
        st_cat = jnp.concatenate([st_f.astype(bf16), sb_ref[h]], axis=1)
        o_s[h] = _dot_nt(qin_s[h], st_cat) + jnp.dot(sc_s[h], v, preferred_element_type=f32)
        st_ref[h] = st_f * dec_s[h][0:1, :] + _dot_tn(v, kx_s[h])
        return carry

    def head_levelled(h, carry):
        rtok = lax.broadcasted_iota(jnp.int32, (c, dk), 0)
        rmod4 = rtok & 3
        q = q_ref[h].astype(f32)
        kf = kf_ref[h].astype(f32)
        kb = kb_ref[h].astype(f32)
        gf16 = gf_ref[h]
        gb16 = gb_ref[h]
        gf = gf16.astype(f32)
        gb = gb16.astype(f32)
        b, x = _chunk_prefix(tri_incl, tri_excl, gf16, gb16)
        scores = jnp.zeros((c, c), f32)
        for l in range(n_levels):
            ef, eb = _neg_abs_exponents(gf, gb, b, x, l, rmod4)
            af = jnp.exp2(ef)
            ab = jnp.exp2(eb)
            up = ((rtok >> l) & 1) == 1
            lhs = jnp.concatenate([jnp.where(up, q * af, 0.0), jnp.where(up, 0.0, q * ab)], axis=1)
            rhs = jnp.concatenate([jnp.where(up, 0.0, kf * af), jnp.where(up, kb * ab, 0.0)], axis=1)
            p = _dot_nt(lhs.astype(bf16), rhs.astype(bf16))
            if l < n_levels - 1:
                p = jnp.where((row >> (l + 1)) == (col >> (l + 1)), p, 0.0)
            scores = scores + p
        b_last = b[c - 1:c, :]
        x_tot = x[c - 1:c, :] + gb[c - 1:c, :]
        q_in = jnp.concatenate([q * jnp.exp2(b), q * jnp.exp2(x_tot - x)], axis=1).astype(bf16)
        diag = jnp.sum(q * (kf + kb), axis=1, keepdims=True)
        finish(h, kf * jnp.exp2(b_last - b), b_last, q_in, scores, diag)
        return carry

    lax.fori_loop(0, n_heads, direct_operands, 0, unroll=2 * HEAD_UNROLL)
    direct_ok = jnp.min(tot_s[...]) >= -MAX_DIRECT_LOG2_DECAY

    @pl.when(direct_ok)
    def _():
        lax.fori_loop(0, n_heads, direct_scores, 0, unroll=2 * HEAD_UNROLL)
        lax.fori_loop(0, n_heads, direct_output, 0, unroll=2 * HEAD_UNROLL)

    @pl.when(jnp.logical_not(direct_ok))
    def _():
        lax.fori_loop(0, n_heads, head_levelled, 0)

    sq = jnp.zeros((c, dk), f32)
    for hh in range(n_heads):
        o = o_s[hh]
        sq = sq + o * o
    inv = lax.rsqrt(jnp.sum(sq, axis=1, keepdims=True) / (n_heads * dk) + EPS)
    for hh in range(n_heads):
        cols = slice(hh * dk, (hh + 1) * dk)
        out_ref[:, cols] = (o_s[hh] * inv * gain_ref[:, cols] * gate_ref[:, cols].astype(f32)
                            ).astype(out_ref.dtype)


def _scan(q, kf, kb, gf, gb, v, s_bw, gate, gain, seq, c):
    n_heads, n, dk = q.shape
    d = n_heads * dk
    n_chunks = n // c
    spec = pl.BlockSpec((n_heads, c, dk), lambda i: (0, i, 0))
    tri = pl.BlockSpec((c, c), lambda i: (0, 0))
    tril = jnp.tril(jnp.ones((c, c), f32))
    return pl.pallas_call(
        functools.partial(_scan_kernel, seq=seq),
        grid=(n_chunks,),
        in_specs=[spec] * 6 + [pl.BlockSpec((None, n_heads, dk, dk), lambda i: (i, 0, 0, 0)), tri, tri,
                               pl.BlockSpec((c, d), lambda i: (i, 0)),
                               pl.BlockSpec((1, d), lambda i: (0, 0))],
        out_specs=pl.BlockSpec((c, d), lambda i: (i, 0)),
        out_shape=jax.ShapeDtypeStruct((n, d), bf16),
        scratch_shapes=[pltpu.VMEM((n_heads, dk, dk), f32), pltpu.VMEM((n_heads, c, dk), f32)]
        + [pltpu.VMEM((n_heads, c, dk), bf16)] * 4
        + [pltpu.VMEM((n_heads, c, 2 * dk), bf16), pltpu.VMEM((n_heads, c, dk), bf16),
           pltpu.VMEM((n_heads, c, c), bf16)] + [pltpu.VMEM((n_heads, 8, dk), f32)] * 2,
        compiler_params=_cparams(("arbitrary",)),
    )(q, kf, kb, gf, gb, v, s_bw, tril, tril.T, gate, gain.reshape(1, d))


def _pack_halves(x):
    half = x.shape[1] // 2
    bits = lax.bitcast_convert_type(x.astype(bf16).astype(f32), jnp.int32)
    return lax.shift_right_logical(bits[:, :half], 16) | (bits[:, half:] & jnp.int32(-65536))


def _unpack_halves(words):
    lo = lax.bitcast_convert_type(lax.shift_left(words, 16), f32)
    hi = lax.bitcast_convert_type(words & jnp.int32(-65536), f32)
    return lo, hi


def _router_kernel(x_ref, g_ref, r2_ref, h_ref, e_ref, rank_ref, w_ref, cnt_ref, run_ref,
                   *, n_experts):
    @pl.when(pl.program_id(0) == 0)
    def _():
        run_ref[...] = jnp.zeros_like(run_ref)

    h = _rms(x_ref[...], g_ref[...])
    h_ref[...] = _pack_halves(h)
    hi = h.astype(bf16)
    lo = (h - hi.astype(f32)).astype(bf16)
    tr = h.shape[0]
    prod = jnp.dot(jnp.concatenate([hi, lo], axis=0), r2_ref[...], preferred_element_type=f32)
    logits = prod[:tr, :LANES] + prod[:tr, LANES:] + prod[tr:, :LANES]
    lane = lax.broadcasted_iota(jnp.int32, logits.shape, 1)
    neg = jnp.float32(-jnp.inf)
    logits = jnp.where(lane < n_experts, logits, neg)
    m1 = jnp.max(logits, axis=1, keepdims=True)
    i1 = jnp.min(jnp.where(logits == m1, lane, LANES), axis=1, keepdims=True)
    rest = jnp.where(lane == i1, neg, logits)
    m2 = jnp.max(rest, axis=1, keepdims=True)
    i2 = jnp.min(jnp.where(rest == m2, lane, LANES), axis=1, keepdims=True)
    e2 = jnp.exp(m2 - m1)
    w_ref[...] = jnp.concatenate([1.0 / (1.0 + e2), e2 / (1.0 + e2)], axis=1)
    e_ref[...] = jnp.concatenate([i1, i2], axis=1)
    member = jnp.where((lane == i1) | (lane == i2), 1.0, 0.0)
    row = lax.broadcasted_iota(jnp.int32, (tr, tr), 0)
    col = lax.broadcasted_iota(jnp.int32, (tr, tr), 1)
    before = jnp.dot((col < row).astype(bf16), member.astype(bf16), preferred_element_type=f32)
    before = before + run_ref[...]
    r1 = jnp.sum(jnp.where(lane == i1, before, 0.0), axis=1, keepdims=True)
    r2 = jnp.sum(jnp.where(lane == i2, before, 0.0), axis=1, keepdims=True)
    rank_ref[...] = jnp.concatenate([r1, r2], axis=1).astype(jnp.int32)
    run = run_ref[...] + jnp.sum(member, axis=0, keepdims=True)
    run_ref[...] = run
    cnt_ref[...] = run.astype(jnp.int32)


def _router(x, g, router):
    n, d = x.shape
    e = router.shape[1]
    rpad = jnp.zeros((d, LANES), f32).at[:, :e].set(router)
    rhi = rpad.astype(bf16)
    r2 = jnp.concatenate([rhi, (rpad - rhi.astype(f32)).astype(bf16)], axis=1)
    tr = _tile(n, 512)
    pair = pl.BlockSpec((tr, TOP_K), lambda i: (i, 0))
    return pl.pallas_call(
        functools.partial(_router_kernel, n_experts=e),
        grid=(n // tr,),
        in_specs=[pl.BlockSpec((tr, d), lambda i: (i, 0)),
                  pl.BlockSpec((1, d), lambda i: (0, 0)),
                  pl.BlockSpec((d, 2 * LANES), lambda i: (0, 0))],
        out_specs=[pl.BlockSpec((tr, d // 2), lambda i: (i, 0)), pair, pair, pair,
                   pl.BlockSpec((1, LANES), lambda i: (0, 0))],
        out_shape=[jax.ShapeDtypeStruct((n, d // 2), jnp.int32),
                   jax.ShapeDtypeStruct((n, TOP_K), jnp.int32),
                   jax.ShapeDtypeStruct((n, TOP_K), jnp.int32),
                   jax.ShapeDtypeStruct((n, TOP_K), f32),
                   jax.ShapeDtypeStruct((1, LANES), jnp.int32)],
        scratch_shapes=[pltpu.VMEM((1, LANES), f32)],
        compiler_params=_cparams(("arbitrary",)),
    )(x, g.reshape(1, d), r2)


def _row_copies(pos_ref, i, src, dst, sem, gather):
    out = []
    for k in range(TOP_K):
        p = pos_ref[k, i]
        if gather:
            out.append(pltpu.make_async_copy(src.at[pl.ds(p, 1)], dst.at[k, pl.ds(i, 1)], sem))
        else:
            out.append(pltpu.make_async_copy(src.at[pl.ds(i, 1)], dst.at[pl.ds(p, 1)], sem))
    return out


def _start_rows(n_rows, pos_ref, src, dst, sem, gather):
    def start(i, carry):
        for cp in _row_copies(pos_ref, i, src, dst, sem, gather):
            cp.start()
        return carry

    lax.fori_loop(0, n_rows, start, 0, unroll=ROW_DMA_UNROLL)


def _wait_rows(n_rows, pos_ref, src, dst, sem, gather):
    def wait(i, carry):
        for cp in _row_copies(pos_ref, i, src, dst, sem, gather):
            cp.wait()
        return carry

    lax.fori_loop(0, n_rows, wait, 0, unroll=ROW_DMA_UNROLL)


def _all_rows(n_rows, pos_ref, src, dst, sem, gather):
    _start_rows(n_rows, pos_ref, src, dst, sem, gather)
    _wait_rows(n_rows, pos_ref, src, dst, sem, gather)


def _moe_dispatch_kernel(nv_ref, pos_ref, h_ref, xs_ref, zero_s, sem):
    tm = zero_s.shape[0]

    @pl.when(pl.program_id(0) == 0)
    def _():
        zero_s[...] = jnp.zeros_like(zero_s)

        def fill(r, carry):
            @pl.when(nv_ref[r] < tm)
            def _():
                cp = pltpu.make_async_copy(zero_s, xs_ref.at[pl.ds(pl.multiple_of(r * tm, tm), tm)], sem)
                cp.start()
                cp.wait()
            return carry

        lax.fori_loop(0, xs_ref.shape[0] // tm, fill, 0)

    _all_rows(h_ref.shape[0], pos_ref, h_ref, xs_ref, sem, gather=False)


def _moe_dispatch(hp, pos3, tile_rows, tm):
    n, dw = hp.shape
    n_tiles, _, tr = pos3.shape
    grid_spec = pltpu.PrefetchScalarGridSpec(
        num_scalar_prefetch=1,
        grid=(n_tiles,),
        in_specs=[pl.BlockSpec((None, TOP_K, tr), lambda i, nv: (i, 0, 0), memory_space=pltpu.SMEM),
                  pl.BlockSpec((tr, dw), lambda i, nv: (i, 0))],
        out_specs=pl.BlockSpec(memory_space=pl.ANY),
        scratch_shapes=[pltpu.VMEM((tm, dw), jnp.int32), pltpu.SemaphoreType.DMA(())],
    )
    return pl.pallas_call(
        _moe_dispatch_kernel,
        grid_spec=grid_spec,
        out_shape=jax.ShapeDtypeStruct((tile_rows.shape[0] * tm, dw), jnp.int32),
        compiler_params=_cparams(("arbitrary",)),
    )(tile_rows, pos3, hp)


def _moe_up_kernel(te_ref, nv_ref, xs_ref, wg_ref, wu_ref, a_ref, x_s):
    r = pl.program_id(0)
    half = xs_ref.shape[1]

    @pl.when(nv_ref[r] > 0)
    def _():
        @pl.when(pl.program_id(1) == 0)
        def _():
            lo, hi = _unpack_halves(xs_ref[...])
            x_s[:, :half] = lo.astype(bf16)
            x_s[:, half:] = hi.astype(bf16)

        x = x_s[...]
        gate = jnp.dot(x, wg_ref[...], preferred_element_type=f32)
        up = jnp.dot(x, wu_ref[...], preferred_element_type=f32)
        a_ref[...] = (_silu(gate) * up).astype(a_ref.dtype)

    @pl.when(nv_ref[r] == 0)
    def _():
        a_ref[...] = jnp.zeros_like(a_ref)


def _moe_down_kernel(te_ref, nv_ref, a_ref, wd_ref, y_ref):
    @pl.when(nv_ref[pl.program_id(0)] > 0)
    def _():
        y_ref[...] = _pack_halves(jnp.dot(a_ref[...], wd_ref[...], preferred_element_type=f32))

    @pl.when(nv_ref[pl.program_id(0)] == 0)
    def _():
        y_ref[...] = jnp.zeros_like(y_ref)


def _moe_group(xs, tile_expert, tile_rows, wg, wu, wd, tm):
    p, dw = xs.shape
    _, d, fe = wg.shape
    tf = _tile(fe, MOE_FF_TILE)
    wspec = pl.BlockSpec((None, d, tf), lambda r, f, te, nv: (te[r], 0, f))
    act = pl.pallas_call(
        _moe_up_kernel,
        grid_spec=pltpu.PrefetchScalarGridSpec(
            num_scalar_prefetch=2,
            grid=(p // tm, fe // tf),
            in_specs=[pl.BlockSpec((tm, dw), lambda r, f, te, nv: (r, 0)), wspec, wspec],
            out_specs=pl.BlockSpec((tm, tf), lambda r, f, te, nv: (r, f)),
            scratch_shapes=[pltpu.VMEM((tm, d), bf16)],
        ),
        out_shape=jax.ShapeDtypeStruct((p, fe), bf16),
        compiler_params=_cparams(("arbitrary", "arbitrary")),
    )(tile_expert, tile_rows, xs, wg, wu)
    return pl.pallas_call(
        _moe_down_kernel,
        grid_spec=pltpu.PrefetchScalarGridSpec(
            num_scalar_prefetch=2,
            grid=(p // tm,),
            in_specs=[pl.BlockSpec((tm, fe), lambda r, te, nv: (r, 0)),
                      pl.BlockSpec((None, fe, d), lambda r, te, nv: (te[r], 0, 0))],
            out_specs=pl.BlockSpec((tm, dw), lambda r, te, nv: (r, 0)),
        ),
        out_shape=jax.ShapeDtypeStruct((p, dw), jnp.int32),
        compiler_params=_cparams(("arbitrary",)),
    )(tile_expert, tile_rows, act, wd)


def _moe_combine_kernel(pos_ref, pos_next_ref, x_ref, w_ref, g_ref, y_ref, o_ref, ybuf, sem, *, final_norm):
    tr, d = x_ref.shape
    half = d // 2
    i = pl.program_id(0)
    slot = i % 2

    @pl.when(i == 0)
    def _():
        _start_rows(tr, pos_ref, y_ref, ybuf.at[0], sem.at[0], gather=True)

    _wait_rows(tr, pos_ref, y_ref, ybuf.at[slot], sem.at[slot], gather=True)
    for r0 in range(0, tr, SUBLANES):
        rows = slice(r0, r0 + SUBLANES)
        for ii in range(r0, r0 + SUBLANES):
            for cp in _row_copies(pos_next_ref, ii, y_ref, ybuf.at[1 - slot], sem.at[1 - slot], True):
                cp.start()
        w = w_ref[rows, :]
        lo0, hi0 = _unpack_halves(ybuf[slot, 0, rows, :])
        lo1, hi1 = _unpack_halves(ybuf[slot, 1, rows, :])
        o_lo = x_ref[rows, :half] + (w[:, 0:1] * lo0 + w[:, 1:2] * lo1)
        o_hi = x_ref[rows, half:] + (w[:, 0:1] * hi0 + w[:, 1:2] * hi1)
        if final_norm:
            ms = (jnp.sum(o_lo * o_lo, axis=1, keepdims=True)
                  + jnp.sum(o_hi * o_hi, axis=1, keepdims=True)) / d
            inv = lax.rsqrt(ms + EPS)
            o_lo = o_lo * inv * g_ref[:, :half]
            o_hi = o_hi * inv * g_ref[:, half:]
        o_ref[rows, :half] = o_lo
        o_ref[rows, half:] = o_hi

    @pl.when(i == pl.num_programs(0) - 1)
    def _():
        _wait_rows(tr, pos_next_ref, y_ref, ybuf.at[1 - slot], sem.at[1 - slot], gather=True)


def _moe_combine(x, w, y, pos3, gain, final_norm, row_start, n_rows):
    d = x.shape[1]
    tr = pos3.shape[2]
    off = row_start // tr
    last = off + n_rows // tr - 1
    return pl.pallas_call(
        functools.partial(_moe_combine_kernel, final_norm=final_norm),
        grid=(n_rows // tr,),
        in_specs=[pl.BlockSpec((None, TOP_K, tr), lambda i: (i + off, 0, 0), memory_space=pltpu.SMEM),
                  pl.BlockSpec((None, TOP_K, tr), lambda i: (jnp.minimum(i + off + 1, last), 0, 0),
                               memory_space=pltpu.SMEM),
                  pl.BlockSpec((tr, d), lambda i: (i + off, 0)),
                  pl.BlockSpec((tr, TOP_K), lambda i: (i + off, 0)),
                  pl.BlockSpec((1, d), lambda i: (0, 0)),
                  pl.BlockSpec(memory_space=pl.ANY)],
        out_specs=pl.BlockSpec((tr, d), lambda i: (i, 0)),
        out_shape=jax.ShapeDtypeStruct((n_rows, d), f32),
        scratch_shapes=[pltpu.VMEM((2, TOP_K, tr, d // 2), jnp.int32), pltpu.SemaphoreType.DMA((2,))],
        compiler_params=_cparams(("arbitrary",)),
    )(pos3, pos3, x, w, gain.reshape(1, d), y)


def _moe_layer(x, g, router, wg, wu, wd, final_gain, parts):
    n, d = x.shape
    n_exp = router.shape[1]
    tm = MOE_ROW_TILE
    hp, eid, rank, w, cnt = _router(x, g, router)
    cnt = cnt[0, :n_exp]
    padded = (cnt + tm - 1) // tm * tm
    seg_end = jnp.cumsum(padded)
    seg_start = seg_end - padded
    pos = seg_start[eid] + rank
    n_rows_padded = TOP_K * n + n_exp * tm
    tile0 = jnp.arange(n_rows_padded // tm, dtype=jnp.int32) * tm
    tile_expert = jnp.minimum(jnp.searchsorted(seg_end, tile0, side="right"), n_exp - 1).astype(jnp.int32)
    tile_rows = jnp.clip(cnt[tile_expert] - (tile0 - seg_start[tile_expert]), 0, tm).astype(jnp.int32)
    def by_tile(tr):
        return pos.reshape(n // tr, tr, TOP_K).transpose(0, 2, 1)

    pos3 = by_tile(_tile(n, MOE_COMBINE_TILE))
    xs = _moe_dispatch(hp, by_tile(_tile(n, MOE_DISPATCH_TILE)), tile_rows, tm)
    y = _moe_group(xs, tile_expert, tile_rows, wg, wu, wd, tm)
    if final_gain is None:
        return _moe_combine(x, w, y, pos3, g, False, 0, n)
    return [_moe_combine(x, w, y, pos3, final_gain, True, start, rows) for start, rows in parts]


def kernel(x_prompt, x_sample, norm_mix, norm_ffn, norm_final, pool_w, pool_scale, hgrn_w_in, hgrn_lb,
           hgrn_norm, hgrn_w_out, ffn_w_gate, ffn_w_up, ffn_w_down, moe_router, moe_w_gate, moe_w_up,
           moe_w_down):
    bp, sp, d = x_prompt.shape
    bs, ss, _ = x_sample.shape
    n_prompt = bp * sp
    n = n_prompt + bs * ss
    seq = (n_prompt, sp, ss)
    depth = norm_mix.shape[0]
    chunk = _tile(min(sp, ss), 128)

    parts = ((0, n_prompt), (n_prompt, bs * ss))
    streams = (x_prompt.reshape(n_prompt, d), x_sample.reshape(bs * ss, d))
    x = None if depth else jnp.concatenate(streams, axis=0)
    outs = None

    for i in range(depth):
        j = i // 2
        if i % 2 == 0:
            xa, xb = streams if x is None else (x[:n_prompt], x[n_prompt:])
            x, h = _pool_layer(xa, xb, norm_mix[i], pool_w[j].astype(bf16), pool_scale[j], norm_ffn[i], seq)
            act = _glu_up(h, ffn_w_gate[j].astype(bf16)[None], ffn_w_up[j].astype(bf16)[None])
            x = _down_res(act, ffn_w_down[j].astype(bf16), x, 512)
        else:
            h = _rmsnorm(x, norm_mix[i], bf16)
            w_in = hgrn_w_in[j].astype(bf16)
            q = _proj_section(_proj_silu_heads_kernel, h, w_in, 0, True)
            forget = functools.partial(_proj_forget_kernel, layer=i)
            kf, gf = _proj_section(forget, h, w_in, 1, True, 2, hgrn_lb)
            kb, gb = _proj_section(forget, h, w_in, 2, True, 2, hgrn_lb)
            v = _proj_section(_proj_heads_kernel, h, w_in, 3, True)
            gate = _proj_section(_proj_silu_kernel, h, w_in, 4, False)
            s_bw = _scan_bw_states(kb, gb, v, seq, chunk)
            og = _scan(q, kf, kb, gf, gb, v, s_bw, gate, hgrn_norm[j], seq, chunk)
            x = _down_res(og, hgrn_w_out[j].astype(bf16), x, 1024)
            last = i == depth - 1
            res = _moe_layer(x, norm_ffn[i], moe_router[j], moe_w_gate[j].astype(bf16),
                             moe_w_up[j].astype(bf16), moe_w_down[j].astype(bf16),
                             norm_final if last else None, parts)
            x, outs = (None, res) if last else (res, None)

    if outs is None:
        outs = [_rmsnorm(x, norm_final, f32, start, rows) for start, rows in parts]
    return (outs[0].reshape(bp, sp, d), outs[1].reshape(bs, ss, d))
```

```python
import functools

import jax
import jax.numpy as jnp
from jax import lax
from jax.experimental import pallas as pl
from jax.experimental.pallas import tpu as pltpu

EPS = 1e-6
POOL_WINDOWS = (2, 4, 8, 16)
POOL_HALO = 8
POOL_PAD = 16
POOL_BLOCK = 128
HEAD_DIM = 128
TOP_K = 2
V7X_VMEM_LIMIT = 56 * 1024 * 1024
LANES = 128
SUBLANES = 8
MOE_ROW_TILE = 512
MOE_FF_TILE = 512
MOE_DISPATCH_TILE = 512
MOE_COMBINE_TILE = 256
ROW_DMA_UNROLL = 8
EPILOGUE_COLS = 256
SIDE_CAST_COLS = 4096
BF16_SUBLANES = 16

bf16 = jnp.bfloat16
f32 = jnp.float32


def _cparams(sem):
    return pltpu.CompilerParams(dimension_semantics=sem, vmem_limit_bytes=V7X_VMEM_LIMIT)


def _tile(n, pref):
    t = min(n, pref)
    while n % t:
        t //= 2
    return t


def _rms(x, g):
    ms = jnp.mean(x * x, axis=-1, keepdims=True)
    return x * lax.rsqrt(ms + EPS) * g


def _silu(x):
    return x * jax.nn.sigmoid(x)


def _seq_flags(tok, n_tok, n_prompt, seq_p, seq_s):
    in_p = tok < n_prompt
    rel = jnp.where(in_p, tok, tok - n_prompt)
    slen = jnp.where(in_p, seq_p, seq_s)
    is_start = (rel % slen) == 0
    is_end = ((rel + n_tok) % slen) == 0
    return is_start, is_end


def _rmsnorm_kernel(x_ref, g_ref, o_ref):
    o_ref[...] = _rms(x_ref[...], g_ref[...]).astype(o_ref.dtype)


def _rmsnorm(x, g, out_dtype, row_start=0, n_rows=None):
    n, d = x.shape
    n_rows = n if n_rows is None else n_rows
    tr = _tile(n_rows, 512)
    off = row_start // tr
    return pl.pallas_call(
        _rmsnorm_kernel,
        grid=(n_rows // tr,),
        in_specs=[pl.BlockSpec((tr, d), lambda i: (i + off, 0)),
                  pl.BlockSpec((1, d), lambda i: (0, 0))],
        out_specs=pl.BlockSpec((tr, d), lambda i: (i, 0)),
        out_shape=jax.ShapeDtypeStruct((n_rows, d), out_dtype),
        compiler_params=_cparams(("parallel",)),
    )(x, g.reshape(1, d))


def _pool_kernel(ap_ref, a_ref, an_ref, bp_ref, b_ref, bn_ref, gmix_ref, w_ref, scale_ref, gffn_ref,
                 x1_ref, h1_ref, hi_s, lo_s, *, seq):
    t, d = a_ref.shape
    dg = d // len(POOL_WINDOWS)
    tok = pl.program_id(0) * t
    is_start, is_end = _seq_flags(tok, t, *seq)
    in_a = tok < seq[0]
    g = gmix_ref[...]
    x = jnp.where(in_a, a_ref[...], b_ref[...])
    h = _rms(x, g)
    h_prev = jnp.where(is_start, 0.0, _rms(jnp.where(in_a, ap_ref[...], bp_ref[...]), g))
    h_next = jnp.where(is_end, 0.0, _rms(jnp.where(in_a, an_ref[...], bn_ref[...]), g))
    zeros = jnp.zeros_like(h_prev)

    def put(r0, val):
        hi = val.astype(bf16)
        hi_s[r0:r0 + val.shape[0], :] = hi
        lo_s[r0:r0 + val.shape[0], :] = (val - hi.astype(f32)).astype(bf16)

    put(0, jnp.concatenate([zeros, h_prev], axis=0))
    put(POOL_PAD, h)
    put(POOL_PAD + t, jnp.concatenate([h_next, zeros], axis=0))
    tail = hi_s.shape[0] - (2 * POOL_PAD + t)
    hi_s[2 * POOL_PAD + t:, :] = jnp.zeros((tail, d), bf16)
    lo_s[2 * POOL_PAD + t:, :] = jnp.zeros((tail, d), bf16)

    rr = lax.broadcasted_iota(jnp.int32, (POOL_BLOCK, 2 * POOL_BLOCK), 0)
    cc = lax.broadcasted_iota(jnp.int32, (POOL_BLOCK, 2 * POOL_BLOCK), 1)
    r = lax.broadcasted_iota(jnp.int32, (t, 1), 0)
    for gi, w in enumerate(POOL_WINDOWS):
        c0 = gi * dg
        half = w // 2
        band = ((cc >= POOL_PAD + rr - half) & (cc < POOL_PAD + rr + half)).astype(bf16)
        sums = []
        for r0 in range(0, t, POOL_BLOCK):
            sums.append(jnp.dot(band, hi_s[r0:r0 + 2 * POOL_BLOCK, c0:c0 + dg], preferred_element_type=f32)
                        + jnp.dot(band, lo_s[r0:r0 + 2 * POOL_BLOCK, c0:c0 + dg], preferred_element_type=f32))
        s = jnp.concatenate(sums, axis=0)
        lo = jnp.where(is_start, jnp.maximum(r - half, 0), r - half)
        hi = jnp.where(is_end, jnp.minimum(r + half, t), r + half)
        cnt = (hi - lo).astype(f32)
        dlt = s / cnt - h[:, c0:c0 + dg]
        y = jnp.dot(dlt.astype(bf16), w_ref[gi], preferred_element_type=f32)
        x1_ref[:, c0:c0 + dg] = x[:, c0:c0 + dg] + y * scale_ref[:, c0:c0 + dg]
    h1_ref[...] = _rms(x1_ref[...], gffn_ref[...]).astype(h1_ref.dtype)


def _stream_specs(rows, first_tile, t, d):
    hb = t // POOL_HALO
    last_halo = rows // POOL_HALO - 1
    last_tile = rows // t - 1
    return [
        pl.BlockSpec((POOL_HALO, d), lambda i: (jnp.clip((i - first_tile) * hb - 1, 0, last_halo), 0)),
        pl.BlockSpec((t, d), lambda i: (jnp.clip(i - first_tile, 0, last_tile), 0)),
        pl.BlockSpec((POOL_HALO, d), lambda i: (jnp.clip((i - first_tile + 1) * hb, 0, last_halo), 0)),
    ]


def _pool_layer(xa, xb, g_mix, w_grp, scale, g_ffn, seq):
    d = xa.shape[1]
    n = xa.shape[0] + xb.shape[0]
    t = _tile(min(seq[1], seq[2]), 256)
    row = lambda i: (0, 0)
    return pl.pallas_call(
        functools.partial(_pool_kernel, seq=seq),
        grid=(n // t,),
        in_specs=_stream_specs(xa.shape[0], 0, t, d) + _stream_specs(xb.shape[0], xa.shape[0] // t, t, d) + [
            pl.BlockSpec((1, d), row),
            pl.BlockSpec(w_grp.shape, lambda i: (0, 0, 0)),
            pl.BlockSpec((1, d), row),
            pl.BlockSpec((1, d), row),
        ],
        out_specs=[pl.BlockSpec((t, d), lambda i: (i, 0)),
                   pl.BlockSpec((t, d), lambda i: (i, 0))],
        out_shape=[jax.ShapeDtypeStruct((n, d), f32), jax.ShapeDtypeStruct((n, d), bf16)],
        scratch_shapes=[pltpu.VMEM((t + POOL_BLOCK, d), bf16)] * 2,
        compiler_params=_cparams(("parallel",)),
    )(xa, xa, xa, xb, xb, xb, g_mix.reshape(1, d), w_grp, scale.reshape(1, d), g_ffn.reshape(1, d))


def _matmul_call_with_casts(kernel, grid, in_specs, out_spec, out_shape, args, to_cast):
    n_steps = grid[0] * grid[1]
    plans = []
    for arr in to_cast:
        rows = arr.size // SIDE_CAST_COLS
        steps = 1
        while steps * 2 <= n_steps and rows % (steps * 2 * BF16_SUBLANES) == 0:
            steps *= 2
        plans.append((rows, steps, rows // steps))
    n_in, n_side = len(in_specs), len(to_cast)

    def side_spec(steps, rb):
        return pl.BlockSpec((rb, SIDE_CAST_COLS), lambda i, j: (jnp.minimum(i * grid[1] + j, steps - 1), 0))

    def body(*refs):
        kernel(*refs[:n_in], refs[n_in + n_side])
        step = pl.program_id(0) * grid[1] + pl.program_id(1)
        for k, (_, steps, _) in enumerate(plans):
            @pl.when(step < steps)
            def _(src=refs[n_in + k], dst=refs[n_in + n_side + 1 + k]):
                dst[...] = src[...].astype(bf16)

    outs = pl.pallas_call(
        body,
        grid=grid,
        in_specs=list(in_specs) + [side_spec(steps, rb) for _, steps, rb in plans],
        out_specs=[out_spec] + [side_spec(steps, rb) for _, steps, rb in plans],
        out_shape=[out_shape] + [jax.ShapeDtypeStruct((rows, SIDE_CAST_COLS), bf16) for rows, _, _ in plans],
        compiler_params=_cparams(("arbitrary", "arbitrary")),
    )(*args, *[arr.reshape(rows, SIDE_CAST_COLS) for arr, (rows, _, _) in zip(to_cast, plans)])
    return outs[0], [o.reshape(arr.shape) for o, arr in zip(outs[1:], to_cast)]


def _glu_up_kernel(a_ref, wg_ref, wu_ref, o_ref):
    a = a_ref[...]
    gate = jnp.dot(a, wg_ref[...], preferred_element_type=f32)
    up = jnp.dot(a, wu_ref[...], preferred_element_type=f32)
    o_ref[...] = (_silu(gate) * up).astype(o_ref.dtype)


def _glu_up(a, wg, wu, to_cast=()):
    m, k = a.shape
    f = wg.shape[1]
    tm = _tile(m, 1024)
    tn = _tile(f, 512)
    wspec = pl.BlockSpec((k, tn), lambda i, j: (0, j))
    return _matmul_call_with_casts(
        _glu_up_kernel, (m // tm, f // tn),
        [pl.BlockSpec((tm, k), lambda i, j: (i, 0)), wspec, wspec],
        pl.BlockSpec((tm, tn), lambda i, j: (i, j)),
        jax.ShapeDtypeStruct((m, f), bf16), (a, wg, wu), to_cast)


def _down_kernel(a_ref, w_ref, res_ref, o_ref):
    o_ref[...] = res_ref[...] + jnp.dot(a_ref[...], w_ref[...], preferred_element_type=f32)


def _down_res(a, w, res, tm, to_cast=()):
    m, kk = a.shape
    n = w.shape[1]
    tm = _tile(m, tm)
    tn = _tile(n, 512)
    return _matmul_call_with_casts(
        _down_kernel, (m // tm, n // tn),
        [pl.BlockSpec((tm, kk), lambda i, j: (i, 0)),
         pl.BlockSpec((kk, tn), lambda i, j: (0, j)),
         pl.BlockSpec((tm, tn), lambda i, j: (i, j))],
        pl.BlockSpec((tm, tn), lambda i, j: (i, j)),
        jax.ShapeDtypeStruct((m, n), f32), (a, w, res), to_cast)


def _column_blocks(a_ref, w_ref):
    a = a_ref[...]
    tn = w_ref.shape[1]
    cb = min(tn, EPILOGUE_COLS)
    for c0 in range(0, tn, cb):
        yield c0, jnp.dot(a, w_ref[:, c0:c0 + cb], preferred_element_type=f32)


def _store_head_block(o_ref, c0, val):
    for hh in range(val.shape[1] // HEAD_DIM):
        o_ref[c0 // HEAD_DIM + hh] = val[:, hh * HEAD_DIM:(hh + 1) * HEAD_DIM].astype(o_ref.dtype)


def _proj_silu_heads_kernel(a_ref, w_ref, o_ref):
    for c0, r in _column_blocks(a_ref, w_ref):
        _store_head_block(o_ref, c0, _silu(r))


def _proj_heads_kernel(a_ref, w_ref, o_ref):
    for c0, r in _column_blocks(a_ref, w_ref):
        _store_head_block(o_ref, c0, r)


def _proj_silu_kernel(a_ref, w_ref, o_ref):
    for c0, r in _column_blocks(a_ref, w_ref):
        o_ref[:, c0:c0 + r.shape[1]] = _silu(r).astype(o_ref.dtype)


def _proj_forget_kernel(a_ref, w_ref, lb_ref, k_ref, g_ref, *, layer):
    logits = lb_ref[...]
    ex = jnp.exp(logits - jnp.max(logits, axis=0, keepdims=True))
    sm = ex / jnp.sum(ex, axis=0, keepdims=True)
    lb_all = jnp.sum(sm[0:layer + 1], axis=0, keepdims=True) - sm[0:1]
    for c0, r in _column_blocks(a_ref, w_ref):
        lb = lb_all[:, c0:c0 + r.shape[1]]
        f = lb + (1.0 - lb) * jax.nn.sigmoid(r)
        _store_head_block(k_ref, c0, 1.0 - f)
        _store_head_block(g_ref, c0, jnp.log2(f))


def _proj_section(kernel, a, w, section, out_heads, n_out=1, extra=None):
    m, k = a.shape
    d = k
    tm = _tile(m, 1024)
    tn = _tile(d, 1024)
    nb = d // tn
    in_specs = [pl.BlockSpec((tm, k), lambda i, j: (i, 0)),
                pl.BlockSpec((k, tn), lambda i, j: (0, section * nb + j))]
    args = [a, w]
    if extra is not None:
        in_specs.append(pl.BlockSpec((extra.shape[0], tn), lambda i, j: (0, j)))
        args.append(extra)
    if out_heads:
        hpb = tn // HEAD_DIM
        ospec = pl.BlockSpec((hpb, tm, HEAD_DIM), lambda i, j: (j, i, 0))
        oshape = jax.ShapeDtypeStruct((d // HEAD_DIM, m, HEAD_DIM), bf16)
    else:
        ospec = pl.BlockSpec((tm, tn), lambda i, j: (i, j))
        oshape = jax.ShapeDtypeStruct((m, d), bf16)
    out = pl.pallas_call(
        kernel,
        grid=(m // tm, nb),
        in_specs=in_specs,
        out_specs=[ospec] * n_out if n_out > 1 else ospec,
        out_shape=[oshape] * n_out if n_out > 1 else oshape,
        compiler_params=_cparams(("parallel", "arbitrary")),
    )(*args)
    return out


MAX_DIRECT_LOG2_DECAY = 100.0
HEAD_UNROLL = 8


def _dot_nt(a, b):
    return lax.dot_general(a, b, (((1,), (1,)), ((), ())), preferred_element_type=f32)


def _dot_tn(a, b):
    return lax.dot_general(a, b, (((0,), (0,)), ((), ())), preferred_element_type=f32)


def _mid_ref(p, level, offset):
    c, w = p.shape
    blk = 2 << level
    p3 = p.reshape(c // blk, blk, w)
    return jnp.broadcast_to(p3[:, offset:offset + 1, :], p3.shape).reshape(c, w)


def _neg_abs_exponents(gf, gb, b, x, level, rmod4):
    if level == 0:
        odd = (rmod4 & 1) == 1
        return jnp.where(odd, gf, 0.0), jnp.where(odd, 0.0, gb)
    if level == 1:
        c = gf.shape[0]
        gf_up = pltpu.roll(gf, c - 1, 0)
        gf_dn = pltpu.roll(gf, 1, 0)
        gb_up = pltpu.roll(gb, c - 1, 0)
        gb_dn = pltpu.roll(gb, 1, 0)
        ef = jnp.where(rmod4 == 0, gf_up, jnp.where(rmod4 == 1, 0.0, jnp.where(rmod4 == 2, gf, gf + gf_dn)))
        eb = jnp.where(rmod4 == 0, gb + gb_up, jnp.where(rmod4 == 1, gb, jnp.where(rmod4 == 2, 0.0, gb_dn)))
        return ef, eb
    half = 1 << level
    ef = b - _mid_ref(b, level, half - 1)
    eb = x - _mid_ref(x, level, half)
    return -jnp.abs(ef), -jnp.abs(eb)


def _chunk_prefix(tri_incl, tri_excl, gf16, gb16):
    b = jnp.dot(tri_incl, gf16, preferred_element_type=f32)
    x = jnp.dot(tri_excl, gb16, preferred_element_type=f32)
    return b, x


def _scan_bw_kernel(kb_ref, gb_ref, v_ref, s_ref, st_ref, *, seq, n_chunks):
    n_heads, c, _ = kb_ref.shape
    chunk = n_chunks - 1 - pl.program_id(0)
    _, is_end = _seq_flags(chunk * c, c, *seq)

    @pl.when(is_end)
    def _():
        st_ref[...] = jnp.zeros_like(st_ref)

    row = lax.broadcasted_iota(jnp.int32, (c, c), 0)
    col = lax.broadcasted_iota(jnp.int32, (c, c), 1)
    tri_excl = (col < row).astype(bf16)

    def head(h, carry):
        gb16 = gb_ref[h]
        x = jnp.dot(tri_excl, gb16, preferred_element_type=f32)
        tot = x[c - 1:c, :] + gb16[c - 1:c, :].astype(f32)
        st = st_ref[h]
        s_ref[h] = st.astype(s_ref.dtype)
        kx = (kb_ref[h].astype(f32) * jnp.exp2(x)).astype(bf16)
        st_ref[h] = st * jnp.exp2(tot) + _dot_tn(v_ref[h], kx)
        return carry

    lax.fori_loop(0, n_heads, head, 0, unroll=2 * HEAD_UNROLL)


def _scan_bw_states(kb, gb, v, seq, c):
    n_heads, n, dk = kb.shape
    n_chunks = n // c
    spec = pl.BlockSpec((n_heads, c, dk), lambda i: (0, n_chunks - 1 - i, 0))
    return pl.pallas_call(
        functools.partial(_scan_bw_kernel, seq=seq, n_chunks=n_chunks),
        grid=(n_chunks,),
        in_specs=[spec, spec, spec],
        out_specs=pl.BlockSpec((None, n_heads, dk, dk), lambda i: (n_chunks - 1 - i, 0, 0, 0)),
        out_shape=jax.ShapeDtypeStruct((n_chunks, n_heads, dk, dk), bf16),
        scratch_shapes=[pltpu.VMEM((n_heads, dk, dk), f32)],
        compiler_params=_cparams(("arbitrary",)),
    )(kb, gb, v)


def _scan_kernel(q_ref, kf_ref, kb_ref, gf_ref, gb_ref, v_ref, sb_ref, tril_ref, triu_ref, gate_ref,
                 gain_ref, out_ref, st_ref, o_s, qf_s, kf_s, qb_s, kb_s, qin_s, kx_s, sc_s, dec_s, tot_s,
                 *, seq):
    n_heads, c, dk = q_ref.shape
    n_levels = c.bit_length() - 1
    mid = c // 2
    is_start, _ = _seq_flags(pl.program_id(0) * c, c, *seq)

    @pl.when(is_start)
    def _():
        st_ref[...] = jnp.zeros_like(st_ref)

    row = lax.broadcasted_iota(jnp.int32, (c, c), 0)
    col = lax.broadcasted_iota(jnp.int32, (c, c), 1)
    tri_incl = (col <= row).astype(bf16)
    tri_excl = (col < row).astype(bf16)

    def finish(h, kf_dec, b_last, q_in, scores, diag):
        v = v_ref[h]
        st_f = st_ref[h]
        st_cat = jnp.concatenate([st_f.astype(bf16), sb_ref[h]], axis=1)
        o = _dot_nt(q_in, st_cat) + jnp.dot(scores.astype(bf16), v, preferred_element_type=f32)
        o_s[h] = o if diag is None else o + diag * v.astype(f32)
        st_ref[h] = st_f * jnp.exp2(b_last) + _dot_tn(v, kf_dec.astype(bf16))

    def direct_operands(h, carry):
        q = q_ref[h].astype(f32)
        gb16 = gb_ref[h]
        b, x = _chunk_prefix(tri_incl, tri_excl, gf_ref[h], gb16)
        b_mid = b[mid - 1:mid, :]
        x_mid = x[mid:mid + 1, :]
        db = b - b_mid
        dx = x - x_mid
        q_f = q * jnp.exp2(db)
        k_f = kf_ref[h].astype(f32) * jnp.exp2(-db)
        q_b = q * jnp.exp2(-dx)
        k_b = kb_ref[h].astype(f32) * jnp.exp2(dx)
        b_last = b[c - 1:c, :]
        x_tot = x[c - 1:c, :] + gb16[c - 1:c, :].astype(f32)
        qf_s[h] = q_f.astype(bf16)
        kf_s[h] = k_f.astype(bf16)
        qb_s[h] = q_b.astype(bf16)
        kb_s[h] = k_b.astype(bf16)
        qin_s[h] = jnp.concatenate([q_f * jnp.exp2(b_mid), q_b * jnp.exp2(x_tot - x_mid)], axis=1).astype(bf16)
        kx_s[h] = (k_f * jnp.exp2(b_last - b_mid)).astype(bf16)
        dec_s[h] = jnp.broadcast_to(jnp.exp2(b_last), dec_s.shape[1:])
        tot_s[h] = jnp.broadcast_to(jnp.minimum(b_last, x_tot), tot_s.shape[1:])
        return carry

    def direct_scores(h, carry):
        sc_s[h] = (jnp.where(tril_ref[...] != 0.0, _dot_nt(qf_s[h], kf_s[h]), 0.0)
                   + jnp.where(triu_ref[...] != 0.0, _dot_nt(qb_s[h], kb_s[h]), 0.0)).astype(bf16)
        return carry

    def direct_output(h, carry):
        v = v_ref[h]
        st_f = st_ref[h]
        st_cat = jnp.concatenate([st_f.astype(bf16), sb_ref[h]], axis=1)
        o_s[h] = _dot_nt(qin_s[h], st_cat) + jnp.dot(sc_s[h], v, preferred_element_type=f32)
        st_ref[h] = st_f * dec_s[h][0:1, :] + _dot_tn(v, kx_s[h])
        return carry

    def head_levelled(h, carry):
        rtok = lax.broadcasted_iota(jnp.int32, (c, dk), 0)
        rmod4 = rtok & 3
        q = q_ref[h].astype(f32)
        kf = kf_ref[h].astype(f32)
        kb = kb_ref[h].astype(f32)
        gf16 = gf_ref[h]
        gb16 = gb_ref[h]
        gf = gf16.astype(f32)
        gb = gb16.astype(f32)
        b, x = _chunk_prefix(tri_incl, tri_excl, gf16, gb16)
        scores = jnp.zeros((c, c), f32)
        for l in range(n_levels):
            ef, eb = _neg_abs_exponents(gf, gb, b, x, l, rmod4)
            af = jnp.exp2(ef)
            ab = jnp.exp2(eb)
            up = ((rtok >> l) & 1) == 1
            lhs = jnp.concatenate([jnp.where(up, q * af, 0.0), jnp.where(up, 0.0, q * ab)], axis=1)
            rhs = jnp.concatenate([jnp.where(up, 0.0, kf * af), jnp.where(up, kb * ab, 0.0)], axis=1)
            p = _dot_nt(lhs.astype(bf16), rhs.astype(bf16))
            if l < n_levels - 1:
                p = jnp.where((row >> (l + 1)) == (col >> (l + 1)), p, 0.0)
            scores = scores + p
        b_last = b[c - 1:c, :]
        x_tot = x[c - 1:c, :] + gb[c - 1:c, :]
        q_in = jnp.concatenate([q * jnp.exp2(b), q * jnp.exp2(x_tot - x)], axis=1).astype(bf16)
        diag = jnp.sum(q * (kf + kb), axis=1, keepdims=True)
        finish(h, kf * jnp.exp2(b_last - b), b_last, q_in, scores, diag)
        return carry

    lax.fori_loop(0, n_heads, direct_operands, 0, unroll=2 * HEAD_UNROLL)
    direct_ok = jnp.min(tot_s[...]) >= -MAX_DIRECT_LOG2_DECAY

    @pl.when(direct_ok)
    def _():
        lax.fori_loop(0, n_heads, direct_scores, 0, unroll=2 * HEAD_UNROLL)
        lax.fori_loop(0, n_heads, direct_output, 0, unroll=2 * HEAD_UNROLL)

    @pl.when(jnp.logical_not(direct_ok))
    def _():
        lax.fori_loop(0, n_heads, head_levelled, 0)

    sq = jnp.zeros((c, dk), f32)
    for hh in range(n_heads):
        o = o_s[hh]
        sq = sq + o * o
    inv = lax.rsqrt(jnp.sum(sq, axis=1, keepdims=True) / (n_heads * dk) + EPS)
    for hh in range(n_heads):
        cols = slice(hh * dk, (hh + 1) * dk)
        out_ref[:, cols] = (o_s[hh] * inv * gain_ref[:, cols] * gate_ref[:, cols].astype(f32)
                            ).astype(out_ref.dtype)


def _scan(q, kf, kb, gf, gb, v, s_bw, gate, gain, seq, c):
    n_heads, n, dk = q.shape
    d = n_heads * dk
    n_chunks = n // c
    spec = pl.BlockSpec((n_heads, c, dk), lambda i: (0, i, 0))
    tri = pl.BlockSpec((c, c), lambda i: (0, 0))
    tril = jnp.tril(jnp.ones((c, c), f32))
    return pl.pallas_call(
        functools.partial(_scan_kernel, seq=seq),
        grid=(n_chunks,),
        in_specs=[spec] * 6 + [pl.BlockSpec((None, n_heads, dk, dk), lambda i: (i, 0, 0, 0)), tri, tri,
                               pl.BlockSpec((c, d), lambda i: (i, 0)),
                               pl.BlockSpec((1, d), lambda i: (0, 0))],
        out_specs=pl.BlockSpec((c, d), lambda i: (i, 0)),
        out_shape=jax.ShapeDtypeStruct((n, d), bf16),
        scratch_shapes=[pltpu.VMEM((n_heads, dk, dk), f32), pltpu.VMEM((n_heads, c, dk), f32)]
        + [pltpu.VMEM((n_heads, c, dk), bf16)] * 4
        + [pltpu.VMEM((n_heads, c, 2 * dk), bf16), pltpu.VMEM((n_heads, c, dk), bf16),
           pltpu.VMEM((n_heads, c, c), bf16)] + [pltpu.VMEM((n_heads, 8, dk), f32)] * 2,
        compiler_params=_cparams(("arbitrary",)),
    )(q, kf, kb, gf, gb, v, s_bw, tril, tril.T, gate, gain.reshape(1, d))


def _pack_halves(x):
    half = x.shape[1] // 2
    bits = lax.bitcast_convert_type(x.astype(bf16).astype(f32), jnp.int32)
    return lax.shift_right_logical(bits[:, :half], 16) | (bits[:, half:] & jnp.int32(-65536))


def _unpack_halves(words):
    lo = lax.bitcast_convert_type(lax.shift_left(words, 16), f32)
    hi = lax.bitcast_convert_type(words & jnp.int32(-65536), f32)
    return lo, hi


def _router_kernel(x_ref, g_ref, r2_ref, h_ref, e_ref, rank_ref, w_ref, cnt_ref, run_ref,
                   *, n_experts):
    @pl.when(pl.program_id(0) == 0)
    def _():
        run_ref[...] = jnp.zeros_like(run_ref)

    h = _rms(x_ref[...], g_ref[...])
    h_ref[...] = _pack_halves(h)
    hi = h.astype(bf16)
    lo = (h - hi.astype(f32)).astype(bf16)
    tr = h.shape[0]
    prod = jnp.dot(jnp.concatenate([hi, lo], axis=0), r2_ref[...], preferred_element_type=f32)
    logits = prod[:tr, :LANES] + prod[:tr, LANES:] + prod[tr:, :LANES]
    lane = lax.broadcasted_iota(jnp.int32, logits.shape, 1)
    neg = jnp.float32(-jnp.inf)
    logits = jnp.where(lane < n_experts, logits, neg)
    m1 = jnp.max(logits, axis=1, keepdims=True)
    i1 = jnp.min(jnp.where(logits == m1, lane, LANES), axis=1, keepdims=True)
    rest = jnp.where(lane == i1, neg, logits)
    m2 = jnp.max(rest, axis=1, keepdims=True)
    i2 = jnp.min(jnp.where(rest == m2, lane, LANES), axis=1, keepdims=True)
    e2 = jnp.exp(m2 - m1)
    w_ref[...] = jnp.concatenate([1.0 / (1.0 + e2), e2 / (1.0 + e2)], axis=1)
    e_ref[...] = jnp.concatenate([i1, i2], axis=1)
    member = jnp.where((lane == i1) | (lane == i2), 1.0, 0.0)
    row = lax.broadcasted_iota(jnp.int32, (tr, tr), 0)
    col = lax.broadcasted_iota(jnp.int32, (tr, tr), 1)
    before = jnp.dot((col < row).astype(bf16), member.astype(bf16), preferred_element_type=f32)
    before = before + run_ref[...]
    r1 = jnp.sum(jnp.where(lane == i1, before, 0.0), axis=1, keepdims=True)
    r2 = jnp.sum(jnp.where(lane == i2, before, 0.0), axis=1, keepdims=True)
    rank_ref[...] = jnp.concatenate([r1, r2], axis=1).astype(jnp.int32)
    run = run_ref[...] + jnp.sum(member, axis=0, keepdims=True)
    run_ref[...] = run
    cnt_ref[...] = run.astype(jnp.int32)


def _router(x, g, router):
    n, d = x.shape
    e = router.shape[1]
    rpad = jnp.zeros((d, LANES), f32).at[:, :e].set(router)
    rhi = rpad.astype(bf16)
    r2 = jnp.concatenate([rhi, (rpad - rhi.astype(f32)).astype(bf16)], axis=1)
    tr = _tile(n, 512)
    pair = pl.BlockSpec((tr, TOP_K), lambda i: (i, 0))
    return pl.pallas_call(
        functools.partial(_router_kernel, n_experts=e),
        grid=(n // tr,),
        in_specs=[pl.BlockSpec((tr, d), lambda i: (i, 0)),
                  pl.BlockSpec((1, d), lambda i: (0, 0)),
                  pl.BlockSpec((d, 2 * LANES), lambda i: (0, 0))],
        out_specs=[pl.BlockSpec((tr, d // 2), lambda i: (i, 0)), pair, pair, pair,
                   pl.BlockSpec((1, LANES), lambda i: (0, 0))],
        out_shape=[jax.ShapeDtypeStruct((n, d // 2), jnp.int32),
                   jax.ShapeDtypeStruct((n, TOP_K), jnp.int32),
                   jax.ShapeDtypeStruct((n, TOP_K), jnp.int32),
                   jax.ShapeDtypeStruct((n, TOP_K), f32),
                   jax.ShapeDtypeStruct((1, LANES), jnp.int32)],
        scratch_shapes=[pltpu.VMEM((1, LANES), f32)],
        compiler_params=_cparams(("arbitrary",)),
    )(x, g.reshape(1, d), r2)


def _row_copies(pos_ref, i, src, dst, sem, gather):
    out = []
    for k in range(TOP_K):
        p = pos_ref[k, i]
        if gather:
            out.append(pltpu.make_async_copy(src.at[pl.ds(p, 1)], dst.at[k, pl.ds(i, 1)], sem))
        else:
            out.append(pltpu.make_async_copy(src.at[pl.ds(i, 1)], dst.at[pl.ds(p, 1)], sem))
    return out


def _start_rows(n_rows, pos_ref, src, dst, sem, gather):
    def start(i, carry):
        for cp in _row_copies(pos_ref, i, src, dst, sem, gather):
            cp.start()
        return carry

    lax.fori_loop(0, n_rows, start, 0, unroll=ROW_DMA_UNROLL)


def _wait_rows(n_rows, pos_ref, src, dst, sem, gather):
    def wait(i, carry):
        for cp in _row_copies(pos_ref, i, src, dst, sem, gather):
            cp.wait()
        return carry

    lax.fori_loop(0, n_rows, wait, 0, unroll=ROW_DMA_UNROLL)


def _all_rows(n_rows, pos_ref, src, dst, sem, gather):
    _start_rows(n_rows, pos_ref, src, dst, sem, gather)
    _wait_rows(n_rows, pos_ref, src, dst, sem, gather)


def _moe_dispatch_kernel(nv_ref, pos_ref, h_ref, xs_ref, zero_s, sem):
    tm = zero_s.shape[0]

    @pl.when(pl.program_id(0) == 0)
    def _():
        zero_s[...] = jnp.zeros_like(zero_s)

        def fill(r, carry):
            @pl.when(nv_ref[r] < tm)
            def _():
                cp = pltpu.make_async_copy(zero_s, xs_ref.at[pl.ds(pl.multiple_of(r * tm, tm), tm)], sem)
                cp.start()
                cp.wait()
            return carry

        lax.fori_loop(0, xs_ref.shape[0] // tm, fill, 0)

    _all_rows(h_ref.shape[0], pos_ref, h_ref, xs_ref, sem, gather=False)


def _moe_dispatch(hp, pos3, tile_rows, tm):
    n, dw = hp.shape
    n_tiles, _, tr = pos3.shape
    grid_spec = pltpu.PrefetchScalarGridSpec(
        num_scalar_prefetch=1,
        grid=(n_tiles,),
        in_specs=[pl.BlockSpec((None, TOP_K, tr), lambda i, nv: (i, 0, 0), memory_space=pltpu.SMEM),
                  pl.BlockSpec((tr, dw), lambda i, nv: (i, 0))],
        out_specs=pl.BlockSpec(memory_space=pl.ANY),
        scratch_shapes=[pltpu.VMEM((tm, dw), jnp.int32), pltpu.SemaphoreType.DMA(())],
    )
    return pl.pallas_call(
        _moe_dispatch_kernel,
        grid_spec=grid_spec,
        out_shape=jax.ShapeDtypeStruct((tile_rows.shape[0] * tm, dw), jnp.int32),
        compiler_params=_cparams(("arbitrary",)),
    )(tile_rows, pos3, hp)


def _moe_up_kernel(te_ref, nv_ref, xs_ref, wg_ref, wu_ref, a_ref, x_s):
    r = pl.program_id(0)
    half = xs_ref.shape[1]

    @pl.when(nv_ref[r] > 0)
    def _():
        @pl.when(pl.program_id(1) == 0)
        def _():
            lo, hi = _unpack_halves(xs_ref[...])
            x_s[:, :half] = lo.astype(bf16)
            x_s[:, half:] = hi.astype(bf16)

        x = x_s[...]
        gate = jnp.dot(x, wg_ref[...], preferred_element_type=f32)
        up = jnp.dot(x, wu_ref[...], preferred_element_type=f32)
        a_ref[...] = (_silu(gate) * up).astype(a_ref.dtype)

    @pl.when(nv_ref[r] == 0)
    def _():
        a_ref[...] = jnp.zeros_like(a_ref)


def _moe_down_kernel(te_ref, nv_ref, a_ref, wd_ref, y_ref):
    @pl.when(nv_ref[pl.program_id(0)] > 0)
    def _():
        y_ref[...] = _pack_halves(jnp.dot(a_ref[...], wd_ref[...], preferred_element_type=f32))

    @pl.when(nv_ref[pl.program_id(0)] == 0)
    def _():
        y_ref[...] = jnp.zeros_like(y_ref)


def _moe_group(xs, tile_expert, tile_rows, wg, wu, wd, tm):
    p, dw = xs.shape
    _, d, fe = wg.shape
    tf = _tile(fe, MOE_FF_TILE)
    wspec = pl.BlockSpec((None, d, tf), lambda r, f, te, nv: (te[r], 0, f))
    act = pl.pallas_call(
        _moe_up_kernel,
        grid_spec=pltpu.PrefetchScalarGridSpec(
            num_scalar_prefetch=2,
            grid=(p // tm, fe // tf),
            in_specs=[pl.BlockSpec((tm, dw), lambda r, f, te, nv: (r, 0)), wspec, wspec],
            out_specs=pl.BlockSpec((tm, tf), lambda r, f, te, nv: (r, f)),
            scratch_shapes=[pltpu.VMEM((tm, d), bf16)],
        ),
        out_shape=jax.ShapeDtypeStruct((p, fe), bf16),
        compiler_params=_cparams(("arbitrary", "arbitrary")),
    )(tile_expert, tile_rows, xs, wg, wu)
    return pl.pallas_call(
        _moe_down_kernel,
        grid_spec=pltpu.PrefetchScalarGridSpec(
            num_scalar_prefetch=2,
            grid=(p // tm,),
            in_specs=[pl.BlockSpec((tm, fe), lambda r, te, nv: (r, 0)),
                      pl.BlockSpec((None, fe, d), lambda r, te, nv: (te[r], 0, 0))],
            out_specs=pl.BlockSpec((tm, dw), lambda r, te, nv: (r, 0)),
        ),
        out_shape=jax.ShapeDtypeStruct((p, dw), jnp.int32),
        compiler_params=_cparams(("arbitrary",)),
    )(tile_expert, tile_rows, act, wd)


def _moe_combine_kernel(pos_ref, pos_next_ref, x_ref, w_ref, g_ref, y_ref, o_ref, ybuf, sem, *, final_norm):
    tr, d = x_ref.shape
    half = d // 2
    i = pl.program_id(0)
    slot = i % 2

    @pl.when(i == 0)
    def _():
        _start_rows(tr, pos_ref, y_ref, ybuf.at[0], sem.at[0], gather=True)

    _wait_rows(tr, pos_ref, y_ref, ybuf.at[slot], sem.at[slot], gather=True)
    for r0 in range(0, tr, SUBLANES):
        rows = slice(r0, r0 + SUBLANES)
        for ii in range(r0, r0 + SUBLANES):
            for cp in _row_copies(pos_next_ref, ii, y_ref, ybuf.at[1 - slot], sem.at[1 - slot], True):
                cp.start()
        w = w_ref[rows, :]
        lo0, hi0 = _unpack_halves(ybuf[slot, 0, rows, :])
        lo1, hi1 = _unpack_halves(ybuf[slot, 1, rows, :])
        o_lo = x_ref[rows, :half] + (w[:, 0:1] * lo0 + w[:, 1:2] * lo1)
        o_hi = x_ref[rows, half:] + (w[:, 0:1] * hi0 + w[:, 1:2] * hi1)
        if final_norm:
            ms = (jnp.sum(o_lo * o_lo, axis=1, keepdims=True)
                  + jnp.sum(o_hi * o_hi, axis=1, keepdims=True)) / d
            inv = lax.rsqrt(ms + EPS)
            o_lo = o_lo * inv * g_ref[:, :half]
            o_hi = o_hi * inv * g_ref[:, half:]
        o_ref[rows, :half] = o_lo
        o_ref[rows, half:] = o_hi

    @pl.when(i == pl.num_programs(0) - 1)
    def _():
        _wait_rows(tr, pos_next_ref, y_ref, ybuf.at[1 - slot], sem.at[1 - slot], gather=True)


def _moe_combine(x, w, y, pos3, gain, final_norm, row_start, n_rows):
    d = x.shape[1]
    tr = pos3.shape[2]
    off = row_start // tr
    last = off + n_rows // tr - 1
    return pl.pallas_call(
        functools.partial(_moe_combine_kernel, final_norm=final_norm),
        grid=(n_rows // tr,),
        in_specs=[pl.BlockSpec((None, TOP_K, tr), lambda i: (i + off, 0, 0), memory_space=pltpu.SMEM),
                  pl.BlockSpec((None, TOP_K, tr), lambda i: (jnp.minimum(i + off + 1, last), 0, 0),
                               memory_space=pltpu.SMEM),
                  pl.BlockSpec((tr, d), lambda i: (i + off, 0)),
                  pl.BlockSpec((tr, TOP_K), lambda i: (i + off, 0)),
                  pl.BlockSpec((1, d), lambda i: (0, 0)),
                  pl.BlockSpec(memory_space=pl.ANY)],
        out_specs=pl.BlockSpec((tr, d), lambda i: (i, 0)),
        out_shape=jax.ShapeDtypeStruct((n_rows, d), f32),
        scratch_shapes=[pltpu.VMEM((2, TOP_K, tr, d // 2), jnp.int32), pltpu.SemaphoreType.DMA((2,))],
        compiler_params=_cparams(("arbitrary",)),
    )(pos3, pos3, x, w, gain.reshape(1, d), y)


def _moe_layer(x, g, router, wg, wu, wd, final_gain, parts):
    n, d = x.shape
    n_exp = router.shape[1]
    tm = MOE_ROW_TILE
    hp, eid, rank, w, cnt = _router(x, g, router)
    cnt = cnt[0, :n_exp]
    padded = (cnt + tm - 1) // tm * tm
    seg_end = jnp.cumsum(padded)
    seg_start = seg_end - padded
    pos = seg_start[eid] + rank
    n_rows_padded = TOP_K * n + n_exp * tm
    tile0 = jnp.arange(n_rows_padded // tm, dtype=jnp.int32) * tm
    tile_expert = jnp.minimum(jnp.searchsorted(seg_end, tile0, side="right"), n_exp - 1).astype(jnp.int32)
    tile_rows = jnp.clip(cnt[tile_expert] - (tile0 - seg_start[tile_expert]), 0, tm).astype(jnp.int32)
    def by_tile(tr):
        return pos.reshape(n // tr, tr, TOP_K).transpose(0, 2, 1)

    pos3 = by_tile(_tile(n, MOE_COMBINE_TILE))
    xs = _moe_dispatch(hp, by_tile(_tile(n, MOE_DISPATCH_TILE)), tile_rows, tm)
    y = _moe_group(xs, tile_expert, tile_rows, wg, wu, wd, tm)
    if final_gain is None:
        return _moe_combine(x, w, y, pos3, g, False, 0, n)
    return [_moe_combine(x, w, y, pos3, final_gain, True, start, rows) for start, rows in parts]


def kernel(x_prompt, x_sample, norm_mix, norm_ffn, norm_final, pool_w, pool_scale, hgrn_w_in, hgrn_lb,
           hgrn_norm, hgrn_w_out, ffn_w_gate, ffn_w_up, ffn_w_down, moe_router, moe_w_gate, moe_w_up,
           moe_w_down):
    bp, sp, d = x_prompt.shape
    bs, ss, _ = x_sample.shape
    n_prompt = bp * sp
    n = n_prompt + bs * ss
    seq = (n_prompt, sp, ss)
    depth = norm_mix.shape[0]
    chunk = _tile(min(sp, ss), 128)

    parts = ((0, n_prompt), (n_prompt, bs * ss))
    streams = (x_prompt.reshape(n_prompt, d), x_sample.reshape(bs * ss, d))
    x = None if depth else jnp.concatenate(streams, axis=0)
    outs = None

    for i in range(depth):
        j = i // 2
        if i % 2 == 0:
            xa, xb = streams if x is None else (x[:n_prompt], x[n_prompt:])
            x, h = _pool_layer(xa, xb, norm_mix[i], pool_w[j].astype(bf16), pool_scale[j], norm_ffn[i], seq)
            nxt = i + 1 < depth
            act, cast_a = _glu_up(h, ffn_w_gate[j].astype(bf16), ffn_w_up[j].astype(bf16),
                                  [ffn_w_down[j]] + ([hgrn_w_in[j]] if nxt else []))
            x, cast_b = _down_res(act, cast_a[0], x, 512,
                                  [hgrn_w_out[j], moe_w_gate[j], moe_w_up[j], moe_w_down[j]] if nxt else [])
            rounded = dict(zip(("w_in", "w_out", "moe_gate", "moe_up", "moe_down"), cast_a[1:] + cast_b))
        else:
            h = _rmsnorm(x, norm_mix[i], bf16)
            w_in = rounded["w_in"]
            q = _proj_section(_proj_silu_heads_kernel, h, w_in, 0, True)
            forget = functools.partial(_proj_forget_kernel, layer=i)
            kf, gf = _proj_section(forget, h, w_in, 1, True, 2, hgrn_lb)
            kb, gb = _proj_section(forget, h, w_in, 2, True, 2, hgrn_lb)
            v = _proj_section(_proj_heads_kernel, h, w_in, 3, True)
            gate = _proj_section(_proj_silu_kernel, h, w_in, 4, False)
            s_bw = _scan_bw_states(kb, gb, v, seq, chunk)
            og = _scan(q, kf, kb, gf, gb, v, s_bw, gate, hgrn_norm[j], seq, chunk)
            x, _ = _down_res(og, rounded["w_out"], x, 1024)
            last = i == depth - 1
            res = _moe_layer(x, norm_ffn[i], moe_router[j], rounded["moe_gate"], rounded["moe_up"],
                             rounded["moe_down"], norm_final if last else None, parts)
            x, outs = (None, res) if last else (res, None)

    if outs is None:
        outs = [_rmsnorm(x, norm_final, f32, start, rows) for start, rows in parts]
    return (outs[0].reshape(bp, sp, d), outs[1].reshape(bs, ss, d))
```

```python
import functools

import jax
import jax.numpy as jnp
from jax import lax
from jax.experimental import pallas as pl
from jax.experimental.pallas import tpu as pltpu

EPS = 1e-6
POOL_WINDOWS = (2, 4, 8, 16)
POOL_HALO = 8
POOL_PAD = 16
POOL_BLOCK = 128
HEAD_DIM = 128
TOP_K = 2
V7X_VMEM_LIMIT = 56 * 1024 * 1024
LANES = 128
SUBLANES = 8
MOE_ROW_TILE = 512
MOE_FF_TILE = 512
MOE_DISPATCH_TILE = 512
MOE_COMBINE_TILE = 256
ROW_DMA_UNROLL = 8
EPILOGUE_COLS = 256
BF16_SUBLANES = 16

bf16 = jnp.bfloat16
f32 = jnp.float32


def _cparams(sem):
    return pltpu.CompilerParams(dimension_semantics=sem, vmem_limit_bytes=V7X_VMEM_LIMIT)


def _tile(n, pref):
    t = min(n, pref)
    while n % t:
        t //= 2
    return t


def _rms(x, g):
    ms = jnp.mean(x * x, axis=-1, keepdims=True)
    return x * lax.rsqrt(ms + EPS) * g


def _silu(x):
    return x * jax.nn.sigmoid(x)


def _seq_flags(tok, n_tok, n_prompt, seq_p, seq_s):
    in_p = tok < n_prompt
    rel = jnp.where(in_p, tok, tok - n_prompt)
    slen = jnp.where(in_p, seq_p, seq_s)
    is_start = (rel % slen) == 0
    is_end = ((rel + n_tok) % slen) == 0
    return is_start, is_end


def _rmsnorm_kernel(x_ref, g_ref, o_ref):
    o_ref[...] = _rms(x_ref[...], g_ref[...]).astype(o_ref.dtype)


def _rmsnorm(x, g, out_dtype, row_start=0, n_rows=None):
    n, d = x.shape
    n_rows = n if n_rows is None else n_rows
    tr = _tile(n_rows, 512)
    off = row_start // tr
    return pl.pallas_call(
        _rmsnorm_kernel,
        grid=(n_rows // tr,),
        in_specs=[pl.BlockSpec((tr, d), lambda i: (i + off, 0)),
                  pl.BlockSpec((1, d), lambda i: (0, 0))],
        out_specs=pl.BlockSpec((tr, d), lambda i: (i, 0)),
        out_shape=jax.ShapeDtypeStruct((n_rows, d), out_dtype),
        compiler_params=_cparams(("parallel",)),
    )(x, g.reshape(1, d))


def _pool_kernel(ap_ref, a_ref, an_ref, bp_ref, b_ref, bn_ref, gmix_ref, w_ref, scale_ref, gffn_ref,
                 x1_ref, h1_ref, hi_s, lo_s, *, seq):
    t, d = a_ref.shape
    dg = d // len(POOL_WINDOWS)
    tok = pl.program_id(0) * t
    is_start, is_end = _seq_flags(tok, t, *seq)
    in_a = tok < seq[0]
    g = gmix_ref[...]
    x = jnp.where(in_a, a_ref[...], b_ref[...])
    h = _rms(x, g)
    h_prev = jnp.where(is_start, 0.0, _rms(jnp.where(in_a, ap_ref[...], bp_ref[...]), g))
    h_next = jnp.where(is_end, 0.0, _rms(jnp.where(in_a, an_ref[...], bn_ref[...]), g))
    zeros = jnp.zeros_like(h_prev)

    def put(r0, val):
        hi = val.astype(bf16)
        hi_s[r0:r0 + val.shape[0], :] = hi
        lo_s[r0:r0 + val.shape[0], :] = (val - hi.astype(f32)).astype(bf16)

    put(0, jnp.concatenate([zeros, h_prev], axis=0))
    put(POOL_PAD, h)
    put(POOL_PAD + t, jnp.concatenate([h_next, zeros], axis=0))
    tail = hi_s.shape[0] - (2 * POOL_PAD + t)
    hi_s[2 * POOL_PAD + t:, :] = jnp.zeros((tail, d), bf16)
    lo_s[2 * POOL_PAD + t:, :] = jnp.zeros((tail, d), bf16)

    rr = lax.broadcasted_iota(jnp.int32, (POOL_BLOCK, 2 * POOL_BLOCK), 0)
    cc = lax.broadcasted_iota(jnp.int32, (POOL_BLOCK, 2 * POOL_BLOCK), 1)
    r = lax.broadcasted_iota(jnp.int32, (t, 1), 0)
    for gi, w in enumerate(POOL_WINDOWS):
        c0 = gi * dg
        half = w // 2
        band = ((cc >= POOL_PAD + rr - half) & (cc < POOL_PAD + rr + half)).astype(bf16)
        sums = []
        for r0 in range(0, t, POOL_BLOCK):
            sums.append(jnp.dot(band, hi_s[r0:r0 + 2 * POOL_BLOCK, c0:c0 + dg], preferred_element_type=f32)
                        + jnp.dot(band, lo_s[r0:r0 + 2 * POOL_BLOCK, c0:c0 + dg], preferred_element_type=f32))
        s = jnp.concatenate(sums, axis=0)
        lo = jnp.where(is_start, jnp.maximum(r - half, 0), r - half)
        hi = jnp.where(is_end, jnp.minimum(r + half, t), r + half)
        cnt = (hi - lo).astype(f32)
        dlt = s / cnt - h[:, c0:c0 + dg]
        y = jnp.dot(dlt.astype(bf16), w_ref[gi], preferred_element_type=f32)
        x1_ref[:, c0:c0 + dg] = x[:, c0:c0 + dg] + y * scale_ref[:, c0:c0 + dg]
    h1_ref[...] = _rms(x1_ref[...], gffn_ref[...]).astype(h1_ref.dtype)


def _stream_specs(rows, first_tile, t, d):
    hb = t // POOL_HALO
    last_halo = rows // POOL_HALO - 1
    last_tile = rows // t - 1
    return [
        pl.BlockSpec((POOL_HALO, d), lambda i: (jnp.clip((i - first_tile) * hb - 1, 0, last_halo), 0)),
        pl.BlockSpec((t, d), lambda i: (jnp.clip(i - first_tile, 0, last_tile), 0)),
        pl.BlockSpec((POOL_HALO, d), lambda i: (jnp.clip((i - first_tile + 1) * hb, 0, last_halo), 0)),
    ]


def _pool_layer(xa, xb, g_mix, w_grp, scale, g_ffn, seq):
    d = xa.shape[1]
    n = xa.shape[0] + xb.shape[0]
    t = _tile(min(seq[1], seq[2]), 256)
    row = lambda i: (0, 0)
    return pl.pallas_call(
        functools.partial(_pool_kernel, seq=seq),
        grid=(n // t,),
        in_specs=_stream_specs(xa.shape[0], 0, t, d) + _stream_specs(xb.shape[0], xa.shape[0] // t, t, d) + [
            pl.BlockSpec((1, d), row),
            pl.BlockSpec(w_grp.shape, lambda i: (0, 0, 0)),
            pl.BlockSpec((1, d), row),
            pl.BlockSpec((1, d), row),
        ],
        out_specs=[pl.BlockSpec((t, d), lambda i: (i, 0)),
                   pl.BlockSpec((t, d), lambda i: (i, 0))],
        out_shape=[jax.ShapeDtypeStruct((n, d), f32), jax.ShapeDtypeStruct((n, d), bf16)],
        scratch_shapes=[pltpu.VMEM((t + POOL_BLOCK, d), bf16)] * 2,
        compiler_params=_cparams(("parallel",)),
    )(xa, xa, xa, xb, xb, xb, g_mix.reshape(1, d), w_grp, scale.reshape(1, d), g_ffn.reshape(1, d))


def _matmul_call_with_casts(kernel, grid, in_specs, out_spec, out_shape, args, to_cast):
    n_steps = grid[0] * grid[1]
    plans = []
    for stacked, layer in to_cast:
        cols = stacked.shape[-1]
        rows = stacked[0].size // cols
        steps = 1
        while steps * 2 <= n_steps and rows % (steps * 2 * BF16_SUBLANES) == 0:
            steps *= 2
        plans.append((rows, cols, steps, rows // steps, layer))
    n_in, n_side = len(in_specs), len(to_cast)

    def step_block(steps):
        return lambda i, j: jnp.minimum(i * grid[1] + j, steps - 1)

    def src_spec(cols, steps, rb, layer):
        blk = step_block(steps)
        return pl.BlockSpec((None, rb, cols), lambda i, j: (layer, blk(i, j), 0))

    def dst_spec(cols, steps, rb):
        blk = step_block(steps)
        return pl.BlockSpec((rb, cols), lambda i, j: (blk(i, j), 0))

    def body(*refs):
        kernel(*refs[:n_in], refs[n_in + n_side])
        step = pl.program_id(0) * grid[1] + pl.program_id(1)
        for k, plan in enumerate(plans):
            @pl.when(step < plan[2])
            def _(src=refs[n_in + k], dst=refs[n_in + n_side + 1 + k]):
                dst[...] = src[...].astype(bf16)

    outs = pl.pallas_call(
        body,
        grid=grid,
        in_specs=list(in_specs) + [src_spec(c, s, rb, layer) for _, c, s, rb, layer in plans],
        out_specs=[out_spec] + [dst_spec(c, s, rb) for _, c, s, rb, _ in plans],
        out_shape=[out_shape] + [jax.ShapeDtypeStruct((r, c), bf16) for r, c, _, _, _ in plans],
        compiler_params=_cparams(("arbitrary", "arbitrary")),
    )(*args, *[stacked.reshape(stacked.shape[0], r, c) for (stacked, _), (r, c, _, _, _) in zip(to_cast, plans)])
    return outs[0], [o.reshape(stacked.shape[1:]) for o, (stacked, _) in zip(outs[1:], to_cast)]


def _glu_up_kernel(a_ref, wg_ref, wu_ref, o_ref):
    a = a_ref[...]
    gate = jnp.dot(a, wg_ref[...], preferred_element_type=f32)
    up = jnp.dot(a, wu_ref[...], preferred_element_type=f32)
    o_ref[...] = (_silu(gate) * up).astype(o_ref.dtype)


def _glu_up(a, wg, wu, to_cast=()):
    m, k = a.shape
    f = wg.shape[1]
    tm = _tile(m, 1024)
    tn = _tile(f, 512)
    wspec = pl.BlockSpec((k, tn), lambda i, j: (0, j))
    return _matmul_call_with_casts(
        _glu_up_kernel, (m // tm, f // tn),
        [pl.BlockSpec((tm, k), lambda i, j: (i, 0)), wspec, wspec],
        pl.BlockSpec((tm, tn), lambda i, j: (i, j)),
        jax.ShapeDtypeStruct((m, f), bf16), (a, wg, wu), to_cast)


def _down_kernel(a_ref, w_ref, res_ref, o_ref):
    o_ref[...] = res_ref[...] + jnp.dot(a_ref[...], w_ref[...], preferred_element_type=f32)


def _down_res(a, w, res, tm, to_cast=()):
    m, kk = a.shape
    n = w.shape[1]
    tm = _tile(m, tm)
    tn = _tile(n, 512)
    return _matmul_call_with_casts(
        _down_kernel, (m // tm, n // tn),
        [pl.BlockSpec((tm, kk), lambda i, j: (i, 0)),
         pl.BlockSpec((kk, tn), lambda i, j: (0, j)),
         pl.BlockSpec((tm, tn), lambda i, j: (i, j))],
        pl.BlockSpec((tm, tn), lambda i, j: (i, j)),
        jax.ShapeDtypeStruct((m, n), f32), (a, w, res), to_cast)


def _column_blocks(a_ref, w_ref):
    a = a_ref[...]
    tn = w_ref.shape[1]
    cb = min(tn, EPILOGUE_COLS)
    for c0 in range(0, tn, cb):
        yield c0, jnp.dot(a, w_ref[:, c0:c0 + cb], preferred_element_type=f32)


def _store_head_block(o_ref, c0, val):
    for hh in range(val.shape[1] // HEAD_DIM):
        o_ref[c0 // HEAD_DIM + hh] = val[:, hh * HEAD_DIM:(hh + 1) * HEAD_DIM].astype(o_ref.dtype)


def _proj_silu_heads_kernel(a_ref, w_ref, o_ref):
    for c0, r in _column_blocks(a_ref, w_ref):
        _store_head_block(o_ref, c0, _silu(r))


def _proj_heads_kernel(a_ref, w_ref, o_ref):
    for c0, r in _column_blocks(a_ref, w_ref):
        _store_head_block(o_ref, c0, r)


def _proj_silu_kernel(a_ref, w_ref, o_ref):
    for c0, r in _column_blocks(a_ref, w_ref):
        o_ref[:, c0:c0 + r.shape[1]] = _silu(r).astype(o_ref.dtype)


def _proj_forget_kernel(a_ref, w_ref, lb_ref, k_ref, g_ref, *, layer):
    logits = lb_ref[...]
    ex = jnp.exp(logits - jnp.max(logits, axis=0, keepdims=True))
    sm = ex / jnp.sum(ex, axis=0, keepdims=True)
    lb_all = jnp.sum(sm[0:layer + 1], axis=0, keepdims=True) - sm[0:1]
    for c0, r in _column_blocks(a_ref, w_ref):
        lb = lb_all[:, c0:c0 + r.shape[1]]
        f = lb + (1.0 - lb) * jax.nn.sigmoid(r)
        _store_head_block(k_ref, c0, 1.0 - f)
        _store_head_block(g_ref, c0, jnp.log2(f))


def _proj_section(kernel, a, w, section, out_heads, n_out=1, extra=None):
    m, k = a.shape
    d = k
    tm = _tile(m, 1024)
    tn = _tile(d, 1024)
    nb = d // tn
    in_specs = [pl.BlockSpec((tm, k), lambda i, j: (i, 0)),
                pl.BlockSpec((k, tn), lambda i, j: (0, section * nb + j))]
    args = [a, w]
    if extra is not None:
        in_specs.append(pl.BlockSpec((extra.shape[0], tn), lambda i, j: (0, j)))
        args.append(extra)
    if out_heads:
        hpb = tn // HEAD_DIM
        ospec = pl.BlockSpec((hpb, tm, HEAD_DIM), lambda i, j: (j, i, 0))
        oshape = jax.ShapeDtypeStruct((d // HEAD_DIM, m, HEAD_DIM), bf16)
    else:
        ospec = pl.BlockSpec((tm, tn), lambda i, j: (i, j))
        oshape = jax.ShapeDtypeStruct((m, d), bf16)
    out = pl.pallas_call(
        kernel,
        grid=(m // tm, nb),
        in_specs=in_specs,
        out_specs=[ospec] * n_out if n_out > 1 else ospec,
        out_shape=[oshape] * n_out if n_out > 1 else oshape,
        compiler_params=_cparams(("parallel", "arbitrary")),
    )(*args)
    return out


MAX_DIRECT_LOG2_DECAY = 100.0
HEAD_UNROLL = 8


def _dot_nt(a, b):
    return lax.dot_general(a, b, (((1,), (1,)), ((), ())), preferred_element_type=f32)


def _dot_tn(a, b):
    return lax.dot_general(a, b, (((0,), (0,)), ((), ())), preferred_element_type=f32)


def _mid_ref(p, level, offset):
    c, w = p.shape
    blk = 2 << level
    p3 = p.reshape(c // blk, blk, w)
    return jnp.broadcast_to(p3[:, offset:offset + 1, :], p3.shape).reshape(c, w)


def _neg_abs_exponents(gf, gb, b, x, level, rmod4):
    if level == 0:
        odd = (rmod4 & 1) == 1
        return jnp.where(odd, gf, 0.0), jnp.where(odd, 0.0, gb)
    if level == 1:
        c = gf.shape[0]
        gf_up = pltpu.roll(gf, c - 1, 0)
        gf_dn = pltpu.roll(gf, 1, 0)
        gb_up = pltpu.roll(gb, c - 1, 0)
        gb_dn = pltpu.roll(gb, 1, 0)
        ef = jnp.where(rmod4 == 0, gf_up, jnp.where(rmod4 == 1, 0.0, jnp.where(rmod4 == 2, gf, gf + gf_dn)))
        eb = jnp.where(rmod4 == 0, gb + gb_up, jnp.where(rmod4 == 1, gb, jnp.where(rmod4 == 2, 0.0, gb_dn)))
        return ef, eb
    half = 1 << level
    ef = b - _mid_ref(b, level, half - 1)
    eb = x - _mid_ref(x, level, half)
    return -jnp.abs(ef), -jnp.abs(eb)


def _chunk_prefix(tri_incl, tri_excl, gf16, gb16):
    b = jnp.dot(tri_incl, gf16, preferred_element_type=f32)
    x = jnp.dot(tri_excl, gb16, preferred_element_type=f32)
    return b, x


def _scan_bw_kernel(kb_ref, gb_ref, v_ref, s_ref, st_ref, *, seq, n_chunks):
    n_heads, c, _ = kb_ref.shape
    chunk = n_chunks - 1 - pl.program_id(0)
    _, is_end = _seq_flags(chunk * c, c, *seq)

    @pl.when(is_end)
    def _():
        st_ref[...] = jnp.zeros_like(st_ref)

    row = lax.broadcasted_iota(jnp.int32, (c, c), 0)
    col = lax.broadcasted_iota(jnp.int32, (c, c), 1)
    tri_excl = (col < row).astype(bf16)

    def head(h, carry):
        gb16 = gb_ref[h]
        x = jnp.dot(tri_excl, gb16, preferred_element_type=f32)
        tot = x[c - 1:c, :] + gb16[c - 1:c, :].astype(f32)
        st = st_ref[h]
        s_ref[h] = st.astype(s_ref.dtype)
        kx = (kb_ref[h].astype(f32) * jnp.exp2(x)).astype(bf16)
        st_ref[h] = st * jnp.exp2(tot) + _dot_tn(v_ref[h], kx)
        return carry

    lax.fori_loop(0, n_heads, head, 0, unroll=2 * HEAD_UNROLL)


def _scan_bw_states(kb, gb, v, seq, c):
    n_heads, n, dk = kb.shape
    n_chunks = n // c
    spec = pl.BlockSpec((n_heads, c, dk), lambda i: (0, n_chunks - 1 - i, 0))
    return pl.pallas_call(
        functools.partial(_scan_bw_kernel, seq=seq, n_chunks=n_chunks),
        grid=(n_chunks,),
        in_specs=[spec, spec, spec],
        out_specs=pl.BlockSpec((None, n_heads, dk, dk), lambda i: (n_chunks - 1 - i, 0, 0, 0)),
        out_shape=jax.ShapeDtypeStruct((n_chunks, n_heads, dk, dk), bf16),
        scratch_shapes=[pltpu.VMEM((n_heads, dk, dk), f32)],
        compiler_params=_cparams(("arbitrary",)),
    )(kb, gb, v)


def _scan_kernel(q_ref, kf_ref, kb_ref, gf_ref, gb_ref, v_ref, sb_ref, tril_ref, triu_ref, gate_ref,
                 gain_ref, out_ref, st_ref, o_s, qf_s, kf_s, qb_s, kb_s, qin_s, kx_s, sc_s, dec_s, tot_s,
                 *, seq):
    n_heads, c, dk = q_ref.shape
    n_levels = c.bit_length() - 1
    mid = c // 2
    is_start, _ = _seq_flags(pl.program_id(0) * c, c, *seq)

    @pl.when(is_start)
    def _():
        st_ref[...] = jnp.zeros_like(st_ref)

    row = lax.broadcasted_iota(jnp.int32, (c, c), 0)
    col = lax.broadcasted_iota(jnp.int32, (c, c), 1)
    tri_incl = (col <= row).astype(bf16)
    tri_excl = (col < row).astype(bf16)

    def finish(h, kf_dec, b_last, q_in, scores, diag):
        v = v_ref[h]
        st_f = st_ref[h]
        st_cat = jnp.concatenate([st_f.astype(bf16), sb_ref[h]], axis=1)
        o = _dot_nt(q_in, st_cat) + jnp.dot(scores.astype(bf16), v, preferred_element_type=f32)
        o_s[h] = o if diag is None else o + diag * v.astype(f32)
        st_ref[h] = st_f * jnp.exp2(b_last) + _dot_tn(v, kf_dec.astype(bf16))

    def direct_operands(h, carry):
        q = q_ref[h].astype(f32)
        gb16 = gb_ref[h]
        b, x = _chunk_prefix(tri_incl, tri_excl, gf_ref[h], gb16)
        b_mid = b[mid - 1:mid, :]
        x_mid = x[mid:mid + 1, :]
        db = b - b_mid
        dx = x - x_mid
        q_f = q * jnp.exp2(db)
        k_f = kf_ref[h].astype(f32) * jnp.exp2(-db)
        q_b = q * jnp.exp2(-dx)
        k_b = kb_ref[h].astype(f32) * jnp.exp2(dx)
        b_last = b[c - 1:c, :]
        x_tot = x[c - 1:c, :] + gb16[c - 1:c, :].astype(f32)
        qf_s[h] = q_f.astype(bf16)
        kf_s[h] = k_f.astype(bf16)
        qb_s[h] = q_b.astype(bf16)
        kb_s[h] = k_b.astype(bf16)
        qin_s[h] = jnp.concatenate([q_f * jnp.exp2(b_mid), q_b * jnp.exp2(x_tot - x_mid)], axis=1).astype(bf16)
        kx_s[h] = (k_f * jnp.exp2(b_last - b_mid)).astype(bf16)
        dec_s[h] = jnp.broadcast_to(jnp.exp2(b_last), dec_s.shape[1:])
        tot_s[h] = jnp.broadcast_to(jnp.minimum(b_last, x_tot), tot_s.shape[1:])
        return carry

    def direct_scores(h, carry):
        sc_s[h] = (jnp.where(tril_ref[...] != 0.0, _dot_nt(qf_s[h], kf_s[h]), 0.0)
                   + jnp.where(triu_ref[...] != 0.0, _dot_nt(qb_s[h], kb_s[h]), 0.0)).astype(bf16)
        return carry

    def direct_output(h, carry):
        v = v_ref[h]
        st_f = st_ref[h]
        st_cat = jnp.concatenate([st_f.astype(bf16), sb_ref[h]], axis=1)
        o_s[h] = _dot_nt(qin_s[h], st_cat) + jnp.dot(sc_s[h], v, preferred_element_type=f32)
        st_ref[h] = st_f * dec_s[h][0:1, :] + _dot_tn(v, kx_s[h])
        return carry

    def head_levelled(h, carry):
        rtok = lax.broadcasted_iota(jnp.int32, (c, dk), 0)
        rmod4 = rtok & 3
        q = q_ref[h].astype(f32)
        kf = kf_ref[h].astype(f32)
        kb = kb_ref[h].astype(f32)
        gf16 = gf_ref[h]
        gb16 = gb_ref[h]
        gf = gf16.astype(f32)
        gb = gb16.astype(f32)
        b, x = _chunk_prefix(tri_incl, tri_excl, gf16, gb16)
        scores = jnp.zeros((c, c), f32)
        for l in range(n_levels):
            ef, eb = _neg_abs_exponents(gf, gb, b, x, l, rmod4)
            af = jnp.exp2(ef)
            ab = jnp.exp2(eb)
            up = ((rtok >> l) & 1) == 1
            lhs = jnp.concatenate([jnp.where(up, q * af, 0.0), jnp.where(up, 0.0, q * ab)], axis=1)
            rhs = jnp.concatenate([jnp.where(up, 0.0, kf * af), jnp.where(up, kb * ab, 0.0)], axis=1)
            p = _dot_nt(lhs.astype(bf16), rhs.astype(bf16))
            if l < n_levels - 1:
                p = jnp.where((row >> (l + 1)) == (col >> (l + 1)), p, 0.0)
            scores = scores + p
        b_last = b[c - 1:c, :]
        x_tot = x[c - 1:c, :] + gb[c - 1:c, :]
        q_in = jnp.concatenate([q * jnp.exp2(b), q * jnp.exp2(x_tot - x)], axis=1).astype(bf16)
        diag = jnp.sum(q * (kf + kb), axis=1, keepdims=True)
        finish(h, kf * jnp.exp2(b_last - b), b_last, q_in, scores, diag)
        return carry

    lax.fori_loop(0, n_heads, direct_operands, 0, unroll=2 * HEAD_UNROLL)
    direct_ok = jnp.min(tot_s[...]) >= -MAX_DIRECT_LOG2_DECAY

    @pl.when(direct_ok)
    def _():
        lax.fori_loop(0, n_heads, direct_scores, 0, unroll=2 * HEAD_UNROLL)
        lax.fori_loop(0, n_heads, direct_output, 0, unroll=2 * HEAD_UNROLL)

    @pl.when(jnp.logical_not(direct_ok))
    def _():
        lax.fori_loop(0, n_heads, head_levelled, 0)

    sq = jnp.zeros((c, dk), f32)
    for hh in range(n_heads):
        o = o_s[hh]
        sq = sq + o * o
    inv = lax.rsqrt(jnp.sum(sq, axis=1, keepdims=True) / (n_heads * dk) + EPS)
    for hh in range(n_heads):
        cols = slice(hh * dk, (hh + 1) * dk)
        out_ref[:, cols] = (o_s[hh] * inv * gain_ref[:, cols] * gate_ref[:, cols].astype(f32)
                            ).astype(out_ref.dtype)


def _scan(q, kf, kb, gf, gb, v, s_bw, gate, gain, seq, c):
    n_heads, n, dk = q.shape
    d = n_heads * dk
    n_chunks = n // c
    spec = pl.BlockSpec((n_heads, c, dk), lambda i: (0, i, 0))
    tri = pl.BlockSpec((c, c), lambda i: (0, 0))
    tril = jnp.tril(jnp.ones((c, c), f32))
    return pl.pallas_call(
        functools.partial(_scan_kernel, seq=seq),
        grid=(n_chunks,),
        in_specs=[spec] * 6 + [pl.BlockSpec((None, n_heads, dk, dk), lambda i: (i, 0, 0, 0)), tri, tri,
                               pl.BlockSpec((c, d), lambda i: (i, 0)),
                               pl.BlockSpec((1, d), lambda i: (0, 0))],
        out_specs=pl.BlockSpec((c, d), lambda i: (i, 0)),
        out_shape=jax.ShapeDtypeStruct((n, d), bf16),
        scratch_shapes=[pltpu.VMEM((n_heads, dk, dk), f32), pltpu.VMEM((n_heads, c, dk), f32)]
        + [pltpu.VMEM((n_heads, c, dk), bf16)] * 4
        + [pltpu.VMEM((n_heads, c, 2 * dk), bf16), pltpu.VMEM((n_heads, c, dk), bf16),
           pltpu.VMEM((n_heads, c, c), bf16)] + [pltpu.VMEM((n_heads, 8, dk), f32)] * 2,
        compiler_params=_cparams(("arbitrary",)),
    )(q, kf, kb, gf, gb, v, s_bw, tril, tril.T, gate, gain.reshape(1, d))


def _pack_halves(x):
    half = x.shape[1] // 2
    bits = lax.bitcast_convert_type(x.astype(bf16).astype(f32), jnp.int32)
    return lax.shift_right_logical(bits[:, :half], 16) | (bits[:, half:] & jnp.int32(-65536))


def _unpack_halves(words):
    lo = lax.bitcast_convert_type(lax.shift_left(words, 16), f32)
    hi = lax.bitcast_convert_type(words & jnp.int32(-65536), f32)
    return lo, hi


def _router_kernel(x_ref, g_ref, r2_ref, h_ref, e_ref, rank_ref, w_ref, cnt_ref, run_ref,
                   *, n_experts):
    @pl.when(pl.program_id(0) == 0)
    def _():
        run_ref[...] = jnp.zeros_like(run_ref)

    h = _rms(x_ref[...], g_ref[...])
    h_ref[...] = _pack_halves(h)
    hi = h.astype(bf16)
    lo = (h - hi.astype(f32)).astype(bf16)
    tr = h.shape[0]
    prod = jnp.dot(jnp.concatenate([hi, lo], axis=0), r2_ref[...], preferred_element_type=f32)
    logits = prod[:tr, :LANES] + prod[:tr, LANES:] + prod[tr:, :LANES]
    lane = lax.broadcasted_iota(jnp.int32, logits.shape, 1)
    neg = jnp.float32(-jnp.inf)
    logits = jnp.where(lane < n_experts, logits, neg)
    m1 = jnp.max(logits, axis=1, keepdims=True)
    i1 = jnp.min(jnp.where(logits == m1, lane, LANES), axis=1, keepdims=True)
    rest = jnp.where(lane == i1, neg, logits)
    m2 = jnp.max(rest, axis=1, keepdims=True)
    i2 = jnp.min(jnp.where(rest == m2, lane, LANES), axis=1, keepdims=True)
    e2 = jnp.exp(m2 - m1)
    w_ref[...] = jnp.concatenate([1.0 / (1.0 + e2), e2 / (1.0 + e2)], axis=1)
    e_ref[...] = jnp.concatenate([i1, i2], axis=1)
    member = jnp.where((lane == i1) | (lane == i2), 1.0, 0.0)
    row = lax.broadcasted_iota(jnp.int32, (tr, tr), 0)
    col = lax.broadcasted_iota(jnp.int32, (tr, tr), 1)
    before = jnp.dot((col < row).astype(bf16), member.astype(bf16), preferred_element_type=f32)
    before = before + run_ref[...]
    r1 = jnp.sum(jnp.where(lane == i1, before, 0.0), axis=1, keepdims=True)
    r2 = jnp.sum(jnp.where(lane == i2, before, 0.0), axis=1, keepdims=True)
    rank_ref[...] = jnp.concatenate([r1, r2], axis=1).astype(jnp.int32)
    run = run_ref[...] + jnp.sum(member, axis=0, keepdims=True)
    run_ref[...] = run
    cnt_ref[...] = run.astype(jnp.int32)


def _router(x, g, router):
    n, d = x.shape
    e = router.shape[1]
    rpad = jnp.zeros((d, LANES), f32).at[:, :e].set(router)
    rhi = rpad.astype(bf16)
    r2 = jnp.concatenate([rhi, (rpad - rhi.astype(f32)).astype(bf16)], axis=1)
    tr = _tile(n, 512)
    pair = pl.BlockSpec((tr, TOP_K), lambda i: (i, 0))
    return pl.pallas_call(
        functools.partial(_router_kernel, n_experts=e),
        grid=(n // tr,),
        in_specs=[pl.BlockSpec((tr, d), lambda i: (i, 0)),
                  pl.BlockSpec((1, d), lambda i: (0, 0)),
                  pl.BlockSpec((d, 2 * LANES), lambda i: (0, 0))],
        out_specs=[pl.BlockSpec((tr, d // 2), lambda i: (i, 0)), pair, pair, pair,
                   pl.BlockSpec((1, LANES), lambda i: (0, 0))],
        out_shape=[jax.ShapeDtypeStruct((n, d // 2), jnp.int32),
                   jax.ShapeDtypeStruct((n, TOP_K), jnp.int32),
                   jax.ShapeDtypeStruct((n, TOP_K), jnp.int32),
                   jax.ShapeDtypeStruct((n, TOP_K), f32),
                   jax.ShapeDtypeStruct((1, LANES), jnp.int32)],
        scratch_shapes=[pltpu.VMEM((1, LANES), f32)],
        compiler_params=_cparams(("arbitrary",)),
    )(x, g.reshape(1, d), r2)


def _row_copies(pos_ref, i, src, dst, sem, gather):
    out = []
    for k in range(TOP_K):
        p = pos_ref[k, i]
        if gather:
            out.append(pltpu.make_async_copy(src.at[pl.ds(p, 1)], dst.at[k, pl.ds(i, 1)], sem))
        else:
            out.append(pltpu.make_async_copy(src.at[pl.ds(i, 1)], dst.at[pl.ds(p, 1)], sem))
    return out


def _start_rows(n_rows, pos_ref, src, dst, sem, gather):
    def start(i, carry):
        for cp in _row_copies(pos_ref, i, src, dst, sem, gather):
            cp.start()
        return carry

    lax.fori_loop(0, n_rows, start, 0, unroll=ROW_DMA_UNROLL)


def _wait_rows(n_rows, pos_ref, src, dst, sem, gather):
    def wait(i, carry):
        for cp in _row_copies(pos_ref, i, src, dst, sem, gather):
            cp.wait()
        return carry

    lax.fori_loop(0, n_rows, wait, 0, unroll=ROW_DMA_UNROLL)


def _all_rows(n_rows, pos_ref, src, dst, sem, gather):
    _start_rows(n_rows, pos_ref, src, dst, sem, gather)
    _wait_rows(n_rows, pos_ref, src, dst, sem, gather)


def _moe_dispatch_kernel(nv_ref, pos_ref, h_ref, xs_ref, zero_s, sem):
    tm = zero_s.shape[0]

    @pl.when(pl.program_id(0) == 0)
    def _():
        zero_s[...] = jnp.zeros_like(zero_s)

        def fill(r, carry):
            @pl.when(nv_ref[r] < tm)
            def _():
                cp = pltpu.make_async_copy(zero_s, xs_ref.at[pl.ds(pl.multiple_of(r * tm, tm), tm)], sem)
                cp.start()
                cp.wait()
            return carry

        lax.fori_loop(0, xs_ref.shape[0] // tm, fill, 0)

    _all_rows(h_ref.shape[0], pos_ref, h_ref, xs_ref, sem, gather=False)


def _moe_dispatch(hp, pos3, tile_rows, tm):
    n, dw = hp.shape
    n_tiles, _, tr = pos3.shape
    grid_spec = pltpu.PrefetchScalarGridSpec(
        num_scalar_prefetch=1,
        grid=(n_tiles,),
        in_specs=[pl.BlockSpec((None, TOP_K, tr), lambda i, nv: (i, 0, 0), memory_space=pltpu.SMEM),
                  pl.BlockSpec((tr, dw), lambda i, nv: (i, 0))],
        out_specs=pl.BlockSpec(memory_space=pl.ANY),
        scratch_shapes=[pltpu.VMEM((tm, dw), jnp.int32), pltpu.SemaphoreType.DMA(())],
    )
    return pl.pallas_call(
        _moe_dispatch_kernel,
        grid_spec=grid_spec,
        out_shape=jax.ShapeDtypeStruct((tile_rows.shape[0] * tm, dw), jnp.int32),
        compiler_params=_cparams(("arbitrary",)),
    )(tile_rows, pos3, hp)


def _moe_up_kernel(te_ref, nv_ref, xs_ref, wg_ref, wu_ref, a_ref, x_s):
    r = pl.program_id(0)
    half = xs_ref.shape[1]

    @pl.when(nv_ref[r] > 0)
    def _():
        @pl.when(pl.program_id(1) == 0)
        def _():
            lo, hi = _unpack_halves(xs_ref[...])
            x_s[:, :half] = lo.astype(bf16)
            x_s[:, half:] = hi.astype(bf16)

        x = x_s[...]
        gate = jnp.dot(x, wg_ref[...], preferred_element_type=f32)
        up = jnp.dot(x, wu_ref[...], preferred_element_type=f32)
        a_ref[...] = (_silu(gate) * up).astype(a_ref.dtype)

    @pl.when(nv_ref[r] == 0)
    def _():
        a_ref[...] = jnp.zeros_like(a_ref)


def _moe_down_kernel(te_ref, nv_ref, a_ref, wd_ref, y_ref):
    @pl.when(nv_ref[pl.program_id(0)] > 0)
    def _():
        y_ref[...] = _pack_halves(jnp.dot(a_ref[...], wd_ref[...], preferred_element_type=f32))

    @pl.when(nv_ref[pl.program_id(0)] == 0)
    def _():
        y_ref[...] = jnp.zeros_like(y_ref)


def _moe_group(xs, tile_expert, tile_rows, wg, wu, wd, tm):
    p, dw = xs.shape
    _, d, fe = wg.shape
    tf = _tile(fe, MOE_FF_TILE)
    wspec = pl.BlockSpec((None, d, tf), lambda r, f, te, nv: (te[r], 0, f))
    act = pl.pallas_call(
        _moe_up_kernel,
        grid_spec=pltpu.PrefetchScalarGridSpec(
            num_scalar_prefetch=2,
            grid=(p // tm, fe // tf),
            in_specs=[pl.BlockSpec((tm, dw), lambda r, f, te, nv: (r, 0)), wspec, wspec],
            out_specs=pl.BlockSpec((tm, tf), lambda r, f, te, nv: (r, f)),
            scratch_shapes=[pltpu.VMEM((tm, d), bf16)],
        ),
        out_shape=jax.ShapeDtypeStruct((p, fe), bf16),
        compiler_params=_cparams(("arbitrary", "arbitrary")),
    )(tile_expert, tile_rows, xs, wg, wu)
    return pl.pallas_call(
        _moe_down_kernel,
        grid_spec=pltpu.PrefetchScalarGridSpec(
            num_scalar_prefetch=2,
            grid=(p // tm,),
            in_specs=[pl.BlockSpec((tm, fe), lambda r, te, nv: (r, 0)),
                      pl.BlockSpec((None, fe, d), lambda r, te, nv: (te[r], 0, 0))],
            out_specs=pl.BlockSpec((tm, dw), lambda r, te, nv: (r, 0)),
        ),
        out_shape=jax.ShapeDtypeStruct((p, dw), jnp.int32),
        compiler_params=_cparams(("arbitrary",)),
    )(tile_expert, tile_rows, act, wd)


def _moe_combine_kernel(pos_ref, pos_next_ref, x_ref, w_ref, g_ref, y_ref, o_ref, ybuf, sem, *, final_norm):
    tr, d = x_ref.shape
    half = d // 2
    i = pl.program_id(0)
    slot = i % 2

    @pl.when(i == 0)
    def _():
        _start_rows(tr, pos_ref, y_ref, ybuf.at[0], sem.at[0], gather=True)

    _wait_rows(tr, pos_ref, y_ref, ybuf.at[slot], sem.at[slot], gather=True)
    for r0 in range(0, tr, SUBLANES):
        rows = slice(r0, r0 + SUBLANES)
        for ii in range(r0, r0 + SUBLANES):
            for cp in _row_copies(pos_next_ref, ii, y_ref, ybuf.at[1 - slot], sem.at[1 - slot], True):
                cp.start()
        w = w_ref[rows, :]
        lo0, hi0 = _unpack_halves(ybuf[slot, 0, rows, :])
        lo1, hi1 = _unpack_halves(ybuf[slot, 1, rows, :])
        o_lo = x_ref[rows, :half] + (w[:, 0:1] * lo0 + w[:, 1:2] * lo1)
        o_hi = x_ref[rows, half:] + (w[:, 0:1] * hi0 + w[:, 1:2] * hi1)
        if final_norm:
            ms = (jnp.sum(o_lo * o_lo, axis=1, keepdims=True)
                  + jnp.sum(o_hi * o_hi, axis=1, keepdims=True)) / d
            inv = lax.rsqrt(ms + EPS)
            o_lo = o_lo * inv * g_ref[:, :half]
            o_hi = o_hi * inv * g_ref[:, half:]
        o_ref[rows, :half] = o_lo
        o_ref[rows, half:] = o_hi

    @pl.when(i == pl.num_programs(0) - 1)
    def _():
        _wait_rows(tr, pos_next_ref, y_ref, ybuf.at[1 - slot], sem.at[1 - slot], gather=True)


def _moe_combine(x, w, y, pos3, gain, final_norm, row_start, n_rows):
    d = x.shape[1]
    tr = pos3.shape[2]
    off = row_start // tr
    last = off + n_rows // tr - 1
    return pl.pallas_call(
        functools.partial(_moe_combine_kernel, final_norm=final_norm),
        grid=(n_rows // tr,),
        in_specs=[pl.BlockSpec((None, TOP_K, tr), lambda i: (i + off, 0, 0), memory_space=pltpu.SMEM),
                  pl.BlockSpec((None, TOP_K, tr), lambda i: (jnp.minimum(i + off + 1, last), 0, 0),
                               memory_space=pltpu.SMEM),
                  pl.BlockSpec((tr, d), lambda i: (i + off, 0)),
                  pl.BlockSpec((tr, TOP_K), lambda i: (i + off, 0)),
                  pl.BlockSpec((1, d), lambda i: (0, 0)),
                  pl.BlockSpec(memory_space=pl.ANY)],
        out_specs=pl.BlockSpec((tr, d), lambda i: (i, 0)),
        out_shape=jax.ShapeDtypeStruct((n_rows, d), f32),
        scratch_shapes=[pltpu.VMEM((2, TOP_K, tr, d // 2), jnp.int32), pltpu.SemaphoreType.DMA((2,))],
        compiler_params=_cparams(("arbitrary",)),
    )(pos3, pos3, x, w, gain.reshape(1, d), y)


def _moe_layer(x, g, router, wg, wu, wd, final_gain, parts):
    n, d = x.shape
    n_exp = router.shape[1]
    tm = MOE_ROW_TILE
    hp, eid, rank, w, cnt = _router(x, g, router)
    cnt = cnt[0, :n_exp]
    padded = (cnt + tm - 1) // tm * tm
    seg_end = jnp.cumsum(padded)
    seg_start = seg_end - padded
    pos = seg_start[eid] + rank
    n_rows_padded = TOP_K * n + n_exp * tm
    tile0 = jnp.arange(n_rows_padded // tm, dtype=jnp.int32) * tm
    tile_expert = jnp.minimum(jnp.searchsorted(seg_end, tile0, side="right"), n_exp - 1).astype(jnp.int32)
    tile_rows = jnp.clip(cnt[tile_expert] - (tile0 - seg_start[tile_expert]), 0, tm).astype(jnp.int32)
    def by_tile(tr):
        return pos.reshape(n // tr, tr, TOP_K).transpose(0, 2, 1)

    pos3 = by_tile(_tile(n, MOE_COMBINE_TILE))
    xs = _moe_dispatch(hp, by_tile(_tile(n, MOE_DISPATCH_TILE)), tile_rows, tm)
    y = _moe_group(xs, tile_expert, tile_rows, wg, wu, wd, tm)
    if final_gain is None:
        return _moe_combine(x, w, y, pos3, g, False, 0, n)
    return [_moe_combine(x, w, y, pos3, final_gain, True, start, rows) for start, rows in parts]


def kernel(x_prompt, x_sample, norm_mix, norm_ffn, norm_final, pool_w, pool_scale, hgrn_w_in, hgrn_lb,
           hgrn_norm, hgrn_w_out, ffn_w_gate, ffn_w_up, ffn_w_down, moe_router, moe_w_gate, moe_w_up,
           moe_w_down):
    bp, sp, d = x_prompt.shape
    bs, ss, _ = x_sample.shape
    n_prompt = bp * sp
    n = n_prompt + bs * ss
    seq = (n_prompt, sp, ss)
    depth = norm_mix.shape[0]
    chunk = _tile(min(sp, ss), 128)

    parts = ((0, n_prompt), (n_prompt, bs * ss))
    streams = (x_prompt.reshape(n_prompt, d), x_sample.reshape(bs * ss, d))
    x = None if depth else jnp.concatenate(streams, axis=0)
    outs = None

    for i in range(depth):
        j = i // 2
        if i % 2 == 0:
            xa, xb = streams if x is None else (x[:n_prompt], x[n_prompt:])
            x, h = _pool_layer(xa, xb, norm_mix[i], pool_w[j].astype(bf16), pool_scale[j], norm_ffn[i], seq)
            nxt = i + 1 < depth
            act, cast_a = _glu_up(h, ffn_w_gate[j].astype(bf16), ffn_w_up[j].astype(bf16),
                                  [(ffn_w_down, j)] + ([(hgrn_w_in, j)] if nxt else []))
            x, cast_b = _down_res(act, cast_a[0], x, 512,
                                  [(w, j) for w in (hgrn_w_out, moe_w_gate, moe_w_up, moe_w_down)] if nxt else [])
            rounded = dict(zip(("w_in", "w_out", "moe_gate", "moe_up", "moe_down"), cast_a[1:] + cast_b))
        else:
            h = _rmsnorm(x, norm_mix[i], bf16)
            w_in = rounded["w_in"]
            q = _proj_section(_proj_silu_heads_kernel, h, w_in, 0, True)
            forget = functools.partial(_proj_forget_kernel, layer=i)
            kf, gf = _proj_section(forget, h, w_in, 1, True, 2, hgrn_lb)
            kb, gb = _proj_section(forget, h, w_in, 2, True, 2, hgrn_lb)
            v = _proj_section(_proj_heads_kernel, h, w_in, 3, True)
            gate = _proj_section(_proj_silu_kernel, h, w_in, 4, False)
            s_bw = _scan_bw_states(kb, gb, v, seq, chunk)
            og = _scan(q, kf, kb, gf, gb, v, s_bw, gate, hgrn_norm[j], seq, chunk)
            x, _ = _down_res(og, rounded["w_out"], x, 1024)
            last = i == depth - 1
            res = _moe_layer(x, norm_ffn[i], moe_router[j], rounded["moe_gate"], rounded["moe_up"],
                             rounded["moe_down"], norm_final if last else None, parts)
            x, outs = (None, res) if last else (res, None)

    if outs is None:
        outs = [_rmsnorm(x, norm_final, f32, start, rows) for start, rows in parts]
    return (outs[0].reshape(bp, sp, d), outs[1].reshape(bs, ss, d))
```

```python
import functools

import jax
import jax.numpy as jnp
from jax import lax
from jax.experimental import pallas as pl
from jax.experimental.pallas import tpu as pltpu

EPS = 1e-6
POOL_WINDOWS = (2, 4, 8, 16)
POOL_HALO = 8
POOL_PAD = 16
POOL_BLOCK = 128
HEAD_DIM = 128
TOP_K = 2
V7X_VMEM_LIMIT = 56 * 1024 * 1024
LANES = 128
SUBLANES = 8
MOE_ROW_TILE = 512
MOE_FF_TILE = 512
MOE_DISPATCH_TILE = 512
MOE_COMBINE_TILE = 256
ROW_DMA_UNROLL = 8
EPILOGUE_COLS = 256
BF16_SUBLANES = 16

bf16 = jnp.bfloat16
f32 = jnp.float32


def _cparams(sem):
    return pltpu.CompilerParams(dimension_semantics=sem, vmem_limit_bytes=V7X_VMEM_LIMIT)


def _tile(n, pref):
    t = min(n, pref)
    while n % t:
        t //= 2
    return t


def _rms(x, g):
    ms = jnp.mean(x * x, axis=-1, keepdims=True)
    return x * lax.rsqrt(ms + EPS) * g


def _silu(x):
    return x * jax.nn.sigmoid(x)


def _seq_flags(tok, n_tok, n_prompt, seq_p, seq_s):
    in_p = tok < n_prompt
    rel = jnp.where(in_p, tok, tok - n_prompt)
    slen = jnp.where(in_p, seq_p, seq_s)
    is_start = (rel % slen) == 0
    is_end = ((rel + n_tok) % slen) == 0
    return is_start, is_end


def _rmsnorm_kernel(x_ref, g_ref, o_ref):
    o_ref[...] = _rms(x_ref[...], g_ref[...]).astype(o_ref.dtype)


def _rmsnorm(x, g, out_dtype, row_start=0, n_rows=None):
    n, d = x.shape
    n_rows = n if n_rows is None else n_rows
    tr = _tile(n_rows, 512)
    off = row_start // tr
    return pl.pallas_call(
        _rmsnorm_kernel,
        grid=(n_rows // tr,),
        in_specs=[pl.BlockSpec((tr, d), lambda i: (i + off, 0)),
                  pl.BlockSpec((1, d), lambda i: (0, 0))],
        out_specs=pl.BlockSpec((tr, d), lambda i: (i, 0)),
        out_shape=jax.ShapeDtypeStruct((n_rows, d), out_dtype),
        compiler_params=_cparams(("parallel",)),
    )(x, g.reshape(1, d))


def _pool_kernel(ap_ref, a_ref, an_ref, bp_ref, b_ref, bn_ref, gmix_ref, w_ref, scale_ref, gffn_ref,
                 x1_ref, h1_ref, hi_s, lo_s, *, seq):
    t, d = a_ref.shape
    dg = d // len(POOL_WINDOWS)
    tok = pl.program_id(0) * t
    is_start, is_end = _seq_flags(tok, t, *seq)
    in_a = tok < seq[0]
    g = gmix_ref[...]
    x = jnp.where(in_a, a_ref[...], b_ref[...])
    h = _rms(x, g)
    h_prev = jnp.where(is_start, 0.0, _rms(jnp.where(in_a, ap_ref[...], bp_ref[...]), g))
    h_next = jnp.where(is_end, 0.0, _rms(jnp.where(in_a, an_ref[...], bn_ref[...]), g))
    zeros = jnp.zeros_like(h_prev)

    def put(r0, val):
        hi = val.astype(bf16)
        hi_s[r0:r0 + val.shape[0], :] = hi
        lo_s[r0:r0 + val.shape[0], :] = (val - hi.astype(f32)).astype(bf16)

    put(0, jnp.concatenate([zeros, h_prev], axis=0))
    put(POOL_PAD, h)
    put(POOL_PAD + t, jnp.concatenate([h_next, zeros], axis=0))
    tail = hi_s.shape[0] - (2 * POOL_PAD + t)
    hi_s[2 * POOL_PAD + t:, :] = jnp.zeros((tail, d), bf16)
    lo_s[2 * POOL_PAD + t:, :] = jnp.zeros((tail, d), bf16)

    rr = lax.broadcasted_iota(jnp.int32, (POOL_BLOCK, 2 * POOL_BLOCK), 0)
    cc = lax.broadcasted_iota(jnp.int32, (POOL_BLOCK, 2 * POOL_BLOCK), 1)
    r = lax.broadcasted_iota(jnp.int32, (t, 1), 0)
    for gi, w in enumerate(POOL_WINDOWS):
        c0 = gi * dg
        half = w // 2
        band = ((cc >= POOL_PAD + rr - half) & (cc < POOL_PAD + rr + half)).astype(bf16)
        sums = []
        for r0 in range(0, t, POOL_BLOCK):
            sums.append(jnp.dot(band, hi_s[r0:r0 + 2 * POOL_BLOCK, c0:c0 + dg], preferred_element_type=f32)
                        + jnp.dot(band, lo_s[r0:r0 + 2 * POOL_BLOCK, c0:c0 + dg], preferred_element_type=f32))
        s = jnp.concatenate(sums, axis=0)
        lo = jnp.where(is_start, jnp.maximum(r - half, 0), r - half)
        hi = jnp.where(is_end, jnp.minimum(r + half, t), r + half)
        cnt = (hi - lo).astype(f32)
        dlt = s / cnt - h[:, c0:c0 + dg]
        y = jnp.dot(dlt.astype(bf16), w_ref[gi], preferred_element_type=f32)
        x1_ref[:, c0:c0 + dg] = x[:, c0:c0 + dg] + y * scale_ref[:, c0:c0 + dg]
    h1_ref[...] = _rms(x1_ref[...], gffn_ref[...]).astype(h1_ref.dtype)


def _stream_specs(rows, first_tile, t, d):
    hb = t // POOL_HALO
    last_halo = rows // POOL_HALO - 1
    last_tile = rows // t - 1
    return [
        pl.BlockSpec((POOL_HALO, d), lambda i: (jnp.clip((i - first_tile) * hb - 1, 0, last_halo), 0)),
        pl.BlockSpec((t, d), lambda i: (jnp.clip(i - first_tile, 0, last_tile), 0)),
        pl.BlockSpec((POOL_HALO, d), lambda i: (jnp.clip((i - first_tile + 1) * hb, 0, last_halo), 0)),
    ]


def _pool_layer(xa, xb, g_mix, w_grp, scale, g_ffn, seq):
    d = xa.shape[1]
    n = xa.shape[0] + xb.shape[0]
    t = _tile(min(seq[1], seq[2]), 256)
    row = lambda i: (0, 0)
    return pl.pallas_call(
        functools.partial(_pool_kernel, seq=seq),
        grid=(n // t,),
        in_specs=_stream_specs(xa.shape[0], 0, t, d) + _stream_specs(xb.shape[0], xa.shape[0] // t, t, d) + [
            pl.BlockSpec((1, d), row),
            pl.BlockSpec(w_grp.shape, lambda i: (0, 0, 0)),
            pl.BlockSpec((1, d), row),
            pl.BlockSpec((1, d), row),
        ],
        out_specs=[pl.BlockSpec((t, d), lambda i: (i, 0)),
                   pl.BlockSpec((t, d), lambda i: (i, 0))],
        out_shape=[jax.ShapeDtypeStruct((n, d), f32), jax.ShapeDtypeStruct((n, d), bf16)],
        scratch_shapes=[pltpu.VMEM((t + POOL_BLOCK, d), bf16)] * 2,
        compiler_params=_cparams(("parallel",)),
    )(xa, xa, xa, xb, xb, xb, g_mix.reshape(1, d), w_grp, scale.reshape(1, d), g_ffn.reshape(1, d))


def _matmul_call_with_casts(kernel, grid, in_specs, out_specs, out_shapes, args, to_cast):
    n_steps = grid[0] * grid[1]
    plans = []
    for stacked, layer, scale in to_cast:
        cols = stacked.shape[-1]
        rows = stacked[0].size // cols
        steps = 1
        while steps * 2 <= n_steps and rows % (steps * 2 * BF16_SUBLANES) == 0:
            steps *= 2
        plans.append((rows, cols, steps, rows // steps, layer))
    scaled = [k for k, (_, _, scale) in enumerate(to_cast) if scale is not None]
    n_in, n_side, n_scale, n_out = len(in_specs), len(to_cast), len(scaled), len(out_specs)

    def step_block(steps):
        return lambda i, j: jnp.minimum(i * grid[1] + j, steps - 1)

    def src_spec(cols, steps, rb, layer):
        blk = step_block(steps)
        return pl.BlockSpec((None, rb, cols), lambda i, j: (layer, blk(i, j), 0))

    def dst_spec(cols, steps, rb):
        blk = step_block(steps)
        return pl.BlockSpec((rb, cols), lambda i, j: (blk(i, j), 0))

    def body(*refs):
        first_out = n_in + n_side + n_scale
        kernel(*refs[:n_in], *refs[first_out:first_out + n_out])
        step = pl.program_id(0) * grid[1] + pl.program_id(1)
        for k, plan in enumerate(plans):
            scale_ref = refs[n_in + n_side + scaled.index(k)] if k in scaled else None

            @pl.when(step < plan[2])
            def _(src=refs[n_in + k], dst=refs[first_out + n_out + k], scale_ref=scale_ref):
                val = src[...] if scale_ref is None else src[...] * scale_ref[...]
                dst[...] = val.astype(bf16)

    outs = pl.pallas_call(
        body,
        grid=grid,
        in_specs=list(in_specs) + [src_spec(c, s, rb, layer) for _, c, s, rb, layer in plans]
        + [dst_spec(1, plans[k][2], plans[k][3]) for k in scaled],
        out_specs=list(out_specs) + [dst_spec(c, s, rb) for _, c, s, rb, _ in plans],
        out_shape=list(out_shapes) + [jax.ShapeDtypeStruct((r, c), bf16) for r, c, _, _, _ in plans],
        compiler_params=_cparams(("arbitrary", "arbitrary")),
    )(*args,
      *[stacked.reshape(stacked.shape[0], r, c) for (stacked, _, _), (r, c, _, _, _) in zip(to_cast, plans)],
      *[to_cast[k][2].reshape(-1, 1) for k in scaled])
    return outs[:n_out], [o.reshape(stacked.shape[1:]) for o, (stacked, _, _) in zip(outs[n_out:], to_cast)]


def _glu_up_kernel(a_ref, wg_ref, wu_ref, o_ref):
    a = a_ref[...]
    gate = jnp.dot(a, wg_ref[...], preferred_element_type=f32)
    up = jnp.dot(a, wu_ref[...], preferred_element_type=f32)
    o_ref[...] = (_silu(gate) * up).astype(o_ref.dtype)


def _glu_up(a, wg, wu, to_cast=()):
    m, k = a.shape
    f = wg.shape[1]
    tm = _tile(m, 1024)
    tn = _tile(f, 512)
    wspec = pl.BlockSpec((k, tn), lambda i, j: (0, j))
    (act,), rounded = _matmul_call_with_casts(
        _glu_up_kernel, (m // tm, f // tn),
        [pl.BlockSpec((tm, k), lambda i, j: (i, 0)), wspec, wspec],
        [pl.BlockSpec((tm, tn), lambda i, j: (i, j))],
        [jax.ShapeDtypeStruct((m, f), bf16)], (a, wg, wu), to_cast)
    return act, rounded


def _down_kernel(a_ref, w_ref, res_ref, o_ref, *norm_refs):
    o = res_ref[...] + jnp.dot(a_ref[...], w_ref[...], preferred_element_type=f32)
    o_ref[...] = o
    if not norm_refs:
        return
    o16_ref, sq_ref = norm_refs
    o16_ref[...] = o.astype(o16_ref.dtype)
    sq = o * o
    part = sq[:, 0:LANES]
    for c0 in range(LANES, sq.shape[1], LANES):
        part = part + sq[:, c0:c0 + LANES]

    @pl.when(pl.program_id(1) == 0)
    def _():
        sq_ref[...] = part

    @pl.when(pl.program_id(1) > 0)
    def _():
        sq_ref[...] += part


def _down_res(a, w, res, tm, norm_rows=False, to_cast=()):
    m, kk = a.shape
    n = w.shape[1]
    tm = _tile(m, tm)
    tn = _tile(n, 512)
    tile = pl.BlockSpec((tm, tn), lambda i, j: (i, j))
    out_specs, out_shapes = [tile], [jax.ShapeDtypeStruct((m, n), f32)]
    if norm_rows:
        out_specs += [tile, pl.BlockSpec((tm, LANES), lambda i, j: (i, 0))]
        out_shapes += [jax.ShapeDtypeStruct((m, n), bf16), jax.ShapeDtypeStruct((m, LANES), f32)]
    return _matmul_call_with_casts(
        _down_kernel, (m // tm, n // tn),
        [pl.BlockSpec((tm, kk), lambda i, j: (i, 0)), pl.BlockSpec((kk, tn), lambda i, j: (0, j)), tile],
        out_specs, out_shapes, (a, w, res), to_cast)


def _column_blocks(a_ref, sq_ref, w_ref):
    a = a_ref[...]
    inv = lax.rsqrt(jnp.sum(sq_ref[...], axis=1, keepdims=True) / a.shape[1] + EPS)
    tn = w_ref.shape[1]
    cb = min(tn, EPILOGUE_COLS)
    for c0 in range(0, tn, cb):
        yield c0, inv * jnp.dot(a, w_ref[:, c0:c0 + cb], preferred_element_type=f32)


def _store_head_block(o_ref, c0, val):
    for hh in range(val.shape[1] // HEAD_DIM):
        o_ref[c0 // HEAD_DIM + hh] = val[:, hh * HEAD_DIM:(hh + 1) * HEAD_DIM].astype(o_ref.dtype)


def _proj_silu_heads_kernel(a_ref, sq_ref, w_ref, o_ref):
    for c0, r in _column_blocks(a_ref, sq_ref, w_ref):
        _store_head_block(o_ref, c0, _silu(r))


def _proj_heads_kernel(a_ref, sq_ref, w_ref, o_ref):
    for c0, r in _column_blocks(a_ref, sq_ref, w_ref):
        _store_head_block(o_ref, c0, r)


def _proj_silu_kernel(a_ref, sq_ref, w_ref, o_ref):
    for c0, r in _column_blocks(a_ref, sq_ref, w_ref):
        o_ref[:, c0:c0 + r.shape[1]] = _silu(r).astype(o_ref.dtype)


def _proj_forget_kernel(a_ref, sq_ref, w_ref, lb_ref, k_ref, g_ref, *, layer):
    logits = lb_ref[...]
    ex = jnp.exp(logits - jnp.max(logits, axis=0, keepdims=True))
    sm = ex / jnp.sum(ex, axis=0, keepdims=True)
    lb_all = jnp.sum(sm[0:layer + 1], axis=0, keepdims=True) - sm[0:1]
    for c0, r in _column_blocks(a_ref, sq_ref, w_ref):
        lb = lb_all[:, c0:c0 + r.shape[1]]
        f = lb + (1.0 - lb) * jax.nn.sigmoid(r)
        _store_head_block(k_ref, c0, 1.0 - f)
        _store_head_block(g_ref, c0, jnp.log2(f))


def _proj_section(kernel, a, sq, w, section, out_heads, n_out=1, extra=None):
    m, k = a.shape
    d = k
    tm = _tile(m, 1024)
    tn = _tile(d, 1024)
    nb = d // tn
    in_specs = [pl.BlockSpec((tm, k), lambda i, j: (i, 0)),
                pl.BlockSpec((tm, LANES), lambda i, j: (i, 0)),
                pl.BlockSpec((k, tn), lambda i, j: (0, section * nb + j))]
    args = [a, sq, w]
    if extra is not None:
        in_specs.append(pl.BlockSpec((extra.shape[0], tn), lambda i, j: (0, j)))
        args.append(extra)
    if out_heads:
        hpb = tn // HEAD_DIM
        ospec = pl.BlockSpec((hpb, tm, HEAD_DIM), lambda i, j: (j, i, 0))
        oshape = jax.ShapeDtypeStruct((d // HEAD_DIM, m, HEAD_DIM), bf16)
    else:
        ospec = pl.BlockSpec((tm, tn), lambda i, j: (i, j))
        oshape = jax.ShapeDtypeStruct((m, d), bf16)
    out = pl.pallas_call(
        kernel,
        grid=(m // tm, nb),
        in_specs=in_specs,
        out_specs=[ospec] * n_out if n_out > 1 else ospec,
        out_shape=[oshape] * n_out if n_out > 1 else oshape,
        compiler_params=_cparams(("parallel", "arbitrary")),
    )(*args)
    return out


MAX_DIRECT_LOG2_DECAY = 100.0
HEAD_UNROLL = 8


def _dot_nt(a, b):
    return lax.dot_general(a, b, (((1,), (1,)), ((), ())), preferred_element_type=f32)


def _dot_tn(a, b):
    return lax.dot_general(a, b, (((0,), (0,)), ((), ())), preferred_element_type=f32)


def _mid_ref(p, level, offset):
    c, w = p.shape
    blk = 2 << level
    p3 = p.reshape(c // blk, blk, w)
    return jnp.broadcast_to(p3[:, offset:offset + 1, :], p3.shape).reshape(c, w)


def _neg_abs_exponents(gf, gb, b, x, level, rmod4):
    if level == 0:
        odd = (rmod4 & 1) == 1
        return jnp.where(odd, gf, 0.0), jnp.where(odd, 0.0, gb)
    if level == 1:
        c = gf.shape[0]
        gf_up = pltpu.roll(gf, c - 1, 0)
        gf_dn = pltpu.roll(gf, 1, 0)
        gb_up = pltpu.roll(gb, c - 1, 0)
        gb_dn = pltpu.roll(gb, 1, 0)
        ef = jnp.where(rmod4 == 0, gf_up, jnp.where(rmod4 == 1, 0.0, jnp.where(rmod4 == 2, gf, gf + gf_dn)))
        eb = jnp.where(rmod4 == 0, gb + gb_up, jnp.where(rmod4 == 1, gb, jnp.where(rmod4 == 2, 0.0, gb_dn)))
        return ef, eb
    half = 1 << level
    ef = b - _mid_ref(b, level, half - 1)
    eb = x - _mid_ref(x, level, half)
    return -jnp.abs(ef), -jnp.abs(eb)


def _chunk_prefix(tri_incl, tri_excl, gf16, gb16):
    b = jnp.dot(tri_incl, gf16, preferred_element_type=f32)
    x = jnp.dot(tri_excl, gb16, preferred_element_type=f32)
    return b, x


def _scan_bw_kernel(kb_ref, gb_ref, v_ref, s_ref, st_ref, *, seq, n_chunks):
    n_heads, c, _ = kb_ref.shape
    chunk = n_chunks - 1 - pl.program_id(0)
    _, is_end = _seq_flags(chunk * c, c, *seq)

    @pl.when(is_end)
    def _():
        st_ref[...] = jnp.zeros_like(st_ref)

    row = lax.broadcasted_iota(jnp.int32, (c, c), 0)
    col = lax.broadcasted_iota(jnp.int32, (c, c), 1)
    tri_excl = (col < row).astype(bf16)

    def head(h, carry):
        gb16 = gb_ref[h]
        x = jnp.dot(tri_excl, gb16, preferred_element_type=f32)
        tot = x[c - 1:c, :] + gb16[c - 1:c, :].astype(f32)
        st = st_ref[h]
        s_ref[h] = st.astype(s_ref.dtype)
        kx = (kb_ref[h].astype(f32) * jnp.exp2(x)).astype(bf16)
        st_ref[h] = st * jnp.exp2(tot) + _dot_tn(v_ref[h], kx)
        return carry

    lax.fori_loop(0, n_heads, head, 0, unroll=2 * HEAD_UNROLL)


def _scan_bw_states(kb, gb, v, seq, c):
    n_heads, n, dk = kb.shape
    n_chunks = n // c
    spec = pl.BlockSpec((n_heads, c, dk), lambda i: (0, n_chunks - 1 - i, 0))
    return pl.pallas_call(
        functools.partial(_scan_bw_kernel, seq=seq, n_chunks=n_chunks),
        grid=(n_chunks,),
        in_specs=[spec, spec, spec],
        out_specs=pl.BlockSpec((None, n_heads, dk, dk), lambda i: (n_chunks - 1 - i, 0, 0, 0)),
        out_shape=jax.ShapeDtypeStruct((n_chunks, n_heads, dk, dk), bf16),
        scratch_shapes=[pltpu.VMEM((n_heads, dk, dk), f32)],
        compiler_params=_cparams(("arbitrary",)),
    )(kb, gb, v)


def _scan_kernel(q_ref, kf_ref, kb_ref, gf_ref, gb_ref, v_ref, sb_ref, tril_ref, triu_ref, gate_ref,
                 gain_ref, out_ref, st_ref, o_s, qf_s, kf_s, qb_s, kb_s, qin_s, kx_s, sc_s, dec_s, tot_s,
                 *, seq):
    n_heads, c, dk = q_ref.shape
    n_levels = c.bit_length() - 1
    mid = c // 2
    is_start, _ = _seq_flags(pl.program_id(0) * c, c, *seq)

    @pl.when(is_start)
    def _():
        st_ref[...] = jnp.zeros_like(st_ref)

    row = lax.broadcasted_iota(jnp.int32, (c, c), 0)
    col = lax.broadcasted_iota(jnp.int32, (c, c), 1)
    tri_incl = (col <= row).astype(bf16)
    tri_excl = (col < row).astype(bf16)

    def finish(h, kf_dec, b_last, q_in, scores, diag):
        v = v_ref[h]
        st_f = st_ref[h]
        st_cat = jnp.concatenate([st_f.astype(bf16), sb_ref[h]], axis=1)
        o = _dot_nt(q_in, st_cat) + jnp.dot(scores.astype(bf16), v, preferred_element_type=f32)
        o_s[h] = o if diag is None else o + diag * v.astype(f32)
        st_ref[h] = st_f * jnp.exp2(b_last) + _dot_tn(v, kf_dec.astype(bf16))

    def direct_operands(h, carry):
        q = q_ref[h].astype(f32)
        gb16 = gb_ref[h]
        b, x = _chunk_prefix(tri_incl, tri_excl, gf_ref[h], gb16)
        b_mid = b[mid - 1:mid, :]
        x_mid = x[mid:mid + 1, :]
        db = b - b_mid
        dx = x - x_mid
        q_f = q * jnp.exp2(db)
        k_f = kf_ref[h].astype(f32) * jnp.exp2(-db)
        q_b = q * jnp.exp2(-dx)
        k_b = kb_ref[h].astype(f32) * jnp.exp2(dx)
        b_last = b[c - 1:c, :]
        x_tot = x[c - 1:c, :] + gb16[c - 1:c, :].astype(f32)
        qf_s[h] = q_f.astype(bf16)
        kf_s[h] = k_f.astype(bf16)
        qb_s[h] = q_b.astype(bf16)
        kb_s[h] = k_b.astype(bf16)
        qin_s[h] = jnp.concatenate([q_f * jnp.exp2(b_mid), q_b * jnp.exp2(x_tot - x_mid)], axis=1).astype(bf16)
        kx_s[h] = (k_f * jnp.exp2(b_last - b_mid)).astype(bf16)
        dec_s[h] = jnp.broadcast_to(jnp.exp2(b_last), dec_s.shape[1:])
        tot_s[h] = jnp.broadcast_to(jnp.minimum(b_last, x_tot), tot_s.shape[1:])
        return carry

    def direct_scores(h, carry):
        sc_s[h] = (jnp.where(tril_ref[...] != 0.0, _dot_nt(qf_s[h], kf_s[h]), 0.0)
                   + jnp.where(triu_ref[...] != 0.0, _dot_nt(qb_s[h], kb_s[h]), 0.0)).astype(bf16)
        return carry

    def direct_output(h, carry):
        v = v_ref[h]
        st_f = st_ref[h]
        st_cat = jnp.concatenate([st_f.astype(bf16), sb_ref[h]], axis=1)
        o_s[h] = _dot_nt(qin_s[h], st_cat) + jnp.dot(sc_s[h], v, preferred_element_type=f32)
        st_ref[h] = st_f * dec_s[h][0:1, :] + _dot_tn(v, kx_s[h])
        return carry

    def head_levelled(h, carry):
        rtok = lax.broadcasted_iota(jnp.int32, (c, dk), 0)
        rmod4 = rtok & 3
        q = q_ref[h].astype(f32)
        kf = kf_ref[h].astype(f32)
        kb = kb_ref[h].astype(f32)
        gf16 = gf_ref[h]
        gb16 = gb_ref[h]
        gf = gf16.astype(f32)
        gb = gb16.astype(f32)
        b, x = _chunk_prefix(tri_incl, tri_excl, gf16, gb16)
        scores = jnp.zeros((c, c), f32)
        for l in range(n_levels):
            ef, eb = _neg_abs_exponents(gf, gb, b, x, l, rmod4)
            af = jnp.exp2(ef)
            ab = jnp.exp2(eb)
            up = ((rtok >> l) & 1) == 1
            lhs = jnp.concatenate([jnp.where(up, q * af, 0.0), jnp.where(up, 0.0, q * ab)], axis=1)
            rhs = jnp.concatenate([jnp.where(up, 0.0, kf * af), jnp.where(up, kb * ab, 0.0)], axis=1)
            p = _dot_nt(lhs.astype(bf16), rhs.astype(bf16))
            if l < n_levels - 1:
                p = jnp.where((row >> (l + 1)) == (col >> (l + 1)), p, 0.0)
            scores = scores + p
        b_last = b[c - 1:c, :]
        x_tot = x[c - 1:c, :] + gb[c - 1:c, :]
        q_in = jnp.concatenate([q * jnp.exp2(b), q * jnp.exp2(x_tot - x)], axis=1).astype(bf16)
        diag = jnp.sum(q * (kf + kb), axis=1, keepdims=True)
        finish(h, kf * jnp.exp2(b_last - b), b_last, q_in, scores, diag)
        return carry

    lax.fori_loop(0, n_heads, direct_operands, 0, unroll=2 * HEAD_UNROLL)
    direct_ok = jnp.min(tot_s[...]) >= -MAX_DIRECT_LOG2_DECAY

    @pl.when(direct_ok)
    def _():
        lax.fori_loop(0, n_heads, direct_scores, 0, unroll=2 * HEAD_UNROLL)
        lax.fori_loop(0, n_heads, direct_output, 0, unroll=2 * HEAD_UNROLL)

    @pl.when(jnp.logical_not(direct_ok))
    def _():
        lax.fori_loop(0, n_heads, head_levelled, 0)

    sq = jnp.zeros((c, dk), f32)
    for hh in range(n_heads):
        o = o_s[hh]
        sq = sq + o * o
    inv = lax.rsqrt(jnp.sum(sq, axis=1, keepdims=True) / (n_heads * dk) + EPS)
    for hh in range(n_heads):
        cols = slice(hh * dk, (hh + 1) * dk)
        out_ref[:, cols] = (o_s[hh] * inv * gain_ref[:, cols] * gate_ref[:, cols].astype(f32)
                            ).astype(out_ref.dtype)


def _scan(q, kf, kb, gf, gb, v, s_bw, gate, gain, seq, c):
    n_heads, n, dk = q.shape
    d = n_heads * dk
    n_chunks = n // c
    spec = pl.BlockSpec((n_heads, c, dk), lambda i: (0, i, 0))
    tri = pl.BlockSpec((c, c), lambda i: (0, 0))
    tril = jnp.tril(jnp.ones((c, c), f32))
    return pl.pallas_call(
        functools.partial(_scan_kernel, seq=seq),
        grid=(n_chunks,),
        in_specs=[spec] * 6 + [pl.BlockSpec((None, n_heads, dk, dk), lambda i: (i, 0, 0, 0)), tri, tri,
                               pl.BlockSpec((c, d), lambda i: (i, 0)),
                               pl.BlockSpec((1, d), lambda i: (0, 0))],
        out_specs=pl.BlockSpec((c, d), lambda i: (i, 0)),
        out_shape=jax.ShapeDtypeStruct((n, d), bf16),
        scratch_shapes=[pltpu.VMEM((n_heads, dk, dk), f32), pltpu.VMEM((n_heads, c, dk), f32)]
        + [pltpu.VMEM((n_heads, c, dk), bf16)] * 4
        + [pltpu.VMEM((n_heads, c, 2 * dk), bf16), pltpu.VMEM((n_heads, c, dk), bf16),
           pltpu.VMEM((n_heads, c, c), bf16)] + [pltpu.VMEM((n_heads, 8, dk), f32)] * 2,
        compiler_params=_cparams(("arbitrary",)),
    )(q, kf, kb, gf, gb, v, s_bw, tril, tril.T, gate, gain.reshape(1, d))


def _pack_halves(x):
    half = x.shape[1] // 2
    bits = lax.bitcast_convert_type(x.astype(bf16).astype(f32), jnp.int32)
    return lax.shift_right_logical(bits[:, :half], 16) | (bits[:, half:] & jnp.int32(-65536))


def _unpack_halves(words):
    lo = lax.bitcast_convert_type(lax.shift_left(words, 16), f32)
    hi = lax.bitcast_convert_type(words & jnp.int32(-65536), f32)
    return lo, hi


def _router_kernel(x_ref, g_ref, r2_ref, h_ref, e_ref, rank_ref, w_ref, cnt_ref, run_ref,
                   *, n_experts):
    @pl.when(pl.program_id(0) == 0)
    def _():
        run_ref[...] = jnp.zeros_like(run_ref)

    h = _rms(x_ref[...], g_ref[...])
    h_ref[...] = _pack_halves(h)
    hi = h.astype(bf16)
    lo = (h - hi.astype(f32)).astype(bf16)
    tr = h.shape[0]
    prod = jnp.dot(jnp.concatenate([hi, lo], axis=0), r2_ref[...], preferred_element_type=f32)
    logits = prod[:tr, :LANES] + prod[:tr, LANES:] + prod[tr:, :LANES]
    lane = lax.broadcasted_iota(jnp.int32, logits.shape, 1)
    neg = jnp.float32(-jnp.inf)
    logits = jnp.where(lane < n_experts, logits, neg)
    m1 = jnp.max(logits, axis=1, keepdims=True)
    i1 = jnp.min(jnp.where(logits == m1, lane, LANES), axis=1, keepdims=True)
    rest = jnp.where(lane == i1, neg, logits)
    m2 = jnp.max(rest, axis=1, keepdims=True)
    i2 = jnp.min(jnp.where(rest == m2, lane, LANES), axis=1, keepdims=True)
    e2 = jnp.exp(m2 - m1)
    w_ref[...] = jnp.concatenate([1.0 / (1.0 + e2), e2 / (1.0 + e2)], axis=1)
    e_ref[...] = jnp.concatenate([i1, i2], axis=1)
    member = jnp.where((lane == i1) | (lane == i2), 1.0, 0.0)
    row = lax.broadcasted_iota(jnp.int32, (tr, tr), 0)
    col = lax.broadcasted_iota(jnp.int32, (tr, tr), 1)
    before = jnp.dot((col < row).astype(bf16), member.astype(bf16), preferred_element_type=f32)
    before = before + run_ref[...]
    r1 = jnp.sum(jnp.where(lane == i1, before, 0.0), axis=1, keepdims=True)
    r2 = jnp.sum(jnp.where(lane == i2, before, 0.0), axis=1, keepdims=True)
    rank_ref[...] = jnp.concatenate([r1, r2], axis=1).astype(jnp.int32)
    run = run_ref[...] + jnp.sum(member, axis=0, keepdims=True)
    run_ref[...] = run
    cnt_ref[...] = run.astype(jnp.int32)


def _router(x, g, router):
    n, d = x.shape
    e = router.shape[1]
    rpad = jnp.zeros((d, LANES), f32).at[:, :e].set(router)
    rhi = rpad.astype(bf16)
    r2 = jnp.concatenate([rhi, (rpad - rhi.astype(f32)).astype(bf16)], axis=1)
    tr = _tile(n, 512)
    pair = pl.BlockSpec((tr, TOP_K), lambda i: (i, 0))
    return pl.pallas_call(
        functools.partial(_router_kernel, n_experts=e),
        grid=(n // tr,),
        in_specs=[pl.BlockSpec((tr, d), lambda i: (i, 0)),
                  pl.BlockSpec((1, d), lambda i: (0, 0)),
                  pl.BlockSpec((d, 2 * LANES), lambda i: (0, 0))],
        out_specs=[pl.BlockSpec((tr, d // 2), lambda i: (i, 0)), pair, pair, pair,
                   pl.BlockSpec((1, LANES), lambda i: (0, 0))],
        out_shape=[jax.ShapeDtypeStruct((n, d // 2), jnp.int32),
                   jax.ShapeDtypeStruct((n, TOP_K), jnp.int32),
                   jax.ShapeDtypeStruct((n, TOP_K), jnp.int32),
                   jax.ShapeDtypeStruct((n, TOP_K), f32),
                   jax.ShapeDtypeStruct((1, LANES), jnp.int32)],
        scratch_shapes=[pltpu.VMEM((1, LANES), f32)],
        compiler_params=_cparams(("arbitrary",)),
    )(x, g.reshape(1, d), r2)


def _row_copies(pos_ref, i, src, dst, sem, gather):
    out = []
    for k in range(TOP_K):
        p = pos_ref[k, i]
        if gather:
            out.append(pltpu.make_async_copy(src.at[pl.ds(p, 1)], dst.at[k, pl.ds(i, 1)], sem))
        else:
            out.append(pltpu.make_async_copy(src.at[pl.ds(i, 1)], dst.at[pl.ds(p, 1)], sem))
    return out


def _start_rows(n_rows, pos_ref, src, dst, sem, gather):
    def start(i, carry):
        for cp in _row_copies(pos_ref, i, src, dst, sem, gather):
            cp.start()
        return carry

    lax.fori_loop(0, n_rows, start, 0, unroll=ROW_DMA_UNROLL)


def _wait_rows(n_rows, pos_ref, src, dst, sem, gather):
    def wait(i, carry):
        for cp in _row_copies(pos_ref, i, src, dst, sem, gather):
            cp.wait()
        return carry

    lax.fori_loop(0, n_rows, wait, 0, unroll=ROW_DMA_UNROLL)


def _all_rows(n_rows, pos_ref, src, dst, sem, gather):
    _start_rows(n_rows, pos_ref, src, dst, sem, gather)
    _wait_rows(n_rows, pos_ref, src, dst, sem, gather)


def _moe_dispatch_kernel(nv_ref, pos_ref, h_ref, xs_ref, zero_s, sem):
    tm = zero_s.shape[0]

    @pl.when(pl.program_id(0) == 0)
    def _():
        zero_s[...] = jnp.zeros_like(zero_s)

        def fill(r, carry):
            @pl.when(nv_ref[r] < tm)
            def _():
                cp = pltpu.make_async_copy(zero_s, xs_ref.at[pl.ds(pl.multiple_of(r * tm, tm), tm)], sem)
                cp.start()
                cp.wait()
            return carry

        lax.fori_loop(0, xs_ref.shape[0] // tm, fill, 0)

    _all_rows(h_ref.shape[0], pos_ref, h_ref, xs_ref, sem, gather=False)


def _moe_dispatch(hp, pos3, tile_rows, tm):
    n, dw = hp.shape
    n_tiles, _, tr = pos3.shape
    grid_spec = pltpu.PrefetchScalarGridSpec(
        num_scalar_prefetch=1,
        grid=(n_tiles,),
        in_specs=[pl.BlockSpec((None, TOP_K, tr), lambda i, nv: (i, 0, 0), memory_space=pltpu.SMEM),
                  pl.BlockSpec((tr, dw), lambda i, nv: (i, 0))],
        out_specs=pl.BlockSpec(memory_space=pl.ANY),
        scratch_shapes=[pltpu.VMEM((tm, dw), jnp.int32), pltpu.SemaphoreType.DMA(())],
    )
    return pl.pallas_call(
        _moe_dispatch_kernel,
        grid_spec=grid_spec,
        out_shape=jax.ShapeDtypeStruct((tile_rows.shape[0] * tm, dw), jnp.int32),
        compiler_params=_cparams(("arbitrary",)),
    )(tile_rows, pos3, hp)


def _moe_up_kernel(te_ref, nv_ref, xs_ref, wg_ref, wu_ref, a_ref, x_s):
    r = pl.program_id(0)
    half = xs_ref.shape[1]

    @pl.when(nv_ref[r] > 0)
    def _():
        @pl.when(pl.program_id(1) == 0)
        def _():
            lo, hi = _unpack_halves(xs_ref[...])
            x_s[:, :half] = lo.astype(bf16)
            x_s[:, half:] = hi.astype(bf16)

        x = x_s[...]
        gate = jnp.dot(x, wg_ref[...], preferred_element_type=f32)
        up = jnp.dot(x, wu_ref[...], preferred_element_type=f32)
        a_ref[...] = (_silu(gate) * up).astype(a_ref.dtype)

    @pl.when(nv_ref[r] == 0)
    def _():
        a_ref[...] = jnp.zeros_like(a_ref)


def _moe_down_kernel(te_ref, nv_ref, a_ref, wd_ref, y_ref):
    @pl.when(nv_ref[pl.program_id(0)] > 0)
    def _():
        y_ref[...] = _pack_halves(jnp.dot(a_ref[...], wd_ref[...], preferred_element_type=f32))

    @pl.when(nv_ref[pl.program_id(0)] == 0)
    def _():
        y_ref[...] = jnp.zeros_like(y_ref)


def _moe_group(xs, tile_expert, tile_rows, wg, wu, wd, tm):
    p, dw = xs.shape
    _, d, fe = wg.shape
    tf = _tile(fe, MOE_FF_TILE)
    wspec = pl.BlockSpec((None, d, tf), lambda r, f, te, nv: (te[r], 0, f))
    act = pl.pallas_call(
        _moe_up_kernel,
        grid_spec=pltpu.PrefetchScalarGridSpec(
            num_scalar_prefetch=2,
            grid=(p // tm, fe // tf),
            in_specs=[pl.BlockSpec((tm, dw), lambda r, f, te, nv: (r, 0)), wspec, wspec],
            out_specs=pl.BlockSpec((tm, tf), lambda r, f, te, nv: (r, f)),
            scratch_shapes=[pltpu.VMEM((tm, d), bf16)],
        ),
        out_shape=jax.ShapeDtypeStruct((p, fe), bf16),
        compiler_params=_cparams(("arbitrary", "arbitrary")),
    )(tile_expert, tile_rows, xs, wg, wu)
    return pl.pallas_call(
        _moe_down_kernel,
        grid_spec=pltpu.PrefetchScalarGridSpec(
            num_scalar_prefetch=2,
            grid=(p // tm,),
            in_specs=[pl.BlockSpec((tm, fe), lambda r, te, nv: (r, 0)),
                      pl.BlockSpec((None, fe, d), lambda r, te, nv: (te[r], 0, 0))],
            out_specs=pl.BlockSpec((tm, dw), lambda r, te, nv: (r, 0)),
        ),
        out_shape=jax.ShapeDtypeStruct((p, dw), jnp.int32),
        compiler_params=_cparams(("arbitrary",)),
    )(tile_expert, tile_rows, act, wd)


def _moe_combine_kernel(pos_ref, pos_next_ref, x_ref, w_ref, g_ref, y_ref, o_ref, ybuf, sem, *, final_norm):
    tr, d = x_ref.shape
    half = d // 2
    i = pl.program_id(0)
    slot = i % 2

    @pl.when(i == 0)
    def _():
        _start_rows(tr, pos_ref, y_ref, ybuf.at[0], sem.at[0], gather=True)

    _wait_rows(tr, pos_ref, y_ref, ybuf.at[slot], sem.at[slot], gather=True)
    for r0 in range(0, tr, SUBLANES):
        rows = slice(r0, r0 + SUBLANES)
        for ii in range(r0, r0 + SUBLANES):
            for cp in _row_copies(pos_next_ref, ii, y_ref, ybuf.at[1 - slot], sem.at[1 - slot], True):
                cp.start()
        w = w_ref[rows, :]
        lo0, hi0 = _unpack_halves(ybuf[slot, 0, rows, :])
        lo1, hi1 = _unpack_halves(ybuf[slot, 1, rows, :])
        o_lo = x_ref[rows, :half] + (w[:, 0:1] * lo0 + w[:, 1:2] * lo1)
        o_hi = x_ref[rows, half:] + (w[:, 0:1] * hi0 + w[:, 1:2] * hi1)
        if final_norm:
            ms = (jnp.sum(o_lo * o_lo, axis=1, keepdims=True)
                  + jnp.sum(o_hi * o_hi, axis=1, keepdims=True)) / d
            inv = lax.rsqrt(ms + EPS)
            o_lo = o_lo * inv * g_ref[:, :half]
            o_hi = o_hi * inv * g_ref[:, half:]
        o_ref[rows, :half] = o_lo
        o_ref[rows, half:] = o_hi

    @pl.when(i == pl.num_programs(0) - 1)
    def _():
        _wait_rows(tr, pos_next_ref, y_ref, ybuf.at[1 - slot], sem.at[1 - slot], gather=True)


def _moe_combine(x, w, y, pos3, gain, final_norm, row_start, n_rows):
    d = x.shape[1]
    tr = pos3.shape[2]
    off = row_start // tr
    last = off + n_rows // tr - 1
    return pl.pallas_call(
        functools.partial(_moe_combine_kernel, final_norm=final_norm),
        grid=(n_rows // tr,),
        in_specs=[pl.BlockSpec((None, TOP_K, tr), lambda i: (i + off, 0, 0), memory_space=pltpu.SMEM),
                  pl.BlockSpec((None, TOP_K, tr), lambda i: (jnp.minimum(i + off + 1, last), 0, 0),
                               memory_space=pltpu.SMEM),
                  pl.BlockSpec((tr, d), lambda i: (i + off, 0)),
                  pl.BlockSpec((tr, TOP_K), lambda i: (i + off, 0)),
                  pl.BlockSpec((1, d), lambda i: (0, 0)),
                  pl.BlockSpec(memory_space=pl.ANY)],
        out_specs=pl.BlockSpec((tr, d), lambda i: (i, 0)),
        out_shape=jax.ShapeDtypeStruct((n_rows, d), f32),
        scratch_shapes=[pltpu.VMEM((2, TOP_K, tr, d // 2), jnp.int32), pltpu.SemaphoreType.DMA((2,))],
        compiler_params=_cparams(("arbitrary",)),
    )(pos3, pos3, x, w, gain.reshape(1, d), y)


def _moe_layer(x, g, router, wg, wu, wd, final_gain, parts):
    n, d = x.shape
    n_exp = router.shape[1]
    tm = MOE_ROW_TILE
    hp, eid, rank, w, cnt = _router(x, g, router)
    cnt = cnt[0, :n_exp]
    padded = (cnt + tm - 1) // tm * tm
    seg_end = jnp.cumsum(padded)
    seg_start = seg_end - padded
    pos = seg_start[eid] + rank
    n_rows_padded = TOP_K * n + n_exp * tm
    tile0 = jnp.arange(n_rows_padded // tm, dtype=jnp.int32) * tm
    tile_expert = jnp.minimum(jnp.searchsorted(seg_end, tile0, side="right"), n_exp - 1).astype(jnp.int32)
    tile_rows = jnp.clip(cnt[tile_expert] - (tile0 - seg_start[tile_expert]), 0, tm).astype(jnp.int32)
    def by_tile(tr):
        return pos.reshape(n // tr, tr, TOP_K).transpose(0, 2, 1)

    pos3 = by_tile(_tile(n, MOE_COMBINE_TILE))
    xs = _moe_dispatch(hp, by_tile(_tile(n, MOE_DISPATCH_TILE)), tile_rows, tm)
    y = _moe_group(xs, tile_expert, tile_rows, wg, wu, wd, tm)
    if final_gain is None:
        return _moe_combine(x, w, y, pos3, g, False, 0, n)
    return [_moe_combine(x, w, y, pos3, final_gain, True, start, rows) for start, rows in parts]


def kernel(x_prompt, x_sample, norm_mix, norm_ffn, norm_final, pool_w, pool_scale, hgrn_w_in, hgrn_lb,
           hgrn_norm, hgrn_w_out, ffn_w_gate, ffn_w_up, ffn_w_down, moe_router, moe_w_gate, moe_w_up,
           moe_w_down):
    bp, sp, d = x_prompt.shape
    bs, ss, _ = x_sample.shape
    n_prompt = bp * sp
    n = n_prompt + bs * ss
    seq = (n_prompt, sp, ss)
    depth = norm_mix.shape[0]
    chunk = _tile(min(sp, ss), 128)

    parts = ((0, n_prompt), (n_prompt, bs * ss))
    streams = (x_prompt.reshape(n_prompt, d), x_sample.reshape(bs * ss, d))
    x = None if depth else jnp.concatenate(streams, axis=0)
    outs = None

    for i in range(depth):
        j = i // 2
        if i % 2 == 0:
            xa, xb = streams if x is None else (x[:n_prompt], x[n_prompt:])
            x, h = _pool_layer(xa, xb, norm_mix[i], pool_w[j].astype(bf16), pool_scale[j], norm_ffn[i], seq)
            nxt = i + 1 < depth
            act, cast_a = _glu_up(h, ffn_w_gate[j].astype(bf16), ffn_w_up[j].astype(bf16),
                                  [(ffn_w_down, j, None)] + ([(hgrn_w_in, j, norm_mix[i + 1])] if nxt else []))
            (x, *norm_rows), cast_b = _down_res(
                act, cast_a[0], x, 512, nxt,
                [(w, j, None) for w in (hgrn_w_out, moe_w_gate, moe_w_up, moe_w_down)] if nxt else [])
            rounded = dict(zip(("w_in", "w_out", "moe_gate", "moe_up", "moe_down"), cast_a[1:] + cast_b))
        else:
            x16, sq = norm_rows
            w_in = rounded["w_in"]
            q = _proj_section(_proj_silu_heads_kernel, x16, sq, w_in, 0, True)
            forget = functools.partial(_proj_forget_kernel, layer=i)
            kf, gf = _proj_section(forget, x16, sq, w_in, 1, True, 2, hgrn_lb)
            kb, gb = _proj_section(forget, x16, sq, w_in, 2, True, 2, hgrn_lb)
            v = _proj_section(_proj_heads_kernel, x16, sq, w_in, 3, True)
            gate = _proj_section(_proj_silu_kernel, x16, sq, w_in, 4, False)
            s_bw = _scan_bw_states(kb, gb, v, seq, chunk)
            og = _scan(q, kf, kb, gf, gb, v, s_bw, gate, hgrn_norm[j], seq, chunk)
            (x,), _ = _down_res(og, rounded["w_out"], x, 1024)
            last = i == depth - 1
            res = _moe_layer(x, norm_ffn[i], moe_router[j], rounded["moe_gate"], rounded["moe_up"],
                             rounded["moe_down"], norm_final if last else None, parts)
            x, outs = (None, res) if last else (res, None)

    if outs is None:
        outs = [_rmsnorm(x, norm_final, f32, start, rows) for start, rows in parts]
    return (outs[0].reshape(bp, sp, d), outs[1].reshape(bs, ss, d))
```

```python
import functools

import jax
import jax.numpy as jnp
from jax import lax
from jax.experimental import pallas as pl
from jax.experimental.pallas import tpu as pltpu

EPS = 1e-6
POOL_WINDOWS = (2, 4, 8, 16)
POOL_HALO = 8
POOL_PAD = 16
POOL_BLOCK = 128
HEAD_DIM = 128
TOP_K = 2
V7X_VMEM_LIMIT = 56 * 1024 * 1024
LANES = 128
SUBLANES = 8
MOE_ROW_TILE = 512
MOE_FF_TILE = 512
MOE_DISPATCH_TILE = 512
MOE_COMBINE_TILE = 256
ROW_DMA_UNROLL = 8
EPILOGUE_COLS = 256
BF16_SUBLANES = 16

bf16 = jnp.bfloat16
f32 = jnp.float32


def _cparams(sem):
    return pltpu.CompilerParams(dimension_semantics=sem, vmem_limit_bytes=V7X_VMEM_LIMIT)


def _tile(n, pref):
    t = min(n, pref)
    while n % t:
        t //= 2
    return t


def _rms(x, g):
    ms = jnp.mean(x * x, axis=-1, keepdims=True)
    return x * lax.rsqrt(ms + EPS) * g


def _silu(x):
    return x * jax.nn.sigmoid(x)


def _seq_flags(tok, n_tok, n_prompt, seq_p, seq_s):
    in_p = tok < n_prompt
    rel = jnp.where(in_p, tok, tok - n_prompt)
    slen = jnp.where(in_p, seq_p, seq_s)
    is_start = (rel % slen) == 0
    is_end = ((rel + n_tok) % slen) == 0
    return is_start, is_end


def _rmsnorm_kernel(x_ref, g_ref, o_ref):
    o_ref[...] = _rms(x_ref[...], g_ref[...]).astype(o_ref.dtype)


def _rmsnorm(x, g, out_dtype, row_start=0, n_rows=None):
    n, d = x.shape
    n_rows = n if n_rows is None else n_rows
    tr = _tile(n_rows, 512)
    off = row_start // tr
    return pl.pallas_call(
        _rmsnorm_kernel,
        grid=(n_rows // tr,),
        in_specs=[pl.BlockSpec((tr, d), lambda i: (i + off, 0)),
                  pl.BlockSpec((1, d), lambda i: (0, 0))],
        out_specs=pl.BlockSpec((tr, d), lambda i: (i, 0)),
        out_shape=jax.ShapeDtypeStruct((n_rows, d), out_dtype),
        compiler_params=_cparams(("parallel",)),
    )(x, g.reshape(1, d))


def _pool_kernel(ap_ref, a_ref, an_ref, bp_ref, b_ref, bn_ref, gmix_ref, w_ref, scale_ref, gffn_ref,
                 x1_ref, h1_ref, hi_s, lo_s, *, seq):
    t, d = a_ref.shape
    dg = d // len(POOL_WINDOWS)
    tok = pl.program_id(0) * t
    is_start, is_end = _seq_flags(tok, t, *seq)
    in_a = tok < seq[0]
    g = gmix_ref[...]
    x = jnp.where(in_a, a_ref[...], b_ref[...])
    h = _rms(x, g)
    h_prev = jnp.where(is_start, 0.0, _rms(jnp.where(in_a, ap_ref[...], bp_ref[...]), g))
    h_next = jnp.where(is_end, 0.0, _rms(jnp.where(in_a, an_ref[...], bn_ref[...]), g))
    zeros = jnp.zeros_like(h_prev)

    def put(r0, val):
        hi = val.astype(bf16)
        hi_s[r0:r0 + val.shape[0], :] = hi
        lo_s[r0:r0 + val.shape[0], :] = (val - hi.astype(f32)).astype(bf16)

    put(0, jnp.concatenate([zeros, h_prev], axis=0))
    put(POOL_PAD, h)
    put(POOL_PAD + t, jnp.concatenate([h_next, zeros], axis=0))
    tail = hi_s.shape[0] - (2 * POOL_PAD + t)
    hi_s[2 * POOL_PAD + t:, :] = jnp.zeros((tail, d), bf16)
    lo_s[2 * POOL_PAD + t:, :] = jnp.zeros((tail, d), bf16)

    rr = lax.broadcasted_iota(jnp.int32, (POOL_BLOCK, 2 * POOL_BLOCK), 0)
    cc = lax.broadcasted_iota(jnp.int32, (POOL_BLOCK, 2 * POOL_BLOCK), 1)
    r = lax.broadcasted_iota(jnp.int32, (t, 1), 0)
    for gi, w in enumerate(POOL_WINDOWS):
        c0 = gi * dg
        half = w // 2
        band = ((cc >= POOL_PAD + rr - half) & (cc < POOL_PAD + rr + half)).astype(bf16)
        sums = []
        for r0 in range(0, t, POOL_BLOCK):
            sums.append(jnp.dot(band, hi_s[r0:r0 + 2 * POOL_BLOCK, c0:c0 + dg], preferred_element_type=f32)
                        + jnp.dot(band, lo_s[r0:r0 + 2 * POOL_BLOCK, c0:c0 + dg], preferred_element_type=f32))
        s = jnp.concatenate(sums, axis=0)
        lo = jnp.where(is_start, jnp.maximum(r - half, 0), r - half)
        hi = jnp.where(is_end, jnp.minimum(r + half, t), r + half)
        cnt = (hi - lo).astype(f32)
        dlt = s / cnt - h[:, c0:c0 + dg]
        y = jnp.dot(dlt.astype(bf16), w_ref[gi], preferred_element_type=f32)
        x1_ref[:, c0:c0 + dg] = x[:, c0:c0 + dg] + y * scale_ref[:, c0:c0 + dg]
    h1_ref[...] = _rms(x1_ref[...], gffn_ref[...]).astype(h1_ref.dtype)


def _stream_specs(rows, first_tile, t, d):
    hb = t // POOL_HALO
    last_halo = rows // POOL_HALO - 1
    last_tile = rows // t - 1
    return [
        pl.BlockSpec((POOL_HALO, d), lambda i: (jnp.clip((i - first_tile) * hb - 1, 0, last_halo), 0)),
        pl.BlockSpec((t, d), lambda i: (jnp.clip(i - first_tile, 0, last_tile), 0)),
        pl.BlockSpec((POOL_HALO, d), lambda i: (jnp.clip((i - first_tile + 1) * hb, 0, last_halo), 0)),
    ]


def _pool_layer(xa, xb, g_mix, w_grp, scale, g_ffn, seq):
    d = xa.shape[1]
    n = xa.shape[0] + xb.shape[0]
    t = _tile(min(seq[1], seq[2]), 256)
    row = lambda i: (0, 0)
    return pl.pallas_call(
        functools.partial(_pool_kernel, seq=seq),
        grid=(n // t,),
        in_specs=_stream_specs(xa.shape[0], 0, t, d) + _stream_specs(xb.shape[0], xa.shape[0] // t, t, d) + [
            pl.BlockSpec((1, d), row),
            pl.BlockSpec(w_grp.shape, lambda i: (0, 0, 0)),
            pl.BlockSpec((1, d), row),
            pl.BlockSpec((1, d), row),
        ],
        out_specs=[pl.BlockSpec((t, d), lambda i: (i, 0)),
                   pl.BlockSpec((t, d), lambda i: (i, 0))],
        out_shape=[jax.ShapeDtypeStruct((n, d), f32), jax.ShapeDtypeStruct((n, d), bf16)],
        scratch_shapes=[pltpu.VMEM((t + POOL_BLOCK, d), bf16)] * 2,
        compiler_params=_cparams(("parallel",)),
    )(xa, xa, xa, xb, xb, xb, g_mix.reshape(1, d), w_grp, scale.reshape(1, d), g_ffn.reshape(1, d))


def _matmul_call_with_casts(kernel, grid, in_specs, out_specs, out_shapes, args, to_cast):
    n_steps = grid[0] * grid[1]
    plans = []
    for stacked, layer, scale in to_cast:
        cols = stacked.shape[-1]
        rows = stacked[0].size // cols
        steps = 1
        while steps * 2 <= n_steps and rows % (steps * 2 * BF16_SUBLANES) == 0:
            steps *= 2
        plans.append((rows, cols, steps, rows // steps, layer))
    scaled = [k for k, (_, _, scale) in enumerate(to_cast) if scale is not None]
    n_in, n_side, n_scale, n_out = len(in_specs), len(to_cast), len(scaled), len(out_specs)

    def step_block(steps):
        return lambda i, j: jnp.minimum(i * grid[1] + j, steps - 1)

    def src_spec(cols, steps, rb, layer):
        blk = step_block(steps)
        return pl.BlockSpec((None, rb, cols), lambda i, j: (layer, blk(i, j), 0))

    def dst_spec(cols, steps, rb):
        blk = step_block(steps)
        return pl.BlockSpec((rb, cols), lambda i, j: (blk(i, j), 0))

    def body(*refs):
        first_out = n_in + n_side + n_scale
        kernel(*refs[:n_in], *refs[first_out:first_out + n_out])
        step = pl.program_id(0) * grid[1] + pl.program_id(1)
        for k, plan in enumerate(plans):
            scale_ref = refs[n_in + n_side + scaled.index(k)] if k in scaled else None

            @pl.when(step < plan[2])
            def _(src=refs[n_in + k], dst=refs[first_out + n_out + k], scale_ref=scale_ref):
                val = src[...] if scale_ref is None else src[...] * scale_ref[...]
                dst[...] = val.astype(bf16)

    outs = pl.pallas_call(
        body,
        grid=grid,
        in_specs=list(in_specs) + [src_spec(c, s, rb, layer) for _, c, s, rb, layer in plans]
        + [dst_spec(1, plans[k][2], plans[k][3]) for k in scaled],
        out_specs=list(out_specs) + [dst_spec(c, s, rb) for _, c, s, rb, _ in plans],
        out_shape=list(out_shapes) + [jax.ShapeDtypeStruct((r, c), bf16) for r, c, _, _, _ in plans],
        compiler_params=_cparams(("arbitrary", "arbitrary")),
    )(*args,
      *[stacked.reshape(stacked.shape[0], r, c) for (stacked, _, _), (r, c, _, _, _) in zip(to_cast, plans)],
      *[to_cast[k][2].reshape(-1, 1) for k in scaled])
    return outs[:n_out], [o.reshape(stacked.shape[1:]) for o, (stacked, _, _) in zip(outs[n_out:], to_cast)]


def _glu_up_kernel(a_ref, wg_ref, wu_ref, o_ref):
    a = a_ref[...]
    gate = jnp.dot(a, wg_ref[...], preferred_element_type=f32)
    up = jnp.dot(a, wu_ref[...], preferred_element_type=f32)
    o_ref[...] = (_silu(gate) * up).astype(o_ref.dtype)


def _glu_up(a, wg, wu, to_cast=()):
    m, k = a.shape
    f = wg.shape[1]
    tm = _tile(m, 1024)
    tn = _tile(f, 512)
    wspec = pl.BlockSpec((k, tn), lambda i, j: (0, j))
    (act,), rounded = _matmul_call_with_casts(
        _glu_up_kernel, (m // tm, f // tn),
        [pl.BlockSpec((tm, k), lambda i, j: (i, 0)), wspec, wspec],
        [pl.BlockSpec((tm, tn), lambda i, j: (i, j))],
        [jax.ShapeDtypeStruct((m, f), bf16)], (a, wg, wu), to_cast)
    return act, rounded


def _down_kernel(a_ref, w_ref, res_ref, o_ref, *norm_refs):
    o = res_ref[...] + jnp.dot(a_ref[...], w_ref[...], preferred_element_type=f32)
    o_ref[...] = o
    if not norm_refs:
        return
    o16_ref, sq_ref = norm_refs
    o16_ref[...] = o.astype(o16_ref.dtype)
    sq = o * o
    part = sq[:, 0:LANES]
    for c0 in range(LANES, sq.shape[1], LANES):
        part = part + sq[:, c0:c0 + LANES]

    @pl.when(pl.program_id(1) == 0)
    def _():
        sq_ref[...] = part

    @pl.when(pl.program_id(1) > 0)
    def _():
        sq_ref[...] += part


def _down_res(a, w, res, tm, norm_rows=False, to_cast=()):
    m, kk = a.shape
    n = w.shape[1]
    tm = _tile(m, tm)
    tn = _tile(n, 512)
    tile = pl.BlockSpec((tm, tn), lambda i, j: (i, j))
    out_specs, out_shapes = [tile], [jax.ShapeDtypeStruct((m, n), f32)]
    if norm_rows:
        out_specs += [tile, pl.BlockSpec((tm, LANES), lambda i, j: (i, 0))]
        out_shapes += [jax.ShapeDtypeStruct((m, n), bf16), jax.ShapeDtypeStruct((m, LANES), f32)]
    return _matmul_call_with_casts(
        _down_kernel, (m // tm, n // tn),
        [pl.BlockSpec((tm, kk), lambda i, j: (i, 0)), pl.BlockSpec((kk, tn), lambda i, j: (0, j)), tile],
        out_specs, out_shapes, (a, w, res), to_cast)


def _column_blocks(a_ref, sq_ref, w_ref):
    a = a_ref[...]
    inv = lax.rsqrt(jnp.sum(sq_ref[...], axis=1, keepdims=True) / a.shape[1] + EPS)
    tn = w_ref.shape[1]
    cb = min(tn, EPILOGUE_COLS)
    for c0 in range(0, tn, cb):
        yield c0, inv * jnp.dot(a, w_ref[:, c0:c0 + cb], preferred_element_type=f32)


def _store_head_block(o_ref, c0, val):
    for hh in range(val.shape[1] // HEAD_DIM):
        o_ref[c0 // HEAD_DIM + hh] = val[:, hh * HEAD_DIM:(hh + 1) * HEAD_DIM].astype(o_ref.dtype)


def _proj_silu_heads_kernel(a_ref, sq_ref, w_ref, o_ref):
    for c0, r in _column_blocks(a_ref, sq_ref, w_ref):
        _store_head_block(o_ref, c0, _silu(r))


def _proj_heads_kernel(a_ref, sq_ref, w_ref, o_ref):
    for c0, r in _column_blocks(a_ref, sq_ref, w_ref):
        _store_head_block(o_ref, c0, r)


def _proj_silu_kernel(a_ref, sq_ref, w_ref, o_ref):
    for c0, r in _column_blocks(a_ref, sq_ref, w_ref):
        o_ref[:, c0:c0 + r.shape[1]] = _silu(r).astype(o_ref.dtype)


def _proj_forget_kernel(a_ref, sq_ref, w_ref, lb_ref, k_ref, g_ref, *, layer):
    logits = lb_ref[...]
    ex = jnp.exp(logits - jnp.max(logits, axis=0, keepdims=True))
    sm = ex / jnp.sum(ex, axis=0, keepdims=True)
    lb_all = jnp.sum(sm[0:layer + 1], axis=0, keepdims=True) - sm[0:1]
    for c0, r in _column_blocks(a_ref, sq_ref, w_ref):
        lb = lb_all[:, c0:c0 + r.shape[1]]
        f = lb + (1.0 - lb) * jax.nn.sigmoid(r)
        _store_head_block(k_ref, c0, 1.0 - f)
        _store_head_block(g_ref, c0, jnp.log2(f))


def _proj_section(kernel, a, sq, w, section, out_heads, n_out=1, extra=None):
    m, k = a.shape
    d = k
    tm = _tile(m, 1024)
    tn = _tile(d, 1024)
    nb = d // tn
    in_specs = [pl.BlockSpec((tm, k), lambda i, j: (i, 0)),
                pl.BlockSpec((tm, LANES), lambda i, j: (i, 0)),
                pl.BlockSpec((k, tn), lambda i, j: (0, section * nb + j))]
    args = [a, sq, w]
    if extra is not None:
        in_specs.append(pl.BlockSpec((extra.shape[0], tn), lambda i, j: (0, j)))
        args.append(extra)
    if out_heads:
        hpb = tn // HEAD_DIM
        ospec = pl.BlockSpec((hpb, tm, HEAD_DIM), lambda i, j: (j, i, 0))
        oshape = jax.ShapeDtypeStruct((d // HEAD_DIM, m, HEAD_DIM), bf16)
    else:
        ospec = pl.BlockSpec((tm, tn), lambda i, j: (i, j))
        oshape = jax.ShapeDtypeStruct((m, d), bf16)
    out = pl.pallas_call(
        kernel,
        grid=(m // tm, nb),
        in_specs=in_specs,
        out_specs=[ospec] * n_out if n_out > 1 else ospec,
        out_shape=[oshape] * n_out if n_out > 1 else oshape,
        compiler_params=_cparams(("parallel", "arbitrary")),
    )(*args)
    return out


MAX_DIRECT_LOG2_DECAY = 100.0
HEAD_UNROLL = 8


def _dot_nt(a, b):
    return lax.dot_general(a, b, (((1,), (1,)), ((), ())), preferred_element_type=f32)


def _dot_tn(a, b):
    return lax.dot_general(a, b, (((0,), (0,)), ((), ())), preferred_element_type=f32)


def _mid_ref(p, level, offset):
    c, w = p.shape
    blk = 2 << level
    p3 = p.reshape(c // blk, blk, w)
    return jnp.broadcast_to(p3[:, offset:offset + 1, :], p3.shape).reshape(c, w)


def _neg_abs_exponents(gf, gb, b, x, level, rmod4):
    if level == 0:
        odd = (rmod4 & 1) == 1
        return jnp.where(odd, gf, 0.0), jnp.where(odd, 0.0, gb)
    if level == 1:
        c = gf.shape[0]
        gf_up = pltpu.roll(gf, c - 1, 0)
        gf_dn = pltpu.roll(gf, 1, 0)
        gb_up = pltpu.roll(gb, c - 1, 0)
        gb_dn = pltpu.roll(gb, 1, 0)
        ef = jnp.where(rmod4 == 0, gf_up, jnp.where(rmod4 == 1, 0.0, jnp.where(rmod4 == 2, gf, gf + gf_dn)))
        eb = jnp.where(rmod4 == 0, gb + gb_up, jnp.where(rmod4 == 1, gb, jnp.where(rmod4 == 2, 0.0, gb_dn)))
        return ef, eb
    half = 1 << level
    ef = b - _mid_ref(b, level, half - 1)
    eb = x - _mid_ref(x, level, half)
    return -jnp.abs(ef), -jnp.abs(eb)


def _chunk_prefix(tri_incl, tri_excl, gf16, gb16):
    b = jnp.dot(tri_incl, gf16, preferred_element_type=f32)
    x = jnp.dot(tri_excl, gb16, preferred_element_type=f32)
    return b, x


def _scan_bw_kernel(kb_ref, gb_ref, v_ref, s_ref, st_ref, *, seq, n_chunks):
    n_heads, c, _ = kb_ref.shape
    chunk = n_chunks - 1 - pl.program_id(0)
    _, is_end = _seq_flags(chunk * c, c, *seq)

    @pl.when(is_end)
    def _():
        st_ref[...] = jnp.zeros_like(st_ref)

    row = lax.broadcasted_iota(jnp.int32, (c, c), 0)
    col = lax.broadcasted_iota(jnp.int32, (c, c), 1)
    tri_excl = (col < row).astype(bf16)

    def head(h, carry):
        gb16 = gb_ref[h]
        x = jnp.dot(tri_excl, gb16, preferred_element_type=f32)
        tot = x[c - 1:c, :] + gb16[c - 1:c, :].astype(f32)
        st = st_ref[h]
        s_ref[h] = st.astype(s_ref.dtype)
        kx = (kb_ref[h].astype(f32) * jnp.exp2(x)).astype(bf16)
        st_ref[h] = st * jnp.exp2(tot) + _dot_tn(v_ref[h], kx)
        return carry

    lax.fori_loop(0, n_heads, head, 0, unroll=2 * HEAD_UNROLL)


def _scan_bw_states(kb, gb, v, seq, c):
    n_heads, n, dk = kb.shape
    n_chunks = n // c
    spec = pl.BlockSpec((n_heads, c, dk), lambda i: (0, n_chunks - 1 - i, 0))
    return pl.pallas_call(
        functools.partial(_scan_bw_kernel, seq=seq, n_chunks=n_chunks),
        grid=(n_chunks,),
        in_specs=[spec, spec, spec],
        out_specs=pl.BlockSpec((None, n_heads, dk, dk), lambda i: (n_chunks - 1 - i, 0, 0, 0)),
        out_shape=jax.ShapeDtypeStruct((n_chunks, n_heads, dk, dk), bf16),
        scratch_shapes=[pltpu.VMEM((n_heads, dk, dk), f32)],
        compiler_params=_cparams(("arbitrary",)),
    )(kb, gb, v)


def _scan_kernel(q_ref, kf_ref, kb_ref, gf_ref, gb_ref, v_ref, sb_ref, tril_ref, triu_ref, gate_ref,
                 gain_ref, out_ref, st_ref, o_s, qf_s, kf_s, qb_s, kb_s, qin_s, kx_s, sc_s, dec_s, tot_s,
                 *, seq):
    n_heads, c, dk = q_ref.shape
    n_levels = c.bit_length() - 1
    mid = c // 2
    is_start, _ = _seq_flags(pl.program_id(0) * c, c, *seq)

    @pl.when(is_start)
    def _():
        st_ref[...] = jnp.zeros_like(st_ref)

    row = lax.broadcasted_iota(jnp.int32, (c, c), 0)
    col = lax.broadcasted_iota(jnp.int32, (c, c), 1)
    tri_incl = (col <= row).astype(bf16)
    tri_excl = (col < row).astype(bf16)

    def finish(h, kf_dec, b_last, q_in, scores, diag):
        v = v_ref[h]
        st_f = st_ref[h]
        st_cat = jnp.concatenate([st_f.astype(bf16), sb_ref[h]], axis=1)
        o = _dot_nt(q_in, st_cat) + jnp.dot(scores.astype(bf16), v, preferred_element_type=f32)
        o_s[h] = o if diag is None else o + diag * v.astype(f32)
        st_ref[h] = st_f * jnp.exp2(b_last) + _dot_tn(v, kf_dec.astype(bf16))

    def direct_operands(h, carry):
        q = q_ref[h].astype(f32)
        gb16 = gb_ref[h]
        b, x = _chunk_prefix(tri_incl, tri_excl, gf_ref[h], gb16)
        b_mid = b[mid - 1:mid, :]
        x_mid = x[mid:mid + 1, :]
        db = b - b_mid
        dx = x - x_mid
        q_f = q * jnp.exp2(db)
        k_f = kf_ref[h].astype(f32) * jnp.exp2(-db)
        q_b = q * jnp.exp2(-dx)
        k_b = kb_ref[h].astype(f32) * jnp.exp2(dx)
        b_last = b[c - 1:c, :]
        x_tot = x[c - 1:c, :] + gb16[c - 1:c, :].astype(f32)
        qf_s[h] = q_f.astype(bf16)
        kf_s[h] = k_f.astype(bf16)
        qb_s[h] = q_b.astype(bf16)
        kb_s[h] = k_b.astype(bf16)
        qin_s[h] = jnp.concatenate([q_f * jnp.exp2(b_mid), q_b * jnp.exp2(x_tot - x_mid)], axis=1).astype(bf16)
        kx_s[h] = (k_f * jnp.exp2(b_last - b_mid)).astype(bf16)
        dec_s[h] = jnp.broadcast_to(jnp.exp2(b_last), dec_s.shape[1:])
        tot_s[h] = jnp.broadcast_to(jnp.minimum(b_last, x_tot), tot_s.shape[1:])
        return carry

    def direct_scores(h, carry):
        sc_s[h] = (jnp.where(tril_ref[...] != 0.0, _dot_nt(qf_s[h], kf_s[h]), 0.0)
                   + jnp.where(triu_ref[...] != 0.0, _dot_nt(qb_s[h], kb_s[h]), 0.0)).astype(bf16)
        return carry

    def direct_output(h, carry):
        v = v_ref[h]
        st_f = st_ref[h]
        st_cat = jnp.concatenate([st_f.astype(bf16), sb_ref[h]], axis=1)
        o_s[h] = _dot_nt(qin_s[h], st_cat) + jnp.dot(sc_s[h], v, preferred_element_type=f32)
        st_ref[h] = st_f * dec_s[h][0:1, :] + _dot_tn(v, kx_s[h])
        return carry

    def head_levelled(h, carry):
        rtok = lax.broadcasted_iota(jnp.int32, (c, dk), 0)
        rmod4 = rtok & 3
        q = q_ref[h].astype(f32)
        kf = kf_ref[h].astype(f32)
        kb = kb_ref[h].astype(f32)
        gf16 = gf_ref[h]
        gb16 = gb_ref[h]
        gf = gf16.astype(f32)
        gb = gb16.astype(f32)
        b, x = _chunk_prefix(tri_incl, tri_excl, gf16, gb16)
        scores = jnp.zeros((c, c), f32)
        for l in range(n_levels):
            ef, eb = _neg_abs_exponents(gf, gb, b, x, l, rmod4)
            af = jnp.exp2(ef)
            ab = jnp.exp2(eb)
            up = ((rtok >> l) & 1) == 1
            lhs = jnp.concatenate([jnp.where(up, q * af, 0.0), jnp.where(up, 0.0, q * ab)], axis=1)
            rhs = jnp.concatenate([jnp.where(up, 0.0, kf * af), jnp.where(up, kb * ab, 0.0)], axis=1)
            p = _dot_nt(lhs.astype(bf16), rhs.astype(bf16))
            if l < n_levels - 1:
                p = jnp.where((row >> (l + 1)) == (col >> (l + 1)), p, 0.0)
            scores = scores + p
        b_last = b[c - 1:c, :]
        x_tot = x[c - 1:c, :] + gb[c - 1:c, :]
        q_in = jnp.concatenate([q * jnp.exp2(b), q * jnp.exp2(x_tot - x)], axis=1).astype(bf16)
        diag = jnp.sum(q * (kf + kb), axis=1, keepdims=True)
        finish(h, kf * jnp.exp2(b_last - b), b_last, q_in, scores, diag)
        return carry

    lax.fori_loop(0, n_heads, direct_operands, 0, unroll=2 * HEAD_UNROLL)
    direct_ok = jnp.min(tot_s[...]) >= -MAX_DIRECT_LOG2_DECAY

    @pl.when(direct_ok)
    def _():
        lax.fori_loop(0, n_heads, direct_scores, 0, unroll=2 * HEAD_UNROLL)
        lax.fori_loop(0, n_heads, direct_output, 0, unroll=2 * HEAD_UNROLL)

    @pl.when(jnp.logical_not(direct_ok))
    def _():
        lax.fori_loop(0, n_heads, head_levelled, 0)

    sq = jnp.zeros((c, dk), f32)
    for hh in range(n_heads):
        o = o_s[hh]
        sq = sq + o * o
    inv = lax.rsqrt(jnp.sum(sq, axis=1, keepdims=True) / (n_heads * dk) + EPS)
    for hh in range(n_heads):
        cols = slice(hh * dk, (hh + 1) * dk)
        out_ref[:, cols] = (o_s[hh] * inv * gain_ref[:, cols] * gate_ref[:, cols].astype(f32)
                            ).astype(out_ref.dtype)


def _scan(q, kf, kb, gf, gb, v, s_bw, gate, gain, seq, c):
    n_heads, n, dk = q.shape
    d = n_heads * dk
    n_chunks = n // c
    spec = pl.BlockSpec((n_heads, c, dk), lambda i: (0, i, 0))
    tri = pl.BlockSpec((c, c), lambda i: (0, 0))
    tril = jnp.tril(jnp.ones((c, c), f32))
    return pl.pallas_call(
        functools.partial(_scan_kernel, seq=seq),
        grid=(n_chunks,),
        in_specs=[spec] * 6 + [pl.BlockSpec((None, n_heads, dk, dk), lambda i: (i, 0, 0, 0)), tri, tri,
                               pl.BlockSpec((c, d), lambda i: (i, 0)),
                               pl.BlockSpec((1, d), lambda i: (0, 0))],
        out_specs=pl.BlockSpec((c, d), lambda i: (i, 0)),
        out_shape=jax.ShapeDtypeStruct((n, d), bf16),
        scratch_shapes=[pltpu.VMEM((n_heads, dk, dk), f32), pltpu.VMEM((n_heads, c, dk), f32)]
        + [pltpu.VMEM((n_heads, c, dk), bf16)] * 4
        + [pltpu.VMEM((n_heads, c, 2 * dk), bf16), pltpu.VMEM((n_heads, c, dk), bf16),
           pltpu.VMEM((n_heads, c, c), bf16)] + [pltpu.VMEM((n_heads, 8, dk), f32)] * 2,
        compiler_params=_cparams(("arbitrary",)),
    )(q, kf, kb, gf, gb, v, s_bw, tril, tril.T, gate, gain.reshape(1, d))


def _pack_halves(x):
    half = x.shape[1] // 2
    bits = lax.bitcast_convert_type(x.astype(bf16).astype(f32), jnp.int32)
    return lax.shift_right_logical(bits[:, :half], 16) | (bits[:, half:] & jnp.int32(-65536))


def _unpack_halves(words):
    lo = lax.bitcast_convert_type(lax.shift_left(words, 16), f32)
    hi = lax.bitcast_convert_type(words & jnp.int32(-65536), f32)
    return lo, hi


def _router_kernel(x_ref, g_ref, r2_ref, h_ref, e_ref, rank_ref, w_ref, cnt_ref, run_ref,
                   *, n_experts):
    @pl.when(pl.program_id(0) == 0)
    def _():
        run_ref[...] = jnp.zeros_like(run_ref)

    h = _rms(x_ref[...], g_ref[...])
    h_ref[...] = _pack_halves(h)
    hi = h.astype(bf16)
    lo = (h - hi.astype(f32)).astype(bf16)
    tr = h.shape[0]
    prod = jnp.dot(jnp.concatenate([hi, lo], axis=0), r2_ref[...], preferred_element_type=f32)
    logits = prod[:tr, :LANES] + prod[:tr, LANES:] + prod[tr:, :LANES]
    lane = lax.broadcasted_iota(jnp.int32, logits.shape, 1)
    neg = jnp.float32(-jnp.inf)
    logits = jnp.where(lane < n_experts, logits, neg)
    m1 = jnp.max(logits, axis=1, keepdims=True)
    i1 = jnp.min(jnp.where(logits == m1, lane, LANES), axis=1, keepdims=True)
    rest = jnp.where(lane == i1, neg, logits)
    m2 = jnp.max(rest, axis=1, keepdims=True)
    i2 = jnp.min(jnp.where(rest == m2, lane, LANES), axis=1, keepdims=True)
    e2 = jnp.exp(m2 - m1)
    w_ref[...] = jnp.concatenate([1.0 / (1.0 + e2), e2 / (1.0 + e2)], axis=1)
    e_ref[...] = jnp.concatenate([i1, i2], axis=1)
    member = jnp.where((lane == i1) | (lane == i2), 1.0, 0.0)
    row = lax.broadcasted_iota(jnp.int32, (tr, tr), 0)
    col = lax.broadcasted_iota(jnp.int32, (tr, tr), 1)
    before = jnp.dot((col < row).astype(bf16), member.astype(bf16), preferred_element_type=f32)
    before = before + run_ref[...]
    r1 = jnp.sum(jnp.where(lane == i1, before, 0.0), axis=1, keepdims=True)
    r2 = jnp.sum(jnp.where(lane == i2, before, 0.0), axis=1, keepdims=True)
    rank_ref[...] = jnp.concatenate([r1, r2], axis=1).astype(jnp.int32)
    run = run_ref[...] + jnp.sum(member, axis=0, keepdims=True)
    run_ref[...] = run
    cnt_ref[...] = run.astype(jnp.int32)


def _router(x, g, router):
    n, d = x.shape
    e = router.shape[1]
    rpad = jnp.zeros((d, LANES), f32).at[:, :e].set(router)
    rhi = rpad.astype(bf16)
    r2 = jnp.concatenate([rhi, (rpad - rhi.astype(f32)).astype(bf16)], axis=1)
    tr = _tile(n, 512)
    pair = pl.BlockSpec((tr, TOP_K), lambda i: (i, 0))
    return pl.pallas_call(
        functools.partial(_router_kernel, n_experts=e),
        grid=(n // tr,),
        in_specs=[pl.BlockSpec((tr, d), lambda i: (i, 0)),
                  pl.BlockSpec((1, d), lambda i: (0, 0)),
                  pl.BlockSpec((d, 2 * LANES), lambda i: (0, 0))],
        out_specs=[pl.BlockSpec((tr, d // 2), lambda i: (i, 0)), pair, pair, pair,
                   pl.BlockSpec((1, LANES), lambda i: (0, 0))],
        out_shape=[jax.ShapeDtypeStruct((n, d // 2), jnp.int32),
                   jax.ShapeDtypeStruct((n, TOP_K), jnp.int32),
                   jax.ShapeDtypeStruct((n, TOP_K), jnp.int32),
                   jax.ShapeDtypeStruct((n, TOP_K), f32),
                   jax.ShapeDtypeStruct((1, LANES), jnp.int32)],
        scratch_shapes=[pltpu.VMEM((1, LANES), f32)],
        compiler_params=_cparams(("arbitrary",)),
    )(x, g.reshape(1, d), r2)


def _row_copies(pos_ref, i, src, dst, sem, gather):
    out = []
    for k in range(TOP_K):
        p = pos_ref[k, i]
        if gather:
            out.append(pltpu.make_async_copy(src.at[pl.ds(p, 1)], dst.at[k, pl.ds(i, 1)], sem))
        else:
            out.append(pltpu.make_async_copy(src.at[pl.ds(i, 1)], dst.at[pl.ds(p, 1)], sem))
    return out


def _start_rows(n_rows, pos_ref, src, dst, sem, gather):
    def start(i, carry):
        for cp in _row_copies(pos_ref, i, src, dst, sem, gather):
            cp.start()
        return carry

    lax.fori_loop(0, n_rows, start, 0, unroll=ROW_DMA_UNROLL)


def _wait_rows(n_rows, pos_ref, src, dst, sem, gather):
    def wait(i, carry):
        for cp in _row_copies(pos_ref, i, src, dst, sem, gather):
            cp.wait()
        return carry

    lax.fori_loop(0, n_rows, wait, 0, unroll=ROW_DMA_UNROLL)


def _all_rows(n_rows, pos_ref, src, dst, sem, gather):
    _start_rows(n_rows, pos_ref, src, dst, sem, gather)
    _wait_rows(n_rows, pos_ref, src, dst, sem, gather)


def _moe_dispatch_kernel(nv_ref, pos_ref, h_ref, xs_ref, zero_s, sem):
    tm = zero_s.shape[0]

    @pl.when(pl.program_id(0) == 0)
    def _():
        zero_s[...] = jnp.zeros_like(zero_s)

        def fill(r, carry):
            @pl.when(nv_ref[r] < tm)
            def _():
                cp = pltpu.make_async_copy(zero_s, xs_ref.at[pl.ds(pl.multiple_of(r * tm, tm), tm)], sem)
                cp.start()
                cp.wait()
            return carry

        lax.fori_loop(0, xs_ref.shape[0] // tm, fill, 0)

    _all_rows(h_ref.shape[0], pos_ref, h_ref, xs_ref, sem, gather=False)


def _moe_dispatch(hp, pos3, tile_rows, tm):
    n, dw = hp.shape
    n_tiles, _, tr = pos3.shape
    grid_spec = pltpu.PrefetchScalarGridSpec(
        num_scalar_prefetch=1,
        grid=(n_tiles,),
        in_specs=[pl.BlockSpec((None, TOP_K, tr), lambda i, nv: (i, 0, 0), memory_space=pltpu.SMEM),
                  pl.BlockSpec((tr, dw), lambda i, nv: (i, 0))],
        out_specs=pl.BlockSpec(memory_space=pl.ANY),
        scratch_shapes=[pltpu.VMEM((tm, dw), jnp.int32), pltpu.SemaphoreType.DMA(())],
    )
    return pl.pallas_call(
        _moe_dispatch_kernel,
        grid_spec=grid_spec,
        out_shape=jax.ShapeDtypeStruct((tile_rows.shape[0] * tm, dw), jnp.int32),
        compiler_params=_cparams(("arbitrary",)),
    )(tile_rows, pos3, hp)


def _moe_up_kernel(te_ref, nv_ref, xs_ref, wg_ref, wu_ref, a_ref, x_s):
    r = pl.program_id(0)
    half = xs_ref.shape[1]

    @pl.when(nv_ref[r] > 0)
    def _():
        @pl.when(pl.program_id(1) == 0)
        def _():
            lo, hi = _unpack_halves(xs_ref[...])
            x_s[:, :half] = lo.astype(bf16)
            x_s[:, half:] = hi.astype(bf16)

        x = x_s[...]
        gate = jnp.dot(x, wg_ref[...], preferred_element_type=f32)
        up = jnp.dot(x, wu_ref[...], preferred_element_type=f32)
        a_ref[...] = (_silu(gate) * up).astype(a_ref.dtype)

    @pl.when(nv_ref[r] == 0)
    def _():
        a_ref[...] = jnp.zeros_like(a_ref)


def _moe_down_kernel(te_ref, nv_ref, a_ref, wd_ref, y_ref):
    @pl.when(nv_ref[pl.program_id(0)] > 0)
    def _():
        y_ref[...] = _pack_halves(jnp.dot(a_ref[...], wd_ref[...], preferred_element_type=f32))

    @pl.when(nv_ref[pl.program_id(0)] == 0)
    def _():
        y_ref[...] = jnp.zeros_like(y_ref)


def _moe_group(xs, tile_expert, tile_rows, wg, wu, wd, tm):
    p, dw = xs.shape
    _, d, fe = wg.shape
    tf = _tile(fe, MOE_FF_TILE)
    wspec = pl.BlockSpec((None, d, tf), lambda r, f, te, nv: (te[r], 0, f))
    act = pl.pallas_call(
        _moe_up_kernel,
        grid_spec=pltpu.PrefetchScalarGridSpec(
            num_scalar_prefetch=2,
            grid=(p // tm, fe // tf),
            in_specs=[pl.BlockSpec((tm, dw), lambda r, f, te, nv: (r, 0)), wspec, wspec],
            out_specs=pl.BlockSpec((tm, tf), lambda r, f, te, nv: (r, f)),
            scratch_shapes=[pltpu.VMEM((tm, d), bf16)],
        ),
        out_shape=jax.ShapeDtypeStruct((p, fe), bf16),
        compiler_params=_cparams(("arbitrary", "arbitrary")),
    )(tile_expert, tile_rows, xs, wg, wu)
    return pl.pallas_call(
        _moe_down_kernel,
        grid_spec=pltpu.PrefetchScalarGridSpec(
            num_scalar_prefetch=2,
            grid=(p // tm,),
            in_specs=[pl.BlockSpec((tm, fe), lambda r, te, nv: (r, 0)),
                      pl.BlockSpec((None, fe, d), lambda r, te, nv: (te[r], 0, 0))],
            out_specs=pl.BlockSpec((tm, dw), lambda r, te, nv: (r, 0)),
        ),
        out_shape=jax.ShapeDtypeStruct((p, dw), jnp.int32),
        compiler_params=_cparams(("arbitrary",)),
    )(tile_expert, tile_rows, act, wd)


def _moe_combine_kernel(pos_ref, pos_next_ref, x_ref, w_ref, g_ref, y_ref, o_ref, ybuf, sem, *, final_norm):
    tr, d = x_ref.shape
    half = d // 2
    i = pl.program_id(0)
    slot = i % 2

    @pl.when(i == 0)
    def _():
        _start_rows(tr, pos_ref, y_ref, ybuf.at[0], sem.at[0], gather=True)

    _wait_rows(tr, pos_ref, y_ref, ybuf.at[slot], sem.at[slot], gather=True)
    for r0 in range(0, tr, SUBLANES):
        rows = slice(r0, r0 + SUBLANES)
        for ii in range(r0, r0 + SUBLANES):
            for cp in _row_copies(pos_next_ref, ii, y_ref, ybuf.at[1 - slot], sem.at[1 - slot], True):
                cp.start()
        w = w_ref[rows, :]
        lo0, hi0 = _unpack_halves(ybuf[slot, 0, rows, :])
        lo1, hi1 = _unpack_halves(ybuf[slot, 1, rows, :])
        o_lo = x_ref[rows, :half] + (w[:, 0:1] * lo0 + w[:, 1:2] * lo1)
        o_hi = x_ref[rows, half:] + (w[:, 0:1] * hi0 + w[:, 1:2] * hi1)
        if final_norm:
            ms = (jnp.sum(o_lo * o_lo, axis=1, keepdims=True)
                  + jnp.sum(o_hi * o_hi, axis=1, keepdims=True)) / d
            inv = lax.rsqrt(ms + EPS)
            o_lo = o_lo * inv * g_ref[:, :half]
            o_hi = o_hi * inv * g_ref[:, half:]
        o_ref[rows, :half] = o_lo
        o_ref[rows, half:] = o_hi

    @pl.when(i == pl.num_programs(0) - 1)
    def _():
        _wait_rows(tr, pos_next_ref, y_ref, ybuf.at[1 - slot], sem.at[1 - slot], gather=True)


def _moe_combine(x, w, y, pos3, gain, final_norm, row_start, n_rows):
    d = x.shape[1]
    tr = pos3.shape[2]
    off = row_start // tr
    last = off + n_rows // tr - 1
    return pl.pallas_call(
        functools.partial(_moe_combine_kernel, final_norm=final_norm),
        grid=(n_rows // tr,),
        in_specs=[pl.BlockSpec((None, TOP_K, tr), lambda i: (i + off, 0, 0), memory_space=pltpu.SMEM),
                  pl.BlockSpec((None, TOP_K, tr), lambda i: (jnp.minimum(i + off + 1, last), 0, 0),
                               memory_space=pltpu.SMEM),
                  pl.BlockSpec((tr, d), lambda i: (i + off, 0)),
                  pl.BlockSpec((tr, TOP_K), lambda i: (i + off, 0)),
                  pl.BlockSpec((1, d), lambda i: (0, 0)),
                  pl.BlockSpec(memory_space=pl.ANY)],
        out_specs=pl.BlockSpec((tr, d), lambda i: (i, 0)),
        out_shape=jax.ShapeDtypeStruct((n_rows, d), f32),
        scratch_shapes=[pltpu.VMEM((2, TOP_K, tr, d // 2), jnp.int32), pltpu.SemaphoreType.DMA((2,))],
        compiler_params=_cparams(("arbitrary",)),
    )(pos3, pos3, x, w, gain.reshape(1, d), y)


def _moe_layer(x, g, router, wg, wu, wd, final_gain, parts):
    n, d = x.shape
    n_exp = router.shape[1]
    tm = MOE_ROW_TILE
    hp, eid, rank, w, cnt = _router(x, g, router)
    cnt = cnt[0, :n_exp]
    padded = (cnt + tm - 1) // tm * tm
    seg_end = jnp.cumsum(padded)
    seg_start = seg_end - padded
    pos = seg_start[eid] + rank
    n_rows_padded = TOP_K * n + n_exp * tm
    tile0 = jnp.arange(n_rows_padded // tm, dtype=jnp.int32) * tm
    tile_expert = jnp.minimum(jnp.searchsorted(seg_end, tile0, side="right"), n_exp - 1).astype(jnp.int32)
    tile_rows = jnp.clip(cnt[tile_expert] - (tile0 - seg_start[tile_expert]), 0, tm).astype(jnp.int32)
    def by_tile(tr):
        return pos.reshape(n // tr, tr, TOP_K).transpose(0, 2, 1)

    pos3 = by_tile(_tile(n, MOE_COMBINE_TILE))
    xs = _moe_dispatch(hp, by_tile(_tile(n, MOE_DISPATCH_TILE)), tile_rows, tm)
    y = _moe_group(xs, tile_expert, tile_rows, wg, wu, wd, tm)
    if final_gain is None:
        return _moe_combine(x, w, y, pos3, g, False, 0, n)
    return [_moe_combine(x, w, y, pos3, final_gain, True, start, rows) for start, rows in parts]


def kernel(x_prompt, x_sample, norm_mix, norm_ffn, norm_final, pool_w, pool_scale, hgrn_w_in, hgrn_lb,
           hgrn_norm, hgrn_w_out, ffn_w_gate, ffn_w_up, ffn_w_down, moe_router, moe_w_gate, moe_w_up,
           moe_w_down):
    bp, sp, d = x_prompt.shape
    bs, ss, _ = x_sample.shape
    n_prompt = bp * sp
    n = n_prompt + bs * ss
    seq = (n_prompt, sp, ss)
    depth = norm_mix.shape[0]
    chunk = _tile(min(sp, ss), 128)

    parts = ((0, n_prompt), (n_prompt, bs * ss))
    streams = (x_prompt.reshape(n_prompt, d), x_sample.reshape(bs * ss, d))
    x = None if depth else jnp.concatenate(streams, axis=0)
    outs = None

    for i in range(depth):
        j = i // 2
        if i % 2 == 0:
            xa, xb = streams if x is None else (x[:n_prompt], x[n_prompt:])
            x, h = _pool_layer(xa, xb, norm_mix[i], pool_w[j].astype(bf16), pool_scale[j], norm_ffn[i], seq)
            nxt = i + 1 < depth
            later = [(hgrn_w_in, j, norm_mix[i + 1])] + [
                (w, j, None) for w in (hgrn_w_out, moe_w_gate, moe_w_up, moe_w_down)] if nxt else []
            act, (w_down, *cast) = _glu_up(h, ffn_w_gate[j].astype(bf16), ffn_w_up[j].astype(bf16),
                                           [(ffn_w_down, j, None)] + later)
            (x, *norm_rows), _ = _down_res(act, w_down, x, 512, nxt)
            rounded = dict(zip(("w_in", "w_out", "moe_gate", "moe_up", "moe_down"), cast))
        else:
            x16, sq = norm_rows
            w_in = rounded["w_in"]
            q = _proj_section(_proj_silu_heads_kernel, x16, sq, w_in, 0, True)
            forget = functools.partial(_proj_forget_kernel, layer=i)
            kf, gf = _proj_section(forget, x16, sq, w_in, 1, True, 2, hgrn_lb)
            kb, gb = _proj_section(forget, x16, sq, w_in, 2, True, 2, hgrn_lb)
            v = _proj_section(_proj_heads_kernel, x16, sq, w_in, 3, True)
            gate = _proj_section(_proj_silu_kernel, x16, sq, w_in, 4, False)
            s_bw = _scan_bw_states(kb, gb, v, seq, chunk)
            og = _scan(q, kf, kb, gf, gb, v, s_bw, gate, hgrn_norm[j], seq, chunk)
            (x,), _ = _down_res(og, rounded["w_out"], x, 1024)
            last = i == depth - 1
            res = _moe_layer(x, norm_ffn[i], moe_router[j], rounded["moe_gate"], rounded["moe_up"],
                             rounded["moe_down"], norm_final if last else None, parts)
            x, outs = (None, res) if last else (res, None)

    if outs is None:
        outs = [_rmsnorm(x, norm_final, f32, start, rows) for start, rows in parts]
    return (outs[0].reshape(bp, sp, d), outs[1].reshape(bs, ss, d))
```

```python
import functools

import jax
import jax.numpy as jnp
from jax import lax
from jax.experimental import pallas as pl
from jax.experimental.pallas import tpu as pltpu

EPS = 1e-6
POOL_WINDOWS = (2, 4, 8, 16)
POOL_HALO = 8
POOL_PAD = 16
POOL_BLOCK = 128
HEAD_DIM = 128
TOP_K = 2
V7X_VMEM_LIMIT = 56 * 1024 * 1024
LANES = 128
SUBLANES = 8
MOE_ROW_TILE = 512
MOE_FF_TILE = 512
MOE_DISPATCH_TILE = 512
MOE_COMBINE_TILE = 256
ROW_DMA_UNROLL = 8
EPILOGUE_COLS = 256
BF16_SUBLANES = 16

bf16 = jnp.bfloat16
f32 = jnp.float32


def _cparams(sem):
    return pltpu.CompilerParams(dimension_semantics=sem, vmem_limit_bytes=V7X_VMEM_LIMIT)


def _tile(n, pref):
    t = min(n, pref)
    while n % t:
        t //= 2
    return t


def _rms(x, g):
    ms = jnp.mean(x * x, axis=-1, keepdims=True)
    return x * lax.rsqrt(ms + EPS) * g


def _silu(x):
    return x * jax.nn.sigmoid(x)


def _seq_flags(tok, n_tok, n_prompt, seq_p, seq_s):
    in_p = tok < n_prompt
    rel = jnp.where(in_p, tok, tok - n_prompt)
    slen = jnp.where(in_p, seq_p, seq_s)
    is_start = (rel % slen) == 0
    is_end = ((rel + n_tok) % slen) == 0
    return is_start, is_end


def _rmsnorm_kernel(x_ref, g_ref, o_ref):
    o_ref[...] = _rms(x_ref[...], g_ref[...]).astype(o_ref.dtype)


def _rmsnorm(x, g, out_dtype, row_start=0, n_rows=None):
    n, d = x.shape
    n_rows = n if n_rows is None else n_rows
    tr = _tile(n_rows, 512)
    off = row_start // tr
    return pl.pallas_call(
        _rmsnorm_kernel,
        grid=(n_rows // tr,),
        in_specs=[pl.BlockSpec((tr, d), lambda i: (i + off, 0)),
                  pl.BlockSpec((1, d), lambda i: (0, 0))],
        out_specs=pl.BlockSpec((tr, d), lambda i: (i, 0)),
        out_shape=jax.ShapeDtypeStruct((n_rows, d), out_dtype),
        compiler_params=_cparams(("parallel",)),
    )(x, g.reshape(1, d))


def _pool_kernel(ap_ref, a_ref, an_ref, bp_ref, b_ref, bn_ref, gmix_ref, w_ref, scale_ref, gffn_ref,
                 x1_ref, h1_ref, hi_s, lo_s, *, seq):
    t, d = a_ref.shape
    dg = d // len(POOL_WINDOWS)
    tok = pl.program_id(0) * t
    is_start, is_end = _seq_flags(tok, t, *seq)
    in_a = tok < seq[0]
    g = gmix_ref[...]
    x = jnp.where(in_a, a_ref[...], b_ref[...])
    h = _rms(x, g)
    h_prev = jnp.where(is_start, 0.0, _rms(jnp.where(in_a, ap_ref[...], bp_ref[...]), g))
    h_next = jnp.where(is_end, 0.0, _rms(jnp.where(in_a, an_ref[...], bn_ref[...]), g))
    zeros = jnp.zeros_like(h_prev)

    def put(r0, val):
        hi = val.astype(bf16)
        hi_s[r0:r0 + val.shape[0], :] = hi
        lo_s[r0:r0 + val.shape[0], :] = (val - hi.astype(f32)).astype(bf16)

    put(0, jnp.concatenate([zeros, h_prev], axis=0))
    put(POOL_PAD, h)
    put(POOL_PAD + t, jnp.concatenate([h_next, zeros], axis=0))
    tail = hi_s.shape[0] - (2 * POOL_PAD + t)
    hi_s[2 * POOL_PAD + t:, :] = jnp.zeros((tail, d), bf16)
    lo_s[2 * POOL_PAD + t:, :] = jnp.zeros((tail, d), bf16)

    rr = lax.broadcasted_iota(jnp.int32, (POOL_BLOCK, 2 * POOL_BLOCK), 0)
    cc = lax.broadcasted_iota(jnp.int32, (POOL_BLOCK, 2 * POOL_BLOCK), 1)
    r = lax.broadcasted_iota(jnp.int32, (t, 1), 0)
    for gi, w in enumerate(POOL_WINDOWS):
        c0 = gi * dg
        half = w // 2
        band = ((cc >= POOL_PAD + rr - half) & (cc < POOL_PAD + rr + half)).astype(bf16)
        sums = []
        for r0 in range(0, t, POOL_BLOCK):
            sums.append(jnp.dot(band, hi_s[r0:r0 + 2 * POOL_BLOCK, c0:c0 + dg], preferred_element_type=f32)
                        + jnp.dot(band, lo_s[r0:r0 + 2 * POOL_BLOCK, c0:c0 + dg], preferred_element_type=f32))
        s = jnp.concatenate(sums, axis=0)
        lo = jnp.where(is_start, jnp.maximum(r - half, 0), r - half)
        hi = jnp.where(is_end, jnp.minimum(r + half, t), r + half)
        cnt = (hi - lo).astype(f32)
        dlt = s / cnt - h[:, c0:c0 + dg]
        y = jnp.dot(dlt.astype(bf16), w_ref[gi], preferred_element_type=f32)
        x1_ref[:, c0:c0 + dg] = x[:, c0:c0 + dg] + y * scale_ref[:, c0:c0 + dg]
    h1_ref[...] = _rms(x1_ref[...], gffn_ref[...]).astype(h1_ref.dtype)


def _stream_specs(rows, first_tile, t, d):
    hb = t // POOL_HALO
    last_halo = rows // POOL_HALO - 1
    last_tile = rows // t - 1
    return [
        pl.BlockSpec((POOL_HALO, d), lambda i: (jnp.clip((i - first_tile) * hb - 1, 0, last_halo), 0)),
        pl.BlockSpec((t, d), lambda i: (jnp.clip(i - first_tile, 0, last_tile), 0)),
        pl.BlockSpec((POOL_HALO, d), lambda i: (jnp.clip((i - first_tile + 1) * hb, 0, last_halo), 0)),
    ]


def _pool_layer(xa, xb, g_mix, w_grp, scale, g_ffn, seq):
    d = xa.shape[1]
    n = xa.shape[0] + xb.shape[0]
    t = _tile(min(seq[1], seq[2]), 256)
    row = lambda i: (0, 0)
    return pl.pallas_call(
        functools.partial(_pool_kernel, seq=seq),
        grid=(n // t,),
        in_specs=_stream_specs(xa.shape[0], 0, t, d) + _stream_specs(xb.shape[0], xa.shape[0] // t, t, d) + [
            pl.BlockSpec((1, d), row),
            pl.BlockSpec(w_grp.shape, lambda i: (0, 0, 0)),
            pl.BlockSpec((1, d), row),
            pl.BlockSpec((1, d), row),
        ],
        out_specs=[pl.BlockSpec((t, d), lambda i: (i, 0)),
                   pl.BlockSpec((t, d), lambda i: (i, 0))],
        out_shape=[jax.ShapeDtypeStruct((n, d), f32), jax.ShapeDtypeStruct((n, d), bf16)],
        scratch_shapes=[pltpu.VMEM((t + POOL_BLOCK, d), bf16)] * 2,
        compiler_params=_cparams(("parallel",)),
    )(xa, xa, xa, xb, xb, xb, g_mix.reshape(1, d), w_grp, scale.reshape(1, d), g_ffn.reshape(1, d))


def _matmul_call_with_casts(kernel, grid, in_specs, out_specs, out_shapes, args, to_cast):
    n_steps = grid[0] * grid[1]
    plans = []
    for stacked, layer, scale in to_cast:
        cols = stacked.shape[-1]
        rows = stacked[0].size // cols
        steps = 1
        while steps * 2 <= n_steps and rows % (steps * 2 * BF16_SUBLANES) == 0:
            steps *= 2
        plans.append((rows, cols, steps, rows // steps, layer))
    scaled = [k for k, (_, _, scale) in enumerate(to_cast) if scale is not None]
    n_in, n_side, n_scale, n_out = len(in_specs), len(to_cast), len(scaled), len(out_specs)

    def step_block(steps):
        return lambda i, j: jnp.minimum(i * grid[1] + j, steps - 1)

    def src_spec(cols, steps, rb, layer):
        blk = step_block(steps)
        return pl.BlockSpec((None, rb, cols), lambda i, j: (layer, blk(i, j), 0))

    def dst_spec(cols, steps, rb):
        blk = step_block(steps)
        return pl.BlockSpec((rb, cols), lambda i, j: (blk(i, j), 0))

    def body(*refs):
        first_out = n_in + n_side + n_scale
        kernel(*refs[:n_in], *refs[first_out:first_out + n_out])
        step = pl.program_id(0) * grid[1] + pl.program_id(1)
        for k, plan in enumerate(plans):
            scale_ref = refs[n_in + n_side + scaled.index(k)] if k in scaled else None

            @pl.when(step < plan[2])
            def _(src=refs[n_in + k], dst=refs[first_out + n_out + k], scale_ref=scale_ref):
                val = src[...] if scale_ref is None else src[...] * scale_ref[...]
                dst[...] = val.astype(bf16)

    outs = pl.pallas_call(
        body,
        grid=grid,
        in_specs=list(in_specs) + [src_spec(c, s, rb, layer) for _, c, s, rb, layer in plans]
        + [dst_spec(1, plans[k][2], plans[k][3]) for k in scaled],
        out_specs=list(out_specs) + [dst_spec(c, s, rb) for _, c, s, rb, _ in plans],
        out_shape=list(out_shapes) + [jax.ShapeDtypeStruct((r, c), bf16) for r, c, _, _, _ in plans],
        compiler_params=_cparams(("arbitrary", "arbitrary")),
    )(*args,
      *[stacked.reshape(stacked.shape[0], r, c) for (stacked, _, _), (r, c, _, _, _) in zip(to_cast, plans)],
      *[to_cast[k][2].reshape(-1, 1) for k in scaled])
    return outs[:n_out], [o.reshape(stacked.shape[1:]) for o, (stacked, _, _) in zip(outs[n_out:], to_cast)]


def _glu_up_kernel(a_ref, wg_ref, wu_ref, o_ref):
    a = a_ref[...]
    gate = jnp.dot(a, wg_ref[...], preferred_element_type=f32)
    up = jnp.dot(a, wu_ref[...], preferred_element_type=f32)
    o_ref[...] = (_silu(gate) * up).astype(o_ref.dtype)


def _glu_up(a, wg, wu, to_cast=()):
    m, k = a.shape
    f = wg.shape[1]
    tm = _tile(m, 1024)
    tn = _tile(f, 512)
    wspec = pl.BlockSpec((k, tn), lambda i, j: (0, j))
    (act,), rounded = _matmul_call_with_casts(
        _glu_up_kernel, (m // tm, f // tn),
        [pl.BlockSpec((tm, k), lambda i, j: (i, 0)), wspec, wspec],
        [pl.BlockSpec((tm, tn), lambda i, j: (i, j))],
        [jax.ShapeDtypeStruct((m, f), bf16)], (a, wg, wu), to_cast)
    return act, rounded


def _down_kernel(a_ref, w_ref, res_ref, o_ref, *norm_refs):
    o = res_ref[...] + jnp.dot(a_ref[...], w_ref[...], preferred_element_type=f32)
    o_ref[...] = o
    if not norm_refs:
        return
    o16_ref, sq_ref = norm_refs
    o16_ref[...] = o.astype(o16_ref.dtype)
    sq = o * o
    part = sq[:, 0:LANES]
    for c0 in range(LANES, sq.shape[1], LANES):
        part = part + sq[:, c0:c0 + LANES]

    @pl.when(pl.program_id(1) == 0)
    def _():
        sq_ref[...] = part

    @pl.when(pl.program_id(1) > 0)
    def _():
        sq_ref[...] += part


def _down_res(a, w, res, tm, tn, norm_rows=False, to_cast=()):
    m, kk = a.shape
    n = w.shape[1]
    tm = _tile(m, tm)
    tn = _tile(n, tn)
    tile = pl.BlockSpec((tm, tn), lambda i, j: (i, j))
    out_specs, out_shapes = [tile], [jax.ShapeDtypeStruct((m, n), f32)]
    if norm_rows:
        out_specs += [tile, pl.BlockSpec((tm, LANES), lambda i, j: (i, 0))]
        out_shapes += [jax.ShapeDtypeStruct((m, n), bf16), jax.ShapeDtypeStruct((m, LANES), f32)]
    return _matmul_call_with_casts(
        _down_kernel, (m // tm, n // tn),
        [pl.BlockSpec((tm, kk), lambda i, j: (i, 0)), pl.BlockSpec((kk, tn), lambda i, j: (0, j)), tile],
        out_specs, out_shapes, (a, w, res), to_cast)


def _column_blocks(a_ref, sq_ref, w_ref):
    a = a_ref[...]
    inv = lax.rsqrt(jnp.sum(sq_ref[...], axis=1, keepdims=True) / a.shape[1] + EPS)
    tn = w_ref.shape[1]
    cb = min(tn, EPILOGUE_COLS)
    for c0 in range(0, tn, cb):
        yield c0, inv * jnp.dot(a, w_ref[:, c0:c0 + cb], preferred_element_type=f32)


def _store_head_block(o_ref, c0, val):
    for hh in range(val.shape[1] // HEAD_DIM):
        o_ref[c0 // HEAD_DIM + hh] = val[:, hh * HEAD_DIM:(hh + 1) * HEAD_DIM].astype(o_ref.dtype)


def _proj_silu_heads_kernel(a_ref, sq_ref, w_ref, o_ref):
    for c0, r in _column_blocks(a_ref, sq_ref, w_ref):
        _store_head_block(o_ref, c0, _silu(r))


def _proj_heads_kernel(a_ref, sq_ref, w_ref, o_ref):
    for c0, r in _column_blocks(a_ref, sq_ref, w_ref):
        _store_head_block(o_ref, c0, r)


def _proj_silu_kernel(a_ref, sq_ref, w_ref, o_ref):
    for c0, r in _column_blocks(a_ref, sq_ref, w_ref):
        o_ref[:, c0:c0 + r.shape[1]] = _silu(r).astype(o_ref.dtype)


def _proj_forget_kernel(a_ref, sq_ref, w_ref, lb_ref, k_ref, g_ref, *, layer):
    logits = lb_ref[...]
    ex = jnp.exp(logits - jnp.max(logits, axis=0, keepdims=True))
    sm = ex / jnp.sum(ex, axis=0, keepdims=True)
    lb_all = jnp.sum(sm[0:layer + 1], axis=0, keepdims=True) - sm[0:1]
    for c0, r in _column_blocks(a_ref, sq_ref, w_ref):
        lb = lb_all[:, c0:c0 + r.shape[1]]
        f = lb + (1.0 - lb) * jax.nn.sigmoid(r)
        _store_head_block(k_ref, c0, 1.0 - f)
        _store_head_block(g_ref, c0, jnp.log2(f))


def _proj_section(kernel, a, sq, w, section, out_heads, n_out=1, extra=None):
    m, k = a.shape
    d = k
    tm = _tile(m, 1024)
    tn = _tile(d, 1024)
    nb = d // tn
    in_specs = [pl.BlockSpec((tm, k), lambda i, j: (i, 0)),
                pl.BlockSpec((tm, LANES), lambda i, j: (i, 0)),
                pl.BlockSpec((k, tn), lambda i, j: (0, section * nb + j))]
    args = [a, sq, w]
    if extra is not None:
        in_specs.append(pl.BlockSpec((extra.shape[0], tn), lambda i, j: (0, j)))
        args.append(extra)
    if out_heads:
        hpb = tn // HEAD_DIM
        ospec = pl.BlockSpec((hpb, tm, HEAD_DIM), lambda i, j: (j, i, 0))
        oshape = jax.ShapeDtypeStruct((d // HEAD_DIM, m, HEAD_DIM), bf16)
    else:
        ospec = pl.BlockSpec((tm, tn), lambda i, j: (i, j))
        oshape = jax.ShapeDtypeStruct((m, d), bf16)
    out = pl.pallas_call(
        kernel,
        grid=(m // tm, nb),
        in_specs=in_specs,
        out_specs=[ospec] * n_out if n_out > 1 else ospec,
        out_shape=[oshape] * n_out if n_out > 1 else oshape,
        compiler_params=_cparams(("parallel", "arbitrary")),
    )(*args)
    return out


MAX_DIRECT_LOG2_DECAY = 100.0
HEAD_UNROLL = 8


def _dot_nt(a, b):
    return lax.dot_general(a, b, (((1,), (1,)), ((), ())), preferred_element_type=f32)


def _dot_tn(a, b):
    return lax.dot_general(a, b, (((0,), (0,)), ((), ())), preferred_element_type=f32)


def _mid_ref(p, level, offset):
    c, w = p.shape
    blk = 2 << level
    p3 = p.reshape(c // blk, blk, w)
    return jnp.broadcast_to(p3[:, offset:offset + 1, :], p3.shape).reshape(c, w)


def _neg_abs_exponents(gf, gb, b, x, level, rmod4):
    if level == 0:
        odd = (rmod4 & 1) == 1
        return jnp.where(odd, gf, 0.0), jnp.where(odd, 0.0, gb)
    if level == 1:
        c = gf.shape[0]
        gf_up = pltpu.roll(gf, c - 1, 0)
        gf_dn = pltpu.roll(gf, 1, 0)
        gb_up = pltpu.roll(gb, c - 1, 0)
        gb_dn = pltpu.roll(gb, 1, 0)
        ef = jnp.where(rmod4 == 0, gf_up, jnp.where(rmod4 == 1, 0.0, jnp.where(rmod4 == 2, gf, gf + gf_dn)))
        eb = jnp.where(rmod4 == 0, gb + gb_up, jnp.where(rmod4 == 1, gb, jnp.where(rmod4 == 2, 0.0, gb_dn)))
        return ef, eb
    half = 1 << level
    ef = b - _mid_ref(b, level, half - 1)
    eb = x - _mid_ref(x, level, half)
    return -jnp.abs(ef), -jnp.abs(eb)


def _chunk_prefix(tri_incl, tri_excl, gf16, gb16):
    b = jnp.dot(tri_incl, gf16, preferred_element_type=f32)
    x = jnp.dot(tri_excl, gb16, preferred_element_type=f32)
    return b, x


def _scan_bw_kernel(kb_ref, gb_ref, v_ref, s_ref, st_ref, *, seq, n_chunks):
    n_heads, c, _ = kb_ref.shape
    chunk = n_chunks - 1 - pl.program_id(0)
    _, is_end = _seq_flags(chunk * c, c, *seq)

    @pl.when(is_end)
    def _():
        st_ref[...] = jnp.zeros_like(st_ref)

    row = lax.broadcasted_iota(jnp.int32, (c, c), 0)
    col = lax.broadcasted_iota(jnp.int32, (c, c), 1)
    tri_excl = (col < row).astype(bf16)

    def head(h, carry):
        gb16 = gb_ref[h]
        x = jnp.dot(tri_excl, gb16, preferred_element_type=f32)
        tot = x[c - 1:c, :] + gb16[c - 1:c, :].astype(f32)
        st = st_ref[h]
        s_ref[h] = st.astype(s_ref.dtype)
        kx = (kb_ref[h].astype(f32) * jnp.exp2(x)).astype(bf16)
        st_ref[h] = st * jnp.exp2(tot) + _dot_tn(v_ref[h], kx)
        return carry

    lax.fori_loop(0, n_heads, head, 0, unroll=2 * HEAD_UNROLL)


def _scan_bw_states(kb, gb, v, seq, c):
    n_heads, n, dk = kb.shape
    n_chunks = n // c
    spec = pl.BlockSpec((n_heads, c, dk), lambda i: (0, n_chunks - 1 - i, 0))
    return pl.pallas_call(
        functools.partial(_scan_bw_kernel, seq=seq, n_chunks=n_chunks),
        grid=(n_chunks,),
        in_specs=[spec, spec, spec],
        out_specs=pl.BlockSpec((None, n_heads, dk, dk), lambda i: (n_chunks - 1 - i, 0, 0, 0)),
        out_shape=jax.ShapeDtypeStruct((n_chunks, n_heads, dk, dk), bf16),
        scratch_shapes=[pltpu.VMEM((n_heads, dk, dk), f32)],
        compiler_params=_cparams(("arbitrary",)),
    )(kb, gb, v)


def _scan_kernel(q_ref, kf_ref, kb_ref, gf_ref, gb_ref, v_ref, sb_ref, tril_ref, triu_ref, gate_ref,
                 gain_ref, out_ref, st_ref, o_s, qf_s, kf_s, qb_s, kb_s, qin_s, kx_s, sc_s, dec_s, tot_s,
                 *, seq):
    n_heads, c, dk = q_ref.shape
    n_levels = c.bit_length() - 1
    mid = c // 2
    is_start, _ = _seq_flags(pl.program_id(0) * c, c, *seq)

    @pl.when(is_start)
    def _():
        st_ref[...] = jnp.zeros_like(st_ref)

    row = lax.broadcasted_iota(jnp.int32, (c, c), 0)
    col = lax.broadcasted_iota(jnp.int32, (c, c), 1)
    tri_incl = (col <= row).astype(bf16)
    tri_excl = (col < row).astype(bf16)

    def finish(h, kf_dec, b_last, q_in, scores, diag):
        v = v_ref[h]
        st_f = st_ref[h]
        st_cat = jnp.concatenate([st_f.astype(bf16), sb_ref[h]], axis=1)
        o = _dot_nt(q_in, st_cat) + jnp.dot(scores.astype(bf16), v, preferred_element_type=f32)
        o_s[h] = o if diag is None else o + diag * v.astype(f32)
        st_ref[h] = st_f * jnp.exp2(b_last) + _dot_tn(v, kf_dec.astype(bf16))

    def direct_operands(h, carry):
        q = q_ref[h].astype(f32)
        gb16 = gb_ref[h]
        b, x = _chunk_prefix(tri_incl, tri_excl, gf_ref[h], gb16)
        b_mid = b[mid - 1:mid, :]
        x_mid = x[mid:mid + 1, :]
        db = b - b_mid
        dx = x - x_mid
        q_f = q * jnp.exp2(db)
        k_f = kf_ref[h].astype(f32) * jnp.exp2(-db)
        q_b = q * jnp.exp2(-dx)
        k_b = kb_ref[h].astype(f32) * jnp.exp2(dx)
        b_last = b[c - 1:c, :]
        x_tot = x[c - 1:c, :] + gb16[c - 1:c, :].astype(f32)
        qf_s[h] = q_f.astype(bf16)
        kf_s[h] = k_f.astype(bf16)
        qb_s[h] = q_b.astype(bf16)
        kb_s[h] = k_b.astype(bf16)
        qin_s[h] = jnp.concatenate([q_f * jnp.exp2(b_mid), q_b * jnp.exp2(x_tot - x_mid)], axis=1).astype(bf16)
        kx_s[h] = (k_f * jnp.exp2(b_last - b_mid)).astype(bf16)
        dec_s[h] = jnp.broadcast_to(jnp.exp2(b_last), dec_s.shape[1:])
        tot_s[h] = jnp.broadcast_to(jnp.minimum(b_last, x_tot), tot_s.shape[1:])
        return carry

    def direct_scores(h, carry):
        sc_s[h] = (jnp.where(tril_ref[...] != 0.0, _dot_nt(qf_s[h], kf_s[h]), 0.0)
                   + jnp.where(triu_ref[...] != 0.0, _dot_nt(qb_s[h], kb_s[h]), 0.0)).astype(bf16)
        return carry

    def direct_output(h, carry):
        v = v_ref[h]
        st_f = st_ref[h]
        st_cat = jnp.concatenate([st_f.astype(bf16), sb_ref[h]], axis=1)
        o_s[h] = _dot_nt(qin_s[h], st_cat) + jnp.dot(sc_s[h], v, preferred_element_type=f32)
        st_ref[h] = st_f * dec_s[h][0:1, :] + _dot_tn(v, kx_s[h])
        return carry

    def head_levelled(h, carry):
        rtok = lax.broadcasted_iota(jnp.int32, (c, dk), 0)
        rmod4 = rtok & 3
        q = q_ref[h].astype(f32)
        kf = kf_ref[h].astype(f32)
        kb = kb_ref[h].astype(f32)
        gf16 = gf_ref[h]
        gb16 = gb_ref[h]
        gf = gf16.astype(f32)
        gb = gb16.astype(f32)
        b, x = _chunk_prefix(tri_incl, tri_excl, gf16, gb16)
        scores = jnp.zeros((c, c), f32)
        for l in range(n_levels):
            ef, eb = _neg_abs_exponents(gf, gb, b, x, l, rmod4)
            af = jnp.exp2(ef)
            ab = jnp.exp2(eb)
            up = ((rtok >> l) & 1) == 1
            lhs = jnp.concatenate([jnp.where(up, q * af, 0.0), jnp.where(up, 0.0, q * ab)], axis=1)
            rhs = jnp.concatenate([jnp.where(up, 0.0, kf * af), jnp.where(up, kb * ab, 0.0)], axis=1)
            p = _dot_nt(lhs.astype(bf16), rhs.astype(bf16))
            if l < n_levels - 1:
                p = jnp.where((row >> (l + 1)) == (col >> (l + 1)), p, 0.0)
            scores = scores + p
        b_last = b[c - 1:c, :]
        x_tot = x[c - 1:c, :] + gb[c - 1:c, :]
        q_in = jnp.concatenate([q * jnp.exp2(b), q * jnp.exp2(x_tot - x)], axis=1).astype(bf16)
        diag = jnp.sum(q * (kf + kb), axis=1, keepdims=True)
        finish(h, kf * jnp.exp2(b_last - b), b_last, q_in, scores, diag)
        return carry

    lax.fori_loop(0, n_heads, direct_operands, 0, unroll=2 * HEAD_UNROLL)
    direct_ok = jnp.min(tot_s[...]) >= -MAX_DIRECT_LOG2_DECAY

    @pl.when(direct_ok)
    def _():
        lax.fori_loop(0, n_heads, direct_scores, 0, unroll=2 * HEAD_UNROLL)
        lax.fori_loop(0, n_heads, direct_output, 0, unroll=2 * HEAD_UNROLL)

    @pl.when(jnp.logical_not(direct_ok))
    def _():
        lax.fori_loop(0, n_heads, head_levelled, 0)

    sq = jnp.zeros((c, dk), f32)
    for hh in range(n_heads):
        o = o_s[hh]
        sq = sq + o * o
    inv = lax.rsqrt(jnp.sum(sq, axis=1, keepdims=True) / (n_heads * dk) + EPS)
    for hh in range(n_heads):
        cols = slice(hh * dk, (hh + 1) * dk)
        out_ref[:, cols] = (o_s[hh] * inv * gain_ref[:, cols] * gate_ref[:, cols].astype(f32)
                            ).astype(out_ref.dtype)


def _scan(q, kf, kb, gf, gb, v, s_bw, gate, gain, seq, c):
    n_heads, n, dk = q.shape
    d = n_heads * dk
    n_chunks = n // c
    spec = pl.BlockSpec((n_heads, c, dk), lambda i: (0, i, 0))
    tri = pl.BlockSpec((c, c), lambda i: (0, 0))
    tril = jnp.tril(jnp.ones((c, c), f32))
    return pl.pallas_call(
        functools.partial(_scan_kernel, seq=seq),
        grid=(n_chunks,),
        in_specs=[spec] * 6 + [pl.BlockSpec((None, n_heads, dk, dk), lambda i: (i, 0, 0, 0)), tri, tri,
                               pl.BlockSpec((c, d), lambda i: (i, 0)),
                               pl.BlockSpec((1, d), lambda i: (0, 0))],
        out_specs=pl.BlockSpec((c, d), lambda i: (i, 0)),
        out_shape=jax.ShapeDtypeStruct((n, d), bf16),
        scratch_shapes=[pltpu.VMEM((n_heads, dk, dk), f32), pltpu.VMEM((n_heads, c, dk), f32)]
        + [pltpu.VMEM((n_heads, c, dk), bf16)] * 4
        + [pltpu.VMEM((n_heads, c, 2 * dk), bf16), pltpu.VMEM((n_heads, c, dk), bf16),
           pltpu.VMEM((n_heads, c, c), bf16)] + [pltpu.VMEM((n_heads, 8, dk), f32)] * 2,
        compiler_params=_cparams(("arbitrary",)),
    )(q, kf, kb, gf, gb, v, s_bw, tril, tril.T, gate, gain.reshape(1, d))


def _pack_halves(x):
    half = x.shape[1] // 2
    bits = lax.bitcast_convert_type(x.astype(bf16).astype(f32), jnp.int32)
    return lax.shift_right_logical(bits[:, :half], 16) | (bits[:, half:] & jnp.int32(-65536))


def _unpack_halves(words):
    lo = lax.bitcast_convert_type(lax.shift_left(words, 16), f32)
    hi = lax.bitcast_convert_type(words & jnp.int32(-65536), f32)
    return lo, hi


def _router_kernel(x_ref, g_ref, r2_ref, h_ref, e_ref, rank_ref, w_ref, cnt_ref, run_ref,
                   *, n_experts):
    @pl.when(pl.program_id(0) == 0)
    def _():
        run_ref[...] = jnp.zeros_like(run_ref)

    h = _rms(x_ref[...], g_ref[...])
    h_ref[...] = _pack_halves(h)
    hi = h.astype(bf16)
    lo = (h - hi.astype(f32)).astype(bf16)
    tr = h.shape[0]
    prod = jnp.dot(jnp.concatenate([hi, lo], axis=0), r2_ref[...], preferred_element_type=f32)
    logits = prod[:tr, :LANES] + prod[:tr, LANES:] + prod[tr:, :LANES]
    lane = lax.broadcasted_iota(jnp.int32, logits.shape, 1)
    neg = jnp.float32(-jnp.inf)
    logits = jnp.where(lane < n_experts, logits, neg)
    m1 = jnp.max(logits, axis=1, keepdims=True)
    i1 = jnp.min(jnp.where(logits == m1, lane, LANES), axis=1, keepdims=True)
    rest = jnp.where(lane == i1, neg, logits)
    m2 = jnp.max(rest, axis=1, keepdims=True)
    i2 = jnp.min(jnp.where(rest == m2, lane, LANES), axis=1, keepdims=True)
    e2 = jnp.exp(m2 - m1)
    w_ref[...] = jnp.concatenate([1.0 / (1.0 + e2), e2 / (1.0 + e2)], axis=1)
    e_ref[...] = jnp.concatenate([i1, i2], axis=1)
    member = jnp.where((lane == i1) | (lane == i2), 1.0, 0.0)
    row = lax.broadcasted_iota(jnp.int32, (tr, tr), 0)
    col = lax.broadcasted_iota(jnp.int32, (tr, tr), 1)
    before = jnp.dot((col < row).astype(bf16), member.astype(bf16), preferred_element_type=f32)
    before = before + run_ref[...]
    r1 = jnp.sum(jnp.where(lane == i1, before, 0.0), axis=1, keepdims=True)
    r2 = jnp.sum(jnp.where(lane == i2, before, 0.0), axis=1, keepdims=True)
    rank_ref[...] = jnp.concatenate([r1, r2], axis=1).astype(jnp.int32)
    run = run_ref[...] + jnp.sum(member, axis=0, keepdims=True)
    run_ref[...] = run
    cnt_ref[...] = run.astype(jnp.int32)


def _router(x, g, router):
    n, d = x.shape
    e = router.shape[1]
    rpad = jnp.zeros((d, LANES), f32).at[:, :e].set(router)
    rhi = rpad.astype(bf16)
    r2 = jnp.concatenate([rhi, (rpad - rhi.astype(f32)).astype(bf16)], axis=1)
    tr = _tile(n, 512)
    pair = pl.BlockSpec((tr, TOP_K), lambda i: (i, 0))
    return pl.pallas_call(
        functools.partial(_router_kernel, n_experts=e),
        grid=(n // tr,),
        in_specs=[pl.BlockSpec((tr, d), lambda i: (i, 0)),
                  pl.BlockSpec((1, d), lambda i: (0, 0)),
                  pl.BlockSpec((d, 2 * LANES), lambda i: (0, 0))],
        out_specs=[pl.BlockSpec((tr, d // 2), lambda i: (i, 0)), pair, pair, pair,
                   pl.BlockSpec((1, LANES), lambda i: (0, 0))],
        out_shape=[jax.ShapeDtypeStruct((n, d // 2), jnp.int32),
                   jax.ShapeDtypeStruct((n, TOP_K), jnp.int32),
                   jax.ShapeDtypeStruct((n, TOP_K), jnp.int32),
                   jax.ShapeDtypeStruct((n, TOP_K), f32),
                   jax.ShapeDtypeStruct((1, LANES), jnp.int32)],
        scratch_shapes=[pltpu.VMEM((1, LANES), f32)],
        compiler_params=_cparams(("arbitrary",)),
    )(x, g.reshape(1, d), r2)


def _row_copies(pos_ref, i, src, dst, sem, gather):
    out = []
    for k in range(TOP_K):
        p = pos_ref[k, i]
        if gather:
            out.append(pltpu.make_async_copy(src.at[pl.ds(p, 1)], dst.at[k, pl.ds(i, 1)], sem))
        else:
            out.append(pltpu.make_async_copy(src.at[pl.ds(i, 1)], dst.at[pl.ds(p, 1)], sem))
    return out


def _start_rows(n_rows, pos_ref, src, dst, sem, gather):
    def start(i, carry):
        for cp in _row_copies(pos_ref, i, src, dst, sem, gather):
            cp.start()
        return carry

    lax.fori_loop(0, n_rows, start, 0, unroll=ROW_DMA_UNROLL)


def _wait_rows(n_rows, pos_ref, src, dst, sem, gather):
    def wait(i, carry):
        for cp in _row_copies(pos_ref, i, src, dst, sem, gather):
            cp.wait()
        return carry

    lax.fori_loop(0, n_rows, wait, 0, unroll=ROW_DMA_UNROLL)


def _all_rows(n_rows, pos_ref, src, dst, sem, gather):
    _start_rows(n_rows, pos_ref, src, dst, sem, gather)
    _wait_rows(n_rows, pos_ref, src, dst, sem, gather)


def _moe_dispatch_kernel(nv_ref, pos_ref, h_ref, xs_ref, zero_s, sem):
    tm = zero_s.shape[0]

    @pl.when(pl.program_id(0) == 0)
    def _():
        zero_s[...] = jnp.zeros_like(zero_s)

        def fill(r, carry):
            @pl.when(nv_ref[r] < tm)
            def _():
                cp = pltpu.make_async_copy(zero_s, xs_ref.at[pl.ds(pl.multiple_of(r * tm, tm), tm)], sem)
                cp.start()
                cp.wait()
            return carry

        lax.fori_loop(0, xs_ref.shape[0] // tm, fill, 0)

    _all_rows(h_ref.shape[0], pos_ref, h_ref, xs_ref, sem, gather=False)


def _moe_dispatch(hp, pos3, tile_rows, tm):
    n, dw = hp.shape
    n_tiles, _, tr = pos3.shape
    grid_spec = pltpu.PrefetchScalarGridSpec(
        num_scalar_prefetch=1,
        grid=(n_tiles,),
        in_specs=[pl.BlockSpec((None, TOP_K, tr), lambda i, nv: (i, 0, 0), memory_space=pltpu.SMEM),
                  pl.BlockSpec((tr, dw), lambda i, nv: (i, 0))],
        out_specs=pl.BlockSpec(memory_space=pl.ANY),
        scratch_shapes=[pltpu.VMEM((tm, dw), jnp.int32), pltpu.SemaphoreType.DMA(())],
    )
    return pl.pallas_call(
        _moe_dispatch_kernel,
        grid_spec=grid_spec,
        out_shape=jax.ShapeDtypeStruct((tile_rows.shape[0] * tm, dw), jnp.int32),
        compiler_params=_cparams(("arbitrary",)),
    )(tile_rows, pos3, hp)


def _moe_up_kernel(te_ref, nv_ref, xs_ref, wg_ref, wu_ref, a_ref, x_s):
    r = pl.program_id(0)
    half = xs_ref.shape[1]

    @pl.when(nv_ref[r] > 0)
    def _():
        @pl.when(pl.program_id(1) == 0)
        def _():
            lo, hi = _unpack_halves(xs_ref[...])
            x_s[:, :half] = lo.astype(bf16)
            x_s[:, half:] = hi.astype(bf16)

        x = x_s[...]
        gate = jnp.dot(x, wg_ref[...], preferred_element_type=f32)
        up = jnp.dot(x, wu_ref[...], preferred_element_type=f32)
        a_ref[...] = (_silu(gate) * up).astype(a_ref.dtype)

    @pl.when(nv_ref[r] == 0)
    def _():
        a_ref[...] = jnp.zeros_like(a_ref)


def _moe_down_kernel(te_ref, nv_ref, a_ref, wd_ref, y_ref):
    @pl.when(nv_ref[pl.program_id(0)] > 0)
    def _():
        y_ref[...] = _pack_halves(jnp.dot(a_ref[...], wd_ref[...], preferred_element_type=f32))

    @pl.when(nv_ref[pl.program_id(0)] == 0)
    def _():
        y_ref[...] = jnp.zeros_like(y_ref)


def _moe_group(xs, tile_expert, tile_rows, wg, wu, wd, tm):
    p, dw = xs.shape
    _, d, fe = wg.shape
    tf = _tile(fe, MOE_FF_TILE)
    wspec = pl.BlockSpec((None, d, tf), lambda r, f, te, nv: (te[r], 0, f))
    act = pl.pallas_call(
        _moe_up_kernel,
        grid_spec=pltpu.PrefetchScalarGridSpec(
            num_scalar_prefetch=2,
            grid=(p // tm, fe // tf),
            in_specs=[pl.BlockSpec((tm, dw), lambda r, f, te, nv: (r, 0)), wspec, wspec],
            out_specs=pl.BlockSpec((tm, tf), lambda r, f, te, nv: (r, f)),
            scratch_shapes=[pltpu.VMEM((tm, d), bf16)],
        ),
        out_shape=jax.ShapeDtypeStruct((p, fe), bf16),
        compiler_params=_cparams(("arbitrary", "arbitrary")),
    )(tile_expert, tile_rows, xs, wg, wu)
    return pl.pallas_call(
        _moe_down_kernel,
        grid_spec=pltpu.PrefetchScalarGridSpec(
            num_scalar_prefetch=2,
            grid=(p // tm,),
            in_specs=[pl.BlockSpec((tm, fe), lambda r, te, nv: (r, 0)),
                      pl.BlockSpec((None, fe, d), lambda r, te, nv: (te[r], 0, 0))],
            out_specs=pl.BlockSpec((tm, dw), lambda r, te, nv: (r, 0)),
        ),
        out_shape=jax.ShapeDtypeStruct((p, dw), jnp.int32),
        compiler_params=_cparams(("arbitrary",)),
    )(tile_expert, tile_rows, act, wd)


def _moe_combine_kernel(pos_ref, pos_next_ref, x_ref, w_ref, g_ref, y_ref, o_ref, ybuf, sem, *, final_norm):
    tr, d = x_ref.shape
    half = d // 2
    i = pl.program_id(0)
    slot = i % 2

    @pl.when(i == 0)
    def _():
        _start_rows(tr, pos_ref, y_ref, ybuf.at[0], sem.at[0], gather=True)

    _wait_rows(tr, pos_ref, y_ref, ybuf.at[slot], sem.at[slot], gather=True)
    for r0 in range(0, tr, SUBLANES):
        rows = slice(r0, r0 + SUBLANES)
        for ii in range(r0, r0 + SUBLANES):
            for cp in _row_copies(pos_next_ref, ii, y_ref, ybuf.at[1 - slot], sem.at[1 - slot], True):
                cp.start()
        w = w_ref[rows, :]
        lo0, hi0 = _unpack_halves(ybuf[slot, 0, rows, :])
        lo1, hi1 = _unpack_halves(ybuf[slot, 1, rows, :])
        o_lo = x_ref[rows, :half] + (w[:, 0:1] * lo0 + w[:, 1:2] * lo1)
        o_hi = x_ref[rows, half:] + (w[:, 0:1] * hi0 + w[:, 1:2] * hi1)
        if final_norm:
            ms = (jnp.sum(o_lo * o_lo, axis=1, keepdims=True)
                  + jnp.sum(o_hi * o_hi, axis=1, keepdims=True)) / d
            inv = lax.rsqrt(ms + EPS)
            o_lo = o_lo * inv * g_ref[:, :half]
            o_hi = o_hi * inv * g_ref[:, half:]
        o_ref[rows, :half] = o_lo
        o_ref[rows, half:] = o_hi

    @pl.when(i == pl.num_programs(0) - 1)
    def _():
        _wait_rows(tr, pos_next_ref, y_ref, ybuf.at[1 - slot], sem.at[1 - slot], gather=True)


def _moe_combine(x, w, y, pos3, gain, final_norm, row_start, n_rows):
    d = x.shape[1]
    tr = pos3.shape[2]
    off = row_start // tr
    last = off + n_rows // tr - 1
    return pl.pallas_call(
        functools.partial(_moe_combine_kernel, final_norm=final_norm),
        grid=(n_rows // tr,),
        in_specs=[pl.BlockSpec((None, TOP_K, tr), lambda i: (i + off, 0, 0), memory_space=pltpu.SMEM),
                  pl.BlockSpec((None, TOP_K, tr), lambda i: (jnp.minimum(i + off + 1, last), 0, 0),
                               memory_space=pltpu.SMEM),
                  pl.BlockSpec((tr, d), lambda i: (i + off, 0)),
                  pl.BlockSpec((tr, TOP_K), lambda i: (i + off, 0)),
                  pl.BlockSpec((1, d), lambda i: (0, 0)),
                  pl.BlockSpec(memory_space=pl.ANY)],
        out_specs=pl.BlockSpec((tr, d), lambda i: (i, 0)),
        out_shape=jax.ShapeDtypeStruct((n_rows, d), f32),
        scratch_shapes=[pltpu.VMEM((2, TOP_K, tr, d // 2), jnp.int32), pltpu.SemaphoreType.DMA((2,))],
        compiler_params=_cparams(("arbitrary",)),
    )(pos3, pos3, x, w, gain.reshape(1, d), y)


def _moe_layer(x, g, router, wg, wu, wd, final_gain, parts):
    n, d = x.shape
    n_exp = router.shape[1]
    tm = MOE_ROW_TILE
    hp, eid, rank, w, cnt = _router(x, g, router)
    cnt = cnt[0, :n_exp]
    padded = (cnt + tm - 1) // tm * tm
    seg_end = jnp.cumsum(padded)
    seg_start = seg_end - padded
    pos = seg_start[eid] + rank
    n_rows_padded = TOP_K * n + n_exp * tm
    tile0 = jnp.arange(n_rows_padded // tm, dtype=jnp.int32) * tm
    tile_expert = jnp.minimum(jnp.searchsorted(seg_end, tile0, side="right"), n_exp - 1).astype(jnp.int32)
    tile_rows = jnp.clip(cnt[tile_expert] - (tile0 - seg_start[tile_expert]), 0, tm).astype(jnp.int32)
    def by_tile(tr):
        return pos.reshape(n // tr, tr, TOP_K).transpose(0, 2, 1)

    pos3 = by_tile(_tile(n, MOE_COMBINE_TILE))
    xs = _moe_dispatch(hp, by_tile(_tile(n, MOE_DISPATCH_TILE)), tile_rows, tm)
    y = _moe_group(xs, tile_expert, tile_rows, wg, wu, wd, tm)
    if final_gain is None:
        return _moe_combine(x, w, y, pos3, g, False, 0, n)
    return [_moe_combine(x, w, y, pos3, final_gain, True, start, rows) for start, rows in parts]


def kernel(x_prompt, x_sample, norm_mix, norm_ffn, norm_final, pool_w, pool_scale, hgrn_w_in, hgrn_lb,
           hgrn_norm, hgrn_w_out, ffn_w_gate, ffn_w_up, ffn_w_down, moe_router, moe_w_gate, moe_w_up,
           moe_w_down):
    bp, sp, d = x_prompt.shape
    bs, ss, _ = x_sample.shape
    n_prompt = bp * sp
    n = n_prompt + bs * ss
    seq = (n_prompt, sp, ss)
    depth = norm_mix.shape[0]
    chunk = _tile(min(sp, ss), 128)

    parts = ((0, n_prompt), (n_prompt, bs * ss))
    streams = (x_prompt.reshape(n_prompt, d), x_sample.reshape(bs * ss, d))
    x = None if depth else jnp.concatenate(streams, axis=0)
    outs = None

    for i in range(depth):
        j = i // 2
        if i % 2 == 0:
            xa, xb = streams if x is None else (x[:n_prompt], x[n_prompt:])
            x, h = _pool_layer(xa, xb, norm_mix[i], pool_w[j].astype(bf16), pool_scale[j], norm_ffn[i], seq)
            nxt = i + 1 < depth
            later = [(hgrn_w_in, j, norm_mix[i + 1])] + [
                (w, j, None) for w in (hgrn_w_out, moe_w_gate, moe_w_up, moe_w_down)] if nxt else []
            act, (w_down, *cast) = _glu_up(h, ffn_w_gate[j].astype(bf16), ffn_w_up[j].astype(bf16),
                                           [(ffn_w_down, j, None)] + later)
            (x, *norm_rows), _ = _down_res(act, w_down, x, 1024, 256, nxt)
            rounded = dict(zip(("w_in", "w_out", "moe_gate", "moe_up", "moe_down"), cast))
        else:
            x16, sq = norm_rows
            w_in = rounded["w_in"]
            q = _proj_section(_proj_silu_heads_kernel, x16, sq, w_in, 0, True)
            forget = functools.partial(_proj_forget_kernel, layer=i)
            kf, gf = _proj_section(forget, x16, sq, w_in, 1, True, 2, hgrn_lb)
            kb, gb = _proj_section(forget, x16, sq, w_in, 2, True, 2, hgrn_lb)
            v = _proj_section(_proj_heads_kernel, x16, sq, w_in, 3, True)
            gate = _proj_section(_proj_silu_kernel, x16, sq, w_in, 4, False)
            s_bw = _scan_bw_states(kb, gb, v, seq, chunk)
            og = _scan(q, kf, kb, gf, gb, v, s_bw, gate, hgrn_norm[j], seq, chunk)
            (x,), _ = _down_res(og, rounded["w_out"], x, 1024, 512)
            last = i == depth - 1
            res = _moe_layer(x, norm_ffn[i], moe_router[j], rounded["moe_gate"], rounded["moe_up"],
                             rounded["moe_down"], norm_final if last else None, parts)
            x, outs = (None, res) if last else (res, None)

    if outs is None:
        outs = [_rmsnorm(x, norm_final, f32, start, rows) for start, rows in parts]
    return (outs[0].reshape(bp, sp, d), outs[1].reshape(bs, ss, d))
```

```python
import functools

import jax
import jax.numpy as jnp
from jax import lax
from jax.experimental import pallas as pl
from jax.experimental.pallas import tpu as pltpu

EPS = 1e-6
POOL_WINDOWS = (2, 4, 8, 16)
POOL_HALO = 8
POOL_PAD = 16
POOL_BLOCK = 128
HEAD_DIM = 128
TOP_K = 2
V7X_VMEM_LIMIT = 56 * 1024 * 1024
LANES = 128
SUBLANES = 8
MOE_ROW_TILE = 512
MOE_FF_TILE = 512
MOE_DISPATCH_TILE = 512
MOE_COMBINE_TILE = 256
ROW_DMA_UNROLL = 8
EPILOGUE_COLS = 256
BF16_SUBLANES = 16

bf16 = jnp.bfloat16
f32 = jnp.float32


def _cparams(sem):
    return pltpu.CompilerParams(dimension_semantics=sem, vmem_limit_bytes=V7X_VMEM_LIMIT)


def _tile(n, pref):
    t = min(n, pref)
    while n % t:
        t //= 2
    return t


def _rms(x, g):
    ms = jnp.mean(x * x, axis=-1, keepdims=True)
    return x * lax.rsqrt(ms + EPS) * g


def _silu(x):
    return x * jax.nn.sigmoid(x)


def _seq_flags(tok, n_tok, n_prompt, seq_p, seq_s):
    in_p = tok < n_prompt
    rel = jnp.where(in_p, tok, tok - n_prompt)
    slen = jnp.where(in_p, seq_p, seq_s)
    is_start = (rel % slen) == 0
    is_end = ((rel + n_tok) % slen) == 0
    return is_start, is_end


def _rmsnorm_kernel(x_ref, g_ref, o_ref):
    o_ref[...] = _rms(x_ref[...], g_ref[...]).astype(o_ref.dtype)


def _rmsnorm(x, g, out_dtype, row_start=0, n_rows=None):
    n, d = x.shape
    n_rows = n if n_rows is None else n_rows
    tr = _tile(n_rows, 512)
    off = row_start // tr
    return pl.pallas_call(
        _rmsnorm_kernel,
        grid=(n_rows // tr,),
        in_specs=[pl.BlockSpec((tr, d), lambda i: (i + off, 0)),
                  pl.BlockSpec((1, d), lambda i: (0, 0))],
        out_specs=pl.BlockSpec((tr, d), lambda i: (i, 0)),
        out_shape=jax.ShapeDtypeStruct((n_rows, d), out_dtype),
        compiler_params=_cparams(("parallel",)),
    )(x, g.reshape(1, d))


def _pool_kernel(ap_ref, a_ref, an_ref, bp_ref, b_ref, bn_ref, gmix_ref, w_ref, scale_ref, gffn_ref,
                 x1_ref, h1_ref, hi_s, lo_s, *, seq):
    t, d = a_ref.shape
    dg = d // len(POOL_WINDOWS)
    tok = pl.program_id(0) * t
    is_start, is_end = _seq_flags(tok, t, *seq)
    in_a = tok < seq[0]
    g = gmix_ref[...]
    x = jnp.where(in_a, a_ref[...], b_ref[...])
    h = _rms(x, g)
    h_prev = jnp.where(is_start, 0.0, _rms(jnp.where(in_a, ap_ref[...], bp_ref[...]), g))
    h_next = jnp.where(is_end, 0.0, _rms(jnp.where(in_a, an_ref[...], bn_ref[...]), g))
    zeros = jnp.zeros_like(h_prev)

    def put(r0, val):
        hi = val.astype(bf16)
        hi_s[r0:r0 + val.shape[0], :] = hi
        lo_s[r0:r0 + val.shape[0], :] = (val - hi.astype(f32)).astype(bf16)

    put(0, jnp.concatenate([zeros, h_prev], axis=0))
    put(POOL_PAD, h)
    put(POOL_PAD + t, jnp.concatenate([h_next, zeros], axis=0))
    tail = hi_s.shape[0] - (2 * POOL_PAD + t)
    hi_s[2 * POOL_PAD + t:, :] = jnp.zeros((tail, d), bf16)
    lo_s[2 * POOL_PAD + t:, :] = jnp.zeros((tail, d), bf16)

    rr = lax.broadcasted_iota(jnp.int32, (POOL_BLOCK, 2 * POOL_BLOCK), 0)
    cc = lax.broadcasted_iota(jnp.int32, (POOL_BLOCK, 2 * POOL_BLOCK), 1)
    r = lax.broadcasted_iota(jnp.int32, (t, 1), 0)
    for gi, w in enumerate(POOL_WINDOWS):
        c0 = gi * dg
        half = w // 2
        band = ((cc >= POOL_PAD + rr - half) & (cc < POOL_PAD + rr + half)).astype(bf16)
        sums = []
        for r0 in range(0, t, POOL_BLOCK):
            sums.append(jnp.dot(band, hi_s[r0:r0 + 2 * POOL_BLOCK, c0:c0 + dg], preferred_element_type=f32)
                        + jnp.dot(band, lo_s[r0:r0 + 2 * POOL_BLOCK, c0:c0 + dg], preferred_element_type=f32))
        s = jnp.concatenate(sums, axis=0)
        lo = jnp.where(is_start, jnp.maximum(r - half, 0), r - half)
        hi = jnp.where(is_end, jnp.minimum(r + half, t), r + half)
        cnt = (hi - lo).astype(f32)
        dlt = s / cnt - h[:, c0:c0 + dg]
        y = jnp.dot(dlt.astype(bf16), w_ref[gi], preferred_element_type=f32)
        x1_ref[:, c0:c0 + dg] = x[:, c0:c0 + dg] + y * scale_ref[:, c0:c0 + dg]
    h1_ref[...] = _rms(x1_ref[...], gffn_ref[...]).astype(h1_ref.dtype)


def _stream_specs(rows, first_tile, t, d):
    hb = t // POOL_HALO
    last_halo = rows // POOL_HALO - 1
    last_tile = rows // t - 1
    return [
        pl.BlockSpec((POOL_HALO, d), lambda i: (jnp.clip((i - first_tile) * hb - 1, 0, last_halo), 0)),
        pl.BlockSpec((t, d), lambda i: (jnp.clip(i - first_tile, 0, last_tile), 0)),
        pl.BlockSpec((POOL_HALO, d), lambda i: (jnp.clip((i - first_tile + 1) * hb, 0, last_halo), 0)),
    ]


def _pool_layer(xa, xb, g_mix, w_grp, scale, g_ffn, seq):
    d = xa.shape[1]
    n = xa.shape[0] + xb.shape[0]
    t = _tile(min(seq[1], seq[2]), 256)
    row = lambda i: (0, 0)
    return pl.pallas_call(
        functools.partial(_pool_kernel, seq=seq),
        grid=(n // t,),
        in_specs=_stream_specs(xa.shape[0], 0, t, d) + _stream_specs(xb.shape[0], xa.shape[0] // t, t, d) + [
            pl.BlockSpec((1, d), row),
            pl.BlockSpec(w_grp.shape, lambda i: (0, 0, 0)),
            pl.BlockSpec((1, d), row),
            pl.BlockSpec((1, d), row),
        ],
        out_specs=[pl.BlockSpec((t, d), lambda i: (i, 0)),
                   pl.BlockSpec((t, d), lambda i: (i, 0))],
        out_shape=[jax.ShapeDtypeStruct((n, d), f32), jax.ShapeDtypeStruct((n, d), bf16)],
        scratch_shapes=[pltpu.VMEM((t + POOL_BLOCK, d), bf16)] * 2,
        compiler_params=_cparams(("parallel",)),
    )(xa, xa, xa, xb, xb, xb, g_mix.reshape(1, d), w_grp, scale.reshape(1, d), g_ffn.reshape(1, d))


def _matmul_call_with_casts(kernel, grid, in_specs, out_specs, out_shapes, args, to_cast):
    n_steps = grid[0] * grid[1]
    plans = []
    for stacked, layer, scale in to_cast:
        cols = stacked.shape[-1]
        rows = stacked[0].size // cols
        steps = 1
        while steps * 2 <= n_steps and rows % (steps * 2 * BF16_SUBLANES) == 0:
            steps *= 2
        plans.append((rows, cols, steps, rows // steps, layer))
    scaled = [k for k, (_, _, scale) in enumerate(to_cast) if scale is not None]
    n_in, n_side, n_scale, n_out = len(in_specs), len(to_cast), len(scaled), len(out_specs)

    def step_block(steps):
        return lambda i, j: jnp.minimum(i * grid[1] + j, steps - 1)

    def src_spec(cols, steps, rb, layer):
        blk = step_block(steps)
        return pl.BlockSpec((None, rb, cols), lambda i, j: (layer, blk(i, j), 0))

    def dst_spec(cols, steps, rb):
        blk = step_block(steps)
        return pl.BlockSpec((rb, cols), lambda i, j: (blk(i, j), 0))

    def body(*refs):
        first_out = n_in + n_side + n_scale
        kernel(*refs[:n_in], *refs[first_out:first_out + n_out])
        step = pl.program_id(0) * grid[1] + pl.program_id(1)
        for k, plan in enumerate(plans):
            scale_ref = refs[n_in + n_side + scaled.index(k)] if k in scaled else None

            @pl.when(step < plan[2])
            def _(src=refs[n_in + k], dst=refs[first_out + n_out + k], scale_ref=scale_ref):
                val = src[...] if scale_ref is None else src[...] * scale_ref[...]
                dst[...] = val.astype(bf16)

    outs = pl.pallas_call(
        body,
        grid=grid,
        in_specs=list(in_specs) + [src_spec(c, s, rb, layer) for _, c, s, rb, layer in plans]
        + [dst_spec(1, plans[k][2], plans[k][3]) for k in scaled],
        out_specs=list(out_specs) + [dst_spec(c, s, rb) for _, c, s, rb, _ in plans],
        out_shape=list(out_shapes) + [jax.ShapeDtypeStruct((r, c), bf16) for r, c, _, _, _ in plans],
        compiler_params=_cparams(("arbitrary", "arbitrary")),
    )(*args,
      *[stacked.reshape(stacked.shape[0], r, c) for (stacked, _, _), (r, c, _, _, _) in zip(to_cast, plans)],
      *[to_cast[k][2].reshape(-1, 1) for k in scaled])
    return outs[:n_out], [o.reshape(stacked.shape[1:]) for o, (stacked, _, _) in zip(outs[n_out:], to_cast)]


def _glu_up_kernel(a_ref, wg_ref, wu_ref, o_ref):
    a = a_ref[...]
    gate = jnp.dot(a, wg_ref[...], preferred_element_type=f32)
    up = jnp.dot(a, wu_ref[...], preferred_element_type=f32)
    o_ref[...] = (_silu(gate) * up).astype(o_ref.dtype)


def _glu_up(a, wg, wu, to_cast=()):
    m, k = a.shape
    f = wg.shape[1]
    tm = _tile(m, 1024)
    tn = _tile(f, 512)
    wspec = pl.BlockSpec((k, tn), lambda i, j: (0, j))
    (act,), rounded = _matmul_call_with_casts(
        _glu_up_kernel, (m // tm, f // tn),
        [pl.BlockSpec((tm, k), lambda i, j: (i, 0)), wspec, wspec],
        [pl.BlockSpec((tm, tn), lambda i, j: (i, j))],
        [jax.ShapeDtypeStruct((m, f), bf16)], (a, wg, wu), to_cast)
    return act, rounded


def _down_kernel(a_ref, w_ref, res_ref, o_ref, *norm_refs):
    o = res_ref[...] + jnp.dot(a_ref[...], w_ref[...], preferred_element_type=f32)
    o_ref[...] = o
    if not norm_refs:
        return
    o16_ref, sq_ref = norm_refs
    o16_ref[...] = o.astype(o16_ref.dtype)
    sq = o * o
    part = sq[:, 0:LANES]
    for c0 in range(LANES, sq.shape[1], LANES):
        part = part + sq[:, c0:c0 + LANES]

    @pl.when(pl.program_id(1) == 0)
    def _():
        sq_ref[...] = part

    @pl.when(pl.program_id(1) > 0)
    def _():
        sq_ref[...] += part


def _down_res(a, w, res, tm, tn, norm_rows=False, to_cast=()):
    m, kk = a.shape
    n = w.shape[1]
    tm = _tile(m, tm)
    tn = _tile(n, tn)
    tile = pl.BlockSpec((tm, tn), lambda i, j: (i, j))
    out_specs, out_shapes = [tile], [jax.ShapeDtypeStruct((m, n), f32)]
    if norm_rows:
        out_specs += [tile, pl.BlockSpec((tm, LANES), lambda i, j: (i, 0))]
        out_shapes += [jax.ShapeDtypeStruct((m, n), bf16), jax.ShapeDtypeStruct((m, LANES), f32)]
    return _matmul_call_with_casts(
        _down_kernel, (m // tm, n // tn),
        [pl.BlockSpec((tm, kk), lambda i, j: (i, 0)), pl.BlockSpec((kk, tn), lambda i, j: (0, j)), tile],
        out_specs, out_shapes, (a, w, res), to_cast)


def _column_blocks(a_ref, sq_ref, w_ref):
    a = a_ref[...]
    inv = lax.rsqrt(jnp.sum(sq_ref[...], axis=1, keepdims=True) / a.shape[1] + EPS)
    tn = w_ref.shape[1]
    cb = min(tn, EPILOGUE_COLS)
    for c0 in range(0, tn, cb):
        yield c0, inv * jnp.dot(a, w_ref[:, c0:c0 + cb], preferred_element_type=f32)


def _store_head_block(o_ref, c0, val):
    for hh in range(val.shape[1] // HEAD_DIM):
        o_ref[c0 // HEAD_DIM + hh] = val[:, hh * HEAD_DIM:(hh + 1) * HEAD_DIM].astype(o_ref.dtype)


def _proj_silu_heads_kernel(a_ref, sq_ref, w_ref, o_ref):
    for c0, r in _column_blocks(a_ref, sq_ref, w_ref):
        _store_head_block(o_ref, c0, _silu(r))


def _proj_heads_kernel(a_ref, sq_ref, w_ref, o_ref):
    for c0, r in _column_blocks(a_ref, sq_ref, w_ref):
        _store_head_block(o_ref, c0, r)


def _proj_silu_kernel(a_ref, sq_ref, w_ref, o_ref):
    for c0, r in _column_blocks(a_ref, sq_ref, w_ref):
        o_ref[:, c0:c0 + r.shape[1]] = _silu(r).astype(o_ref.dtype)


def _proj_forget_kernel(a_ref, sq_ref, w_ref, lb_ref, k_ref, g_ref, *, layer):
    logits = lb_ref[...]
    ex = jnp.exp(logits - jnp.max(logits, axis=0, keepdims=True))
    sm = ex / jnp.sum(ex, axis=0, keepdims=True)
    lb_all = jnp.sum(sm[0:layer + 1], axis=0, keepdims=True) - sm[0:1]
    for c0, r in _column_blocks(a_ref, sq_ref, w_ref):
        lb = lb_all[:, c0:c0 + r.shape[1]]
        f = lb + (1.0 - lb) * jax.nn.sigmoid(r)
        _store_head_block(k_ref, c0, 1.0 - f)
        _store_head_block(g_ref, c0, jnp.log2(f))


def _proj_section(kernel, a, sq, w, section, out_heads, n_out=1, extra=None):
    m, k = a.shape
    d = k
    tm = _tile(m, 1024)
    tn = _tile(d, 1024)
    nb = d // tn
    in_specs = [pl.BlockSpec((tm, k), lambda i, j: (i, 0)),
                pl.BlockSpec((tm, LANES), lambda i, j: (i, 0)),
                pl.BlockSpec((k, tn), lambda i, j: (0, section * nb + j))]
    args = [a, sq, w]
    if extra is not None:
        in_specs.append(pl.BlockSpec((extra.shape[0], tn), lambda i, j: (0, j)))
        args.append(extra)
    if out_heads:
        hpb = tn // HEAD_DIM
        ospec = pl.BlockSpec((hpb, tm, HEAD_DIM), lambda i, j: (j, i, 0))
        oshape = jax.ShapeDtypeStruct((d // HEAD_DIM, m, HEAD_DIM), bf16)
    else:
        ospec = pl.BlockSpec((tm, tn), lambda i, j: (i, j))
        oshape = jax.ShapeDtypeStruct((m, d), bf16)
    out = pl.pallas_call(
        kernel,
        grid=(m // tm, nb),
        in_specs=in_specs,
        out_specs=[ospec] * n_out if n_out > 1 else ospec,
        out_shape=[oshape] * n_out if n_out > 1 else oshape,
        compiler_params=_cparams(("parallel", "arbitrary")),
    )(*args)
    return out


MAX_DIRECT_LOG2_DECAY = 100.0
HEAD_UNROLL = 8


def _dot_nt(a, b):
    return lax.dot_general(a, b, (((1,), (1,)), ((), ())), preferred_element_type=f32)


def _dot_tn(a, b):
    return lax.dot_general(a, b, (((0,), (0,)), ((), ())), preferred_element_type=f32)


def _mid_ref(p, level, offset):
    c, w = p.shape
    blk = 2 << level
    p3 = p.reshape(c // blk, blk, w)
    return jnp.broadcast_to(p3[:, offset:offset + 1, :], p3.shape).reshape(c, w)


def _neg_abs_exponents(gf, gb, b, x, level, rmod4):
    if level == 0:
        odd = (rmod4 & 1) == 1
        return jnp.where(odd, gf, 0.0), jnp.where(odd, 0.0, gb)
    if level == 1:
        c = gf.shape[0]
        gf_up = pltpu.roll(gf, c - 1, 0)
        gf_dn = pltpu.roll(gf, 1, 0)
        gb_up = pltpu.roll(gb, c - 1, 0)
        gb_dn = pltpu.roll(gb, 1, 0)
        ef = jnp.where(rmod4 == 0, gf_up, jnp.where(rmod4 == 1, 0.0, jnp.where(rmod4 == 2, gf, gf + gf_dn)))
        eb = jnp.where(rmod4 == 0, gb + gb_up, jnp.where(rmod4 == 1, gb, jnp.where(rmod4 == 2, 0.0, gb_dn)))
        return ef, eb
    half = 1 << level
    ef = b - _mid_ref(b, level, half - 1)
    eb = x - _mid_ref(x, level, half)
    return -jnp.abs(ef), -jnp.abs(eb)


def _chunk_prefix(tri_incl, tri_excl, gf16, gb16):
    b = jnp.dot(tri_incl, gf16, preferred_element_type=f32)
    x = jnp.dot(tri_excl, gb16, preferred_element_type=f32)
    return b, x


def _scan_bw_kernel(kb_ref, gb_ref, v_ref, s_ref, st_ref, *, seq, n_chunks):
    n_heads, c, _ = kb_ref.shape
    chunk = n_chunks - 1 - pl.program_id(0)
    _, is_end = _seq_flags(chunk * c, c, *seq)

    @pl.when(is_end)
    def _():
        st_ref[...] = jnp.zeros_like(st_ref)

    row = lax.broadcasted_iota(jnp.int32, (c, c), 0)
    col = lax.broadcasted_iota(jnp.int32, (c, c), 1)
    tri_excl = (col < row).astype(bf16)

    def head(h, carry):
        gb16 = gb_ref[h]
        x = jnp.dot(tri_excl, gb16, preferred_element_type=f32)
        tot = x[c - 1:c, :] + gb16[c - 1:c, :].astype(f32)
        st = st_ref[h]
        s_ref[h] = st.astype(s_ref.dtype)
        kx = (kb_ref[h].astype(f32) * jnp.exp2(x)).astype(bf16)
        st_ref[h] = st * jnp.exp2(tot) + _dot_tn(v_ref[h], kx)
        return carry

    lax.fori_loop(0, n_heads, head, 0, unroll=2 * HEAD_UNROLL)


def _scan_bw_states(kb, gb, v, seq, c):
    n_heads, n, dk = kb.shape
    n_chunks = n // c
    spec = pl.BlockSpec((n_heads, c, dk), lambda i: (0, n_chunks - 1 - i, 0))
    return pl.pallas_call(
        functools.partial(_scan_bw_kernel, seq=seq, n_chunks=n_chunks),
        grid=(n_chunks,),
        in_specs=[spec, spec, spec],
        out_specs=pl.BlockSpec((None, n_heads, dk, dk), lambda i: (n_chunks - 1 - i, 0, 0, 0)),
        out_shape=jax.ShapeDtypeStruct((n_chunks, n_heads, dk, dk), bf16),
        scratch_shapes=[pltpu.VMEM((n_heads, dk, dk), f32)],
        compiler_params=_cparams(("arbitrary",)),
    )(kb, gb, v)


def _scan_kernel(q_ref, kf_ref, kb_ref, gf_ref, gb_ref, v_ref, sb_ref, tril_ref, triu_ref, gate_ref,
                 gain_ref, out_ref, st_ref, o_s, qf_s, kf_s, qb_s, kb_s, qin_s, kx_s, sc_s, dec_s, tot_s,
                 *, seq):
    n_heads, c, dk = q_ref.shape
    n_levels = c.bit_length() - 1
    mid = c // 2
    is_start, _ = _seq_flags(pl.program_id(0) * c, c, *seq)

    @pl.when(is_start)
    def _():
        st_ref[...] = jnp.zeros_like(st_ref)

    row = lax.broadcasted_iota(jnp.int32, (c, c), 0)
    col = lax.broadcasted_iota(jnp.int32, (c, c), 1)
    tri_incl = (col <= row).astype(bf16)
    tri_excl = (col < row).astype(bf16)

    def finish(h, kf_dec, b_last, q_in, scores, diag):
        v = v_ref[h]
        st_f = st_ref[h]
        st_cat = jnp.concatenate([st_f.astype(bf16), sb_ref[h]], axis=1)
        o = _dot_nt(q_in, st_cat) + jnp.dot(scores.astype(bf16), v, preferred_element_type=f32)
        o_s[h] = o if diag is None else o + diag * v.astype(f32)
        st_ref[h] = st_f * jnp.exp2(b_last) + _dot_tn(v, kf_dec.astype(bf16))

    def direct_operands(h, carry):
        q = q_ref[h].astype(f32)
        gb16 = gb_ref[h]
        b, x = _chunk_prefix(tri_incl, tri_excl, gf_ref[h], gb16)
        b_mid = b[mid - 1:mid, :]
        x_mid = x[mid:mid + 1, :]
        db = b - b_mid
        dx = x - x_mid
        q_f = q * jnp.exp2(db)
        k_f = kf_ref[h].astype(f32) * jnp.exp2(-db)
        q_b = q * jnp.exp2(-dx)
        k_b = kb_ref[h].astype(f32) * jnp.exp2(dx)
        b_last = b[c - 1:c, :]
        x_tot = x[c - 1:c, :] + gb16[c - 1:c, :].astype(f32)
        qf_s[h] = q_f.astype(bf16)
        kf_s[h] = k_f.astype(bf16)
        qb_s[h] = q_b.astype(bf16)
        kb_s[h] = k_b.astype(bf16)
        qin_s[h] = jnp.concatenate([q_f * jnp.exp2(b_mid), q_b * jnp.exp2(x_tot - x_mid)], axis=1).astype(bf16)
        kx_s[h] = (k_f * jnp.exp2(b_last - b_mid)).astype(bf16)
        dec_s[h] = jnp.broadcast_to(jnp.exp2(b_last), dec_s.shape[1:])
        tot_s[h] = jnp.broadcast_to(jnp.minimum(b_last, x_tot), tot_s.shape[1:])
        return carry

    def direct_scores(h, carry):
        sc_s[h] = (jnp.where(tril_ref[...] != 0.0, _dot_nt(qf_s[h], kf_s[h]), 0.0)
                   + jnp.where(triu_ref[...] != 0.0, _dot_nt(qb_s[h], kb_s[h]), 0.0)).astype(bf16)
        return carry

    def direct_output(h, carry):
        v = v_ref[h]
        st_f = st_ref[h]
        st_cat = jnp.concatenate([st_f.astype(bf16), sb_ref[h]], axis=1)
        o_s[h] = _dot_nt(qin_s[h], st_cat) + jnp.dot(sc_s[h], v, preferred_element_type=f32)
        st_ref[h] = st_f * dec_s[h][0:1, :] + _dot_tn(v, kx_s[h])
        return carry

    def head_levelled(h, carry):
        rtok = lax.broadcasted_iota(jnp.int32, (c, dk), 0)
        rmod4 = rtok & 3
        q = q_ref[h].astype(f32)
        kf = kf_ref[h].astype(f32)
        kb = kb_ref[h].astype(f32)
        gf16 = gf_ref[h]
        gb16 = gb_ref[h]
        gf = gf16.astype(f32)
        gb = gb16.astype(f32)
        b, x = _chunk_prefix(tri_incl, tri_excl, gf16, gb16)
        scores = jnp.zeros((c, c), f32)
        for l in range(n_levels):
            ef, eb = _neg_abs_exponents(gf, gb, b, x, l, rmod4)
            af = jnp.exp2(ef)
            ab = jnp.exp2(eb)
            up = ((rtok >> l) & 1) == 1
            lhs = jnp.concatenate([jnp.where(up, q * af, 0.0), jnp.where(up, 0.0, q * ab)], axis=1)
            rhs = jnp.concatenate([jnp.where(up, 0.0, kf * af), jnp.where(up, kb * ab, 0.0)], axis=1)
            p = _dot_nt(lhs.astype(bf16), rhs.astype(bf16))
            if l < n_levels - 1:
                p = jnp.where((row >> (l + 1)) == (col >> (l + 1)), p, 0.0)
            scores = scores + p
        b_last = b[c - 1:c, :]
        x_tot = x[c - 1:c, :] + gb[c - 1:c, :]
        q_in = jnp.concatenate([q * jnp.exp2(b), q * jnp.exp2(x_tot - x)], axis=1).astype(bf16)
        diag = jnp.sum(q * (kf + kb), axis=1, keepdims=True)
        finish(h, kf * jnp.exp2(b_last - b), b_last, q_in, scores, diag)
        return carry

    lax.fori_loop(0, n_heads, direct_operands, 0, unroll=2 * HEAD_UNROLL)
    direct_ok = jnp.min(tot_s[...]) >= -MAX_DIRECT_LOG2_DECAY

    @pl.when(direct_ok)
    def _():
        lax.fori_loop(0, n_heads, direct_scores, 0, unroll=2 * HEAD_UNROLL)
        lax.fori_loop(0, n_heads, direct_output, 0, unroll=2 * HEAD_UNROLL)

    @pl.when(jnp.logical_not(direct_ok))
    def _():
        lax.fori_loop(0, n_heads, head_levelled, 0)

    sq = jnp.zeros((c, dk), f32)
    for hh in range(n_heads):
        o = o_s[hh]
        sq = sq + o * o
    inv = lax.rsqrt(jnp.sum(sq, axis=1, keepdims=True) / (n_heads * dk) + EPS)
    for hh in range(n_heads):
        cols = slice(hh * dk, (hh + 1) * dk)
        out_ref[:, cols] = (o_s[hh] * inv * gain_ref[:, cols] * gate_ref[:, cols].astype(f32)
                            ).astype(out_ref.dtype)


def _scan(q, kf, kb, gf, gb, v, s_bw, gate, gain, seq, c):
    n_heads, n, dk = q.shape
    d = n_heads * dk
    n_chunks = n // c
    spec = pl.BlockSpec((n_heads, c, dk), lambda i: (0, i, 0))
    tri = pl.BlockSpec((c, c), lambda i: (0, 0))
    tril = jnp.tril(jnp.ones((c, c), f32))
    return pl.pallas_call(
        functools.partial(_scan_kernel, seq=seq),
        grid=(n_chunks,),
        in_specs=[spec] * 6 + [pl.BlockSpec((None, n_heads, dk, dk), lambda i: (i, 0, 0, 0)), tri, tri,
                               pl.BlockSpec((c, d), lambda i: (i, 0)),
                               pl.BlockSpec((1, d), lambda i: (0, 0))],
        out_specs=pl.BlockSpec((c, d), lambda i: (i, 0)),
        out_shape=jax.ShapeDtypeStruct((n, d), bf16),
        scratch_shapes=[pltpu.VMEM((n_heads, dk, dk), f32), pltpu.VMEM((n_heads, c, dk), f32)]
        + [pltpu.VMEM((n_heads, c, dk), bf16)] * 4
        + [pltpu.VMEM((n_heads, c, 2 * dk), bf16), pltpu.VMEM((n_heads, c, dk), bf16),
           pltpu.VMEM((n_heads, c, c), bf16)] + [pltpu.VMEM((n_heads, 8, dk), f32)] * 2,
        compiler_params=_cparams(("arbitrary",)),
    )(q, kf, kb, gf, gb, v, s_bw, tril, tril.T, gate, gain.reshape(1, d))


def _pack_halves(x):
    half = x.shape[1] // 2
    bits = lax.bitcast_convert_type(x.astype(bf16).astype(f32), jnp.int32)
    return lax.shift_right_logical(bits[:, :half], 16) | (bits[:, half:] & jnp.int32(-65536))


def _unpack_halves(words):
    lo = lax.bitcast_convert_type(lax.shift_left(words, 16), f32)
    hi = lax.bitcast_convert_type(words & jnp.int32(-65536), f32)
    return lo, hi


def _router_kernel(x_ref, g_ref, r2_ref, h_ref, e_ref, rank_ref, w_ref, cnt_ref, run_ref,
                   *, n_experts):
    @pl.when(pl.program_id(0) == 0)
    def _():
        run_ref[...] = jnp.zeros_like(run_ref)

    h = _rms(x_ref[...], g_ref[...])
    h_ref[...] = _pack_halves(h)
    hi = h.astype(bf16)
    lo = (h - hi.astype(f32)).astype(bf16)
    tr = h.shape[0]
    prod = jnp.dot(jnp.concatenate([hi, lo], axis=0), r2_ref[...], preferred_element_type=f32)
    logits = prod[:tr, :LANES] + prod[:tr, LANES:] + prod[tr:, :LANES]
    lane = lax.broadcasted_iota(jnp.int32, logits.shape, 1)
    neg = jnp.float32(-jnp.inf)
    logits = jnp.where(lane < n_experts, logits, neg)
    m1 = jnp.max(logits, axis=1, keepdims=True)
    i1 = jnp.min(jnp.where(logits == m1, lane, LANES), axis=1, keepdims=True)
    rest = jnp.where(lane == i1, neg, logits)
    m2 = jnp.max(rest, axis=1, keepdims=True)
    i2 = jnp.min(jnp.where(rest == m2, lane, LANES), axis=1, keepdims=True)
    e2 = jnp.exp(m2 - m1)
    w_ref[...] = jnp.concatenate([1.0 / (1.0 + e2), e2 / (1.0 + e2)], axis=1)
    e_ref[...] = jnp.concatenate([i1, i2], axis=1)
    member = jnp.where((lane == i1) | (lane == i2), 1.0, 0.0)
    row = lax.broadcasted_iota(jnp.int32, (tr, tr), 0)
    col = lax.broadcasted_iota(jnp.int32, (tr, tr), 1)
    before = jnp.dot((col < row).astype(bf16), member.astype(bf16), preferred_element_type=f32)
    before = before + run_ref[...]
    r1 = jnp.sum(jnp.where(lane == i1, before, 0.0), axis=1, keepdims=True)
    r2 = jnp.sum(jnp.where(lane == i2, before, 0.0), axis=1, keepdims=True)
    rank_ref[...] = jnp.concatenate([r1, r2], axis=1).astype(jnp.int32)
    run = run_ref[...] + jnp.sum(member, axis=0, keepdims=True)
    run_ref[...] = run
    cnt_ref[...] = run.astype(jnp.int32)


def _router(x, g, router):
    n, d = x.shape
    e = router.shape[1]
    rpad = jnp.zeros((d, LANES), f32).at[:, :e].set(router)
    rhi = rpad.astype(bf16)
    r2 = jnp.concatenate([rhi, (rpad - rhi.astype(f32)).astype(bf16)], axis=1)
    tr = _tile(n, 512)
    pair = pl.BlockSpec((tr, TOP_K), lambda i: (i, 0))
    return pl.pallas_call(
        functools.partial(_router_kernel, n_experts=e),
        grid=(n // tr,),
        in_specs=[pl.BlockSpec((tr, d), lambda i: (i, 0)),
                  pl.BlockSpec((1, d), lambda i: (0, 0)),
                  pl.BlockSpec((d, 2 * LANES), lambda i: (0, 0))],
        out_specs=[pl.BlockSpec((tr, d // 2), lambda i: (i, 0)), pair, pair, pair,
                   pl.BlockSpec((1, LANES), lambda i: (0, 0))],
        out_shape=[jax.ShapeDtypeStruct((n, d // 2), jnp.int32),
                   jax.ShapeDtypeStruct((n, TOP_K), jnp.int32),
                   jax.ShapeDtypeStruct((n, TOP_K), jnp.int32),
                   jax.ShapeDtypeStruct((n, TOP_K), f32),
                   jax.ShapeDtypeStruct((1, LANES), jnp.int32)],
        scratch_shapes=[pltpu.VMEM((1, LANES), f32)],
        compiler_params=_cparams(("arbitrary",)),
    )(x, g.reshape(1, d), r2)


def _row_copies(pos_ref, i, src, dst, sem, gather):
    out = []
    for k in range(TOP_K):
        p = pos_ref[k, i]
        if gather:
            out.append(pltpu.make_async_copy(src.at[pl.ds(p, 1)], dst.at[k, pl.ds(i, 1)], sem))
        else:
            out.append(pltpu.make_async_copy(src.at[pl.ds(i, 1)], dst.at[pl.ds(p, 1)], sem))
    return out


def _start_rows(n_rows, pos_ref, src, dst, sem, gather):
    def start(i, carry):
        for cp in _row_copies(pos_ref, i, src, dst, sem, gather):
            cp.start()
        return carry

    lax.fori_loop(0, n_rows, start, 0, unroll=ROW_DMA_UNROLL)


def _wait_rows(n_rows, pos_ref, src, dst, sem, gather):
    def wait(i, carry):
        for cp in _row_copies(pos_ref, i, src, dst, sem, gather):
            cp.wait()
        return carry

    lax.fori_loop(0, n_rows, wait, 0, unroll=ROW_DMA_UNROLL)


def _all_rows(n_rows, pos_ref, src, dst, sem, gather):
    _start_rows(n_rows, pos_ref, src, dst, sem, gather)
    _wait_rows(n_rows, pos_ref, src, dst, sem, gather)


def _moe_dispatch_kernel(nv_ref, pos_ref, h_ref, xs_ref, zero_s, sem):
    tm = zero_s.shape[0]

    @pl.when(pl.program_id(0) == 0)
    def _():
        zero_s[...] = jnp.zeros_like(zero_s)

        def fill(r, carry):
            @pl.when(nv_ref[r] < tm)
            def _():
                cp = pltpu.make_async_copy(zero_s, xs_ref.at[pl.ds(pl.multiple_of(r * tm, tm), tm)], sem)
                cp.start()
                cp.wait()
            return carry

        lax.fori_loop(0, xs_ref.shape[0] // tm, fill, 0)

    _all_rows(h_ref.shape[0], pos_ref, h_ref, xs_ref, sem, gather=False)


def _moe_dispatch(hp, pos3, tile_rows, tm):
    n, dw = hp.shape
    n_tiles, _, tr = pos3.shape
    grid_spec = pltpu.PrefetchScalarGridSpec(
        num_scalar_prefetch=1,
        grid=(n_tiles,),
        in_specs=[pl.BlockSpec((None, TOP_K, tr), lambda i, nv: (i, 0, 0), memory_space=pltpu.SMEM),
                  pl.BlockSpec((tr, dw), lambda i, nv: (i, 0))],
        out_specs=pl.BlockSpec(memory_space=pl.ANY),
        scratch_shapes=[pltpu.VMEM((tm, dw), jnp.int32), pltpu.SemaphoreType.DMA(())],
    )
    return pl.pallas_call(
        _moe_dispatch_kernel,
        grid_spec=grid_spec,
        out_shape=jax.ShapeDtypeStruct((tile_rows.shape[0] * tm, dw), jnp.int32),
        compiler_params=_cparams(("arbitrary",)),
    )(tile_rows, pos3, hp)


def _moe_up_kernel(te_ref, nv_ref, xs_ref, wg_ref, wu_ref, a_ref, x_s):
    r = pl.program_id(0)
    half = xs_ref.shape[1]

    @pl.when(nv_ref[r] > 0)
    def _():
        @pl.when(pl.program_id(1) == 0)
        def _():
            lo, hi = _unpack_halves(xs_ref[...])
            x_s[:, :half] = lo.astype(bf16)
            x_s[:, half:] = hi.astype(bf16)

        x = x_s[...]
        gate = jnp.dot(x, wg_ref[...], preferred_element_type=f32)
        up = jnp.dot(x, wu_ref[...], preferred_element_type=f32)
        a_ref[...] = (_silu(gate) * up).astype(a_ref.dtype)

    @pl.when(nv_ref[r] == 0)
    def _():
        a_ref[...] = jnp.zeros_like(a_ref)


def _moe_down_kernel(te_ref, nv_ref, a_ref, wd_ref, y_ref):
    @pl.when(nv_ref[pl.program_id(0)] > 0)
    def _():
        y_ref[...] = _pack_halves(jnp.dot(a_ref[...], wd_ref[...], preferred_element_type=f32))

    @pl.when(nv_ref[pl.program_id(0)] == 0)
    def _():
        y_ref[...] = jnp.zeros_like(y_ref)


def _moe_group(xs, tile_expert, tile_rows, wg, wu, wd, tm):
    p, dw = xs.shape
    _, d, fe = wg.shape
    tf = _tile(fe, MOE_FF_TILE)
    wspec = pl.BlockSpec((None, d, tf), lambda r, f, te, nv: (te[r], 0, f))
    act = pl.pallas_call(
        _moe_up_kernel,
        grid_spec=pltpu.PrefetchScalarGridSpec(
            num_scalar_prefetch=2,
            grid=(p // tm, fe // tf),
            in_specs=[pl.BlockSpec((tm, dw), lambda r, f, te, nv: (r, 0)), wspec, wspec],
            out_specs=pl.BlockSpec((tm, tf), lambda r, f, te, nv: (r, f)),
            scratch_shapes=[pltpu.VMEM((tm, d), bf16)],
        ),
        out_shape=jax.ShapeDtypeStruct((p, fe), bf16),
        compiler_params=_cparams(("arbitrary", "arbitrary")),
    )(tile_expert, tile_rows, xs, wg, wu)
    return pl.pallas_call(
        _moe_down_kernel,
        grid_spec=pltpu.PrefetchScalarGridSpec(
            num_scalar_prefetch=2,
            grid=(p // tm,),
            in_specs=[pl.BlockSpec((tm, fe), lambda r, te, nv: (r, 0)),
                      pl.BlockSpec((None, fe, d), lambda r, te, nv: (te[r], 0, 0))],
            out_specs=pl.BlockSpec((tm, dw), lambda r, te, nv: (r, 0)),
        ),
        out_shape=jax.ShapeDtypeStruct((p, dw), jnp.int32),
        compiler_params=_cparams(("arbitrary",)),
    )(tile_expert, tile_rows, act, wd)


def _moe_combine_kernel(pos_ref, pos_next_ref, x_ref, w_ref, g_ref, y_ref, o_ref, ybuf, sem, *, final_norm):
    tr, d = x_ref.shape
    half = d // 2
    i = pl.program_id(0)
    slot = i % 2

    @pl.when(i == 0)
    def _():
        _start_rows(tr, pos_ref, y_ref, ybuf.at[0], sem.at[0], gather=True)

    _wait_rows(tr, pos_ref, y_ref, ybuf.at[slot], sem.at[slot], gather=True)
    for r0 in range(0, tr, SUBLANES):
        rows = slice(r0, r0 + SUBLANES)
        for ii in range(r0, r0 + SUBLANES):
            for cp in _row_copies(pos_next_ref, ii, y_ref, ybuf.at[1 - slot], sem.at[1 - slot], True):
                cp.start()
        w = w_ref[rows, :]
        lo0, hi0 = _unpack_halves(ybuf[slot, 0, rows, :])
        lo1, hi1 = _unpack_halves(ybuf[slot, 1, rows, :])
        o_lo = x_ref[rows, :half] + (w[:, 0:1] * lo0 + w[:, 1:2] * lo1)
        o_hi = x_ref[rows, half:] + (w[:, 0:1] * hi0 + w[:, 1:2] * hi1)
        if final_norm:
            ms = (jnp.sum(o_lo * o_lo, axis=1, keepdims=True)
                  + jnp.sum(o_hi * o_hi, axis=1, keepdims=True)) / d
            inv = lax.rsqrt(ms + EPS)
            o_lo = o_lo * inv * g_ref[:, :half]
            o_hi = o_hi * inv * g_ref[:, half:]
        o_ref[rows, :half] = o_lo
        o_ref[rows, half:] = o_hi

    @pl.when(i == pl.num_programs(0) - 1)
    def _():
        _wait_rows(tr, pos_next_ref, y_ref, ybuf.at[1 - slot], sem.at[1 - slot], gather=True)


def _moe_combine(x, w, y, pos3, gain, final_norm, row_start, n_rows):
    d = x.shape[1]
    tr = pos3.shape[2]
    off = row_start // tr
    last = off + n_rows // tr - 1
    return pl.pallas_call(
        functools.partial(_moe_combine_kernel, final_norm=final_norm),
        grid=(n_rows // tr,),
        in_specs=[pl.BlockSpec((None, TOP_K, tr), lambda i: (i + off, 0, 0), memory_space=pltpu.SMEM),
                  pl.BlockSpec((None, TOP_K, tr), lambda i: (jnp.minimum(i + off + 1, last), 0, 0),
                               memory_space=pltpu.SMEM),
                  pl.BlockSpec((tr, d), lambda i: (i + off, 0)),
                  pl.BlockSpec((tr, TOP_K), lambda i: (i + off, 0)),
                  pl.BlockSpec((1, d), lambda i: (0, 0)),
                  pl.BlockSpec(memory_space=pl.ANY)],
        out_specs=pl.BlockSpec((tr, d), lambda i: (i, 0)),
        out_shape=jax.ShapeDtypeStruct((n_rows, d), f32),
        scratch_shapes=[pltpu.VMEM((2, TOP_K, tr, d // 2), jnp.int32), pltpu.SemaphoreType.DMA((2,))],
        compiler_params=_cparams(("arbitrary",)),
    )(pos3, pos3, x, w, gain.reshape(1, d), y)


def _moe_layer(x, g, router, wg, wu, wd, final_gain, parts):
    n, d = x.shape
    n_exp = router.shape[1]
    tm = MOE_ROW_TILE
    hp, eid, rank, w, cnt = _router(x, g, router)
    cnt = cnt[0, :n_exp]
    padded = (cnt + tm - 1) // tm * tm
    seg_end = jnp.cumsum(padded)
    seg_start = seg_end - padded
    pos = seg_start[eid] + rank
    n_rows_padded = TOP_K * n + n_exp * tm
    tile0 = jnp.arange(n_rows_padded // tm, dtype=jnp.int32) * tm
    tile_expert = jnp.minimum(jnp.sum(tile0[:, None] >= seg_end[None, :], axis=1), n_exp - 1).astype(jnp.int32)
    tile_rows = jnp.clip(cnt[tile_expert] - (tile0 - seg_start[tile_expert]), 0, tm).astype(jnp.int32)
    def by_tile(tr):
        return pos.reshape(n // tr, tr, TOP_K).transpose(0, 2, 1)

    pos3 = by_tile(_tile(n, MOE_COMBINE_TILE))
    xs = _moe_dispatch(hp, by_tile(_tile(n, MOE_DISPATCH_TILE)), tile_rows, tm)
    y = _moe_group(xs, tile_expert, tile_rows, wg, wu, wd, tm)
    if final_gain is None:
        return _moe_combine(x, w, y, pos3, g, False, 0, n)
    return [_moe_combine(x, w, y, pos3, final_gain, True, start, rows) for start, rows in parts]


def kernel(x_prompt, x_sample, norm_mix, norm_ffn, norm_final, pool_w, pool_scale, hgrn_w_in, hgrn_lb,
           hgrn_norm, hgrn_w_out, ffn_w_gate, ffn_w_up, ffn_w_down, moe_router, moe_w_gate, moe_w_up,
           moe_w_down):
    bp, sp, d = x_prompt.shape
    bs, ss, _ = x_sample.shape
    n_prompt = bp * sp
    n = n_prompt + bs * ss
    seq = (n_prompt, sp, ss)
    depth = norm_mix.shape[0]
    chunk = _tile(min(sp, ss), 128)

    parts = ((0, n_prompt), (n_prompt, bs * ss))
    streams = (x_prompt.reshape(n_prompt, d), x_sample.reshape(bs * ss, d))
    x = None if depth else jnp.concatenate(streams, axis=0)
    outs = None

    for i in range(depth):
        j = i // 2
        if i % 2 == 0:
            xa, xb = streams if x is None else (x[:n_prompt], x[n_prompt:])
            x, h = _pool_layer(xa, xb, norm_mix[i], pool_w[j].astype(bf16), pool_scale[j], norm_ffn[i], seq)
            nxt = i + 1 < depth
            later = [(hgrn_w_in, j, norm_mix[i + 1])] + [
                (w, j, None) for w in (hgrn_w_out, moe_w_gate, moe_w_up, moe_w_down)] if nxt else []
            act, (w_down, *cast) = _glu_up(h, ffn_w_gate[j].astype(bf16), ffn_w_up[j].astype(bf16),
                                           [(ffn_w_down, j, None)] + later)
            (x, *norm_rows), _ = _down_res(act, w_down, x, 1024, 256, nxt)
            rounded = dict(zip(("w_in", "w_out", "moe_gate", "moe_up", "moe_down"), cast))
        else:
            x16, sq = norm_rows
            w_in = rounded["w_in"]
            q = _proj_section(_proj_silu_heads_kernel, x16, sq, w_in, 0, True)
            forget = functools.partial(_proj_forget_kernel, layer=i)
            kf, gf = _proj_section(forget, x16, sq, w_in, 1, True, 2, hgrn_lb)
            kb, gb = _proj_section(forget, x16, sq, w_in, 2, True, 2, hgrn_lb)
            v = _proj_section(_proj_heads_kernel, x16, sq, w_in, 3, True)
            gate = _proj_section(_proj_silu_kernel, x16, sq, w_in, 4, False)
            s_bw = _scan_bw_states(kb, gb, v, seq, chunk)
            og = _scan(q, kf, kb, gf, gb, v, s_bw, gate, hgrn_norm[j], seq, chunk)
            (x,), _ = _down_res(og, rounded["w_out"], x, 1024, 512)
            last = i == depth - 1
            res = _moe_layer(x, norm_ffn[i], moe_router[j], rounded["moe_gate"], rounded["moe_up"],
                             rounded["moe_down"], norm_final if last else None, parts)
            x, outs = (None, res) if last else (res, None)

    if outs is None:
        outs = [_rmsnorm(x, norm_final, f32, start, rows) for start, rows in parts]
    return (outs[0].reshape(bp, sp, d), outs[1].reshape(bs, ss, d))
```

```python
import functools

import jax
import jax.numpy as jnp
from jax import lax
from jax.experimental import pallas as pl
from jax.experimental.pallas import tpu as pltpu

EPS = 1e-6
POOL_WINDOWS = (2, 4, 8, 16)
POOL_HALO = 8
POOL_PAD = 16
POOL_BLOCK = 128
HEAD_DIM = 128
TOP_K = 2
V7X_VMEM_LIMIT = 56 * 1024 * 1024
LANES = 128
SUBLANES = 8
MOE_ROW_TILE = 512
MOE_FF_TILE = 512
MOE_DISPATCH_TILE = 512
MOE_COMBINE_TILE = 256
ROW_DMA_UNROLL = 8
EPILOGUE_COLS = 256
BF16_SUBLANES = 16

bf16 = jnp.bfloat16
f32 = jnp.float32


def _cparams(sem):
    return pltpu.CompilerParams(dimension_semantics=sem, vmem_limit_bytes=V7X_VMEM_LIMIT)


def _tile(n, pref):
    t = min(n, pref)
    while n % t:
        t //= 2
    return t


def _rms(x, g):
    ms = jnp.mean(x * x, axis=-1, keepdims=True)
    return x * lax.rsqrt(ms + EPS) * g


def _silu(x):
    return x * jax.nn.sigmoid(x)


def _seq_flags(tok, n_tok, n_prompt, seq_p, seq_s):
    in_p = tok < n_prompt
    rel = jnp.where(in_p, tok, tok - n_prompt)
    slen = jnp.where(in_p, seq_p, seq_s)
    is_start = (rel % slen) == 0
    is_end = ((rel + n_tok) % slen) == 0
    return is_start, is_end


def _rmsnorm_kernel(x_ref, g_ref, o_ref):
    o_ref[...] = _rms(x_ref[...], g_ref[...]).astype(o_ref.dtype)


def _rmsnorm(x, g, out_dtype, row_start=0, n_rows=None):
    n, d = x.shape
    n_rows = n if n_rows is None else n_rows
    tr = _tile(n_rows, 512)
    off = row_start // tr
    return pl.pallas_call(
        _rmsnorm_kernel,
        grid=(n_rows // tr,),
        in_specs=[pl.BlockSpec((tr, d), lambda i: (i + off, 0)),
                  pl.BlockSpec((1, d), lambda i: (0, 0))],
        out_specs=pl.BlockSpec((tr, d), lambda i: (i, 0)),
        out_shape=jax.ShapeDtypeStruct((n_rows, d), out_dtype),
        compiler_params=_cparams(("parallel",)),
    )(x, g.reshape(1, d))


def _pool_kernel(ap_ref, a_ref, an_ref, bp_ref, b_ref, bn_ref, gmix_ref, w_ref, scale_ref, gffn_ref,
                 x1_ref, h1_ref, hi_s, lo_s, *, seq):
    t, d = a_ref.shape
    dg = d // len(POOL_WINDOWS)
    tok = pl.program_id(0) * t
    is_start, is_end = _seq_flags(tok, t, *seq)
    in_a = tok < seq[0]
    g = gmix_ref[...]
    x = jnp.where(in_a, a_ref[...], b_ref[...])
    h = _rms(x, g)
    h_prev = jnp.where(is_start, 0.0, _rms(jnp.where(in_a, ap_ref[...], bp_ref[...]), g))
    h_next = jnp.where(is_end, 0.0, _rms(jnp.where(in_a, an_ref[...], bn_ref[...]), g))
    zeros = jnp.zeros_like(h_prev)

    def put(r0, val):
        hi = val.astype(bf16)
        hi_s[r0:r0 + val.shape[0], :] = hi
        lo_s[r0:r0 + val.shape[0], :] = (val - hi.astype(f32)).astype(bf16)

    put(0, jnp.concatenate([zeros, h_prev], axis=0))
    put(POOL_PAD, h)
    put(POOL_PAD + t, jnp.concatenate([h_next, zeros], axis=0))
    tail = hi_s.shape[0] - (2 * POOL_PAD + t)
    hi_s[2 * POOL_PAD + t:, :] = jnp.zeros((tail, d), bf16)
    lo_s[2 * POOL_PAD + t:, :] = jnp.zeros((tail, d), bf16)

    rr = lax.broadcasted_iota(jnp.int32, (POOL_BLOCK, 2 * POOL_BLOCK), 0)
    cc = lax.broadcasted_iota(jnp.int32, (POOL_BLOCK, 2 * POOL_BLOCK), 1)
    r = lax.broadcasted_iota(jnp.int32, (t, 1), 0)
    for gi, w in enumerate(POOL_WINDOWS):
        c0 = gi * dg
        half = w // 2
        band = ((cc >= POOL_PAD + rr - half) & (cc < POOL_PAD + rr + half)).astype(bf16)
        sums = []
        for r0 in range(0, t, POOL_BLOCK):
            sums.append(jnp.dot(band, hi_s[r0:r0 + 2 * POOL_BLOCK, c0:c0 + dg], preferred_element_type=f32)
                        + jnp.dot(band, lo_s[r0:r0 + 2 * POOL_BLOCK, c0:c0 + dg], preferred_element_type=f32))
        s = jnp.concatenate(sums, axis=0)
        lo = jnp.where(is_start, jnp.maximum(r - half, 0), r - half)
        hi = jnp.where(is_end, jnp.minimum(r + half, t), r + half)
        cnt = (hi - lo).astype(f32)
        dlt = s / cnt - h[:, c0:c0 + dg]
        y = jnp.dot(dlt.astype(bf16), w_ref[gi], preferred_element_type=f32)
        x1_ref[:, c0:c0 + dg] = x[:, c0:c0 + dg] + y * scale_ref[:, c0:c0 + dg]
    h1_ref[...] = _rms(x1_ref[...], gffn_ref[...]).astype(h1_ref.dtype)


def _stream_specs(rows, first_tile, t, d):
    hb = t // POOL_HALO
    last_halo = rows // POOL_HALO - 1
    last_tile = rows // t - 1
    return [
        pl.BlockSpec((POOL_HALO, d), lambda i: (jnp.clip((i - first_tile) * hb - 1, 0, last_halo), 0)),
        pl.BlockSpec((t, d), lambda i: (jnp.clip(i - first_tile, 0, last_tile), 0)),
        pl.BlockSpec((POOL_HALO, d), lambda i: (jnp.clip((i - first_tile + 1) * hb, 0, last_halo), 0)),
    ]


def _pool_layer(xa, xb, g_mix, w_grp, scale, g_ffn, seq):
    d = xa.shape[1]
    n = xa.shape[0] + xb.shape[0]
    t = _tile(min(seq[1], seq[2]), 256)
    row = lambda i: (0, 0)
    return pl.pallas_call(
        functools.partial(_pool_kernel, seq=seq),
        grid=(n // t,),
        in_specs=_stream_specs(xa.shape[0], 0, t, d) + _stream_specs(xb.shape[0], xa.shape[0] // t, t, d) + [
            pl.BlockSpec((1, d), row),
            pl.BlockSpec(w_grp.shape, lambda i: (0, 0, 0)),
            pl.BlockSpec((1, d), row),
            pl.BlockSpec((1, d), row),
        ],
        out_specs=[pl.BlockSpec((t, d), lambda i: (i, 0)),
                   pl.BlockSpec((t, d), lambda i: (i, 0))],
        out_shape=[jax.ShapeDtypeStruct((n, d), f32), jax.ShapeDtypeStruct((n, d), bf16)],
        scratch_shapes=[pltpu.VMEM((t + POOL_BLOCK, d), bf16)] * 2,
        compiler_params=_cparams(("parallel",)),
    )(xa, xa, xa, xb, xb, xb, g_mix.reshape(1, d), w_grp, scale.reshape(1, d), g_ffn.reshape(1, d))


def _matmul_call_with_casts(kernel, grid, in_specs, out_specs, out_shapes, args, to_cast):
    n_steps = grid[0] * grid[1]
    plans = []
    for stacked, layer, scale in to_cast:
        cols = stacked.shape[-1]
        rows = stacked[0].size // cols
        steps = 1
        while steps * 2 <= n_steps and rows % (steps * 2 * BF16_SUBLANES) == 0:
            steps *= 2
        plans.append((rows, cols, steps, rows // steps, layer))
    scaled = [k for k, (_, _, scale) in enumerate(to_cast) if scale is not None]
    n_in, n_side, n_scale, n_out = len(in_specs), len(to_cast), len(scaled), len(out_specs)

    def step_block(steps):
        return lambda i, j: jnp.minimum(i * grid[1] + j, steps - 1)

    def src_spec(cols, steps, rb, layer):
        blk = step_block(steps)
        return pl.BlockSpec((None, rb, cols), lambda i, j: (layer, blk(i, j), 0))

    def dst_spec(cols, steps, rb):
        blk = step_block(steps)
        return pl.BlockSpec((rb, cols), lambda i, j: (blk(i, j), 0))

    def body(*refs):
        first_out = n_in + n_side + n_scale
        kernel(*refs[:n_in], *refs[first_out:first_out + n_out])
        step = pl.program_id(0) * grid[1] + pl.program_id(1)
        for k, plan in enumerate(plans):
            scale_ref = refs[n_in + n_side + scaled.index(k)] if k in scaled else None

            @pl.when(step < plan[2])
            def _(src=refs[n_in + k], dst=refs[first_out + n_out + k], scale_ref=scale_ref):
                val = src[...] if scale_ref is None else src[...] * scale_ref[...]
                dst[...] = val.astype(bf16)

    outs = pl.pallas_call(
        body,
        grid=grid,
        in_specs=list(in_specs) + [src_spec(c, s, rb, layer) for _, c, s, rb, layer in plans]
        + [dst_spec(1, plans[k][2], plans[k][3]) for k in scaled],
        out_specs=list(out_specs) + [dst_spec(c, s, rb) for _, c, s, rb, _ in plans],
        out_shape=list(out_shapes) + [jax.ShapeDtypeStruct((r, c), bf16) for r, c, _, _, _ in plans],
        compiler_params=_cparams(("arbitrary", "arbitrary")),
    )(*args,
      *[stacked.reshape(stacked.shape[0], r, c) for (stacked, _, _), (r, c, _, _, _) in zip(to_cast, plans)],
      *[to_cast[k][2].reshape(-1, 1) for k in scaled])
    return outs[:n_out], [o.reshape(stacked.shape[1:]) for o, (stacked, _, _) in zip(outs[n_out:], to_cast)]


def _glu_up_kernel(a_ref, wg_ref, wu_ref, o_ref):
    a = a_ref[...]
    gate = jnp.dot(a, wg_ref[...], preferred_element_type=f32)
    up = jnp.dot(a, wu_ref[...], preferred_element_type=f32)
    o_ref[...] = (_silu(gate) * up).astype(o_ref.dtype)


def _glu_up(a, wg, wu, to_cast=()):
    m, k = a.shape
    f = wg.shape[1]
    tm = _tile(m, 1024)
    tn = _tile(f, 512)
    wspec = pl.BlockSpec((k, tn), lambda i, j: (0, j))
    (act,), rounded = _matmul_call_with_casts(
        _glu_up_kernel, (m // tm, f // tn),
        [pl.BlockSpec((tm, k), lambda i, j: (i, 0)), wspec, wspec],
        [pl.BlockSpec((tm, tn), lambda i, j: (i, j))],
        [jax.ShapeDtypeStruct((m, f), bf16)], (a, wg, wu), to_cast)
    return act, rounded


def _down_kernel(a_ref, w_ref, res_ref, o_ref, *norm_refs):
    o = res_ref[...] + jnp.dot(a_ref[...], w_ref[...], preferred_element_type=f32)
    o_ref[...] = o
    if not norm_refs:
        return
    o16_ref, sq_ref = norm_refs
    o16_ref[...] = o.astype(o16_ref.dtype)
    sq = o * o
    part = sq[:, 0:LANES]
    for c0 in range(LANES, sq.shape[1], LANES):
        part = part + sq[:, c0:c0 + LANES]

    @pl.when(pl.program_id(1) == 0)
    def _():
        sq_ref[...] = part

    @pl.when(pl.program_id(1) > 0)
    def _():
        sq_ref[...] += part


def _down_res(a, w, res, tm, tn, norm_rows=False, to_cast=()):
    m, kk = a.shape
    n = w.shape[1]
    tm = _tile(m, tm)
    tn = _tile(n, tn)
    tile = pl.BlockSpec((tm, tn), lambda i, j: (i, j))
    out_specs, out_shapes = [tile], [jax.ShapeDtypeStruct((m, n), f32)]
    if norm_rows:
        out_specs += [tile, pl.BlockSpec((tm, LANES), lambda i, j: (i, 0))]
        out_shapes += [jax.ShapeDtypeStruct((m, n), bf16), jax.ShapeDtypeStruct((m, LANES), f32)]
    return _matmul_call_with_casts(
        _down_kernel, (m // tm, n // tn),
        [pl.BlockSpec((tm, kk), lambda i, j: (i, 0)), pl.BlockSpec((kk, tn), lambda i, j: (0, j)), tile],
        out_specs, out_shapes, (a, w, res), to_cast)


def _column_blocks(a_ref, sq_ref, w_ref):
    a = a_ref[...]
    inv = lax.rsqrt(jnp.sum(sq_ref[...], axis=1, keepdims=True) / a.shape[1] + EPS)
    tn = w_ref.shape[1]
    cb = min(tn, EPILOGUE_COLS)
    for c0 in range(0, tn, cb):
        yield c0, inv * jnp.dot(a, w_ref[:, c0:c0 + cb], preferred_element_type=f32)


def _store_head_block(o_ref, c0, val):
    for hh in range(val.shape[1] // HEAD_DIM):
        o_ref[c0 // HEAD_DIM + hh] = val[:, hh * HEAD_DIM:(hh + 1) * HEAD_DIM].astype(o_ref.dtype)


def _proj_silu_heads_kernel(a_ref, sq_ref, w_ref, o_ref):
    for c0, r in _column_blocks(a_ref, sq_ref, w_ref):
        _store_head_block(o_ref, c0, _silu(r))


def _proj_heads_kernel(a_ref, sq_ref, w_ref, o_ref):
    for c0, r in _column_blocks(a_ref, sq_ref, w_ref):
        _store_head_block(o_ref, c0, r)


def _proj_silu_kernel(a_ref, sq_ref, w_ref, o_ref):
    for c0, r in _column_blocks(a_ref, sq_ref, w_ref):
        o_ref[:, c0:c0 + r.shape[1]] = _silu(r).astype(o_ref.dtype)


def _proj_forget_kernel(a_ref, sq_ref, w_ref, lb_ref, k_ref, g_ref, *, layer):
    logits = lb_ref[...]
    ex = jnp.exp(logits - jnp.max(logits, axis=0, keepdims=True))
    sm = ex / jnp.sum(ex, axis=0, keepdims=True)
    lb_all = jnp.sum(sm[0:layer + 1], axis=0, keepdims=True) - sm[0:1]
    for c0, r in _column_blocks(a_ref, sq_ref, w_ref):
        lb = lb_all[:, c0:c0 + r.shape[1]]
        f = lb + (1.0 - lb) * jax.nn.sigmoid(r)
        _store_head_block(k_ref, c0, 1.0 - f)
        _store_head_block(g_ref, c0, jnp.log2(f))


def _proj_section(kernel, a, sq, w, section, out_heads, n_out=1, extra=None):
    m, k = a.shape
    d = k
    tm = _tile(m, 1024)
    tn = _tile(d, 1024)
    nb = d // tn
    in_specs = [pl.BlockSpec((tm, k), lambda i, j: (i, 0)),
                pl.BlockSpec((tm, LANES), lambda i, j: (i, 0)),
                pl.BlockSpec((k, tn), lambda i, j: (0, section * nb + j))]
    args = [a, sq, w]
    if extra is not None:
        in_specs.append(pl.BlockSpec((extra.shape[0], tn), lambda i, j: (0, j)))
        args.append(extra)
    if out_heads:
        hpb = tn // HEAD_DIM
        ospec = pl.BlockSpec((hpb, tm, HEAD_DIM), lambda i, j: (j, i, 0))
        oshape = jax.ShapeDtypeStruct((d // HEAD_DIM, m, HEAD_DIM), bf16)
    else:
        ospec = pl.BlockSpec((tm, tn), lambda i, j: (i, j))
        oshape = jax.ShapeDtypeStruct((m, d), bf16)
    out = pl.pallas_call(
        kernel,
        grid=(m // tm, nb),
        in_specs=in_specs,
        out_specs=[ospec] * n_out if n_out > 1 else ospec,
        out_shape=[oshape] * n_out if n_out > 1 else oshape,
        compiler_params=_cparams(("parallel", "arbitrary")),
    )(*args)
    return out


MAX_DIRECT_LOG2_DECAY = 100.0
HEAD_UNROLL = 8


def _dot_nt(a, b):
    return lax.dot_general(a, b, (((1,), (1,)), ((), ())), preferred_element_type=f32)


def _dot_tn(a, b):
    return lax.dot_general(a, b, (((0,), (0,)), ((), ())), preferred_element_type=f32)


def _mid_ref(p, level, offset):
    c, w = p.shape
    blk = 2 << level
    p3 = p.reshape(c // blk, blk, w)
    return jnp.broadcast_to(p3[:, offset:offset + 1, :], p3.shape).reshape(c, w)


def _neg_abs_exponents(gf, gb, b, x, level, rmod4):
    if level == 0:
        odd = (rmod4 & 1) == 1
        return jnp.where(odd, gf, 0.0), jnp.where(odd, 0.0, gb)
    if level == 1:
        c = gf.shape[0]
        gf_up = pltpu.roll(gf, c - 1, 0)
        gf_dn = pltpu.roll(gf, 1, 0)
        gb_up = pltpu.roll(gb, c - 1, 0)
        gb_dn = pltpu.roll(gb, 1, 0)
        ef = jnp.where(rmod4 == 0, gf_up, jnp.where(rmod4 == 1, 0.0, jnp.where(rmod4 == 2, gf, gf + gf_dn)))
        eb = jnp.where(rmod4 == 0, gb + gb_up, jnp.where(rmod4 == 1, gb, jnp.where(rmod4 == 2, 0.0, gb_dn)))
        return ef, eb
    half = 1 << level
    ef = b - _mid_ref(b, level, half - 1)
    eb = x - _mid_ref(x, level, half)
    return -jnp.abs(ef), -jnp.abs(eb)


def _chunk_prefix(tri_incl, tri_excl, gf16, gb16):
    b = jnp.dot(tri_incl, gf16, preferred_element_type=f32)
    x = jnp.dot(tri_excl, gb16, preferred_element_type=f32)
    return b, x


def _scan_bw_kernel(kb_ref, gb_ref, v_ref, s_ref, st_ref, *, seq, n_chunks):
    n_heads, c, _ = kb_ref.shape
    chunk = n_chunks - 1 - pl.program_id(0)
    _, is_end = _seq_flags(chunk * c, c, *seq)

    @pl.when(is_end)
    def _():
        st_ref[...] = jnp.zeros_like(st_ref)

    row = lax.broadcasted_iota(jnp.int32, (c, c), 0)
    col = lax.broadcasted_iota(jnp.int32, (c, c), 1)
    tri_excl = (col < row).astype(bf16)

    def head(h, carry):
        gb16 = gb_ref[h]
        x = jnp.dot(tri_excl, gb16, preferred_element_type=f32)
        tot = x[c - 1:c, :] + gb16[c - 1:c, :].astype(f32)
        st = st_ref[h]
        s_ref[h] = st.astype(s_ref.dtype)
        kx = (kb_ref[h].astype(f32) * jnp.exp2(x)).astype(bf16)
        st_ref[h] = st * jnp.exp2(tot) + _dot_tn(v_ref[h], kx)
        return carry

    lax.fori_loop(0, n_heads, head, 0, unroll=4 * HEAD_UNROLL)


def _scan_bw_states(kb, gb, v, seq, c):
    n_heads, n, dk = kb.shape
    n_chunks = n // c
    spec = pl.BlockSpec((n_heads, c, dk), lambda i: (0, n_chunks - 1 - i, 0))
    return pl.pallas_call(
        functools.partial(_scan_bw_kernel, seq=seq, n_chunks=n_chunks),
        grid=(n_chunks,),
        in_specs=[spec, spec, spec],
        out_specs=pl.BlockSpec((None, n_heads, dk, dk), lambda i: (n_chunks - 1 - i, 0, 0, 0)),
        out_shape=jax.ShapeDtypeStruct((n_chunks, n_heads, dk, dk), bf16),
        scratch_shapes=[pltpu.VMEM((n_heads, dk, dk), f32)],
        compiler_params=_cparams(("arbitrary",)),
    )(kb, gb, v)


def _scan_kernel(q_ref, kf_ref, kb_ref, gf_ref, gb_ref, v_ref, sb_ref, tril_ref, triu_ref, gate_ref,
                 gain_ref, out_ref, st_ref, o_s, qf_s, kf_s, qb_s, kb_s, qin_s, kx_s, sc_s, dec_s, tot_s,
                 *, seq):
    n_heads, c, dk = q_ref.shape
    n_levels = c.bit_length() - 1
    mid = c // 2
    is_start, _ = _seq_flags(pl.program_id(0) * c, c, *seq)

    @pl.when(is_start)
    def _():
        st_ref[...] = jnp.zeros_like(st_ref)

    row = lax.broadcasted_iota(jnp.int32, (c, c), 0)
    col = lax.broadcasted_iota(jnp.int32, (c, c), 1)
    tri_incl = (col <= row).astype(bf16)
    tri_excl = (col < row).astype(bf16)

    def finish(h, kf_dec, b_last, q_in, scores, diag):
        v = v_ref[h]
        st_f = st_ref[h]
        st_cat = jnp.concatenate([st_f.astype(bf16), sb_ref[h]], axis=1)
        o = _dot_nt(q_in, st_cat) + jnp.dot(scores.astype(bf16), v, preferred_element_type=f32)
        o_s[h] = o if diag is None else o + diag * v.astype(f32)
        st_ref[h] = st_f * jnp.exp2(b_last) + _dot_tn(v, kf_dec.astype(bf16))

    def direct_operands(h, carry):
        q = q_ref[h].astype(f32)
        gb16 = gb_ref[h]
        b, x = _chunk_prefix(tri_incl, tri_excl, gf_ref[h], gb16)
        b_mid = b[mid - 1:mid, :]
        x_mid = x[mid:mid + 1, :]
        db = b - b_mid
        dx = x - x_mid
        q_f = q * jnp.exp2(db)
        k_f = kf_ref[h].astype(f32) * jnp.exp2(-db)
        q_b = q * jnp.exp2(-dx)
        k_b = kb_ref[h].astype(f32) * jnp.exp2(dx)
        b_last = b[c - 1:c, :]
        x_tot = x[c - 1:c, :] + gb16[c - 1:c, :].astype(f32)
        qf_s[h] = q_f.astype(bf16)
        kf_s[h] = k_f.astype(bf16)
        qb_s[h] = q_b.astype(bf16)
        kb_s[h] = k_b.astype(bf16)
        qin_s[h] = jnp.concatenate([q_f * jnp.exp2(b_mid), q_b * jnp.exp2(x_tot - x_mid)], axis=1).astype(bf16)
        kx_s[h] = (k_f * jnp.exp2(b_last - b_mid)).astype(bf16)
        dec_s[h] = jnp.broadcast_to(jnp.exp2(b_last), dec_s.shape[1:])
        tot_s[h] = jnp.broadcast_to(jnp.minimum(b_last, x_tot), tot_s.shape[1:])
        return carry

    def direct_scores(h, carry):
        sc_s[h] = (jnp.where(tril_ref[...] != 0.0, _dot_nt(qf_s[h], kf_s[h]), 0.0)
                   + jnp.where(triu_ref[...] != 0.0, _dot_nt(qb_s[h], kb_s[h]), 0.0)).astype(bf16)
        return carry

    def direct_output(h, carry):
        v = v_ref[h]
        st_f = st_ref[h]
        st_cat = jnp.concatenate([st_f.astype(bf16), sb_ref[h]], axis=1)
        o_s[h] = _dot_nt(qin_s[h], st_cat) + jnp.dot(sc_s[h], v, preferred_element_type=f32)
        st_ref[h] = st_f * dec_s[h][0:1, :] + _dot_tn(v, kx_s[h])
        return carry

    def head_levelled(h, carry):
        rtok = lax.broadcasted_iota(jnp.int32, (c, dk), 0)
        rmod4 = rtok & 3
        q = q_ref[h].astype(f32)
        kf = kf_ref[h].astype(f32)
        kb = kb_ref[h].astype(f32)
        gf16 = gf_ref[h]
        gb16 = gb_ref[h]
        gf = gf16.astype(f32)
        gb = gb16.astype(f32)
        b, x = _chunk_prefix(tri_incl, tri_excl, gf16, gb16)
        scores = jnp.zeros((c, c), f32)
        for l in range(n_levels):
            ef, eb = _neg_abs_exponents(gf, gb, b, x, l, rmod4)
            af = jnp.exp2(ef)
            ab = jnp.exp2(eb)
            up = ((rtok >> l) & 1) == 1
            lhs = jnp.concatenate([jnp.where(up, q * af, 0.0), jnp.where(up, 0.0, q * ab)], axis=1)
            rhs = jnp.concatenate([jnp.where(up, 0.0, kf * af), jnp.where(up, kb * ab, 0.0)], axis=1)
            p = _dot_nt(lhs.astype(bf16), rhs.astype(bf16))
            if l < n_levels - 1:
                p = jnp.where((row >> (l + 1)) == (col >> (l + 1)), p, 0.0)
            scores = scores + p
        b_last = b[c - 1:c, :]
        x_tot = x[c - 1:c, :] + gb[c - 1:c, :]
        q_in = jnp.concatenate([q * jnp.exp2(b), q * jnp.exp2(x_tot - x)], axis=1).astype(bf16)
        diag = jnp.sum(q * (kf + kb), axis=1, keepdims=True)
        finish(h, kf * jnp.exp2(b_last - b), b_last, q_in, scores, diag)
        return carry

    lax.fori_loop(0, n_heads, direct_operands, 0, unroll=4 * HEAD_UNROLL)
    direct_ok = jnp.min(tot_s[...]) >= -MAX_DIRECT_LOG2_DECAY

    @pl.when(direct_ok)
    def _():
        lax.fori_loop(0, n_heads, direct_scores, 0, unroll=4 * HEAD_UNROLL)
        lax.fori_loop(0, n_heads, direct_output, 0, unroll=4 * HEAD_UNROLL)

    @pl.when(jnp.logical_not(direct_ok))
    def _():
        lax.fori_loop(0, n_heads, head_levelled, 0)

    sq = jnp.zeros((c, dk), f32)
    for hh in range(n_heads):
        o = o_s[hh]
        sq = sq + o * o
    inv = lax.rsqrt(jnp.sum(sq, axis=1, keepdims=True) / (n_heads * dk) + EPS)
    for hh in range(n_heads):
        cols = slice(hh * dk, (hh + 1) * dk)
        out_ref[:, cols] = (o_s[hh] * inv * gain_ref[:, cols] * gate_ref[:, cols].astype(f32)
                            ).astype(out_ref.dtype)


def _scan(q, kf, kb, gf, gb, v, s_bw, gate, gain, seq, c):
    n_heads, n, dk = q.shape
    d = n_heads * dk
    n_chunks = n // c
    spec = pl.BlockSpec((n_heads, c, dk), lambda i: (0, i, 0))
    tri = pl.BlockSpec((c, c), lambda i: (0, 0))
    tril = jnp.tril(jnp.ones((c, c), f32))
    return pl.pallas_call(
        functools.partial(_scan_kernel, seq=seq),
        grid=(n_chunks,),
        in_specs=[spec] * 6 + [pl.BlockSpec((None, n_heads, dk, dk), lambda i: (i, 0, 0, 0)), tri, tri,
                               pl.BlockSpec((c, d), lambda i: (i, 0)),
                               pl.BlockSpec((1, d), lambda i: (0, 0))],
        out_specs=pl.BlockSpec((c, d), lambda i: (i, 0)),
        out_shape=jax.ShapeDtypeStruct((n, d), bf16),
        scratch_shapes=[pltpu.VMEM((n_heads, dk, dk), f32), pltpu.VMEM((n_heads, c, dk), f32)]
        + [pltpu.VMEM((n_heads, c, dk), bf16)] * 4
        + [pltpu.VMEM((n_heads, c, 2 * dk), bf16), pltpu.VMEM((n_heads, c, dk), bf16),
           pltpu.VMEM((n_heads, c, c), bf16)] + [pltpu.VMEM((n_heads, 8, dk), f32)] * 2,
        compiler_params=_cparams(("arbitrary",)),
    )(q, kf, kb, gf, gb, v, s_bw, tril, tril.T, gate, gain.reshape(1, d))


def _pack_halves(x):
    half = x.shape[1] // 2
    bits = lax.bitcast_convert_type(x.astype(bf16).astype(f32), jnp.int32)
    return lax.shift_right_logical(bits[:, :half], 16) | (bits[:, half:] & jnp.int32(-65536))


def _unpack_halves(words):
    lo = lax.bitcast_convert_type(lax.shift_left(words, 16), f32)
    hi = lax.bitcast_convert_type(words & jnp.int32(-65536), f32)
    return lo, hi


def _router_kernel(x_ref, g_ref, r2_ref, h_ref, e_ref, rank_ref, w_ref, cnt_ref, run_ref,
                   *, n_experts):
    @pl.when(pl.program_id(0) == 0)
    def _():
        run_ref[...] = jnp.zeros_like(run_ref)

    h = _rms(x_ref[...], g_ref[...])
    h_ref[...] = _pack_halves(h)
    hi = h.astype(bf16)
    lo = (h - hi.astype(f32)).astype(bf16)
    tr = h.shape[0]
    prod = jnp.dot(jnp.concatenate([hi, lo], axis=0), r2_ref[...], preferred_element_type=f32)
    logits = prod[:tr, :LANES] + prod[:tr, LANES:] + prod[tr:, :LANES]
    lane = lax.broadcasted_iota(jnp.int32, logits.shape, 1)
    neg = jnp.float32(-jnp.inf)
    logits = jnp.where(lane < n_experts, logits, neg)
    m1 = jnp.max(logits, axis=1, keepdims=True)
    i1 = jnp.min(jnp.where(logits == m1, lane, LANES), axis=1, keepdims=True)
    rest = jnp.where(lane == i1, neg, logits)
    m2 = jnp.max(rest, axis=1, keepdims=True)
    i2 = jnp.min(jnp.where(rest == m2, lane, LANES), axis=1, keepdims=True)
    e2 = jnp.exp(m2 - m1)
    w_ref[...] = jnp.concatenate([1.0 / (1.0 + e2), e2 / (1.0 + e2)], axis=1)
    e_ref[...] = jnp.concatenate([i1, i2], axis=1)
    member = jnp.where((lane == i1) | (lane == i2), 1.0, 0.0)
    row = lax.broadcasted_iota(jnp.int32, (tr, tr), 0)
    col = lax.broadcasted_iota(jnp.int32, (tr, tr), 1)
    before = jnp.dot((col < row).astype(bf16), member.astype(bf16), preferred_element_type=f32)
    before = before + run_ref[...]
    r1 = jnp.sum(jnp.where(lane == i1, before, 0.0), axis=1, keepdims=True)
    r2 = jnp.sum(jnp.where(lane == i2, before, 0.0), axis=1, keepdims=True)
    rank_ref[...] = jnp.concatenate([r1, r2], axis=1).astype(jnp.int32)
    run = run_ref[...] + jnp.sum(member, axis=0, keepdims=True)
    run_ref[...] = run
    cnt_ref[...] = run.astype(jnp.int32)


def _router(x, g, router):
    n, d = x.shape
    e = router.shape[1]
    rpad = jnp.zeros((d, LANES), f32).at[:, :e].set(router)
    rhi = rpad.astype(bf16)
    r2 = jnp.concatenate([rhi, (rpad - rhi.astype(f32)).astype(bf16)], axis=1)
    tr = _tile(n, 512)
    pair = pl.BlockSpec((tr, TOP_K), lambda i: (i, 0))
    return pl.pallas_call(
        functools.partial(_router_kernel, n_experts=e),
        grid=(n // tr,),
        in_specs=[pl.BlockSpec((tr, d), lambda i: (i, 0)),
                  pl.BlockSpec((1, d), lambda i: (0, 0)),
                  pl.BlockSpec((d, 2 * LANES), lambda i: (0, 0))],
        out_specs=[pl.BlockSpec((tr, d // 2), lambda i: (i, 0)), pair, pair, pair,
                   pl.BlockSpec((1, LANES), lambda i: (0, 0))],
        out_shape=[jax.ShapeDtypeStruct((n, d // 2), jnp.int32),
                   jax.ShapeDtypeStruct((n, TOP_K), jnp.int32),
                   jax.ShapeDtypeStruct((n, TOP_K), jnp.int32),
                   jax.ShapeDtypeStruct((n, TOP_K), f32),
                   jax.ShapeDtypeStruct((1, LANES), jnp.int32)],
        scratch_shapes=[pltpu.VMEM((1, LANES), f32)],
        compiler_params=_cparams(("arbitrary",)),
    )(x, g.reshape(1, d), r2)


def _row_copies(pos_ref, i, src, dst, sem, gather):
    out = []
    for k in range(TOP_K):
        p = pos_ref[k, i]
        if gather:
            out.append(pltpu.make_async_copy(src.at[pl.ds(p, 1)], dst.at[k, pl.ds(i, 1)], sem))
        else:
            out.append(pltpu.make_async_copy(src.at[pl.ds(i, 1)], dst.at[pl.ds(p, 1)], sem))
    return out


def _start_rows(n_rows, pos_ref, src, dst, sem, gather):
    def start(i, carry):
        for cp in _row_copies(pos_ref, i, src, dst, sem, gather):
            cp.start()
        return carry

    lax.fori_loop(0, n_rows, start, 0, unroll=ROW_DMA_UNROLL)


def _wait_rows(n_rows, pos_ref, src, dst, sem, gather):
    def wait(i, carry):
        for cp in _row_copies(pos_ref, i, src, dst, sem, gather):
            cp.wait()
        return carry

    lax.fori_loop(0, n_rows, wait, 0, unroll=ROW_DMA_UNROLL)


def _all_rows(n_rows, pos_ref, src, dst, sem, gather):
    _start_rows(n_rows, pos_ref, src, dst, sem, gather)
    _wait_rows(n_rows, pos_ref, src, dst, sem, gather)


def _moe_dispatch_kernel(nv_ref, pos_ref, h_ref, xs_ref, zero_s, sem):
    tm = zero_s.shape[0]

    @pl.when(pl.program_id(0) == 0)
    def _():
        zero_s[...] = jnp.zeros_like(zero_s)

        def fill(r, carry):
            @pl.when(nv_ref[r] < tm)
            def _():
                cp = pltpu.make_async_copy(zero_s, xs_ref.at[pl.ds(pl.multiple_of(r * tm, tm), tm)], sem)
                cp.start()
                cp.wait()
            return carry

        lax.fori_loop(0, xs_ref.shape[0] // tm, fill, 0)

    _all_rows(h_ref.shape[0], pos_ref, h_ref, xs_ref, sem, gather=False)


def _moe_dispatch(hp, pos3, tile_rows, tm):
    n, dw = hp.shape
    n_tiles, _, tr = pos3.shape
    grid_spec = pltpu.PrefetchScalarGridSpec(
        num_scalar_prefetch=1,
        grid=(n_tiles,),
        in_specs=[pl.BlockSpec((None, TOP_K, tr), lambda i, nv: (i, 0, 0), memory_space=pltpu.SMEM),
                  pl.BlockSpec((tr, dw), lambda i, nv: (i, 0))],
        out_specs=pl.BlockSpec(memory_space=pl.ANY),
        scratch_shapes=[pltpu.VMEM((tm, dw), jnp.int32), pltpu.SemaphoreType.DMA(())],
    )
    return pl.pallas_call(
        _moe_dispatch_kernel,
        grid_spec=grid_spec,
        out_shape=jax.ShapeDtypeStruct((tile_rows.shape[0] * tm, dw), jnp.int32),
        compiler_params=_cparams(("arbitrary",)),
    )(tile_rows, pos3, hp)


def _moe_up_kernel(te_ref, nv_ref, xs_ref, wg_ref, wu_ref, a_ref, x_s):
    r = pl.program_id(0)
    half = xs_ref.shape[1]

    @pl.when(nv_ref[r] > 0)
    def _():
        @pl.when(pl.program_id(1) == 0)
        def _():
            lo, hi = _unpack_halves(xs_ref[...])
            x_s[:, :half] = lo.astype(bf16)
            x_s[:, half:] = hi.astype(bf16)

        x = x_s[...]
        gate = jnp.dot(x, wg_ref[...], preferred_element_type=f32)
        up = jnp.dot(x, wu_ref[...], preferred_element_type=f32)
        a_ref[...] = (_silu(gate) * up).astype(a_ref.dtype)

    @pl.when(nv_ref[r] == 0)
    def _():
        a_ref[...] = jnp.zeros_like(a_ref)


def _moe_down_kernel(te_ref, nv_ref, a_ref, wd_ref, y_ref):
    @pl.when(nv_ref[pl.program_id(0)] > 0)
    def _():
        y_ref[...] = _pack_halves(jnp.dot(a_ref[...], wd_ref[...], preferred_element_type=f32))

    @pl.when(nv_ref[pl.program_id(0)] == 0)
    def _():
        y_ref[...] = jnp.zeros_like(y_ref)


def _moe_group(xs, tile_expert, tile_rows, wg, wu, wd, tm):
    p, dw = xs.shape
    _, d, fe = wg.shape
    tf = _tile(fe, MOE_FF_TILE)
    wspec = pl.BlockSpec((None, d, tf), lambda r, f, te, nv: (te[r], 0, f))
    act = pl.pallas_call(
        _moe_up_kernel,
        grid_spec=pltpu.PrefetchScalarGridSpec(
            num_scalar_prefetch=2,
            grid=(p // tm, fe // tf),
            in_specs=[pl.BlockSpec((tm, dw), lambda r, f, te, nv: (r, 0)), wspec, wspec],
            out_specs=pl.BlockSpec((tm, tf), lambda r, f, te, nv: (r, f)),
            scratch_shapes=[pltpu.VMEM((tm, d), bf16)],
        ),
        out_shape=jax.ShapeDtypeStruct((p, fe), bf16),
        compiler_params=_cparams(("arbitrary", "arbitrary")),
    )(tile_expert, tile_rows, xs, wg, wu)
    return pl.pallas_call(
        _moe_down_kernel,
        grid_spec=pltpu.PrefetchScalarGridSpec(
            num_scalar_prefetch=2,
            grid=(p // tm,),
            in_specs=[pl.BlockSpec((tm, fe), lambda r, te, nv: (r, 0)),
                      pl.BlockSpec((None, fe, d), lambda r, te, nv: (te[r], 0, 0))],
            out_specs=pl.BlockSpec((tm, dw), lambda r, te, nv: (r, 0)),
        ),
        out_shape=jax.ShapeDtypeStruct((p, dw), jnp.int32),
        compiler_params=_cparams(("arbitrary",)),
    )(tile_expert, tile_rows, act, wd)


def _moe_combine_kernel(pos_ref, pos_next_ref, x_ref, w_ref, g_ref, y_ref, o_ref, ybuf, sem, *, final_norm):
    tr, d = x_ref.shape
    half = d // 2
    i = pl.program_id(0)
    slot = i % 2

    @pl.when(i == 0)
    def _():
        _start_rows(tr, pos_ref, y_ref, ybuf.at[0], sem.at[0], gather=True)

    _wait_rows(tr, pos_ref, y_ref, ybuf.at[slot], sem.at[slot], gather=True)
    for r0 in range(0, tr, SUBLANES):
        rows = slice(r0, r0 + SUBLANES)
        for ii in range(r0, r0 + SUBLANES):
            for cp in _row_copies(pos_next_ref, ii, y_ref, ybuf.at[1 - slot], sem.at[1 - slot], True):
                cp.start()
        w = w_ref[rows, :]
        lo0, hi0 = _unpack_halves(ybuf[slot, 0, rows, :])
        lo1, hi1 = _unpack_halves(ybuf[slot, 1, rows, :])
        o_lo = x_ref[rows, :half] + (w[:, 0:1] * lo0 + w[:, 1:2] * lo1)
        o_hi = x_ref[rows, half:] + (w[:, 0:1] * hi0 + w[:, 1:2] * hi1)
        if final_norm:
            ms = (jnp.sum(o_lo * o_lo, axis=1, keepdims=True)
                  + jnp.sum(o_hi * o_hi, axis=1, keepdims=True)) / d
            inv = lax.rsqrt(ms + EPS)
            o_lo = o_lo * inv * g_ref[:, :half]
            o_hi = o_hi * inv * g_ref[:, half:]
        o_ref[rows, :half] = o_lo
        o_ref[rows, half:] = o_hi

    @pl.when(i == pl.num_programs(0) - 1)
    def _():
        _wait_rows(tr, pos_next_ref, y_ref, ybuf.at[1 - slot], sem.at[1 - slot], gather=True)


def _moe_combine(x, w, y, pos3, gain, final_norm, row_start, n_rows):
    d = x.shape[1]
    tr = pos3.shape[2]
    off = row_start // tr
    last = off + n_rows // tr - 1
    return pl.pallas_call(
        functools.partial(_moe_combine_kernel, final_norm=final_norm),
        grid=(n_rows // tr,),
        in_specs=[pl.BlockSpec((None, TOP_K, tr), lambda i: (i + off, 0, 0), memory_space=pltpu.SMEM),
                  pl.BlockSpec((None, TOP_K, tr), lambda i: (jnp.minimum(i + off + 1, last), 0, 0),
                               memory_space=pltpu.SMEM),
                  pl.BlockSpec((tr, d), lambda i: (i + off, 0)),
                  pl.BlockSpec((tr, TOP_K), lambda i: (i + off, 0)),
                  pl.BlockSpec((1, d), lambda i: (0, 0)),
                  pl.BlockSpec(memory_space=pl.ANY)],
        out_specs=pl.BlockSpec((tr, d), lambda i: (i, 0)),
        out_shape=jax.ShapeDtypeStruct((n_rows, d), f32),
        scratch_shapes=[pltpu.VMEM((2, TOP_K, tr, d // 2), jnp.int32), pltpu.SemaphoreType.DMA((2,))],
        compiler_params=_cparams(("arbitrary",)),
    )(pos3, pos3, x, w, gain.reshape(1, d), y)


def _moe_layer(x, g, router, wg, wu, wd, final_gain, parts):
    n, d = x.shape
    n_exp = router.shape[1]
    tm = MOE_ROW_TILE
    hp, eid, rank, w, cnt = _router(x, g, router)
    cnt = cnt[0, :n_exp]
    padded = (cnt + tm - 1) // tm * tm
    seg_end = jnp.cumsum(padded)
    seg_start = seg_end - padded
    pos = seg_start[eid] + rank
    n_rows_padded = TOP_K * n + n_exp * tm
    tile0 = jnp.arange(n_rows_padded // tm, dtype=jnp.int32) * tm
    tile_expert = jnp.minimum(jnp.sum(tile0[:, None] >= seg_end[None, :], axis=1), n_exp - 1).astype(jnp.int32)
    tile_rows = jnp.clip(cnt[tile_expert] - (tile0 - seg_start[tile_expert]), 0, tm).astype(jnp.int32)
    def by_tile(tr):
        return pos.reshape(n // tr, tr, TOP_K).transpose(0, 2, 1)

    pos3 = by_tile(_tile(n, MOE_COMBINE_TILE))
    xs = _moe_dispatch(hp, by_tile(_tile(n, MOE_DISPATCH_TILE)), tile_rows, tm)
    y = _moe_group(xs, tile_expert, tile_rows, wg, wu, wd, tm)
    if final_gain is None:
        return _moe_combine(x, w, y, pos3, g, False, 0, n)
    return [_moe_combine(x, w, y, pos3, final_gain, True, start, rows) for start, rows in parts]


def kernel(x_prompt, x_sample, norm_mix, norm_ffn, norm_final, pool_w, pool_scale, hgrn_w_in, hgrn_lb,
           hgrn_norm, hgrn_w_out, ffn_w_gate, ffn_w_up, ffn_w_down, moe_router, moe_w_gate, moe_w_up,
           moe_w_down):
    bp, sp, d = x_prompt.shape
    bs, ss, _ = x_sample.shape
    n_prompt = bp * sp
    n = n_prompt + bs * ss
    seq = (n_prompt, sp, ss)
    depth = norm_mix.shape[0]
    chunk = _tile(min(sp, ss), 128)

    parts = ((0, n_prompt), (n_prompt, bs * ss))
    streams = (x_prompt.reshape(n_prompt, d), x_sample.reshape(bs * ss, d))
    x = None if depth else jnp.concatenate(streams, axis=0)
    outs = None

    for i in range(depth):
        j = i // 2
        if i % 2 == 0:
            xa, xb = streams if x is None else (x[:n_prompt], x[n_prompt:])
            x, h = _pool_layer(xa, xb, norm_mix[i], pool_w[j].astype(bf16), pool_scale[j], norm_ffn[i], seq)
            nxt = i + 1 < depth
            later = [(hgrn_w_in, j, norm_mix[i + 1])] + [
                (w, j, None) for w in (hgrn_w_out, moe_w_gate, moe_w_up, moe_w_down)] if nxt else []
            act, (w_down, *cast) = _glu_up(h, ffn_w_gate[j].astype(bf16), ffn_w_up[j].astype(bf16),
                                           [(ffn_w_down, j, None)] + later)
            (x, *norm_rows), _ = _down_res(act, w_down, x, 1024, 256, nxt)
            rounded = dict(zip(("w_in", "w_out", "moe_gate", "moe_up", "moe_down"), cast))
        else:
            x16, sq = norm_rows
            w_in = rounded["w_in"]
            q = _proj_section(_proj_silu_heads_kernel, x16, sq, w_in, 0, True)
            forget = functools.partial(_proj_forget_kernel, layer=i)
            kf, gf = _proj_section(forget, x16, sq, w_in, 1, True, 2, hgrn_lb)
            kb, gb = _proj_section(forget, x16, sq, w_in, 2, True, 2, hgrn_lb)
            v = _proj_section(_proj_heads_kernel, x16, sq, w_in, 3, True)
            gate = _proj_section(_proj_silu_kernel, x16, sq, w_in, 4, False)
            s_bw = _scan_bw_states(kb, gb, v, seq, chunk)
            og = _scan(q, kf, kb, gf, gb, v, s_bw, gate, hgrn_norm[j], seq, chunk)
            (x,), _ = _down_res(og, rounded["w_out"], x, 1024, 512)
            last = i == depth - 1
            res = _moe_layer(x, norm_ffn[i], moe_router[j], rounded["moe_gate"], rounded["moe_up"],
                             rounded["moe_down"], norm_final if last else None, parts)
            x, outs = (None, res) if last else (res, None)

    if outs is None:
        outs = [_rmsnorm(x, norm_final, f32, start, rows) for start, rows in parts]
    return (outs[0].reshape(bp, sp, d), outs[1].reshape(bs, ss, d))
```

```python
import functools

import jax
import jax.numpy as jnp
from jax import lax
from jax.experimental import pallas as pl
from jax.experimental.pallas import tpu as pltpu

EPS = 1e-6
POOL_WINDOWS = (2, 4, 8, 16)
POOL_HALO = 8
POOL_PAD = 16
POOL_BLOCK = 128
HEAD_DIM = 128
TOP_K = 2
V7X_VMEM_LIMIT = 56 * 1024 * 1024
LANES = 128
SUBLANES = 8
MOE_ROW_TILE = 512
MOE_FF_TILE = 512
MOE_DISPATCH_TILE = 512
MOE_COMBINE_TILE = 256
ROW_DMA_UNROLL = 8
EPILOGUE_COLS = 256
BF16_SUBLANES = 16

bf16 = jnp.bfloat16
f32 = jnp.float32


def _cparams(sem):
    return pltpu.CompilerParams(dimension_semantics=sem, vmem_limit_bytes=V7X_VMEM_LIMIT)


def _tile(n, pref):
    t = min(n, pref)
    while n % t:
        t //= 2
    return t


def _rms(x, g):
    ms = jnp.mean(x * x, axis=-1, keepdims=True)
    return x * lax.rsqrt(ms + EPS) * g


def _silu(x):
    return x * jax.nn.sigmoid(x)


def _seq_flags(tok, n_tok, n_prompt, seq_p, seq_s):
    in_p = tok < n_prompt
    rel = jnp.where(in_p, tok, tok - n_prompt)
    slen = jnp.where(in_p, seq_p, seq_s)
    is_start = (rel % slen) == 0
    is_end = ((rel + n_tok) % slen) == 0
    return is_start, is_end


def _rmsnorm_kernel(x_ref, g_ref, o_ref):
    o_ref[...] = _rms(x_ref[...], g_ref[...]).astype(o_ref.dtype)


def _rmsnorm(x, g, out_dtype, row_start=0, n_rows=None):
    n, d = x.shape
    n_rows = n if n_rows is None else n_rows
    tr = _tile(n_rows, 512)
    off = row_start // tr
    return pl.pallas_call(
        _rmsnorm_kernel,
        grid=(n_rows // tr,),
        in_specs=[pl.BlockSpec((tr, d), lambda i: (i + off, 0)),
                  pl.BlockSpec((1, d), lambda i: (0, 0))],
        out_specs=pl.BlockSpec((tr, d), lambda i: (i, 0)),
        out_shape=jax.ShapeDtypeStruct((n_rows, d), out_dtype),
        compiler_params=_cparams(("parallel",)),
    )(x, g.reshape(1, d))


def _pool_kernel(ap_ref, a_ref, an_ref, bp_ref, b_ref, bn_ref, gmix_ref, w_ref, scale_ref, gffn_ref,
                 x1_ref, h1_ref, hi_s, lo_s, *, seq):
    t, d = a_ref.shape
    dg = d // len(POOL_WINDOWS)
    tok = pl.program_id(0) * t
    is_start, is_end = _seq_flags(tok, t, *seq)
    in_a = tok < seq[0]
    g = gmix_ref[...]
    x = jnp.where(in_a, a_ref[...], b_ref[...])
    h = _rms(x, g)
    h_prev = jnp.where(is_start, 0.0, _rms(jnp.where(in_a, ap_ref[...], bp_ref[...]), g))
    h_next = jnp.where(is_end, 0.0, _rms(jnp.where(in_a, an_ref[...], bn_ref[...]), g))
    zeros = jnp.zeros_like(h_prev)

    def put(r0, val):
        hi = val.astype(bf16)
        hi_s[r0:r0 + val.shape[0], :] = hi
        lo_s[r0:r0 + val.shape[0], :] = (val - hi.astype(f32)).astype(bf16)

    put(0, jnp.concatenate([zeros, h_prev], axis=0))
    put(POOL_PAD, h)
    put(POOL_PAD + t, jnp.concatenate([h_next, zeros], axis=0))
    tail = hi_s.shape[0] - (2 * POOL_PAD + t)
    hi_s[2 * POOL_PAD + t:, :] = jnp.zeros((tail, d), bf16)
    lo_s[2 * POOL_PAD + t:, :] = jnp.zeros((tail, d), bf16)

    rr = lax.broadcasted_iota(jnp.int32, (POOL_BLOCK, 2 * POOL_BLOCK), 0)
    cc = lax.broadcasted_iota(jnp.int32, (POOL_BLOCK, 2 * POOL_BLOCK), 1)
    r = lax.broadcasted_iota(jnp.int32, (t, 1), 0)
    for gi, w in enumerate(POOL_WINDOWS):
        c0 = gi * dg
        half = w // 2
        band = ((cc >= POOL_PAD + rr - half) & (cc < POOL_PAD + rr + half)).astype(bf16)
        sums = []
        for r0 in range(0, t, POOL_BLOCK):
            sums.append(jnp.dot(band, hi_s[r0:r0 + 2 * POOL_BLOCK, c0:c0 + dg], preferred_element_type=f32)
                        + jnp.dot(band, lo_s[r0:r0 + 2 * POOL_BLOCK, c0:c0 + dg], preferred_element_type=f32))
        s = jnp.concatenate(sums, axis=0)
        lo = jnp.where(is_start, jnp.maximum(r - half, 0), r - half)
        hi = jnp.where(is_end, jnp.minimum(r + half, t), r + half)
        cnt = (hi - lo).astype(f32)
        dlt = s / cnt - h[:, c0:c0 + dg]
        y = jnp.dot(dlt.astype(bf16), w_ref[gi], preferred_element_type=f32)
        x1_ref[:, c0:c0 + dg] = x[:, c0:c0 + dg] + y * scale_ref[:, c0:c0 + dg]
    h1_ref[...] = _rms(x1_ref[...], gffn_ref[...]).astype(h1_ref.dtype)


def _stream_specs(rows, first_tile, t, d):
    hb = t // POOL_HALO
    last_halo = rows // POOL_HALO - 1
    last_tile = rows // t - 1
    return [
        pl.BlockSpec((POOL_HALO, d), lambda i: (jnp.clip((i - first_tile) * hb - 1, 0, last_halo), 0)),
        pl.BlockSpec((t, d), lambda i: (jnp.clip(i - first_tile, 0, last_tile), 0)),
        pl.BlockSpec((POOL_HALO, d), lambda i: (jnp.clip((i - first_tile + 1) * hb, 0, last_halo), 0)),
    ]


def _pool_layer(xa, xb, g_mix, w_grp, scale, g_ffn, seq):
    d = xa.shape[1]
    n = xa.shape[0] + xb.shape[0]
    t = _tile(min(seq[1], seq[2]), 256)
    row = lambda i: (0, 0)
    return pl.pallas_call(
        functools.partial(_pool_kernel, seq=seq),
        grid=(n // t,),
        in_specs=_stream_specs(xa.shape[0], 0, t, d) + _stream_specs(xb.shape[0], xa.shape[0] // t, t, d) + [
            pl.BlockSpec((1, d), row),
            pl.BlockSpec(w_grp.shape, lambda i: (0, 0, 0)),
            pl.BlockSpec((1, d), row),
            pl.BlockSpec((1, d), row),
        ],
        out_specs=[pl.BlockSpec((t, d), lambda i: (i, 0)),
                   pl.BlockSpec((t, d), lambda i: (i, 0))],
        out_shape=[jax.ShapeDtypeStruct((n, d), f32), jax.ShapeDtypeStruct((n, d), bf16)],
        scratch_shapes=[pltpu.VMEM((t + POOL_BLOCK, d), bf16)] * 2,
        compiler_params=_cparams(("parallel",)),
    )(xa, xa, xa, xb, xb, xb, g_mix.reshape(1, d), w_grp, scale.reshape(1, d), g_ffn.reshape(1, d))


def _matmul_call_with_casts(kernel, grid, in_specs, out_specs, out_shapes, args, to_cast):
    n_steps = grid[0] * grid[1]
    plans = []
    for stacked, layer, scale in to_cast:
        cols = stacked.shape[-1]
        rows = stacked[0].size // cols
        steps = 1
        while steps * 2 <= n_steps and rows % (steps * 2 * BF16_SUBLANES) == 0:
            steps *= 2
        plans.append((rows, cols, steps, rows // steps, layer))
    scaled = [k for k, (_, _, scale) in enumerate(to_cast) if scale is not None]
    n_in, n_side, n_scale, n_out = len(in_specs), len(to_cast), len(scaled), len(out_specs)

    def step_block(steps):
        return lambda i, j: jnp.minimum(i * grid[1] + j, steps - 1)

    def src_spec(cols, steps, rb, layer):
        blk = step_block(steps)
        return pl.BlockSpec((None, rb, cols), lambda i, j: (layer, blk(i, j), 0))

    def dst_spec(cols, steps, rb):
        blk = step_block(steps)
        return pl.BlockSpec((rb, cols), lambda i, j: (blk(i, j), 0))

    def body(*refs):
        first_out = n_in + n_side + n_scale
        kernel(*refs[:n_in], *refs[first_out:first_out + n_out])
        step = pl.program_id(0) * grid[1] + pl.program_id(1)
        for k, plan in enumerate(plans):
            scale_ref = refs[n_in + n_side + scaled.index(k)] if k in scaled else None

            @pl.when(step < plan[2])
            def _(src=refs[n_in + k], dst=refs[first_out + n_out + k], scale_ref=scale_ref):
                val = src[...] if scale_ref is None else src[...] * scale_ref[...]
                dst[...] = val.astype(bf16)

    outs = pl.pallas_call(
        body,
        grid=grid,
        in_specs=list(in_specs) + [src_spec(c, s, rb, layer) for _, c, s, rb, layer in plans]
        + [dst_spec(1, plans[k][2], plans[k][3]) for k in scaled],
        out_specs=list(out_specs) + [dst_spec(c, s, rb) for _, c, s, rb, _ in plans],
        out_shape=list(out_shapes) + [jax.ShapeDtypeStruct((r, c), bf16) for r, c, _, _, _ in plans],
        compiler_params=_cparams(("arbitrary", "arbitrary")),
    )(*args,
      *[stacked.reshape(stacked.shape[0], r, c) for (stacked, _, _), (r, c, _, _, _) in zip(to_cast, plans)],
      *[to_cast[k][2].reshape(-1, 1) for k in scaled])
    return outs[:n_out], [o.reshape(stacked.shape[1:]) for o, (stacked, _, _) in zip(outs[n_out:], to_cast)]


def _glu_up_kernel(a_ref, wg_ref, wu_ref, o_ref):
    a = a_ref[...]
    gate = jnp.dot(a, wg_ref[...], preferred_element_type=f32)
    up = jnp.dot(a, wu_ref[...], preferred_element_type=f32)
    o_ref[...] = (_silu(gate) * up).astype(o_ref.dtype)


def _glu_up(a, wg, wu, to_cast=()):
    m, k = a.shape
    f = wg.shape[1]
    tm = _tile(m, 1024)
    tn = _tile(f, 512)
    wspec = pl.BlockSpec((k, tn), lambda i, j: (0, j))
    (act,), rounded = _matmul_call_with_casts(
        _glu_up_kernel, (m // tm, f // tn),
        [pl.BlockSpec((tm, k), lambda i, j: (i, 0)), wspec, wspec],
        [pl.BlockSpec((tm, tn), lambda i, j: (i, j))],
        [jax.ShapeDtypeStruct((m, f), bf16)], (a, wg, wu), to_cast)
    return act, rounded


def _down_kernel(a_ref, w_ref, res_ref, o_ref, *norm_refs):
    o = res_ref[...] + jnp.dot(a_ref[...], w_ref[...], preferred_element_type=f32)
    o_ref[...] = o
    if not norm_refs:
        return
    o16_ref, sq_ref = norm_refs
    o16_ref[...] = o.astype(o16_ref.dtype)
    sq = o * o
    part = sq[:, 0:LANES]
    for c0 in range(LANES, sq.shape[1], LANES):
        part = part + sq[:, c0:c0 + LANES]

    @pl.when(pl.program_id(1) == 0)
    def _():
        sq_ref[...] = part

    @pl.when(pl.program_id(1) > 0)
    def _():
        sq_ref[...] += part


def _down_res(a, w, res, tm, tn, norm_rows=False, to_cast=()):
    m, kk = a.shape
    n = w.shape[1]
    tm = _tile(m, tm)
    tn = _tile(n, tn)
    tile = pl.BlockSpec((tm, tn), lambda i, j: (i, j))
    out_specs, out_shapes = [tile], [jax.ShapeDtypeStruct((m, n), f32)]
    if norm_rows:
        out_specs += [tile, pl.BlockSpec((tm, LANES), lambda i, j: (i, 0))]
        out_shapes += [jax.ShapeDtypeStruct((m, n), bf16), jax.ShapeDtypeStruct((m, LANES), f32)]
    return _matmul_call_with_casts(
        _down_kernel, (m // tm, n // tn),
        [pl.BlockSpec((tm, kk), lambda i, j: (i, 0)), pl.BlockSpec((kk, tn), lambda i, j: (0, j)), tile],
        out_specs, out_shapes, (a, w, res), to_cast)


def _column_blocks(a_ref, sq_ref, w_ref):
    a = a_ref[...]
    inv = lax.rsqrt(jnp.sum(sq_ref[...], axis=1, keepdims=True) / a.shape[1] + EPS)
    tn = w_ref.shape[1]
    cb = min(tn, EPILOGUE_COLS)
    for c0 in range(0, tn, cb):
        yield c0, inv * jnp.dot(a, w_ref[:, c0:c0 + cb], preferred_element_type=f32)


def _store_head_block(o_ref, c0, val):
    for hh in range(val.shape[1] // HEAD_DIM):
        o_ref[c0 // HEAD_DIM + hh] = val[:, hh * HEAD_DIM:(hh + 1) * HEAD_DIM].astype(o_ref.dtype)


def _proj_silu_heads_kernel(a_ref, sq_ref, w_ref, o_ref):
    for c0, r in _column_blocks(a_ref, sq_ref, w_ref):
        _store_head_block(o_ref, c0, _silu(r))


def _proj_heads_kernel(a_ref, sq_ref, w_ref, o_ref):
    for c0, r in _column_blocks(a_ref, sq_ref, w_ref):
        _store_head_block(o_ref, c0, r)


def _proj_silu_kernel(a_ref, sq_ref, w_ref, o_ref):
    for c0, r in _column_blocks(a_ref, sq_ref, w_ref):
        o_ref[:, c0:c0 + r.shape[1]] = _silu(r).astype(o_ref.dtype)


def _proj_forget_kernel(a_ref, sq_ref, w_ref, lb_ref, k_ref, g_ref, *, layer):
    logits = lb_ref[...]
    ex = jnp.exp(logits - jnp.max(logits, axis=0, keepdims=True))
    sm = ex / jnp.sum(ex, axis=0, keepdims=True)
    lb_all = jnp.sum(sm[0:layer + 1], axis=0, keepdims=True) - sm[0:1]
    for c0, r in _column_blocks(a_ref, sq_ref, w_ref):
        lb = lb_all[:, c0:c0 + r.shape[1]]
        f = lb + (1.0 - lb) * jax.nn.sigmoid(r)
        _store_head_block(k_ref, c0, 1.0 - f)
        _store_head_block(g_ref, c0, jnp.log2(f))


def _proj_section(kernel, a, sq, w, section, out_heads, n_out=1, extra=None):
    m, k = a.shape
    d = k
    tm = _tile(m, 1024)
    tn = _tile(d, 1024)
    nb = d // tn
    in_specs = [pl.BlockSpec((tm, k), lambda i, j: (i, 0)),
                pl.BlockSpec((tm, LANES), lambda i, j: (i, 0)),
                pl.BlockSpec((k, tn), lambda i, j: (0, section * nb + j))]
    args = [a, sq, w]
    if extra is not None:
        in_specs.append(pl.BlockSpec((extra.shape[0], tn), lambda i, j: (0, j)))
        args.append(extra)
    if out_heads:
        hpb = tn // HEAD_DIM
        ospec = pl.BlockSpec((hpb, tm, HEAD_DIM), lambda i, j: (j, i, 0))
        oshape = jax.ShapeDtypeStruct((d // HEAD_DIM, m, HEAD_DIM), bf16)
    else:
        ospec = pl.BlockSpec((tm, tn), lambda i, j: (i, j))
        oshape = jax.ShapeDtypeStruct((m, d), bf16)
    out = pl.pallas_call(
        kernel,
        grid=(m // tm, nb),
        in_specs=in_specs,
        out_specs=[ospec] * n_out if n_out > 1 else ospec,
        out_shape=[oshape] * n_out if n_out > 1 else oshape,
        compiler_params=_cparams(("parallel", "arbitrary")),
    )(*args)
    return out


MAX_DIRECT_LOG2_DECAY = 100.0
HEAD_UNROLL = 8


def _dot_nt(a, b):
    return lax.dot_general(a, b, (((1,), (1,)), ((), ())), preferred_element_type=f32)


def _dot_tn(a, b):
    return lax.dot_general(a, b, (((0,), (0,)), ((), ())), preferred_element_type=f32)


def _mid_ref(p, level, offset):
    c, w = p.shape
    blk = 2 << level
    p3 = p.reshape(c // blk, blk, w)
    return jnp.broadcast_to(p3[:, offset:offset + 1, :], p3.shape).reshape(c, w)


def _neg_abs_exponents(gf, gb, b, x, level, rmod4):
    if level == 0:
        odd = (rmod4 & 1) == 1
        return jnp.where(odd, gf, 0.0), jnp.where(odd, 0.0, gb)
    if level == 1:
        c = gf.shape[0]
        gf_up = pltpu.roll(gf, c - 1, 0)
        gf_dn = pltpu.roll(gf, 1, 0)
        gb_up = pltpu.roll(gb, c - 1, 0)
        gb_dn = pltpu.roll(gb, 1, 0)
        ef = jnp.where(rmod4 == 0, gf_up, jnp.where(rmod4 == 1, 0.0, jnp.where(rmod4 == 2, gf, gf + gf_dn)))
        eb = jnp.where(rmod4 == 0, gb + gb_up, jnp.where(rmod4 == 1, gb, jnp.where(rmod4 == 2, 0.0, gb_dn)))
        return ef, eb
    half = 1 << level
    ef = b - _mid_ref(b, level, half - 1)
    eb = x - _mid_ref(x, level, half)
    return -jnp.abs(ef), -jnp.abs(eb)


def _chunk_prefix(tri_incl, tri_excl, gf16, gb16):
    b = jnp.dot(tri_incl, gf16, preferred_element_type=f32)
    x = jnp.dot(tri_excl, gb16, preferred_element_type=f32)
    return b, x


def _scan_bw_kernel(kb_ref, gb_ref, v_ref, s_ref, st_ref, *, seq, n_chunks):
    n_heads, c, _ = kb_ref.shape
    chunk = n_chunks - 1 - pl.program_id(0)
    _, is_end = _seq_flags(chunk * c, c, *seq)

    @pl.when(is_end)
    def _():
        st_ref[...] = jnp.zeros_like(st_ref)

    row = lax.broadcasted_iota(jnp.int32, (c, c), 0)
    col = lax.broadcasted_iota(jnp.int32, (c, c), 1)
    tri_excl = (col < row).astype(bf16)

    def head(h, carry):
        gb16 = gb_ref[h]
        x = jnp.dot(tri_excl, gb16, preferred_element_type=f32)
        tot = x[c - 1:c, :] + gb16[c - 1:c, :].astype(f32)
        st = st_ref[h]
        s_ref[h] = st.astype(s_ref.dtype)
        kx = (kb_ref[h].astype(f32) * jnp.exp2(x)).astype(bf16)
        st_ref[h] = st * jnp.exp2(tot) + _dot_tn(v_ref[h], kx)
        return carry

    lax.fori_loop(0, n_heads, head, 0, unroll=4 * HEAD_UNROLL)


def _scan_bw_states(kb, gb, v, seq, c):
    n_heads, n, dk = kb.shape
    n_chunks = n // c
    spec = pl.BlockSpec((n_heads, c, dk), lambda i: (0, n_chunks - 1 - i, 0))
    return pl.pallas_call(
        functools.partial(_scan_bw_kernel, seq=seq, n_chunks=n_chunks),
        grid=(n_chunks,),
        in_specs=[spec, spec, spec],
        out_specs=pl.BlockSpec((None, n_heads, dk, dk), lambda i: (n_chunks - 1 - i, 0, 0, 0)),
        out_shape=jax.ShapeDtypeStruct((n_chunks, n_heads, dk, dk), bf16),
        scratch_shapes=[pltpu.VMEM((n_heads, dk, dk), f32)],
        compiler_params=_cparams(("arbitrary",)),
    )(kb, gb, v)


def _scan_kernel(q_ref, kf_ref, kb_ref, gf_ref, gb_ref, v_ref, sb_ref, tril_ref, triu_ref, gate_ref,
                 gain_ref, out_ref, st_ref, o_s, qf_s, kf_s, qb_s, kb_s, qin_s, kx_s, sc_s, dec_s, tot_s,
                 sq_s, *, seq):
    n_heads, c, dk = q_ref.shape
    n_levels = c.bit_length() - 1
    mid = c // 2
    is_start, _ = _seq_flags(pl.program_id(0) * c, c, *seq)

    @pl.when(is_start)
    def _():
        st_ref[...] = jnp.zeros_like(st_ref)

    row = lax.broadcasted_iota(jnp.int32, (c, c), 0)
    col = lax.broadcasted_iota(jnp.int32, (c, c), 1)
    tri_incl = (col <= row).astype(bf16)
    tri_excl = (col < row).astype(bf16)

    def finish(h, kf_dec, b_last, q_in, scores, diag):
        v = v_ref[h]
        st_f = st_ref[h]
        st_cat = jnp.concatenate([st_f.astype(bf16), sb_ref[h]], axis=1)
        o = _dot_nt(q_in, st_cat) + jnp.dot(scores.astype(bf16), v, preferred_element_type=f32)
        o = o if diag is None else o + diag * v.astype(f32)
        o_s[h] = o
        st_ref[h] = st_f * jnp.exp2(b_last) + _dot_tn(v, kf_dec.astype(bf16))
        return o * o

    def direct_operands(h, carry):
        q = q_ref[h].astype(f32)
        gb16 = gb_ref[h]
        b, x = _chunk_prefix(tri_incl, tri_excl, gf_ref[h], gb16)
        b_mid = b[mid - 1:mid, :]
        x_mid = x[mid:mid + 1, :]
        db = b - b_mid
        dx = x - x_mid
        q_f = q * jnp.exp2(db)
        k_f = kf_ref[h].astype(f32) * jnp.exp2(-db)
        q_b = q * jnp.exp2(-dx)
        k_b = kb_ref[h].astype(f32) * jnp.exp2(dx)
        b_last = b[c - 1:c, :]
        x_tot = x[c - 1:c, :] + gb16[c - 1:c, :].astype(f32)
        qf_s[h] = q_f.astype(bf16)
        kf_s[h] = k_f.astype(bf16)
        qb_s[h] = q_b.astype(bf16)
        kb_s[h] = k_b.astype(bf16)
        qin_s[h] = jnp.concatenate([q_f * jnp.exp2(b_mid), q_b * jnp.exp2(x_tot - x_mid)], axis=1).astype(bf16)
        kx_s[h] = (k_f * jnp.exp2(b_last - b_mid)).astype(bf16)
        dec_s[h] = jnp.broadcast_to(jnp.exp2(b_last), dec_s.shape[1:])
        tot_s[h] = jnp.broadcast_to(jnp.minimum(b_last, x_tot), tot_s.shape[1:])
        return carry

    def direct_scores(h, carry):
        sc_s[h] = (jnp.where(tril_ref[...] != 0.0, _dot_nt(qf_s[h], kf_s[h]), 0.0)
                   + jnp.where(triu_ref[...] != 0.0, _dot_nt(qb_s[h], kb_s[h]), 0.0)).astype(bf16)
        return carry

    def direct_output(h, carry):
        v = v_ref[h]
        st_f = st_ref[h]
        st_cat = jnp.concatenate([st_f.astype(bf16), sb_ref[h]], axis=1)
        o = _dot_nt(qin_s[h], st_cat) + jnp.dot(sc_s[h], v, preferred_element_type=f32)
        o_s[h] = o
        st_ref[h] = st_f * dec_s[h][0:1, :] + _dot_tn(v, kx_s[h])
        return carry + o * o

    def head_levelled(h, carry):
        rtok = lax.broadcasted_iota(jnp.int32, (c, dk), 0)
        rmod4 = rtok & 3
        q = q_ref[h].astype(f32)
        kf = kf_ref[h].astype(f32)
        kb = kb_ref[h].astype(f32)
        gf16 = gf_ref[h]
        gb16 = gb_ref[h]
        gf = gf16.astype(f32)
        gb = gb16.astype(f32)
        b, x = _chunk_prefix(tri_incl, tri_excl, gf16, gb16)
        scores = jnp.zeros((c, c), f32)
        for l in range(n_levels):
            ef, eb = _neg_abs_exponents(gf, gb, b, x, l, rmod4)
            af = jnp.exp2(ef)
            ab = jnp.exp2(eb)
            up = ((rtok >> l) & 1) == 1
            lhs = jnp.concatenate([jnp.where(up, q * af, 0.0), jnp.where(up, 0.0, q * ab)], axis=1)
            rhs = jnp.concatenate([jnp.where(up, 0.0, kf * af), jnp.where(up, kb * ab, 0.0)], axis=1)
            p = _dot_nt(lhs.astype(bf16), rhs.astype(bf16))
            if l < n_levels - 1:
                p = jnp.where((row >> (l + 1)) == (col >> (l + 1)), p, 0.0)
            scores = scores + p
        b_last = b[c - 1:c, :]
        x_tot = x[c - 1:c, :] + gb[c - 1:c, :]
        q_in = jnp.concatenate([q * jnp.exp2(b), q * jnp.exp2(x_tot - x)], axis=1).astype(bf16)
        diag = jnp.sum(q * (kf + kb), axis=1, keepdims=True)
        return carry + finish(h, kf * jnp.exp2(b_last - b), b_last, q_in, scores, diag)

    lax.fori_loop(0, n_heads, direct_operands, 0, unroll=4 * HEAD_UNROLL)
    direct_ok = jnp.min(tot_s[...]) >= -MAX_DIRECT_LOG2_DECAY

    @pl.when(direct_ok)
    def _():
        lax.fori_loop(0, n_heads, direct_scores, 0, unroll=4 * HEAD_UNROLL)
        sq_s[...] = lax.fori_loop(0, n_heads, direct_output, jnp.zeros((c, dk), f32), unroll=4 * HEAD_UNROLL)

    @pl.when(jnp.logical_not(direct_ok))
    def _():
        sq_s[...] = lax.fori_loop(0, n_heads, head_levelled, jnp.zeros((c, dk), f32))

    inv = lax.rsqrt(jnp.sum(sq_s[...], axis=1, keepdims=True) / (n_heads * dk) + EPS)
    for hh in range(n_heads):
        cols = slice(hh * dk, (hh + 1) * dk)
        out_ref[:, cols] = (o_s[hh] * inv * gain_ref[:, cols] * gate_ref[:, cols].astype(f32)
                            ).astype(out_ref.dtype)


def _scan(q, kf, kb, gf, gb, v, s_bw, gate, gain, seq, c):
    n_heads, n, dk = q.shape
    d = n_heads * dk
    n_chunks = n // c
    spec = pl.BlockSpec((n_heads, c, dk), lambda i: (0, i, 0))
    tri = pl.BlockSpec((c, c), lambda i: (0, 0))
    tril = jnp.tril(jnp.ones((c, c), f32))
    return pl.pallas_call(
        functools.partial(_scan_kernel, seq=seq),
        grid=(n_chunks,),
        in_specs=[spec] * 6 + [pl.BlockSpec((None, n_heads, dk, dk), lambda i: (i, 0, 0, 0)), tri, tri,
                               pl.BlockSpec((c, d), lambda i: (i, 0)),
                               pl.BlockSpec((1, d), lambda i: (0, 0))],
        out_specs=pl.BlockSpec((c, d), lambda i: (i, 0)),
        out_shape=jax.ShapeDtypeStruct((n, d), bf16),
        scratch_shapes=[pltpu.VMEM((n_heads, dk, dk), f32), pltpu.VMEM((n_heads, c, dk), f32)]
        + [pltpu.VMEM((n_heads, c, dk), bf16)] * 4
        + [pltpu.VMEM((n_heads, c, 2 * dk), bf16), pltpu.VMEM((n_heads, c, dk), bf16),
           pltpu.VMEM((n_heads, c, c), bf16)] + [pltpu.VMEM((n_heads, 8, dk), f32)] * 2
        + [pltpu.VMEM((c, dk), f32)],
        compiler_params=_cparams(("arbitrary",)),
    )(q, kf, kb, gf, gb, v, s_bw, tril, tril.T, gate, gain.reshape(1, d))


def _pack_halves(x):
    half = x.shape[1] // 2
    bits = lax.bitcast_convert_type(x.astype(bf16).astype(f32), jnp.int32)
    return lax.shift_right_logical(bits[:, :half], 16) | (bits[:, half:] & jnp.int32(-65536))


def _unpack_halves(words):
    lo = lax.bitcast_convert_type(lax.shift_left(words, 16), f32)
    hi = lax.bitcast_convert_type(words & jnp.int32(-65536), f32)
    return lo, hi


def _router_kernel(x_ref, g_ref, r2_ref, h_ref, e_ref, rank_ref, w_ref, cnt_ref, run_ref,
                   *, n_experts):
    @pl.when(pl.program_id(0) == 0)
    def _():
        run_ref[...] = jnp.zeros_like(run_ref)

    h = _rms(x_ref[...], g_ref[...])
    h_ref[...] = _pack_halves(h)
    hi = h.astype(bf16)
    lo = (h - hi.astype(f32)).astype(bf16)
    tr = h.shape[0]
    prod = jnp.dot(jnp.concatenate([hi, lo], axis=0), r2_ref[...], preferred_element_type=f32)
    logits = prod[:tr, :LANES] + prod[:tr, LANES:] + prod[tr:, :LANES]
    lane = lax.broadcasted_iota(jnp.int32, logits.shape, 1)
    neg = jnp.float32(-jnp.inf)
    logits = jnp.where(lane < n_experts, logits, neg)
    m1 = jnp.max(logits, axis=1, keepdims=True)
    i1 = jnp.min(jnp.where(logits == m1, lane, LANES), axis=1, keepdims=True)
    rest = jnp.where(lane == i1, neg, logits)
    m2 = jnp.max(rest, axis=1, keepdims=True)
    i2 = jnp.min(jnp.where(rest == m2, lane, LANES), axis=1, keepdims=True)
    e2 = jnp.exp(m2 - m1)
    w_ref[...] = jnp.concatenate([1.0 / (1.0 + e2), e2 / (1.0 + e2)], axis=1)
    e_ref[...] = jnp.concatenate([i1, i2], axis=1)
    member = jnp.where((lane == i1) | (lane == i2), 1.0, 0.0)
    row = lax.broadcasted_iota(jnp.int32, (tr, tr), 0)
    col = lax.broadcasted_iota(jnp.int32, (tr, tr), 1)
    before = jnp.dot((col < row).astype(bf16), member.astype(bf16), preferred_element_type=f32)
    before = before + run_ref[...]
    r1 = jnp.sum(jnp.where(lane == i1, before, 0.0), axis=1, keepdims=True)
    r2 = jnp.sum(jnp.where(lane == i2, before, 0.0), axis=1, keepdims=True)
    rank_ref[...] = jnp.concatenate([r1, r2], axis=1).astype(jnp.int32)
    run = run_ref[...] + jnp.sum(member, axis=0, keepdims=True)
    run_ref[...] = run
    cnt_ref[...] = run.astype(jnp.int32)


def _router(x, g, router):
    n, d = x.shape
    e = router.shape[1]
    rpad = jnp.zeros((d, LANES), f32).at[:, :e].set(router)
    rhi = rpad.astype(bf16)
    r2 = jnp.concatenate([rhi, (rpad - rhi.astype(f32)).astype(bf16)], axis=1)
    tr = _tile(n, 512)
    pair = pl.BlockSpec((tr, TOP_K), lambda i: (i, 0))
    return pl.pallas_call(
        functools.partial(_router_kernel, n_experts=e),
        grid=(n // tr,),
        in_specs=[pl.BlockSpec((tr, d), lambda i: (i, 0)),
                  pl.BlockSpec((1, d), lambda i: (0, 0)),
                  pl.BlockSpec((d, 2 * LANES), lambda i: (0, 0))],
        out_specs=[pl.BlockSpec((tr, d // 2), lambda i: (i, 0)), pair, pair, pair,
                   pl.BlockSpec((1, LANES), lambda i: (0, 0))],
        out_shape=[jax.ShapeDtypeStruct((n, d // 2), jnp.int32),
                   jax.ShapeDtypeStruct((n, TOP_K), jnp.int32),
                   jax.ShapeDtypeStruct((n, TOP_K), jnp.int32),
                   jax.ShapeDtypeStruct((n, TOP_K), f32),
                   jax.ShapeDtypeStruct((1, LANES), jnp.int32)],
        scratch_shapes=[pltpu.VMEM((1, LANES), f32)],
        compiler_params=_cparams(("arbitrary",)),
    )(x, g.reshape(1, d), r2)


def _row_copies(pos_ref, i, src, dst, sem, gather):
    out = []
    for k in range(TOP_K):
        p = pos_ref[k, i]
        if gather:
            out.append(pltpu.make_async_copy(src.at[pl.ds(p, 1)], dst.at[k, pl.ds(i, 1)], sem))
        else:
            out.append(pltpu.make_async_copy(src.at[pl.ds(i, 1)], dst.at[pl.ds(p, 1)], sem))
    return out


def _start_rows(n_rows, pos_ref, src, dst, sem, gather):
    def start(i, carry):
        for cp in _row_copies(pos_ref, i, src, dst, sem, gather):
            cp.start()
        return carry

    lax.fori_loop(0, n_rows, start, 0, unroll=ROW_DMA_UNROLL)


def _wait_rows(n_rows, pos_ref, src, dst, sem, gather):
    def wait(i, carry):
        for cp in _row_copies(pos_ref, i, src, dst, sem, gather):
            cp.wait()
        return carry

    lax.fori_loop(0, n_rows, wait, 0, unroll=ROW_DMA_UNROLL)


def _all_rows(n_rows, pos_ref, src, dst, sem, gather):
    _start_rows(n_rows, pos_ref, src, dst, sem, gather)
    _wait_rows(n_rows, pos_ref, src, dst, sem, gather)


def _moe_dispatch_kernel(nv_ref, pos_ref, h_ref, xs_ref, zero_s, sem):
    tm = zero_s.shape[0]

    @pl.when(pl.program_id(0) == 0)
    def _():
        zero_s[...] = jnp.zeros_like(zero_s)

        def fill(r, carry):
            @pl.when(nv_ref[r] < tm)
            def _():
                cp = pltpu.make_async_copy(zero_s, xs_ref.at[pl.ds(pl.multiple_of(r * tm, tm), tm)], sem)
                cp.start()
                cp.wait()
            return carry

        lax.fori_loop(0, xs_ref.shape[0] // tm, fill, 0)

    _all_rows(h_ref.shape[0], pos_ref, h_ref, xs_ref, sem, gather=False)


def _moe_dispatch(hp, pos3, tile_rows, tm):
    n, dw = hp.shape
    n_tiles, _, tr = pos3.shape
    grid_spec = pltpu.PrefetchScalarGridSpec(
        num_scalar_prefetch=1,
        grid=(n_tiles,),
        in_specs=[pl.BlockSpec((None, TOP_K, tr), lambda i, nv: (i, 0, 0), memory_space=pltpu.SMEM),
                  pl.BlockSpec((tr, dw), lambda i, nv: (i, 0))],
        out_specs=pl.BlockSpec(memory_space=pl.ANY),
        scratch_shapes=[pltpu.VMEM((tm, dw), jnp.int32), pltpu.SemaphoreType.DMA(())],
    )
    return pl.pallas_call(
        _moe_dispatch_kernel,
        grid_spec=grid_spec,
        out_shape=jax.ShapeDtypeStruct((tile_rows.shape[0] * tm, dw), jnp.int32),
        compiler_params=_cparams(("arbitrary",)),
    )(tile_rows, pos3, hp)


def _moe_up_kernel(te_ref, nv_ref, xs_ref, wg_ref, wu_ref, a_ref, x_s):
    r = pl.program_id(0)
    half = xs_ref.shape[1]

    @pl.when(nv_ref[r] > 0)
    def _():
        @pl.when(pl.program_id(1) == 0)
        def _():
            lo, hi = _unpack_halves(xs_ref[...])
            x_s[:, :half] = lo.astype(bf16)
            x_s[:, half:] = hi.astype(bf16)

        x = x_s[...]
        gate = jnp.dot(x, wg_ref[...], preferred_element_type=f32)
        up = jnp.dot(x, wu_ref[...], preferred_element_type=f32)
        a_ref[...] = (_silu(gate) * up).astype(a_ref.dtype)

    @pl.when(nv_ref[r] == 0)
    def _():
        a_ref[...] = jnp.zeros_like(a_ref)


def _moe_down_kernel(te_ref, nv_ref, a_ref, wd_ref, y_ref):
    @pl.when(nv_ref[pl.program_id(0)] > 0)
    def _():
        y_ref[...] = _pack_halves(jnp.dot(a_ref[...], wd_ref[...], preferred_element_type=f32))

    @pl.when(nv_ref[pl.program_id(0)] == 0)
    def _():
        y_ref[...] = jnp.zeros_like(y_ref)


def _moe_group(xs, tile_expert, tile_rows, wg, wu, wd, tm):
    p, dw = xs.shape
    _, d, fe = wg.shape
    tf = _tile(fe, MOE_FF_TILE)
    wspec = pl.BlockSpec((None, d, tf), lambda r, f, te, nv: (te[r], 0, f))
    act = pl.pallas_call(
        _moe_up_kernel,
        grid_spec=pltpu.PrefetchScalarGridSpec(
            num_scalar_prefetch=2,
            grid=(p // tm, fe // tf),
            in_specs=[pl.BlockSpec((tm, dw), lambda r, f, te, nv: (r, 0)), wspec, wspec],
            out_specs=pl.BlockSpec((tm, tf), lambda r, f, te, nv: (r, f)),
            scratch_shapes=[pltpu.VMEM((tm, d), bf16)],
        ),
        out_shape=jax.ShapeDtypeStruct((p, fe), bf16),
        compiler_params=_cparams(("arbitrary", "arbitrary")),
    )(tile_expert, tile_rows, xs, wg, wu)
    return pl.pallas_call(
        _moe_down_kernel,
        grid_spec=pltpu.PrefetchScalarGridSpec(
            num_scalar_prefetch=2,
            grid=(p // tm,),
            in_specs=[pl.BlockSpec((tm, fe), lambda r, te, nv: (r, 0)),
                      pl.BlockSpec((None, fe, d), lambda r, te, nv: (te[r], 0, 0))],
            out_specs=pl.BlockSpec((tm, dw), lambda r, te, nv: (r, 0)),
        ),
        out_shape=jax.ShapeDtypeStruct((p, dw), jnp.int32),
        compiler_params=_cparams(("arbitrary",)),
    )(tile_expert, tile_rows, act, wd)


def _moe_combine_kernel(pos_ref, pos_next_ref, x_ref, w_ref, g_ref, y_ref, o_ref, ybuf, sem, *, final_norm):
    tr, d = x_ref.shape
    half = d // 2
    i = pl.program_id(0)
    slot = i % 2

    @pl.when(i == 0)
    def _():
        _start_rows(tr, pos_ref, y_ref, ybuf.at[0], sem.at[0], gather=True)

    _wait_rows(tr, pos_ref, y_ref, ybuf.at[slot], sem.at[slot], gather=True)
    for r0 in range(0, tr, SUBLANES):
        rows = slice(r0, r0 + SUBLANES)
        for ii in range(r0, r0 + SUBLANES):
            for cp in _row_copies(pos_next_ref, ii, y_ref, ybuf.at[1 - slot], sem.at[1 - slot], True):
                cp.start()
        w = w_ref[rows, :]
        lo0, hi0 = _unpack_halves(ybuf[slot, 0, rows, :])
        lo1, hi1 = _unpack_halves(ybuf[slot, 1, rows, :])
        o_lo = x_ref[rows, :half] + (w[:, 0:1] * lo0 + w[:, 1:2] * lo1)
        o_hi = x_ref[rows, half:] + (w[:, 0:1] * hi0 + w[:, 1:2] * hi1)
        if final_norm:
            ms = (jnp.sum(o_lo * o_lo, axis=1, keepdims=True)
                  + jnp.sum(o_hi * o_hi, axis=1, keepdims=True)) / d
            inv = lax.rsqrt(ms + EPS)
            o_lo = o_lo * inv * g_ref[:, :half]
            o_hi = o_hi * inv * g_ref[:, half:]
        o_ref[rows, :half] = o_lo
        o_ref[rows, half:] = o_hi

    @pl.when(i == pl.num_programs(0) - 1)
    def _():
        _wait_rows(tr, pos_next_ref, y_ref, ybuf.at[1 - slot], sem.at[1 - slot], gather=True)


def _moe_combine(x, w, y, pos3, gain, final_norm, row_start, n_rows):
    d = x.shape[1]
    tr = pos3.shape[2]
    off = row_start // tr
    last = off + n_rows // tr - 1
    return pl.pallas_call(
        functools.partial(_moe_combine_kernel, final_norm=final_norm),
        grid=(n_rows // tr,),
        in_specs=[pl.BlockSpec((None, TOP_K, tr), lambda i: (i + off, 0, 0), memory_space=pltpu.SMEM),
                  pl.BlockSpec((None, TOP_K, tr), lambda i: (jnp.minimum(i + off + 1, last), 0, 0),
                               memory_space=pltpu.SMEM),
                  pl.BlockSpec((tr, d), lambda i: (i + off, 0)),
                  pl.BlockSpec((tr, TOP_K), lambda i: (i + off, 0)),
                  pl.BlockSpec((1, d), lambda i: (0, 0)),
                  pl.BlockSpec(memory_space=pl.ANY)],
        out_specs=pl.BlockSpec((tr, d), lambda i: (i, 0)),
        out_shape=jax.ShapeDtypeStruct((n_rows, d), f32),
        scratch_shapes=[pltpu.VMEM((2, TOP_K, tr, d // 2), jnp.int32), pltpu.SemaphoreType.DMA((2,))],
        compiler_params=_cparams(("arbitrary",)),
    )(pos3, pos3, x, w, gain.reshape(1, d), y)


def _moe_layer(x, g, router, wg, wu, wd, final_gain, parts):
    n, d = x.shape
    n_exp = router.shape[1]
    tm = MOE_ROW_TILE
    hp, eid, rank, w, cnt = _router(x, g, router)
    cnt = cnt[0, :n_exp]
    padded = (cnt + tm - 1) // tm * tm
    seg_end = jnp.cumsum(padded)
    seg_start = seg_end - padded
    pos = seg_start[eid] + rank
    n_rows_padded = TOP_K * n + n_exp * tm
    tile0 = jnp.arange(n_rows_padded // tm, dtype=jnp.int32) * tm
    tile_expert = jnp.minimum(jnp.sum(tile0[:, None] >= seg_end[None, :], axis=1), n_exp - 1).astype(jnp.int32)
    tile_rows = jnp.clip(cnt[tile_expert] - (tile0 - seg_start[tile_expert]), 0, tm).astype(jnp.int32)
    def by_tile(tr):
        return pos.reshape(n // tr, tr, TOP_K).transpose(0, 2, 1)

    pos3 = by_tile(_tile(n, MOE_COMBINE_TILE))
    xs = _moe_dispatch(hp, by_tile(_tile(n, MOE_DISPATCH_TILE)), tile_rows, tm)
    y = _moe_group(xs, tile_expert, tile_rows, wg, wu, wd, tm)
    if final_gain is None:
        return _moe_combine(x, w, y, pos3, g, False, 0, n)
    return [_moe_combine(x, w, y, pos3, final_gain, True, start, rows) for start, rows in parts]


def kernel(x_prompt, x_sample, norm_mix, norm_ffn, norm_final, pool_w, pool_scale, hgrn_w_in, hgrn_lb,
           hgrn_norm, hgrn_w_out, ffn_w_gate, ffn_w_up, ffn_w_down, moe_router, moe_w_gate, moe_w_up,
           moe_w_down):
    bp, sp, d = x_prompt.shape
    bs, ss, _ = x_sample.shape
    n_prompt = bp * sp
    n = n_prompt + bs * ss
    seq = (n_prompt, sp, ss)
    depth = norm_mix.shape[0]
    chunk = _tile(min(sp, ss), 128)

    parts = ((0, n_prompt), (n_prompt, bs * ss))
    streams = (x_prompt.reshape(n_prompt, d), x_sample.reshape(bs * ss, d))
    x = None if depth else jnp.concatenate(streams, axis=0)
    outs = None

    for i in range(depth):
        j = i // 2
        if i % 2 == 0:
            xa, xb = streams if x is None else (x[:n_prompt], x[n_prompt:])
            x, h = _pool_layer(xa, xb, norm_mix[i], pool_w[j].astype(bf16), pool_scale[j], norm_ffn[i], seq)
            nxt = i + 1 < depth
            later = [(hgrn_w_in, j, norm_mix[i + 1])] + [
                (w, j, None) for w in (hgrn_w_out, moe_w_gate, moe_w_up, moe_w_down)] if nxt else []
            act, (w_down, *cast) = _glu_up(h, ffn_w_gate[j].astype(bf16), ffn_w_up[j].astype(bf16),
                                           [(ffn_w_down, j, None)] + later)
            (x, *norm_rows), _ = _down_res(act, w_down, x, 1024, 256, nxt)
            rounded = dict(zip(("w_in", "w_out", "moe_gate", "moe_up", "moe_down"), cast))
        else:
            x16, sq = norm_rows
            w_in = rounded["w_in"]
            q = _proj_section(_proj_silu_heads_kernel, x16, sq, w_in, 0, True)
            forget = functools.partial(_proj_forget_kernel, layer=i)
            kf, gf = _proj_section(forget, x16, sq, w_in, 1, True, 2, hgrn_lb)
            kb, gb = _proj_section(forget, x16, sq, w_in, 2, True, 2, hgrn_lb)
            v = _proj_section(_proj_heads_kernel, x16, sq, w_in, 3, True)
            gate = _proj_section(_proj_silu_kernel, x16, sq, w_in, 4, False)
            s_bw = _scan_bw_states(kb, gb, v, seq, chunk)
            og = _scan(q, kf, kb, gf, gb, v, s_bw, gate, hgrn_norm[j], seq, chunk)
            (x,), _ = _down_res(og, rounded["w_out"], x, 1024, 512)
            last = i == depth - 1
            res = _moe_layer(x, norm_ffn[i], moe_router[j], rounded["moe_gate"], rounded["moe_up"],
                             rounded["moe_down"], norm_final if last else None, parts)
            x, outs = (None, res) if last else (res, None)

    if outs is None:
        outs = [_rmsnorm(x, norm_final, f32, start, rows) for start, rows in parts]
    return (outs[0].reshape(bp, sp, d), outs[1].reshape(bs, ss, d))
```

```python
import functools

import jax
import jax.numpy as jnp
from jax import lax
from jax.experimental import pallas as pl
from jax.experimental.pallas import tpu as pltpu

EPS = 1e-6
POOL_WINDOWS = (2, 4, 8, 16)
POOL_HALO = 8
POOL_PAD = 16
POOL_BLOCK = 128
HEAD_DIM = 128
TOP_K = 2
V7X_VMEM_LIMIT = 56 * 1024 * 1024
LANES = 128
SUBLANES = 8
MOE_ROW_TILE = 512
MOE_FF_TILE = 512
MOE_DISPATCH_TILE = 512
MOE_COMBINE_TILE = 256
ROW_DMA_UNROLL = 8
EPILOGUE_COLS = 256
BF16_SUBLANES = 16

bf16 = jnp.bfloat16
f32 = jnp.float32


def _cparams(sem):
    return pltpu.CompilerParams(dimension_semantics=sem, vmem_limit_bytes=V7X_VMEM_LIMIT)


def _tile(n, pref):
    t = min(n, pref)
    while n % t:
        t //= 2
    return t


def _rms(x, g):
    ms = jnp.mean(x * x, axis=-1, keepdims=True)
    return x * lax.rsqrt(ms + EPS) * g


def _silu(x):
    return x * jax.nn.sigmoid(x)


def _seq_flags(tok, n_tok, n_prompt, seq_p, seq_s):
    in_p = tok < n_prompt
    rel = jnp.where(in_p, tok, tok - n_prompt)
    slen = jnp.where(in_p, seq_p, seq_s)
    is_start = (rel % slen) == 0
    is_end = ((rel + n_tok) % slen) == 0
    return is_start, is_end


def _rmsnorm_kernel(x_ref, g_ref, o_ref):
    o_ref[...] = _rms(x_ref[...], g_ref[...]).astype(o_ref.dtype)


def _rmsnorm(x, g, out_dtype, row_start=0, n_rows=None):
    n, d = x.shape
    n_rows = n if n_rows is None else n_rows
    tr = _tile(n_rows, 512)
    off = row_start // tr
    return pl.pallas_call(
        _rmsnorm_kernel,
        grid=(n_rows // tr,),
        in_specs=[pl.BlockSpec((tr, d), lambda i: (i + off, 0)),
                  pl.BlockSpec((1, d), lambda i: (0, 0))],
        out_specs=pl.BlockSpec((tr, d), lambda i: (i, 0)),
        out_shape=jax.ShapeDtypeStruct((n_rows, d), out_dtype),
        compiler_params=_cparams(("parallel",)),
    )(x, g.reshape(1, d))


def _pool_kernel(ap_ref, a_ref, an_ref, bp_ref, b_ref, bn_ref, gmix_ref, w_ref, scale_ref, gffn_ref,
                 x1_ref, h1_ref, hi_s, lo_s, *, seq):
    t, d = a_ref.shape
    dg = d // len(POOL_WINDOWS)
    tok = pl.program_id(0) * t
    is_start, is_end = _seq_flags(tok, t, *seq)
    in_a = tok < seq[0]
    g = gmix_ref[...]
    x = jnp.where(in_a, a_ref[...], b_ref[...])
    h = _rms(x, g)
    h_prev = jnp.where(is_start, 0.0, _rms(jnp.where(in_a, ap_ref[...], bp_ref[...]), g))
    h_next = jnp.where(is_end, 0.0, _rms(jnp.where(in_a, an_ref[...], bn_ref[...]), g))
    zeros = jnp.zeros_like(h_prev)

    def put(r0, val):
        hi = val.astype(bf16)
        hi_s[r0:r0 + val.shape[0], :] = hi
        lo_s[r0:r0 + val.shape[0], :] = (val - hi.astype(f32)).astype(bf16)

    put(0, jnp.concatenate([zeros, h_prev], axis=0))
    put(POOL_PAD, h)
    put(POOL_PAD + t, jnp.concatenate([h_next, zeros], axis=0))
    tail = hi_s.shape[0] - (2 * POOL_PAD + t)
    hi_s[2 * POOL_PAD + t:, :] = jnp.zeros((tail, d), bf16)
    lo_s[2 * POOL_PAD + t:, :] = jnp.zeros((tail, d), bf16)

    rr = lax.broadcasted_iota(jnp.int32, (POOL_BLOCK, 2 * POOL_BLOCK), 0)
    cc = lax.broadcasted_iota(jnp.int32, (POOL_BLOCK, 2 * POOL_BLOCK), 1)
    r = lax.broadcasted_iota(jnp.int32, (t, 1), 0)
    for gi, w in enumerate(POOL_WINDOWS):
        c0 = gi * dg
        half = w // 2
        band = ((cc >= POOL_PAD + rr - half) & (cc < POOL_PAD + rr + half)).astype(bf16)
        sums = []
        for r0 in range(0, t, POOL_BLOCK):
            sums.append(jnp.dot(band, hi_s[r0:r0 + 2 * POOL_BLOCK, c0:c0 + dg], preferred_element_type=f32)
                        + jnp.dot(band, lo_s[r0:r0 + 2 * POOL_BLOCK, c0:c0 + dg], preferred_element_type=f32))
        s = jnp.concatenate(sums, axis=0)
        lo = jnp.where(is_start, jnp.maximum(r - half, 0), r - half)
        hi = jnp.where(is_end, jnp.minimum(r + half, t), r + half)
        cnt = (hi - lo).astype(f32)
        dlt = s / cnt - h[:, c0:c0 + dg]
        y = jnp.dot(dlt.astype(bf16), w_ref[gi], preferred_element_type=f32)
        x1_ref[:, c0:c0 + dg] = x[:, c0:c0 + dg] + y * scale_ref[:, c0:c0 + dg]
    h1_ref[...] = _rms(x1_ref[...], gffn_ref[...]).astype(h1_ref.dtype)


def _stream_specs(rows, first_tile, t, d):
    hb = t // POOL_HALO
    last_halo = rows // POOL_HALO - 1
    last_tile = rows // t - 1
    return [
        pl.BlockSpec((POOL_HALO, d), lambda i: (jnp.clip((i - first_tile) * hb - 1, 0, last_halo), 0)),
        pl.BlockSpec((t, d), lambda i: (jnp.clip(i - first_tile, 0, last_tile), 0)),
        pl.BlockSpec((POOL_HALO, d), lambda i: (jnp.clip((i - first_tile + 1) * hb, 0, last_halo), 0)),
    ]


def _pool_layer(xa, xb, g_mix, w_grp, scale, g_ffn, seq):
    d = xa.shape[1]
    n = xa.shape[0] + xb.shape[0]
    t = _tile(min(seq[1], seq[2]), 256)
    row = lambda i: (0, 0)
    return pl.pallas_call(
        functools.partial(_pool_kernel, seq=seq),
        grid=(n // t,),
        in_specs=_stream_specs(xa.shape[0], 0, t, d) + _stream_specs(xb.shape[0], xa.shape[0] // t, t, d) + [
            pl.BlockSpec((1, d), row),
            pl.BlockSpec(w_grp.shape, lambda i: (0, 0, 0)),
            pl.BlockSpec((1, d), row),
            pl.BlockSpec((1, d), row),
        ],
        out_specs=[pl.BlockSpec((t, d), lambda i: (i, 0)),
                   pl.BlockSpec((t, d), lambda i: (i, 0))],
        out_shape=[jax.ShapeDtypeStruct((n, d), f32), jax.ShapeDtypeStruct((n, d), bf16)],
        scratch_shapes=[pltpu.VMEM((t + POOL_BLOCK, d), bf16)] * 2,
        compiler_params=_cparams(("parallel",)),
    )(xa, xa, xa, xb, xb, xb, g_mix.reshape(1, d), w_grp, scale.reshape(1, d), g_ffn.reshape(1, d))


def _matmul_call_with_casts(kernel, grid, in_specs, out_specs, out_shapes, args, to_cast):
    n_steps = grid[0] * grid[1]
    plans = []
    for stacked, layer, scale in to_cast:
        cols = stacked.shape[-1]
        rows = stacked[0].size // cols
        steps = 1
        while steps * 2 <= n_steps and rows % (steps * 2 * BF16_SUBLANES) == 0:
            steps *= 2
        plans.append((rows, cols, steps, rows // steps, layer))
    scaled = [k for k, (_, _, scale) in enumerate(to_cast) if scale is not None]
    n_in, n_side, n_scale, n_out = len(in_specs), len(to_cast), len(scaled), len(out_specs)

    def step_block(steps):
        return lambda i, j: jnp.minimum(i * grid[1] + j, steps - 1)

    def src_spec(cols, steps, rb, layer):
        blk = step_block(steps)
        return pl.BlockSpec((None, rb, cols), lambda i, j: (layer, blk(i, j), 0))

    def dst_spec(cols, steps, rb):
        blk = step_block(steps)
        return pl.BlockSpec((rb, cols), lambda i, j: (blk(i, j), 0))

    def body(*refs):
        first_out = n_in + n_side + n_scale
        kernel(*refs[:n_in], *refs[first_out:first_out + n_out])
        step = pl.program_id(0) * grid[1] + pl.program_id(1)
        for k, plan in enumerate(plans):
            scale_ref = refs[n_in + n_side + scaled.index(k)] if k in scaled else None

            @pl.when(step < plan[2])
            def _(src=refs[n_in + k], dst=refs[first_out + n_out + k], scale_ref=scale_ref):
                val = src[...] if scale_ref is None else src[...] * scale_ref[...]
                dst[...] = val.astype(bf16)

    outs = pl.pallas_call(
        body,
        grid=grid,
        in_specs=list(in_specs) + [src_spec(c, s, rb, layer) for _, c, s, rb, layer in plans]
        + [dst_spec(1, plans[k][2], plans[k][3]) for k in scaled],
        out_specs=list(out_specs) + [dst_spec(c, s, rb) for _, c, s, rb, _ in plans],
        out_shape=list(out_shapes) + [jax.ShapeDtypeStruct((r, c), bf16) for r, c, _, _, _ in plans],
        compiler_params=_cparams(("arbitrary", "arbitrary")),
    )(*args,
      *[stacked.reshape(stacked.shape[0], r, c) for (stacked, _, _), (r, c, _, _, _) in zip(to_cast, plans)],
      *[to_cast[k][2].reshape(-1, 1) for k in scaled])
    return outs[:n_out], [o.reshape(stacked.shape[1:]) for o, (stacked, _, _) in zip(outs[n_out:], to_cast)]


def _glu_up_kernel(a_ref, wg_ref, wu_ref, o_ref):
    a = a_ref[...]
    gate = jnp.dot(a, wg_ref[...], preferred_element_type=f32)
    up = jnp.dot(a, wu_ref[...], preferred_element_type=f32)
    o_ref[...] = (_silu(gate) * up).astype(o_ref.dtype)


def _glu_up(a, wg, wu, to_cast=()):
    m, k = a.shape
    f = wg.shape[1]
    tm = _tile(m, 1024)
    tn = _tile(f, 512)
    wspec = pl.BlockSpec((k, tn), lambda i, j: (0, j))
    (act,), rounded = _matmul_call_with_casts(
        _glu_up_kernel, (m // tm, f // tn),
        [pl.BlockSpec((tm, k), lambda i, j: (i, 0)), wspec, wspec],
        [pl.BlockSpec((tm, tn), lambda i, j: (i, j))],
        [jax.ShapeDtypeStruct((m, f), bf16)], (a, wg, wu), to_cast)
    return act, rounded


def _down_kernel(a_ref, w_ref, res_ref, o_ref, *norm_refs):
    o = res_ref[...] + jnp.dot(a_ref[...], w_ref[...], preferred_element_type=f32)
    o_ref[...] = o
    if not norm_refs:
        return
    o16_ref, sq_ref = norm_refs
    o16_ref[...] = o.astype(o16_ref.dtype)
    sq = o * o
    part = sq[:, 0:LANES]
    for c0 in range(LANES, sq.shape[1], LANES):
        part = part + sq[:, c0:c0 + LANES]

    @pl.when(pl.program_id(1) == 0)
    def _():
        sq_ref[...] = part

    @pl.when(pl.program_id(1) > 0)
    def _():
        sq_ref[...] += part


def _down_res(a, w, res, tm, tn, norm_rows=False, to_cast=()):
    m, kk = a.shape
    n = w.shape[1]
    tm = _tile(m, tm)
    tn = _tile(n, tn)
    tile = pl.BlockSpec((tm, tn), lambda i, j: (i, j))
    out_specs, out_shapes = [tile], [jax.ShapeDtypeStruct((m, n), f32)]
    if norm_rows:
        out_specs += [tile, pl.BlockSpec((tm, LANES), lambda i, j: (i, 0))]
        out_shapes += [jax.ShapeDtypeStruct((m, n), bf16), jax.ShapeDtypeStruct((m, LANES), f32)]
    return _matmul_call_with_casts(
        _down_kernel, (m // tm, n // tn),
        [pl.BlockSpec((tm, kk), lambda i, j: (i, 0)), pl.BlockSpec((kk, tn), lambda i, j: (0, j)), tile],
        out_specs, out_shapes, (a, w, res), to_cast)


def _column_blocks(a_ref, sq_ref, w_ref):
    a = a_ref[...]
    inv = lax.rsqrt(jnp.sum(sq_ref[...], axis=1, keepdims=True) / a.shape[1] + EPS)
    tn = w_ref.shape[1]
    cb = min(tn, EPILOGUE_COLS)
    for c0 in range(0, tn, cb):
        yield c0, inv * jnp.dot(a, w_ref[:, c0:c0 + cb], preferred_element_type=f32)


def _store_head_block(o_ref, c0, val):
    for hh in range(val.shape[1] // HEAD_DIM):
        o_ref[c0 // HEAD_DIM + hh] = val[:, hh * HEAD_DIM:(hh + 1) * HEAD_DIM].astype(o_ref.dtype)


def _proj_silu_heads_kernel(a_ref, sq_ref, w_ref, o_ref):
    for c0, r in _column_blocks(a_ref, sq_ref, w_ref):
        _store_head_block(o_ref, c0, _silu(r))


def _proj_heads_kernel(a_ref, sq_ref, w_ref, o_ref):
    for c0, r in _column_blocks(a_ref, sq_ref, w_ref):
        _store_head_block(o_ref, c0, r)


def _proj_silu_kernel(a_ref, sq_ref, w_ref, o_ref):
    for c0, r in _column_blocks(a_ref, sq_ref, w_ref):
        o_ref[:, c0:c0 + r.shape[1]] = _silu(r).astype(o_ref.dtype)


def _proj_forget_kernel(a_ref, sq_ref, w_ref, lb_ref, k_ref, g_ref, *, layer):
    logits = lb_ref[...]
    ex = jnp.exp(logits - jnp.max(logits, axis=0, keepdims=True))
    sm = ex / jnp.sum(ex, axis=0, keepdims=True)
    lb_all = jnp.sum(sm[0:layer + 1], axis=0, keepdims=True) - sm[0:1]
    for c0, r in _column_blocks(a_ref, sq_ref, w_ref):
        lb = lb_all[:, c0:c0 + r.shape[1]]
        f = lb + (1.0 - lb) * jax.nn.sigmoid(r)
        _store_head_block(k_ref, c0, 1.0 - f)
        _store_head_block(g_ref, c0, jnp.log2(f))


def _proj_section(kernel, a, sq, w, section, out_heads, n_out=1, extra=None):
    m, k = a.shape
    d = k
    tm = _tile(m, 1024)
    tn = _tile(d, 1024)
    nb = d // tn
    in_specs = [pl.BlockSpec((tm, k), lambda i, j: (i, 0)),
                pl.BlockSpec((tm, LANES), lambda i, j: (i, 0)),
                pl.BlockSpec((k, tn), lambda i, j: (0, section * nb + j))]
    args = [a, sq, w]
    if extra is not None:
        in_specs.append(pl.BlockSpec((extra.shape[0], tn), lambda i, j: (0, j)))
        args.append(extra)
    if out_heads:
        hpb = tn // HEAD_DIM
        ospec = pl.BlockSpec((hpb, tm, HEAD_DIM), lambda i, j: (j, i, 0))
        oshape = jax.ShapeDtypeStruct((d // HEAD_DIM, m, HEAD_DIM), bf16)
    else:
        ospec = pl.BlockSpec((tm, tn), lambda i, j: (i, j))
        oshape = jax.ShapeDtypeStruct((m, d), bf16)
    out = pl.pallas_call(
        kernel,
        grid=(m // tm, nb),
        in_specs=in_specs,
        out_specs=[ospec] * n_out if n_out > 1 else ospec,
        out_shape=[oshape] * n_out if n_out > 1 else oshape,
        compiler_params=_cparams(("parallel", "arbitrary")),
    )(*args)
    return out


MAX_DIRECT_LOG2_DECAY = 100.0
BW_CHUNKS_PER_STEP = 2
HEAD_UNROLL = 8


def _dot_nt(a, b):
    return lax.dot_general(a, b, (((1,), (1,)), ((), ())), preferred_element_type=f32)


def _dot_tn(a, b):
    return lax.dot_general(a, b, (((0,), (0,)), ((), ())), preferred_element_type=f32)


def _mid_ref(p, level, offset):
    c, w = p.shape
    blk = 2 << level
    p3 = p.reshape(c // blk, blk, w)
    return jnp.broadcast_to(p3[:, offset:offset + 1, :], p3.shape).reshape(c, w)


def _neg_abs_exponents(gf, gb, b, x, level, rmod4):
    if level == 0:
        odd = (rmod4 & 1) == 1
        return jnp.where(odd, gf, 0.0), jnp.where(odd, 0.0, gb)
    if level == 1:
        c = gf.shape[0]
        gf_up = pltpu.roll(gf, c - 1, 0)
        gf_dn = pltpu.roll(gf, 1, 0)
        gb_up = pltpu.roll(gb, c - 1, 0)
        gb_dn = pltpu.roll(gb, 1, 0)
        ef = jnp.where(rmod4 == 0, gf_up, jnp.where(rmod4 == 1, 0.0, jnp.where(rmod4 == 2, gf, gf + gf_dn)))
        eb = jnp.where(rmod4 == 0, gb + gb_up, jnp.where(rmod4 == 1, gb, jnp.where(rmod4 == 2, 0.0, gb_dn)))
        return ef, eb
    half = 1 << level
    ef = b - _mid_ref(b, level, half - 1)
    eb = x - _mid_ref(x, level, half)
    return -jnp.abs(ef), -jnp.abs(eb)


def _chunk_prefix(tri_incl, tri_excl, gf16, gb16):
    b = jnp.dot(tri_incl, gf16, preferred_element_type=f32)
    x = jnp.dot(tri_excl, gb16, preferred_element_type=f32)
    return b, x


def _scan_bw_kernel(kb_ref, gb_ref, v_ref, s_ref, st_ref, *, seq, n_steps, c):
    n_heads = kb_ref.shape[0]
    per_step = kb_ref.shape[1] // c
    step = n_steps - 1 - pl.program_id(0)

    row = lax.broadcasted_iota(jnp.int32, (c, c), 0)
    col = lax.broadcasted_iota(jnp.int32, (c, c), 1)
    tri_excl = (col < row).astype(bf16)

    for part in reversed(range(per_step)):
        _, is_end = _seq_flags((step * per_step + part) * c, c, *seq)

        @pl.when(is_end)
        def _():
            st_ref[...] = jnp.zeros_like(st_ref)

        rows = pl.ds(part * c, c)

        def head(h, carry, part=part, rows=rows):
            gb16 = gb_ref[h, rows, :]
            x = jnp.dot(tri_excl, gb16, preferred_element_type=f32)
            tot = x[c - 1:c, :] + gb16[c - 1:c, :].astype(f32)
            st = st_ref[h]
            s_ref[part, h] = st.astype(s_ref.dtype)
            kx = (kb_ref[h, rows, :].astype(f32) * jnp.exp2(x)).astype(bf16)
            st_ref[h] = st * jnp.exp2(tot) + _dot_tn(v_ref[h, rows, :], kx)
            return carry

        lax.fori_loop(0, n_heads, head, 0, unroll=4 * HEAD_UNROLL)


def _scan_bw_states(kb, gb, v, seq, c):
    n_heads, n, dk = kb.shape
    n_chunks = n // c
    per_step = BW_CHUNKS_PER_STEP if (min(seq[1], seq[2]) // c) % BW_CHUNKS_PER_STEP == 0 else 1
    n_steps = n_chunks // per_step
    spec = pl.BlockSpec((n_heads, per_step * c, dk), lambda i: (0, n_steps - 1 - i, 0))
    return pl.pallas_call(
        functools.partial(_scan_bw_kernel, seq=seq, n_steps=n_steps, c=c),
        grid=(n_steps,),
        in_specs=[spec, spec, spec],
        out_specs=pl.BlockSpec((per_step, n_heads, dk, dk), lambda i: (n_steps - 1 - i, 0, 0, 0)),
        out_shape=jax.ShapeDtypeStruct((n_chunks, n_heads, dk, dk), bf16),
        scratch_shapes=[pltpu.VMEM((n_heads, dk, dk), f32)],
        compiler_params=_cparams(("arbitrary",)),
    )(kb, gb, v)


def _scan_kernel(q_ref, kf_ref, kb_ref, gf_ref, gb_ref, v_ref, sb_ref, tril_ref, triu_ref, gate_ref,
                 gain_ref, out_ref, st_ref, o_s, qf_s, kf_s, qb_s, kb_s, qin_s, kx_s, sc_s, dec_s, tot_s,
                 *, seq):
    n_heads, c, dk = q_ref.shape
    n_levels = c.bit_length() - 1
    mid = c // 2
    is_start, _ = _seq_flags(pl.program_id(0) * c, c, *seq)

    @pl.when(is_start)
    def _():
        st_ref[...] = jnp.zeros_like(st_ref)

    row = lax.broadcasted_iota(jnp.int32, (c, c), 0)
    col = lax.broadcasted_iota(jnp.int32, (c, c), 1)
    tri_incl = (col <= row).astype(bf16)
    tri_excl = (col < row).astype(bf16)

    def finish(h, kf_dec, b_last, q_in, scores, diag):
        v = v_ref[h]
        st_f = st_ref[h]
        st_cat = jnp.concatenate([st_f.astype(bf16), sb_ref[h]], axis=1)
        o = _dot_nt(q_in, st_cat) + jnp.dot(scores.astype(bf16), v, preferred_element_type=f32)
        o_s[h] = o if diag is None else o + diag * v.astype(f32)
        st_ref[h] = st_f * jnp.exp2(b_last) + _dot_tn(v, kf_dec.astype(bf16))

    def direct_operands(h, carry):
        q = q_ref[h].astype(f32)
        gb16 = gb_ref[h]
        b, x = _chunk_prefix(tri_incl, tri_excl, gf_ref[h], gb16)
        b_mid = b[mid - 1:mid, :]
        x_mid = x[mid:mid + 1, :]
        db = b - b_mid
        dx = x - x_mid
        q_f = q * jnp.exp2(db)
        k_f = kf_ref[h].astype(f32) * jnp.exp2(-db)
        q_b = q * jnp.exp2(-dx)
        k_b = kb_ref[h].astype(f32) * jnp.exp2(dx)
        b_last = b[c - 1:c, :]
        x_tot = x[c - 1:c, :] + gb16[c - 1:c, :].astype(f32)
        qf_s[h] = q_f.astype(bf16)
        kf_s[h] = k_f.astype(bf16)
        qb_s[h] = q_b.astype(bf16)
        kb_s[h] = k_b.astype(bf16)
        qin_s[h] = jnp.concatenate([q_f * jnp.exp2(b_mid), q_b * jnp.exp2(x_tot - x_mid)], axis=1).astype(bf16)
        kx_s[h] = (k_f * jnp.exp2(b_last - b_mid)).astype(bf16)
        dec_s[h] = jnp.broadcast_to(jnp.exp2(b_last), dec_s.shape[1:])
        tot_s[h] = jnp.broadcast_to(jnp.minimum(b_last, x_tot), tot_s.shape[1:])
        return carry

    def direct_scores(h, carry):
        sc_s[h] = (jnp.where(tril_ref[...] != 0.0, _dot_nt(qf_s[h], kf_s[h]), 0.0)
                   + jnp.where(triu_ref[...] != 0.0, _dot_nt(qb_s[h], kb_s[h]), 0.0)).astype(bf16)
        return carry

    def direct_output(h, carry):
        v = v_ref[h]
        st_f = st_ref[h]
        st_cat = jnp.concatenate([st_f.astype(bf16), sb_ref[h]], axis=1)
        o_s[h] = _dot_nt(qin_s[h], st_cat) + jnp.dot(sc_s[h], v, preferred_element_type=f32)
        st_ref[h] = st_f * dec_s[h][0:1, :] + _dot_tn(v, kx_s[h])
        return carry

    def head_levelled(h, carry):
        rtok = lax.broadcasted_iota(jnp.int32, (c, dk), 0)
        rmod4 = rtok & 3
        q = q_ref[h].astype(f32)
        kf = kf_ref[h].astype(f32)
        kb = kb_ref[h].astype(f32)
        gf16 = gf_ref[h]
        gb16 = gb_ref[h]
        gf = gf16.astype(f32)
        gb = gb16.astype(f32)
        b, x = _chunk_prefix(tri_incl, tri_excl, gf16, gb16)
        scores = jnp.zeros((c, c), f32)
        for l in range(n_levels):
            ef, eb = _neg_abs_exponents(gf, gb, b, x, l, rmod4)
            af = jnp.exp2(ef)
            ab = jnp.exp2(eb)
            up = ((rtok >> l) & 1) == 1
            lhs = jnp.concatenate([jnp.where(up, q * af, 0.0), jnp.where(up, 0.0, q * ab)], axis=1)
            rhs = jnp.concatenate([jnp.where(up, 0.0, kf * af), jnp.where(up, kb * ab, 0.0)], axis=1)
            p = _dot_nt(lhs.astype(bf16), rhs.astype(bf16))
            if l < n_levels - 1:
                p = jnp.where((row >> (l + 1)) == (col >> (l + 1)), p, 0.0)
            scores = scores + p
        b_last = b[c - 1:c, :]
        x_tot = x[c - 1:c, :] + gb[c - 1:c, :]
        q_in = jnp.concatenate([q * jnp.exp2(b), q * jnp.exp2(x_tot - x)], axis=1).astype(bf16)
        diag = jnp.sum(q * (kf + kb), axis=1, keepdims=True)
        finish(h, kf * jnp.exp2(b_last - b), b_last, q_in, scores, diag)
        return carry

    lax.fori_loop(0, n_heads, direct_operands, 0, unroll=4 * HEAD_UNROLL)
    direct_ok = jnp.min(tot_s[...]) >= -MAX_DIRECT_LOG2_DECAY

    @pl.when(direct_ok)
    def _():
        lax.fori_loop(0, n_heads, direct_scores, 0, unroll=4 * HEAD_UNROLL)
        lax.fori_loop(0, n_heads, direct_output, 0, unroll=4 * HEAD_UNROLL)

    @pl.when(jnp.logical_not(direct_ok))
    def _():
        lax.fori_loop(0, n_heads, head_levelled, 0)

    sq = jnp.zeros((c, dk), f32)
    for hh in range(n_heads):
        o = o_s[hh]
        sq = sq + o * o
    inv = lax.rsqrt(jnp.sum(sq, axis=1, keepdims=True) / (n_heads * dk) + EPS)
    for hh in range(n_heads):
        cols = slice(hh * dk, (hh + 1) * dk)
        out_ref[:, cols] = (o_s[hh] * inv * gain_ref[:, cols] * gate_ref[:, cols].astype(f32)
                            ).astype(out_ref.dtype)


def _scan(q, kf, kb, gf, gb, v, s_bw, gate, gain, seq, c):
    n_heads, n, dk = q.shape
    d = n_heads * dk
    n_chunks = n // c
    spec = pl.BlockSpec((n_heads, c, dk), lambda i: (0, i, 0))
    tri = pl.BlockSpec((c, c), lambda i: (0, 0))
    tril = jnp.tril(jnp.ones((c, c), f32))
    return pl.pallas_call(
        functools.partial(_scan_kernel, seq=seq),
        grid=(n_chunks,),
        in_specs=[spec] * 6 + [pl.BlockSpec((None, n_heads, dk, dk), lambda i: (i, 0, 0, 0)), tri, tri,
                               pl.BlockSpec((c, d), lambda i: (i, 0)),
                               pl.BlockSpec((1, d), lambda i: (0, 0))],
        out_specs=pl.BlockSpec((c, d), lambda i: (i, 0)),
        out_shape=jax.ShapeDtypeStruct((n, d), bf16),
        scratch_shapes=[pltpu.VMEM((n_heads, dk, dk), f32), pltpu.VMEM((n_heads, c, dk), f32)]
        + [pltpu.VMEM((n_heads, c, dk), bf16)] * 4
        + [pltpu.VMEM((n_heads, c, 2 * dk), bf16), pltpu.VMEM((n_heads, c, dk), bf16),
           pltpu.VMEM((n_heads, c, c), bf16)] + [pltpu.VMEM((n_heads, 8, dk), f32)] * 2,
        compiler_params=_cparams(("arbitrary",)),
    )(q, kf, kb, gf, gb, v, s_bw, tril, tril.T, gate, gain.reshape(1, d))


def _pack_halves(x):
    half = x.shape[1] // 2
    bits = lax.bitcast_convert_type(x.astype(bf16).astype(f32), jnp.int32)
    return lax.shift_right_logical(bits[:, :half], 16) | (bits[:, half:] & jnp.int32(-65536))


def _unpack_halves(words):
    lo = lax.bitcast_convert_type(lax.shift_left(words, 16), f32)
    hi = lax.bitcast_convert_type(words & jnp.int32(-65536), f32)
    return lo, hi


def _router_kernel(x_ref, g_ref, r2_ref, h_ref, e_ref, rank_ref, w_ref, cnt_ref, run_ref,
                   *, n_experts):
    @pl.when(pl.program_id(0) == 0)
    def _():
        run_ref[...] = jnp.zeros_like(run_ref)

    h = _rms(x_ref[...], g_ref[...])
    h_ref[...] = _pack_halves(h)
    hi = h.astype(bf16)
    lo = (h - hi.astype(f32)).astype(bf16)
    tr = h.shape[0]
    prod = jnp.dot(jnp.concatenate([hi, lo], axis=0), r2_ref[...], preferred_element_type=f32)
    logits = prod[:tr, :LANES] + prod[:tr, LANES:] + prod[tr:, :LANES]
    lane = lax.broadcasted_iota(jnp.int32, logits.shape, 1)
    neg = jnp.float32(-jnp.inf)
    logits = jnp.where(lane < n_experts, logits, neg)
    m1 = jnp.max(logits, axis=1, keepdims=True)
    i1 = jnp.min(jnp.where(logits == m1, lane, LANES), axis=1, keepdims=True)
    rest = jnp.where(lane == i1, neg, logits)
    m2 = jnp.max(rest, axis=1, keepdims=True)
    i2 = jnp.min(jnp.where(rest == m2, lane, LANES), axis=1, keepdims=True)
    e2 = jnp.exp(m2 - m1)
    w_ref[...] = jnp.concatenate([1.0 / (1.0 + e2), e2 / (1.0 + e2)], axis=1)
    e_ref[...] = jnp.concatenate([i1, i2], axis=1)
    member = jnp.where((lane == i1) | (lane == i2), 1.0, 0.0)
    row = lax.broadcasted_iota(jnp.int32, (tr, tr), 0)
    col = lax.broadcasted_iota(jnp.int32, (tr, tr), 1)
    before = jnp.dot((col < row).astype(bf16), member.astype(bf16), preferred_element_type=f32)
    before = before + run_ref[...]
    r1 = jnp.sum(jnp.where(lane == i1, before, 0.0), axis=1, keepdims=True)
    r2 = jnp.sum(jnp.where(lane == i2, before, 0.0), axis=1, keepdims=True)
    rank_ref[...] = jnp.concatenate([r1, r2], axis=1).astype(jnp.int32)
    run = run_ref[...] + jnp.sum(member, axis=0, keepdims=True)
    run_ref[...] = run
    cnt_ref[...] = run.astype(jnp.int32)


def _router(x, g, router):
    n, d = x.shape
    e = router.shape[1]
    rpad = jnp.zeros((d, LANES), f32).at[:, :e].set(router)
    rhi = rpad.astype(bf16)
    r2 = jnp.concatenate([rhi, (rpad - rhi.astype(f32)).astype(bf16)], axis=1)
    tr = _tile(n, 512)
    pair = pl.BlockSpec((tr, TOP_K), lambda i: (i, 0))
    return pl.pallas_call(
        functools.partial(_router_kernel, n_experts=e),
        grid=(n // tr,),
        in_specs=[pl.BlockSpec((tr, d), lambda i: (i, 0)),
                  pl.BlockSpec((1, d), lambda i: (0, 0)),
                  pl.BlockSpec((d, 2 * LANES), lambda i: (0, 0))],
        out_specs=[pl.BlockSpec((tr, d // 2), lambda i: (i, 0)), pair, pair, pair,
                   pl.BlockSpec((1, LANES), lambda i: (0, 0))],
        out_shape=[jax.ShapeDtypeStruct((n, d // 2), jnp.int32),
                   jax.ShapeDtypeStruct((n, TOP_K), jnp.int32),
                   jax.ShapeDtypeStruct((n, TOP_K), jnp.int32),
                   jax.ShapeDtypeStruct((n, TOP_K), f32),
                   jax.ShapeDtypeStruct((1, LANES), jnp.int32)],
        scratch_shapes=[pltpu.VMEM((1, LANES), f32)],
        compiler_params=_cparams(("arbitrary",)),
    )(x, g.reshape(1, d), r2)


def _row_copies(pos_ref, i, src, dst, sem, gather):
    out = []
    for k in range(TOP_K):
        p = pos_ref[k, i]
        if gather:
            out.append(pltpu.make_async_copy(src.at[pl.ds(p, 1)], dst.at[k, pl.ds(i, 1)], sem))
        else:
            out.append(pltpu.make_async_copy(src.at[pl.ds(i, 1)], dst.at[pl.ds(p, 1)], sem))
    return out


def _start_rows(n_rows, pos_ref, src, dst, sem, gather):
    def start(i, carry):
        for cp in _row_copies(pos_ref, i, src, dst, sem, gather):
            cp.start()
        return carry

    lax.fori_loop(0, n_rows, start, 0, unroll=ROW_DMA_UNROLL)


def _wait_rows(n_rows, pos_ref, src, dst, sem, gather):
    def wait(i, carry):
        for cp in _row_copies(pos_ref, i, src, dst, sem, gather):
            cp.wait()
        return carry

    lax.fori_loop(0, n_rows, wait, 0, unroll=ROW_DMA_UNROLL)


def _all_rows(n_rows, pos_ref, src, dst, sem, gather):
    _start_rows(n_rows, pos_ref, src, dst, sem, gather)
    _wait_rows(n_rows, pos_ref, src, dst, sem, gather)


def _moe_dispatch_kernel(nv_ref, pos_ref, h_ref, xs_ref, zero_s, sem):
    tm = zero_s.shape[0]

    @pl.when(pl.program_id(0) == 0)
    def _():
        zero_s[...] = jnp.zeros_like(zero_s)

        def fill(r, carry):
            @pl.when(nv_ref[r] < tm)
            def _():
                cp = pltpu.make_async_copy(zero_s, xs_ref.at[pl.ds(pl.multiple_of(r * tm, tm), tm)], sem)
                cp.start()
                cp.wait()
            return carry

        lax.fori_loop(0, xs_ref.shape[0] // tm, fill, 0)

    _all_rows(h_ref.shape[0], pos_ref, h_ref, xs_ref, sem, gather=False)


def _moe_dispatch(hp, pos3, tile_rows, tm):
    n, dw = hp.shape
    n_tiles, _, tr = pos3.shape
    grid_spec = pltpu.PrefetchScalarGridSpec(
        num_scalar_prefetch=1,
        grid=(n_tiles,),
        in_specs=[pl.BlockSpec((None, TOP_K, tr), lambda i, nv: (i, 0, 0), memory_space=pltpu.SMEM),
                  pl.BlockSpec((tr, dw), lambda i, nv: (i, 0))],
        out_specs=pl.BlockSpec(memory_space=pl.ANY),
        scratch_shapes=[pltpu.VMEM((tm, dw), jnp.int32), pltpu.SemaphoreType.DMA(())],
    )
    return pl.pallas_call(
        _moe_dispatch_kernel,
        grid_spec=grid_spec,
        out_shape=jax.ShapeDtypeStruct((tile_rows.shape[0] * tm, dw), jnp.int32),
        compiler_params=_cparams(("arbitrary",)),
    )(tile_rows, pos3, hp)


def _moe_up_kernel(te_ref, nv_ref, xs_ref, wg_ref, wu_ref, a_ref, x_s):
    r = pl.program_id(0)
    half = xs_ref.shape[1]

    @pl.when(nv_ref[r] > 0)
    def _():
        @pl.when(pl.program_id(1) == 0)
        def _():
            lo, hi = _unpack_halves(xs_ref[...])
            x_s[:, :half] = lo.astype(bf16)
            x_s[:, half:] = hi.astype(bf16)

        x = x_s[...]
        gate = jnp.dot(x, wg_ref[...], preferred_element_type=f32)
        up = jnp.dot(x, wu_ref[...], preferred_element_type=f32)
        a_ref[...] = (_silu(gate) * up).astype(a_ref.dtype)

    @pl.when(nv_ref[r] == 0)
    def _():
        a_ref[...] = jnp.zeros_like(a_ref)


def _moe_down_kernel(te_ref, nv_ref, a_ref, wd_ref, y_ref):
    @pl.when(nv_ref[pl.program_id(0)] > 0)
    def _():
        y_ref[...] = _pack_halves(jnp.dot(a_ref[...], wd_ref[...], preferred_element_type=f32))

    @pl.when(nv_ref[pl.program_id(0)] == 0)
    def _():
        y_ref[...] = jnp.zeros_like(y_ref)


def _moe_group(xs, tile_expert, tile_rows, wg, wu, wd, tm):
    p, dw = xs.shape
    _, d, fe = wg.shape
    tf = _tile(fe, MOE_FF_TILE)
    wspec = pl.BlockSpec((None, d, tf), lambda r, f, te, nv: (te[r], 0, f))
    act = pl.pallas_call(
        _moe_up_kernel,
        grid_spec=pltpu.PrefetchScalarGridSpec(
            num_scalar_prefetch=2,
            grid=(p // tm, fe // tf),
            in_specs=[pl.BlockSpec((tm, dw), lambda r, f, te, nv: (r, 0)), wspec, wspec],
            out_specs=pl.BlockSpec((tm, tf), lambda r, f, te, nv: (r, f)),
            scratch_shapes=[pltpu.VMEM((tm, d), bf16)],
        ),
        out_shape=jax.ShapeDtypeStruct((p, fe), bf16),
        compiler_params=_cparams(("arbitrary", "arbitrary")),
    )(tile_expert, tile_rows, xs, wg, wu)
    return pl.pallas_call(
        _moe_down_kernel,
        grid_spec=pltpu.PrefetchScalarGridSpec(
            num_scalar_prefetch=2,
            grid=(p // tm,),
            in_specs=[pl.BlockSpec((tm, fe), lambda r, te, nv: (r, 0)),
                      pl.BlockSpec((None, fe, d), lambda r, te, nv: (te[r], 0, 0))],
            out_specs=pl.BlockSpec((tm, dw), lambda r, te, nv: (r, 0)),
        ),
        out_shape=jax.ShapeDtypeStruct((p, dw), jnp.int32),
        compiler_params=_cparams(("arbitrary",)),
    )(tile_expert, tile_rows, act, wd)


def _moe_combine_kernel(pos_ref, pos_next_ref, x_ref, w_ref, g_ref, y_ref, o_ref, ybuf, sem, *, final_norm):
    tr, d = x_ref.shape
    half = d // 2
    i = pl.program_id(0)
    slot = i % 2

    @pl.when(i == 0)
    def _():
        _start_rows(tr, pos_ref, y_ref, ybuf.at[0], sem.at[0], gather=True)

    _wait_rows(tr, pos_ref, y_ref, ybuf.at[slot], sem.at[slot], gather=True)
    for r0 in range(0, tr, SUBLANES):
        rows = slice(r0, r0 + SUBLANES)
        for ii in range(r0, r0 + SUBLANES):
            for cp in _row_copies(pos_next_ref, ii, y_ref, ybuf.at[1 - slot], sem.at[1 - slot], True):
                cp.start()
        w = w_ref[rows, :]
        lo0, hi0 = _unpack_halves(ybuf[slot, 0, rows, :])
        lo1, hi1 = _unpack_halves(ybuf[slot, 1, rows, :])
        o_lo = x_ref[rows, :half] + (w[:, 0:1] * lo0 + w[:, 1:2] * lo1)
        o_hi = x_ref[rows, half:] + (w[:, 0:1] * hi0 + w[:, 1:2] * hi1)
        if final_norm:
            ms = (jnp.sum(o_lo * o_lo, axis=1, keepdims=True)
                  + jnp.sum(o_hi * o_hi, axis=1, keepdims=True)) / d
            inv = lax.rsqrt(ms + EPS)
            o_lo = o_lo * inv * g_ref[:, :half]
            o_hi = o_hi * inv * g_ref[:, half:]
        o_ref[rows, :half] = o_lo
        o_ref[rows, half:] = o_hi

    @pl.when(i == pl.num_programs(0) - 1)
    def _():
        _wait_rows(tr, pos_next_ref, y_ref, ybuf.at[1 - slot], sem.at[1 - slot], gather=True)


def _moe_combine(x, w, y, pos3, gain, final_norm, row_start, n_rows):
    d = x.shape[1]
    tr = pos3.shape[2]
    off = row_start // tr
    last = off + n_rows // tr - 1
    return pl.pallas_call(
        functools.partial(_moe_combine_kernel, final_norm=final_norm),
        grid=(n_rows // tr,),
        in_specs=[pl.BlockSpec((None, TOP_K, tr), lambda i: (i + off, 0, 0), memory_space=pltpu.SMEM),
                  pl.BlockSpec((None, TOP_K, tr), lambda i: (jnp.minimum(i + off + 1, last), 0, 0),
                               memory_space=pltpu.SMEM),
                  pl.BlockSpec((tr, d), lambda i: (i + off, 0)),
                  pl.BlockSpec((tr, TOP_K), lambda i: (i + off, 0)),
                  pl.BlockSpec((1, d), lambda i: (0, 0)),
                  pl.BlockSpec(memory_space=pl.ANY)],
        out_specs=pl.BlockSpec((tr, d), lambda i: (i, 0)),
        out_shape=jax.ShapeDtypeStruct((n_rows, d), f32),
        scratch_shapes=[pltpu.VMEM((2, TOP_K, tr, d // 2), jnp.int32), pltpu.SemaphoreType.DMA((2,))],
        compiler_params=_cparams(("arbitrary",)),
    )(pos3, pos3, x, w, gain.reshape(1, d), y)


def _moe_layer(x, g, router, wg, wu, wd, final_gain, parts):
    n, d = x.shape
    n_exp = router.shape[1]
    tm = MOE_ROW_TILE
    hp, eid, rank, w, cnt = _router(x, g, router)
    cnt = cnt[0, :n_exp]
    padded = (cnt + tm - 1) // tm * tm
    seg_end = jnp.cumsum(padded)
    seg_start = seg_end - padded
    pos = seg_start[eid] + rank
    n_rows_padded = TOP_K * n + n_exp * tm
    tile0 = jnp.arange(n_rows_padded // tm, dtype=jnp.int32) * tm
    tile_expert = jnp.minimum(jnp.sum(tile0[:, None] >= seg_end[None, :], axis=1), n_exp - 1).astype(jnp.int32)
    tile_rows = jnp.clip(cnt[tile_expert] - (tile0 - seg_start[tile_expert]), 0, tm).astype(jnp.int32)
    def by_tile(tr):
        return pos.reshape(n // tr, tr, TOP_K).transpose(0, 2, 1)

    pos3 = by_tile(_tile(n, MOE_COMBINE_TILE))
    xs = _moe_dispatch(hp, by_tile(_tile(n, MOE_DISPATCH_TILE)), tile_rows, tm)
    y = _moe_group(xs, tile_expert, tile_rows, wg, wu, wd, tm)
    if final_gain is None:
        return _moe_combine(x, w, y, pos3, g, False, 0, n)
    return [_moe_combine(x, w, y, pos3, final_gain, True, start, rows) for start, rows in parts]


def kernel(x_prompt, x_sample, norm_mix, norm_ffn, norm_final, pool_w, pool_scale, hgrn_w_in, hgrn_lb,
           hgrn_norm, hgrn_w_out, ffn_w_gate, ffn_w_up, ffn_w_down, moe_router, moe_w_gate, moe_w_up,
           moe_w_down):
    bp, sp, d = x_prompt.shape
    bs, ss, _ = x_sample.shape
    n_prompt = bp * sp
    n = n_prompt + bs * ss
    seq = (n_prompt, sp, ss)
    depth = norm_mix.shape[0]
    chunk = _tile(min(sp, ss), 128)

    parts = ((0, n_prompt), (n_prompt, bs * ss))
    streams = (x_prompt.reshape(n_prompt, d), x_sample.reshape(bs * ss, d))
    x = None if depth else jnp.concatenate(streams, axis=0)
    outs = None

    for i in range(depth):
        j = i // 2
        if i % 2 == 0:
            xa, xb = streams if x is None else (x[:n_prompt], x[n_prompt:])
            x, h = _pool_layer(xa, xb, norm_mix[i], pool_w[j].astype(bf16), pool_scale[j], norm_ffn[i], seq)
            nxt = i + 1 < depth
            later = [(hgrn_w_in, j, norm_mix[i + 1])] + [
                (w, j, None) for w in (hgrn_w_out, moe_w_gate, moe_w_up, moe_w_down)] if nxt else []
            act, (w_down, *cast) = _glu_up(h, ffn_w_gate[j].astype(bf16), ffn_w_up[j].astype(bf16),
                                           [(ffn_w_down, j, None)] + later)
            (x, *norm_rows), _ = _down_res(act, w_down, x, 1024, 256, nxt)
            rounded = dict(zip(("w_in", "w_out", "moe_gate", "moe_up", "moe_down"), cast))
        else:
            x16, sq = norm_rows
            w_in = rounded["w_in"]
            q = _proj_section(_proj_silu_heads_kernel, x16, sq, w_in, 0, True)
            forget = functools.partial(_proj_forget_kernel, layer=i)
            kf, gf = _proj_section(forget, x16, sq, w_in, 1, True, 2, hgrn_lb)
            kb, gb = _proj_section(forget, x16, sq, w_in, 2, True, 2, hgrn_lb)
            v = _proj_section(_proj_heads_kernel, x16, sq, w_in, 3, True)
            gate = _proj_section(_proj_silu_kernel, x16, sq, w_in, 4, False)
            s_bw = _scan_bw_states(kb, gb, v, seq, chunk)
            og = _scan(q, kf, kb, gf, gb, v, s_bw, gate, hgrn_norm[j], seq, chunk)
            (x,), _ = _down_res(og, rounded["w_out"], x, 1024, 512)
            last = i == depth - 1
            res = _moe_layer(x, norm_ffn[i], moe_router[j], rounded["moe_gate"], rounded["moe_up"],
                             rounded["moe_down"], norm_final if last else None, parts)
            x, outs = (None, res) if last else (res, None)

    if outs is None:
        outs = [_rmsnorm(x, norm_final, f32, start, rows) for start, rows in parts]
    return (outs[0].reshape(bp, sp, d), outs[1].reshape(bs, ss, d))
```
